```python
import math
import jax, jax.numpy as jnp
from jax import lax
import numpy as np

D_MODEL = 1024
BATCH = 8
SEQ = 2048
DEPTH = 1
DEC_BATCH = 2
DEC_SEQ = 16384
PAST_LEN = 128

D_MIX = 2 * D_MODEL
D_ATT = D_MIX // 2
D_SSM = D_MIX - D_ATT
HEAD_DIM = 64
N_ATT_HEADS = D_ATT // HEAD_DIM
ATT_PATTERNS = ((128, 1), (512, 4), (2048, 16))
ATT_BLOCK = 128
N_REL_BUCKETS = 32
REL_MAX_DIST = 1024
SSM_HEAD_DIM = 64
N_SSM_HEADS = D_SSM // SSM_HEAD_DIM
N_SSM_GROUPS = 4
SSM_HEADS_PER_GROUP = N_SSM_HEADS // N_SSM_GROUPS
D_STATE = 128
D_CONV = 5
SSD_CHUNK = 128
CONV_DIM = D_SSM + 2 * N_SSM_GROUPS * D_STATE
IN_DIM = 3 * D_ATT + D_SSM + CONV_DIM + 2 * N_SSM_HEADS
D_FF = 2816
EPS = 1e-6
NEG_INF = -1e30

kernel_name = 'hymba_longnet_ssd_macaron_encoder'


def _rmsnorm(x, g):
    xf = x.astype(jnp.float32)
    y = xf * lax.rsqrt(jnp.mean(xf * xf, axis=-1, keepdims=True) + EPS)
    return (y * g.astype(jnp.float32)).astype(x.dtype)


def _swiglu(x, w_gate, w_up, w_down):
    return (jax.nn.silu(x @ w_gate) * (x @ w_up)) @ w_down


def _t5_bucket(rel):
    nb = N_REL_BUCKETS // 2
    max_exact = nb // 2
    n = np.abs(rel)
    large = max_exact + (np.log(np.maximum(n, 1) / max_exact)
                         / math.log(REL_MAX_DIST / max_exact) * (nb - max_exact)).astype(np.int32)
    large = np.minimum(large, nb - 1)
    return (np.where(rel > 0, nb, 0) + np.where(n < max_exact, n, large)).astype(np.int32)


def _dilated_window_attention(q, k, v, rel_bias, window, dil):
    b, S, h, dh = q.shape
    half = window // (2 * dil)
    L = S // dil
    nblk = -(-L // ATT_BLOCK)
    Lp = nblk * ATT_BLOCK
    kw = ATT_BLOCK + 2 * half
    bd = b * dil

    def to_sub(t):
        return t.reshape(b, L, dil, h, dh).transpose(0, 2, 1, 3, 4).reshape(bd, L, h, dh)

    qs = jnp.pad(to_sub(q), ((0, 0), (0, Lp - L), (0, 0), (0, 0)))
    ks = jnp.pad(to_sub(k), ((0, 0), (half, Lp - L + half), (0, 0), (0, 0)))
    vs = jnp.pad(to_sub(v), ((0, 0), (half, Lp - L + half), (0, 0), (0, 0)))
    key_idx = np.arange(nblk)[:, None] * ATT_BLOCK + np.arange(kw)[None, :]
    kb = ks[:, key_idx]
    vb = vs[:, key_idx]
    qb = qs.reshape(bd, nblk, ATT_BLOCK, h, dh)

    rel_sub = np.arange(kw)[None, :] - half - np.arange(ATT_BLOCK)[:, None]
    bias = jnp.transpose(rel_bias[_t5_bucket(rel_sub * dil)], (2, 0, 1)).astype(jnp.float32)
    in_window = np.abs(rel_sub) <= half
    key_pos = key_idx - half
    valid = in_window[None] & ((key_pos >= 0) & (key_pos < L))[:, None, :]

    logits = jnp.einsum('bnqhd,bnkhd->bhnqk', qb, kb).astype(jnp.float32) * (dh ** -0.5)
    logits = logits + bias[None, :, None]
    logits = jnp.where(valid[None, None], logits, NEG_INF)
    lse = jax.nn.logsumexp(logits, axis=-1)
    p = jnp.exp(logits - lse[..., None])
    o = jnp.einsum('bhnqk,bnkhd->bnqhd', p, vb.astype(jnp.float32))
    o = o.reshape(bd, Lp, h, dh)[:, :L]
    lse = lse.transpose(0, 2, 3, 1).reshape(bd, Lp, h)[:, :L]
    o = o.reshape(b, dil, L, h, dh).transpose(0, 2, 1, 3, 4).reshape(b, S, h, dh)
    lse = lse.reshape(b, dil, L, h).transpose(0, 2, 1, 3).reshape(b, S, h)
    return o, lse


def _dilated_mixture(q, k, v, rel_bias):
    outs, lses = [], []
    for window, dil in ATT_PATTERNS:
        o, lse = _dilated_window_attention(q, k, v, rel_bias, window, dil)
        outs.append(o)
        lses.append(lse)
    wts = jax.nn.softmax(jnp.stack(lses, axis=0), axis=0)
    return jnp.einsum('pbsh,pbshd->bshd', wts, jnp.stack(outs, axis=0))


def _dwconv_centred(x, w, bias):
    pad = D_CONV // 2
    y = lax.conv_general_dilated(x, w[:, None, :].astype(x.dtype), window_strides=(1,),
                                 padding=[(pad, pad)], dimension_numbers=('NWC', 'WIO', 'NWC'),
                                 feature_group_count=x.shape[-1])
    return y + bias


def _ssd_chunked_scan(x, dt, A, Bm, Cm):
    b, S, G, E, P = x.shape
    N = Bm.shape[-1]
    T = SSD_CHUNK
    c = S // T
    a = (dt * A).reshape(b, c, T, G, E)
    xdt = (x.astype(jnp.float32) * dt[..., None]).reshape(b, c, T, G, E, P)
    Bc = Bm.astype(jnp.float32).reshape(b, c, T, G, N)
    Cc = Cm.astype(jnp.float32).reshape(b, c, T, G, N)
    a_cum = jnp.cumsum(a, axis=2)
    lower = np.tril(np.ones((T, T), dtype=bool))[None, None, :, :, None, None]
    seg = a_cum[:, :, :, None] - a_cum[:, :, None, :]
    decay_intra = jnp.where(lower, jnp.exp(jnp.where(lower, seg, 0.0)), 0.0)
    cb = jnp.einsum('bclgn,bcsgn->bclsg', Cc, Bc)
    y_diag = jnp.einsum('bclsg,bclsge,bcsgep->bclgep', cb, decay_intra, xdt)
    decay_to_end = jnp.exp(a_cum[:, :, -1:] - a_cum)
    chunk_states = jnp.einsum('bcsgn,bcsge,bcsgep->bcgepn', Bc, decay_to_end, xdt)
    chunk_decay = jnp.exp(a_cum[:, :, -1])

    def step(hst, inp):
        dec, st = inp
        return hst * dec[..., None, None] + st, hst

    h0 = jnp.zeros((b, G, E, P, N), jnp.float32)
    _, h_prev = lax.scan(step, h0, (jnp.moveaxis(chunk_decay, 1, 0), jnp.moveaxis(chunk_states, 1, 0)))
    h_prev = jnp.moveaxis(h_prev, 0, 1)
    y_off = jnp.einsum('bclgn,bcgepn,bclge->bclgep', Cc, h_prev, jnp.exp(a_cum))
    return (y_diag + y_off).reshape(b, S, G, E, P)


def _ssd_bidirectional(xs, dt_raw, dt_bias, a_log, d_skip, Bm, Cm):
    b, S = xs.shape[:2]
    G, E = N_SSM_GROUPS, SSM_HEADS_PER_GROUP
    dt = jax.nn.softplus(dt_raw.astype(jnp.float32) + dt_bias.astype(jnp.float32))
    A = -jnp.exp(a_log.astype(jnp.float32))
    xg = xs.reshape(b, S, G, E, SSM_HEAD_DIM)
    flip = lambda t: jnp.flip(t, axis=1)
    y_f = _ssd_chunked_scan(xg, dt[:, :, 0].reshape(b, S, G, E), A[0].reshape(G, E), Bm, Cm)
    y_b = flip(_ssd_chunked_scan(flip(xg), flip(dt[:, :, 1]).reshape(b, S, G, E),
                                 A[1].reshape(G, E), flip(Bm), flip(Cm)))
    y = y_f + y_b + d_skip.astype(jnp.float32).reshape(G, E)[:, :, None] * xg.astype(jnp.float32)
    return y.reshape(b, S, D_SSM)


def _layer(x, rel_bias, ffn1_norm_g, ffn1_w_gate, ffn1_w_up, ffn1_w_down, mix_norm_g, w_in,
           q_norm_g, k_norm_g, attn_out_g, conv_w, conv_b, dt_bias, a_log, d_skip, ssm_out_g,
           w_out, ffn2_norm_g, ffn2_w_gate, ffn2_w_up, ffn2_w_down):
    b, S, _ = x.shape
    x = x + (0.5 * _swiglu(_rmsnorm(x, ffn1_norm_g), ffn1_w_gate, ffn1_w_up, ffn1_w_down)).astype(x.dtype)
    hn = _rmsnorm(x, mix_norm_g)
    proj = hn @ w_in
    q, k, v, z, xbc, dt_raw = jnp.split(
        proj, [D_ATT, 2 * D_ATT, 3 * D_ATT, 3 * D_ATT + D_SSM, 3 * D_ATT + D_SSM + CONV_DIM], axis=-1)
    q = _rmsnorm(q.reshape(b, S, N_ATT_HEADS, HEAD_DIM), q_norm_g)
    k = _rmsnorm(k.reshape(b, S, N_ATT_HEADS, HEAD_DIM), k_norm_g)
    v = v.reshape(b, S, N_ATT_HEADS, HEAD_DIM)
    attn = _dilated_mixture(q, k, v, rel_bias).reshape(b, S, D_ATT)
    attn = _rmsnorm(attn, attn_out_g)
    xbc = jax.nn.silu(_dwconv_centred(xbc, conv_w, conv_b))
    xs, Bm, Cm = jnp.split(xbc, [D_SSM, D_SSM + N_SSM_GROUPS * D_STATE], axis=-1)
    y = _ssd_bidirectional(xs.reshape(b, S, N_SSM_HEADS, SSM_HEAD_DIM),
                           dt_raw.reshape(b, S, 2, N_SSM_HEADS), dt_bias, a_log, d_skip,
                           Bm.reshape(b, S, N_SSM_GROUPS, D_STATE), Cm.reshape(b, S, N_SSM_GROUPS, D_STATE))
    y = y * jax.nn.silu(z.astype(jnp.float32))
    gsz = D_SSM // N_SSM_GROUPS
    y = _rmsnorm(y.reshape(b, S, N_SSM_GROUPS, gsz), ssm_out_g.reshape(N_SSM_GROUPS, gsz)).reshape(b, S, D_SSM)
    mix = jnp.concatenate([attn, y], axis=-1)
    x = x + (mix @ w_out).astype(x.dtype)
    x = x + (0.5 * _swiglu(_rmsnorm(x, ffn2_norm_g), ffn2_w_gate, ffn2_w_up, ffn2_w_down)).astype(x.dtype)
    return x


def setup_inputs(seed: int = 0) -> dict:
    key = jax.random.key(seed)
    ks = jax.random.split(key, 24)
    f32 = jnp.float32
    nrm = lambda k, shape, s: jax.random.normal(k, shape, f32) * s
    gain = lambda k, shape: 1.0 + 0.02 * jax.random.normal(k, shape, f32)
    dt0 = jnp.exp(jax.random.uniform(ks[14], (DEPTH, 2, N_SSM_HEADS), f32)
                  * (math.log(0.1) - math.log(0.001)) + math.log(0.001))
    return {
        'x_prompt': jax.random.normal(ks[0], (BATCH, SEQ, D_MODEL), f32),
        'x_sample': jax.random.normal(ks[1], (DEC_BATCH, DEC_SEQ, D_MODEL), f32),
        'rel_bias': nrm(ks[2], (N_REL_BUCKETS, N_ATT_HEADS), 0.5),
        'ffn1_norm_g': gain(ks[3], (DEPTH, D_MODEL)),
        'ffn1_w_gate': nrm(ks[4], (DEPTH, D_MODEL, D_FF), D_MODEL ** -0.5),
        'ffn1_w_up': nrm(ks[5], (DEPTH, D_MODEL, D_FF), D_MODEL ** -0.5),
        'ffn1_w_down': nrm(ks[6], (DEPTH, D_FF, D_MODEL), D_FF ** -0.5),
        'mix_norm_g': gain(ks[7], (DEPTH, D_MODEL)),
        'w_in': nrm(ks[8], (DEPTH, D_MODEL, IN_DIM), D_MODEL ** -0.5),
        'q_norm_g': gain(ks[9], (DEPTH, HEAD_DIM)),
        'k_norm_g': gain(ks[10], (DEPTH, HEAD_DIM)),
        'attn_out_g': gain(ks[11], (DEPTH, D_ATT)),
        'conv_w': nrm(ks[12], (DEPTH, D_CONV, CONV_DIM), D_CONV ** -0.5),
        'conv_b': nrm(ks[13], (DEPTH, CONV_DIM), 0.01),
        'dt_bias': dt0 + jnp.log(-jnp.expm1(-dt0)),
        'a_log': jnp.log(jax.random.uniform(ks[15], (DEPTH, 2, N_SSM_HEADS), f32, 1.0, 16.0)),
        'd_skip': gain(ks[16], (DEPTH, N_SSM_HEADS)),
        'ssm_out_g': gain(ks[17], (DEPTH, D_SSM)),
        'w_out': nrm(ks[18], (DEPTH, D_MIX, D_MODEL), D_MIX ** -0.5),
        'ffn2_norm_g': gain(ks[19], (DEPTH, D_MODEL)),
        'ffn2_w_gate': nrm(ks[20], (DEPTH, D_MODEL, D_FF), D_MODEL ** -0.5),
        'ffn2_w_up': nrm(ks[21], (DEPTH, D_MODEL, D_FF), D_MODEL ** -0.5),
        'ffn2_w_down': nrm(ks[22], (DEPTH, D_FF, D_MODEL), D_FF ** -0.5),
    }


def reference(x_prompt, x_sample, rel_bias, ffn1_norm_g, ffn1_w_gate, ffn1_w_up, ffn1_w_down,
              mix_norm_g, w_in, q_norm_g, k_norm_g, attn_out_g, conv_w, conv_b, dt_bias, a_log,
              d_skip, ssm_out_g, w_out, ffn2_norm_g, ffn2_w_gate, ffn2_w_up, ffn2_w_down):
    def trunk(x):
        for l in range(DEPTH):
            x = _layer(x, rel_bias, ffn1_norm_g[l], ffn1_w_gate[l], ffn1_w_up[l], ffn1_w_down[l],
                       mix_norm_g[l], w_in[l], q_norm_g[l], k_norm_g[l], attn_out_g[l], conv_w[l],
                       conv_b[l], dt_bias[l], a_log[l], d_skip[l], ssm_out_g[l], w_out[l],
                       ffn2_norm_g[l], ffn2_w_gate[l], ffn2_w_up[l], ffn2_w_down[l])
        return x

    y_prompt = trunk(x_prompt)
    y_sample = trunk(x_sample)
    return (y_prompt, y_sample)
```

```python
import functools
import math

import numpy as np
import jax
import jax.numpy as jnp
from jax import lax
from jax.experimental import pallas as pl
from jax.experimental.pallas import tpu as pltpu

D_MODEL = 1024
D_ATT = 1024
D_SSM = 1024
HEAD_DIM = 64
N_HEADS = 16
ATT_PATTERNS = ((128, 1), (512, 4), (2048, 16))
ATT_BLOCK = 128
ATT_HALF = 64
ATT_KW = ATT_BLOCK + 2 * ATT_HALF
N_REL_BUCKETS = 32
REL_MAX_DIST = 1024
N_GROUPS = 4
HEADS_PER_GROUP = 4
D_STATE = 128
D_CONV = 5
CHUNK = 128
CONV_DIM = D_SSM + 2 * N_GROUPS * D_STATE
D_FF = 2816
EPS = 1e-6
NEG_INF = -1e30

LANES = 128
HALO_ROWS = 8
VMEM_LIMIT = 56 * 1024 * 1024

F32 = jnp.float32
BF16 = jnp.bfloat16


def _params(*sem):
    return pltpu.CompilerParams(dimension_semantics=sem, vmem_limit_bytes=VMEM_LIMIT)


def _const_spec(shape):
    n = len(shape)
    return pl.BlockSpec(shape, lambda *_: (0,) * n, pipeline_mode=pl.Buffered(1))


def _rms(x, g):
    ms = jnp.mean(x * x, axis=-1, keepdims=True)
    return x * lax.rsqrt(ms + EPS) * g


def _silu(x):
    return x * (1.0 / (1.0 + jnp.exp(-x)))


def _dot(a, b):
    return jnp.dot(a, b, preferred_element_type=F32)


def _expand(vals, e_ref):
    hi = vals.astype(BF16)
    lo = (vals - hi.astype(F32)).astype(BF16)
    return _dot(jnp.concatenate([hi, lo], axis=1), e_ref[...])


def _ffn_body(x_ref, g_ref, wg_ref, wu_ref, wd_ref, o_ref):
    x = x_ref[...]
    h = _rms(x, g_ref[...]).astype(BF16)
    gate = _dot(h, wg_ref[...])
    up = _dot(h, wu_ref[...])
    act = (_silu(gate) * up).astype(BF16)
    o_ref[...] = x + 0.5 * _dot(act, wd_ref[...])


def _ffn(x, g, wg, wu, wd, tm):
    n = x.shape[0]
    row = pl.BlockSpec((tm, D_MODEL), lambda i: (i, 0))
    return pl.pallas_call(
        _ffn_body,
        grid=(n // tm,),
        in_specs=[row, _const_spec((1, D_MODEL)), _const_spec((D_MODEL, D_FF)),
                  _const_spec((D_MODEL, D_FF)), _const_spec((D_FF, D_MODEL))],
        out_specs=row,
        out_shape=jax.ShapeDtypeStruct((n, D_MODEL), F32),
        compiler_params=_params("parallel"),
        name="ffn",
    )(x, g, wg, wu, wd)


def _inproj_body(x_ref, g_ref, wq_ref, wk_ref, wv_ref, wz_ref, wx_ref, wdt_ref, qg_ref, kg_ref,
                 bd_ref, q_ref, k_ref, v_ref, z_ref, xbc_ref, dt_ref):
    h = _rms(x_ref[...], g_ref[...]).astype(BF16)

    def head_norm(t, gain):
        t2 = (t * t).astype(BF16)
        w = bd_ref.shape[0]
        ms = jnp.concatenate([_dot(t2[:, j * w:(j + 1) * w], bd_ref[...])
                              for j in range(D_ATT // w)], axis=1)
        return t * lax.rsqrt(ms + EPS) * gain

    q_ref[...] = head_norm(_dot(h, wq_ref[...]), qg_ref[...]).astype(BF16)
    k_ref[...] = head_norm(_dot(h, wk_ref[...]), kg_ref[...]).astype(BF16)
    v_ref[...] = _dot(h, wv_ref[...]).astype(BF16)
    z_ref[...] = _dot(h, wz_ref[...])
    xbc_ref[...] = _dot(h, wx_ref[...])
    dt_ref[...] = _dot(h, wdt_ref[...])


def _inproj(x, g, wq, wk, wv, wz, wx, wdt, qg, kg, bd, tm):
    n = x.shape[0]
    row = lambda w: pl.BlockSpec((tm, w), lambda i: (i, 0))
    sq = _const_spec((D_MODEL, D_ATT))
    return pl.pallas_call(
        _inproj_body,
        grid=(n // tm,),
        in_specs=[row(D_MODEL), _const_spec((1, D_MODEL)), sq, sq, sq, sq,
                  _const_spec((D_MODEL, CONV_DIM)), _const_spec((D_MODEL, LANES)),
                  _const_spec((1, D_ATT)), _const_spec((1, D_ATT)), _const_spec(bd.shape)],
        out_specs=[row(D_ATT), row(D_ATT), row(D_ATT), row(D_SSM), row(CONV_DIM), row(LANES)],
        out_shape=[jax.ShapeDtypeStruct((n, D_ATT), BF16), jax.ShapeDtypeStruct((n, D_ATT), BF16),
                   jax.ShapeDtypeStruct((n, D_ATT), BF16), jax.ShapeDtypeStruct((n, D_SSM), F32),
                   jax.ShapeDtypeStruct((n, CONV_DIM), F32), jax.ShapeDtypeStruct((n, LANES), F32)],
        compiler_params=_params("parallel"),
        name="inproj",
    )(x, g, wq, wk, wv, wz, wx, wdt, qg, kg, bd)


def _attn_body(q_ref, kc_ref, kp_ref, kn_ref, vc_ref, vp_ref, vn_ref, bias_ref, o_ref, lse_ref,
               kbuf, vbuf, *, tl, sub_len):
    t = pl.program_id(2)
    kbuf[0:ATT_HALF, :] = kp_ref[...]
    kbuf[ATT_HALF:ATT_HALF + tl, :] = kc_ref[...]
    kbuf[ATT_HALF + tl:, :] = kn_ref[...]
    vbuf[0:ATT_HALF, :] = vp_ref[...]
    vbuf[ATT_HALF:ATT_HALF + tl, :] = vc_ref[...]
    vbuf[ATT_HALF + tl:, :] = vn_ref[...]

    lane = lax.broadcasted_iota(jnp.int32, (ATT_BLOCK, LANES), 1)
    first_head = lane < HEAD_DIM
    col = lax.broadcasted_iota(jnp.int32, (1, ATT_KW), 1)

    def block(j, carry):
        r0 = pl.multiple_of(j * ATT_BLOCK, ATT_BLOCK)
        key_pos = t * tl + r0 - ATT_HALF + col
        valid = (key_pos >= 0) & (key_pos < sub_len)
        lse_acc = jnp.zeros((ATT_BLOCK, LANES), F32)
        for pair in range(N_HEADS // 2):
            lanes = slice(pair * LANES, (pair + 1) * LANES)
            qp = q_ref[pl.ds(r0, ATT_BLOCK), lanes]
            kp = kbuf[pl.ds(r0, ATT_KW), lanes]
            vp = vbuf[pl.ds(r0, ATT_KW), lanes]
            outs = []
            for sub in range(2):
                h = 2 * pair + sub
                qh = jnp.where(first_head if sub == 0 else ~first_head, qp, jnp.zeros_like(qp))
                s = lax.dot_general(qh, kp, (((1,), (1,)), ((), ())), preferred_element_type=F32)
                s = jnp.where(valid, s + bias_ref[h], NEG_INF)
                m = jnp.max(s, axis=-1, keepdims=True)
                p = jnp.exp(s - m)
                l = jnp.sum(p, axis=-1, keepdims=True)
                outs.append(_dot(p.astype(BF16), vp) * (1.0 / l))
                lse_acc = jnp.where(lane == h, m + jnp.log(l), lse_acc)
            o_ref[pl.ds(r0, ATT_BLOCK), lanes] = jnp.where(first_head, outs[0], outs[1])
        lse_ref[pl.ds(r0, ATT_BLOCK), :] = lse_acc
        return carry

    lax.fori_loop(0, tl // ATT_BLOCK, block, 0)


def _attn_pattern(q, k, v, bias, dil):
    b, s, _ = q.shape
    sub_len = s // dil
    tl = min(sub_len, 512)
    nt = sub_len // tl
    hb = tl // ATT_HALF
    last_halo = sub_len // ATT_HALF - 1
    view = lambda a: a.reshape(b, sub_len, dil * a.shape[-1])
    cur = pl.BlockSpec((None, tl, D_ATT), lambda bi, r, t: (bi, t, r))
    prev = pl.BlockSpec((None, ATT_HALF, D_ATT), lambda bi, r, t: (bi, jnp.maximum(t * hb - 1, 0), r))
    nxt = pl.BlockSpec((None, ATT_HALF, D_ATT),
                       lambda bi, r, t: (bi, jnp.minimum((t + 1) * hb, last_halo), r))
    o, lse = pl.pallas_call(
        functools.partial(_attn_body, tl=tl, sub_len=sub_len),
        grid=(b, dil, nt),
        in_specs=[cur, cur, prev, nxt, cur, prev, nxt, _const_spec(bias.shape)],
        out_specs=[cur, pl.BlockSpec((None, tl, LANES), lambda bi, r, t: (bi, t, r))],
        out_shape=[jax.ShapeDtypeStruct((b, sub_len, dil * D_ATT), F32),
                   jax.ShapeDtypeStruct((b, sub_len, dil * LANES), F32)],
        scratch_shapes=[pltpu.VMEM((tl + 2 * ATT_HALF, D_ATT), BF16),
                        pltpu.VMEM((tl + 2 * ATT_HALF, D_ATT), BF16)],
        compiler_params=_params("parallel", "parallel", "parallel"),
        name=f"attn_d{dil}",
    )(view(q), view(k), view(k), view(k), view(v), view(v), view(v), bias)
    return o.reshape(b, s, D_ATT), lse.reshape(b, s, LANES)


def _conv_silu(xc_ref, xp_ref, xn_ref, cw_ref, cb_ref, ext, c, nc):
    ext[0:HALO_ROWS, :] = jnp.where(c > 0, xp_ref[...], 0.0)
    ext[HALO_ROWS:HALO_ROWS + CHUNK, :] = xc_ref[...]
    ext[HALO_ROWS + CHUNK:, :] = jnp.where(c < nc - 1, xn_ref[...], 0.0)
    pad = D_CONV // 2
    acc = cb_ref[...] + ext[HALO_ROWS - pad:HALO_ROWS - pad + CHUNK, :] * cw_ref[0:1, :]
    for j in range(1, D_CONV):
        acc = acc + ext[HALO_ROWS - pad + j:HALO_ROWS - pad + j + CHUNK, :] * cw_ref[j:j + 1, :]
    return _silu(acc)


def _softplus(x):
    return jnp.maximum(x, 0.0) + jnp.log(1.0 + jnp.exp(-jnp.abs(x)))


def _tri(lower):
    r = lax.broadcasted_iota(jnp.int32, (CHUNK, CHUNK), 0)
    c = lax.broadcasted_iota(jnp.int32, (CHUNK, CHUNK), 1)
    return (r >= c) if lower else (r <= c)


def _cumsum_rows(a, lower):
    return jnp.dot(_tri(lower).astype(F32), a, precision=lax.Precision.HIGHEST,
                   preferred_element_type=F32)


def _ssd_bwd_body(xc_ref, xp_ref, xn_ref, dt_ref, cw_ref, cb_ref, dtb_ref, a_ref, e_ref,
                  yoff_ref, ext, hst, *, nc):
    i = pl.program_id(1)
    c = nc - 1 - i

    @pl.when(i == 0)
    def _():
        hst[...] = jnp.zeros_like(hst)

    xa = _conv_silu(xc_ref, xp_ref, xn_ref, cw_ref, cb_ref, ext, c, nc)
    dt = _softplus(dt_ref[...] + dtb_ref[...])
    rcum = _cumsum_rows(dt * a_ref[...], lower=False)
    decay_in = _expand(jnp.exp(rcum), e_ref)
    w_in = _expand(jnp.exp(rcum[0:1, :] - rcum) * dt, e_ref)
    chunk_decay = _expand(jnp.broadcast_to(jnp.exp(rcum[0:1, :]), (HALO_ROWS, LANES)), e_ref)[0:1, :]

    xs = xa[:, :D_SSM]
    xw = (xs * w_in).astype(BF16)
    for g in range(N_GROUPS):
        gl = slice(g * HEADS_PER_GROUP * HEAD_DIM, (g + 1) * HEADS_PER_GROUP * HEAD_DIM)
        bg = xa[:, D_SSM + g * D_STATE:D_SSM + (g + 1) * D_STATE].astype(BF16)
        cg = xa[:, D_SSM + (N_GROUPS + g) * D_STATE:D_SSM + (N_GROUPS + g + 1) * D_STATE].astype(BF16)
        h_in = hst[:, gl]
        yoff_ref[:, gl] = _dot(cg, h_in.astype(BF16)) * decay_in[:, gl]
        upd = lax.dot_general(bg, xw[:, gl], (((0,), (0,)), ((), ())), preferred_element_type=F32)
        hst[:, gl] = h_in * chunk_decay[:, gl] + upd


def _ssd_fwd_body(xc_ref, xp_ref, xn_ref, dt_ref, z_ref, yb_ref, cw_ref, cb_ref, dtb_ref, a_ref,
                  dsk_ref, og_ref, e_ref, y_ref, ext, hst, *, nc):
    c = pl.program_id(1)

    @pl.when(c == 0)
    def _():
        hst[...] = jnp.zeros_like(hst)

    xa = _conv_silu(xc_ref, xp_ref, xn_ref, cw_ref, cb_ref, ext, c, nc)
    dt = _softplus(dt_ref[...] + dtb_ref[...])
    a = dt * a_ref[...]
    fcum = _cumsum_rows(a, lower=True)
    rcum = _cumsum_rows(a, lower=False)
    dt_t, fcum_t, rcum_t = dt.T, fcum.T, rcum.T
    decay_in = _expand(jnp.exp(fcum), e_ref)
    w_in = _expand(jnp.exp(fcum[CHUNK - 1:CHUNK, :] - fcum) * dt, e_ref)
    chunk_decay = _expand(jnp.broadcast_to(jnp.exp(fcum[CHUNK - 1:CHUNK, :]), (HALO_ROWS, LANES)),
                          e_ref)[0:1, :]
    lower, upper = _tri(True), _tri(False)
    lane = lax.broadcasted_iota(jnp.int32, (CHUNK, LANES), 1)

    xs = xa[:, :D_SSM]
    xsb = xs.astype(BF16)
    xw = (xs * w_in).astype(BF16)
    for g in range(N_GROUPS):
        gl = slice(g * HEADS_PER_GROUP * HEAD_DIM, (g + 1) * HEADS_PER_GROUP * HEAD_DIM)
        bg = xa[:, D_SSM + g * D_STATE:D_SSM + (g + 1) * D_STATE].astype(BF16)
        cg = xa[:, D_SSM + (N_GROUPS + g) * D_STATE:D_SSM + (N_GROUPS + g + 1) * D_STATE].astype(BF16)
        cb = lax.dot_general(cg, bg, (((1,), (1,)), ((), ())), preferred_element_type=F32)
        h_in = hst[:, gl]
        y_off = _dot(cg, h_in.astype(BF16)) * decay_in[:, gl]
        upd = lax.dot_general(bg, xw[:, gl], (((0,), (0,)), ((), ())), preferred_element_type=F32)
        hst[:, gl] = h_in * chunk_decay[:, gl] + upd
        gated = []
        for pair in range(HEADS_PER_GROUP // 2):
            pl_ = slice(g * 256 + pair * LANES, g * 256 + (pair + 1) * LANES)
            x_pair = xsb[:, pl_]
            halves = []
            for sub in range(2):
                h = g * HEADS_PER_GROUP + 2 * pair + sub
                hb = N_HEADS + h
                seg_f = jnp.where(lower, fcum[:, h:h + 1] - fcum_t[h:h + 1, :], NEG_INF)
                seg_b = jnp.where(upper, rcum[:, hb:hb + 1] - rcum_t[hb:hb + 1, :], NEG_INF)
                mix = jnp.exp(seg_f) * dt_t[h:h + 1, :] + jnp.exp(seg_b) * dt_t[hb:hb + 1, :]
                halves.append(_dot((cb * mix).astype(BF16), x_pair))
            y_diag = jnp.where(lane < HEAD_DIM, halves[0], halves[1])
            po = slice(pair * LANES, (pair + 1) * LANES)
            y = (y_diag + y_off[:, po] + yb_ref[:, pl_] + dsk_ref[:, pl_] * xs[:, pl_])
            gated.append(y * _silu(z_ref[:, pl_]))
        y_ref[:, gl] = _rms(jnp.concatenate(gated, axis=1), og_ref[:, gl]).astype(BF16)


def _ssd(xbc, dt, z, conv_w, conv_b, dt_bias, a_neg, d_skip, out_g, e_fwd, e_bwd):
    b, s, _ = xbc.shape
    nc = s // CHUNK
    hpc = CHUNK // HALO_ROWS
    last_halo = s // HALO_ROWS - 1

    def specs(cidx):
        chunk = lambda w: pl.BlockSpec((None, CHUNK, w), lambda bi, i: (bi, cidx(i), 0))
        prev = pl.BlockSpec((None, HALO_ROWS, CONV_DIM),
                            lambda bi, i: (bi, jnp.maximum(cidx(i) * hpc - 1, 0), 0))
        nxt = pl.BlockSpec((None, HALO_ROWS, CONV_DIM),
                           lambda bi, i: (bi, jnp.minimum((cidx(i) + 1) * hpc, last_halo), 0))
        return chunk, prev, nxt

    scratch = [pltpu.VMEM((CHUNK + 2 * HALO_ROWS, CONV_DIM), F32), pltpu.VMEM((D_STATE, D_SSM), F32)]
    small = [_const_spec(conv_w.shape), _const_spec(conv_b.shape), _const_spec(dt_bias.shape),
             _const_spec(a_neg.shape)]

    chunk, prev, nxt = specs(lambda i: nc - 1 - i)
    yb_off = pl.pallas_call(
        functools.partial(_ssd_bwd_body, nc=nc),
        grid=(b, nc),
        in_specs=[chunk(CONV_DIM), prev, nxt, chunk(LANES)] + small + [_const_spec(e_bwd.shape)],
        out_specs=chunk(D_SSM),
        out_shape=jax.ShapeDtypeStruct((b, s, D_SSM), F32),
        scratch_shapes=scratch,
        compiler_params=_params("parallel", "arbitrary"),
        name="ssd_bwd",
    )(xbc, xbc, xbc, dt, conv_w, conv_b, dt_bias, a_neg, e_bwd)

    chunk, prev, nxt = specs(lambda i: i)
    return pl.pallas_call(
        functools.partial(_ssd_fwd_body, nc=nc),
        grid=(b, nc),
        in_specs=[chunk(CONV_DIM), prev, nxt, chunk(LANES), chunk(D_SSM), chunk(D_SSM)] + small
                 + [_const_spec(d_skip.shape), _const_spec(out_g.shape), _const_spec(e_fwd.shape)],
        out_specs=chunk(D_SSM),
        out_shape=jax.ShapeDtypeStruct((b, s, D_SSM), BF16),
        scratch_shapes=scratch,
        compiler_params=_params("parallel", "arbitrary"),
        name="ssd_fwd",
    )(xbc, xbc, xbc, dt, z, yb_off, conv_w, conv_b, dt_bias, a_neg, d_skip, out_g, e_fwd)


def _outproj_body(x_ref, o1_ref, o2_ref, o3_ref, l1_ref, l2_ref, l3_ref, y_ref, g_ref, e_ref,
                  wa_ref, ws_ref, out_ref):
    l1, l2, l3 = l1_ref[...], l2_ref[...], l3_ref[...]
    m = jnp.maximum(jnp.maximum(l1, l2), l3)
    e1, e2, e3 = jnp.exp(l1 - m), jnp.exp(l2 - m), jnp.exp(l3 - m)
    inv = 1.0 / (e1 + e2 + e3)
    attn = (_expand(e1 * inv, e_ref) * o1_ref[...] + _expand(e2 * inv, e_ref) * o2_ref[...]
            + _expand(e3 * inv, e_ref) * o3_ref[...])
    attn = _rms(attn, g_ref[...]).astype(BF16)
    out_ref[...] = x_ref[...] + _dot(attn, wa_ref[...]) + _dot(y_ref[...], ws_ref[...])


def _outproj(x, o1, o2, o3, l1, l2, l3, y, g, e, wa, ws, tm):
    n = x.shape[0]
    row = lambda w: pl.BlockSpec((tm, w), lambda i: (i, 0))
    sq = _const_spec((D_ATT, D_MODEL))
    return pl.pallas_call(
        _outproj_body,
        grid=(n // tm,),
        in_specs=[row(D_MODEL), row(D_ATT), row(D_ATT), row(D_ATT), row(LANES), row(LANES),
                  row(LANES), row(D_SSM), _const_spec((1, D_ATT)), _const_spec(e.shape), sq, sq],
        out_specs=row(D_MODEL),
        out_shape=jax.ShapeDtypeStruct((n, D_MODEL), F32),
        compiler_params=_params("parallel"),
        name="outproj",
    )(x, o1, o2, o3, l1, l2, l3, y, g, e, wa, ws)


def _t5_bucket(rel):
    nb = N_REL_BUCKETS // 2
    max_exact = nb // 2
    n = np.abs(rel)
    large = max_exact + (np.log(np.maximum(n, 1) / max_exact)
                         / math.log(REL_MAX_DIST / max_exact) * (nb - max_exact)).astype(np.int32)
    large = np.minimum(large, nb - 1)
    return (np.where(rel > 0, nb, 0) + np.where(n < max_exact, n, large)).astype(np.int32)


def _bias_table(rel_bias, dil):
    rel_sub = np.arange(ATT_KW)[None, :] - ATT_HALF - np.arange(ATT_BLOCK)[:, None]
    bias = jnp.transpose(rel_bias[_t5_bucket(rel_sub * dil)], (2, 0, 1)).astype(F32)
    return jnp.where(np.abs(rel_sub)[None] <= ATT_HALF, bias, NEG_INF)


def _head_expander(first_row, width):
    e = np.zeros((2 * LANES, width), np.float32)
    for h in range(width // HEAD_DIM):
        e[first_row + h, h * HEAD_DIM:(h + 1) * HEAD_DIM] = 1.0
        e[LANES + first_row + h, h * HEAD_DIM:(h + 1) * HEAD_DIM] = 1.0
    return jnp.asarray(e, BF16)


def _block_diag_mean(width):
    i = np.arange(width)
    return jnp.asarray((i[:, None] // HEAD_DIM == i[None, :] // HEAD_DIM) / HEAD_DIM, BF16)


def _layer(x, p, tm):
    b, s, _ = x.shape
    n = b * s
    xf = x.reshape(n, D_MODEL)
    x1 = _ffn(xf, p["ffn1_g"], p["ffn1_wg"], p["ffn1_wu"], p["ffn1_wd"], tm)
    q, k, v, z, xbc, dt = _inproj(x1, p["mix_g"], p["wq"], p["wk"], p["wv"], p["wz"], p["wx"],
                                  p["wdt"], p["qg"], p["kg"], p["bd"], tm)
    seq = lambda a: a.reshape(b, s, a.shape[-1])
    att = [_attn_pattern(seq(q), seq(k), seq(v), bias, dil)
           for bias, (_, dil) in zip(p["bias"], ATT_PATTERNS)]
    y = _ssd(seq(xbc), seq(dt), seq(z), p["conv_w"], p["conv_b"], p["dt_bias"], p["a_neg"],
             p["d_skip"], p["ssm_g"], p["e_fwd"], p["e_bwd"])
    flat = lambda a: a.reshape(n, a.shape[-1])
    x2 = _outproj(x1, flat(att[0][0]), flat(att[1][0]), flat(att[2][0]), flat(att[0][1]),
                  flat(att[1][1]), flat(att[2][1]), flat(y), p["attn_g"], p["e_fwd"], p["wo_att"],
                  p["wo_ssm"], tm)
    out = _ffn(x2, p["ffn2_g"], p["ffn2_wg"], p["ffn2_wu"], p["ffn2_wd"], tm)
    return out.reshape(b, s, D_MODEL)


def _prepare(rel_bias, ffn1_norm_g, ffn1_w_gate, ffn1_w_up, ffn1_w_down, mix_norm_g, w_in,
             q_norm_g, k_norm_g, attn_out_g, conv_w, conv_b, dt_bias, a_log, d_skip, ssm_out_g,
             w_out, ffn2_norm_g, ffn2_w_gate, ffn2_w_up, ffn2_w_down):
    row = lambda a: a.reshape(1, -1).astype(F32)
    w16 = lambda a: a.astype(BF16)
    c0, c1, c2, c3 = D_ATT, 2 * D_ATT, 3 * D_ATT, 3 * D_ATT + D_SSM
    c4 = c3 + CONV_DIM
    pad32 = lambda a: jnp.pad(a.reshape(1, 2 * N_HEADS).astype(F32), ((0, 0), (0, LANES - 2 * N_HEADS)))
    return {
        "ffn1_g": row(ffn1_norm_g), "ffn1_wg": w16(ffn1_w_gate), "ffn1_wu": w16(ffn1_w_up),
        "ffn1_wd": w16(ffn1_w_down),
        "ffn2_g": row(ffn2_norm_g), "ffn2_wg": w16(ffn2_w_gate), "ffn2_wu": w16(ffn2_w_up),
        "ffn2_wd": w16(ffn2_w_down),
        "mix_g": row(mix_norm_g),
        "wq": w16(w_in[:, :c0]), "wk": w16(w_in[:, c0:c1]), "wv": w16(w_in[:, c1:c2]),
        "wz": w16(w_in[:, c2:c3]), "wx": w16(w_in[:, c3:c4]),
        "wdt": w16(jnp.pad(w_in[:, c4:], ((0, 0), (0, LANES - 2 * N_HEADS)))),
        "qg": row(jnp.tile(q_norm_g, N_HEADS)) * HEAD_DIM ** -0.5,
        "kg": row(jnp.tile(k_norm_g, N_HEADS)),
        "bd": _block_diag_mean(256),
        "bias": [_bias_table(rel_bias, dil) for _, dil in ATT_PATTERNS],
        "attn_g": row(attn_out_g),
        "conv_w": conv_w.astype(F32), "conv_b": row(conv_b),
        "dt_bias": pad32(dt_bias), "a_neg": pad32(-jnp.exp(a_log.astype(F32))),
        "d_skip": row(jnp.repeat(d_skip, HEAD_DIM)), "ssm_g": row(ssm_out_g),
        "e_fwd": _head_expander(0, D_SSM), "e_bwd": _head_expander(N_HEADS, D_SSM),
        "wo_att": w16(w_out[:D_ATT]), "wo_ssm": w16(w_out[D_ATT:]),
    }


def _trunk(x, layers, tm=512):
    for p in layers:
        x = _layer(x, p, tm)
    return x


def kernel(x_prompt, x_sample, rel_bias, ffn1_norm_g, ffn1_w_gate, ffn1_w_up, ffn1_w_down, mix_norm_g, w_in, q_norm_g, k_norm_g, attn_out_g, conv_w, conv_b, dt_bias, a_log, d_skip, ssm_out_g, w_out, ffn2_norm_g, ffn2_w_gate, ffn2_w_up, ffn2_w_down):
    per_layer = (ffn1_norm_g, ffn1_w_gate, ffn1_w_up, ffn1_w_down, mix_norm_g, w_in, q_norm_g,
                 k_norm_g, attn_out_g, conv_w, conv_b, dt_bias, a_log, d_skip, ssm_out_g, w_out,
                 ffn2_norm_g, ffn2_w_gate, ffn2_w_up, ffn2_w_down)
    layers = [_prepare(rel_bias, *(a[l] for a in per_layer)) for l in range(ffn1_norm_g.shape[0])]
    return (_trunk(x_prompt, layers), _trunk(x_sample, layers))
```

```python
import functools
import math

import numpy as np
import jax
import jax.numpy as jnp
from jax import lax
from jax.experimental import pallas as pl
from jax.experimental.pallas import tpu as pltpu

D_MODEL = 1024
D_ATT = 1024
D_SSM = 1024
HEAD_DIM = 64
N_HEADS = 16
ATT_PATTERNS = ((128, 1), (512, 4), (2048, 16))
ATT_BLOCK = 128
ATT_HALF = 64
ATT_KW = ATT_BLOCK + 2 * ATT_HALF
ATT_MAX_DIL = max(d for _, d in ATT_PATTERNS)
ATT_TILE = ATT_BLOCK * ATT_MAX_DIL
ATT_REACH = ATT_HALF * ATT_MAX_DIL
ATT_UNROLL = 4
N_REL_BUCKETS = 32
REL_MAX_DIST = 1024
N_GROUPS = 4
HEADS_PER_GROUP = 4
D_STATE = 128
D_CONV = 5
CHUNK = 128
CONV_DIM = D_SSM + 2 * N_GROUPS * D_STATE
D_FF = 2816
EPS = 1e-6
NEG_INF = -1e30

LANES = 128
HALO_ROWS = 8
VMEM_LIMIT = 56 * 1024 * 1024

F32 = jnp.float32
BF16 = jnp.bfloat16


def _params(*sem):
    return pltpu.CompilerParams(dimension_semantics=sem, vmem_limit_bytes=VMEM_LIMIT)


def _const_spec(shape):
    n = len(shape)
    return pl.BlockSpec(shape, lambda *_: (0,) * n, pipeline_mode=pl.Buffered(1))


def _rms(x, g):
    ms = jnp.mean(x * x, axis=-1, keepdims=True)
    return x * lax.rsqrt(ms + EPS) * g


def _silu(x):
    return x * (1.0 / (1.0 + jnp.exp(-x)))


def _dot(a, b):
    return jnp.dot(a, b, preferred_element_type=F32)


def _expand(vals, e_ref):
    hi = vals.astype(BF16)
    lo = (vals - hi.astype(F32)).astype(BF16)
    return _dot(jnp.concatenate([hi, lo], axis=1), e_ref[...])


def _ffn_body(x_ref, g_ref, wg_ref, wu_ref, wd_ref, o_ref):
    x = x_ref[...]
    h = _rms(x, g_ref[...]).astype(BF16)
    gate = _dot(h, wg_ref[...])
    up = _dot(h, wu_ref[...])
    act = (_silu(gate) * up).astype(BF16)
    o_ref[...] = x + 0.5 * _dot(act, wd_ref[...])


def _ffn(x, g, wg, wu, wd, tm):
    n = x.shape[0]
    row = pl.BlockSpec((tm, D_MODEL), lambda i: (i, 0))
    return pl.pallas_call(
        _ffn_body,
        grid=(n // tm,),
        in_specs=[row, _const_spec((1, D_MODEL)), _const_spec((D_MODEL, D_FF)),
                  _const_spec((D_MODEL, D_FF)), _const_spec((D_FF, D_MODEL))],
        out_specs=row,
        out_shape=jax.ShapeDtypeStruct((n, D_MODEL), F32),
        compiler_params=_params("parallel"),
        name="ffn",
    )(x, g, wg, wu, wd)


def _inproj_body(x_ref, g_ref, wq_ref, wk_ref, wv_ref, wz_ref, wx_ref, wdt_ref, qg_ref, kg_ref,
                 bd_ref, q_ref, k_ref, v_ref, z_ref, xbc_ref, dt_ref):
    h = _rms(x_ref[...], g_ref[...]).astype(BF16)

    def head_norm(t, gain):
        t2 = (t * t).astype(BF16)
        w = bd_ref.shape[0]
        ms = jnp.concatenate([_dot(t2[:, j * w:(j + 1) * w], bd_ref[...])
                              for j in range(D_ATT // w)], axis=1)
        return t * lax.rsqrt(ms + EPS) * gain

    q_ref[...] = head_norm(_dot(h, wq_ref[...]), qg_ref[...])
    k_ref[...] = head_norm(_dot(h, wk_ref[...]), kg_ref[...])
    v_ref[...] = _dot(h, wv_ref[...])
    z_ref[...] = _dot(h, wz_ref[...])
    xbc_ref[...] = _dot(h, wx_ref[...])
    dt_ref[...] = _dot(h, wdt_ref[...])


def _inproj(x, g, wq, wk, wv, wz, wx, wdt, qg, kg, bd, tm):
    n = x.shape[0]
    row = lambda w: pl.BlockSpec((tm, w), lambda i: (i, 0))
    sq = _const_spec((D_MODEL, D_ATT))
    out = lambda w: jax.ShapeDtypeStruct((n, w), F32)
    return pl.pallas_call(
        _inproj_body,
        grid=(n // tm,),
        in_specs=[row(D_MODEL), _const_spec((1, D_MODEL)), sq, sq, sq, sq,
                  _const_spec((D_MODEL, CONV_DIM)), _const_spec((D_MODEL, LANES)),
                  _const_spec((1, D_ATT)), _const_spec((1, D_ATT)), _const_spec(bd.shape)],
        out_specs=[row(D_ATT), row(D_ATT), row(D_ATT), row(D_SSM), row(CONV_DIM), row(LANES)],
        out_shape=[out(D_ATT), out(D_ATT), out(D_ATT), out(D_SSM), out(CONV_DIM), out(LANES)],
        compiler_params=_params("parallel"),
        name="inproj",
    )(x, g, wq, wk, wv, wz, wx, wdt, qg, kg, bd)


def _rows(start, size, dil):
    return pl.ds(start, size) if dil == 1 else pl.ds(start, size, stride=dil)


def _attn_body(q_ref, kc_ref, kp_ref, kn_ref, vc_ref, vp_ref, vn_ref, bias_ref, out_ref,
               kbuf, vbuf, o_scr, lse_scr, *, seq_len):
    t0 = pl.program_id(1) * ATT_TILE
    for buf, prev, cur, nxt in ((kbuf, kp_ref, kc_ref, kn_ref), (vbuf, vp_ref, vc_ref, vn_ref)):
        buf[0:ATT_REACH, :] = prev[...]
        buf[ATT_REACH:ATT_REACH + ATT_TILE, :] = cur[...]
        buf[ATT_REACH + ATT_TILE:, :] = nxt[...]

    lane = lax.broadcasted_iota(jnp.int32, (ATT_BLOCK, LANES), 1)
    first_head = lane < HEAD_DIM
    col = lax.broadcasted_iota(jnp.int32, (1, ATT_KW), 1)

    for p, (_, dil) in enumerate(ATT_PATTERNS):
        blocks_per_residue = ATT_TILE // (dil * ATT_BLOCK)

        def block(idx, carry, p=p, dil=dil, blocks_per_residue=blocks_per_residue):
            r = idx // blocks_per_residue
            j = idx % blocks_per_residue
            q0 = j * (ATT_BLOCK * dil) + r
            k0 = q0 + ATT_REACH - ATT_HALF * dil
            key_tok = t0 + q0 + (col - ATT_HALF) * dil
            valid = (key_tok >= 0) & (key_tok < seq_len)
            qp = q_ref[_rows(q0, ATT_BLOCK, dil), :].astype(BF16)
            kp = kbuf[_rows(k0, ATT_KW, dil), :].astype(BF16)
            vp = vbuf[_rows(k0, ATT_KW, dil), :].astype(BF16)
            outs, lses = [], []
            for sub in range(2):
                qh = jnp.where(first_head if sub == 0 else ~first_head, qp, jnp.zeros_like(qp))
                s = lax.dot_general(qh, kp, (((1,), (1,)), ((), ())), preferred_element_type=F32)
                s = jnp.where(valid, s + bias_ref[p, sub], NEG_INF)
                m = jnp.max(s, axis=-1, keepdims=True)
                e = jnp.exp(s - m)
                l = jnp.sum(e, axis=-1, keepdims=True)
                outs.append(_dot(e.astype(BF16), vp) * (1.0 / l))
                lses.append(m + jnp.log(l))
            o_scr[p, _rows(q0, ATT_BLOCK, dil), :] = jnp.where(first_head, outs[0], outs[1])
            lse_scr[p, _rows(q0, ATT_BLOCK, dil), :] = jnp.where(first_head, lses[0], lses[1])
            return carry

        lax.fori_loop(0, ATT_TILE // ATT_BLOCK, block, 0, unroll=ATT_UNROLL)

    def merge(i, carry):
        rows = pl.ds(pl.multiple_of(i * ATT_BLOCK, ATT_BLOCK), ATT_BLOCK)
        l1, l2, l3 = lse_scr[0, rows, :], lse_scr[1, rows, :], lse_scr[2, rows, :]
        m = jnp.maximum(jnp.maximum(l1, l2), l3)
        e1, e2, e3 = jnp.exp(l1 - m), jnp.exp(l2 - m), jnp.exp(l3 - m)
        mixed = e1 * o_scr[0, rows, :] + e2 * o_scr[1, rows, :] + e3 * o_scr[2, rows, :]
        out_ref[rows, :] = (mixed * (1.0 / (e1 + e2 + e3))).astype(BF16)
        return carry

    lax.fori_loop(0, ATT_TILE // ATT_BLOCK, merge, 0)


def _attention(q, k, v, bias):
    b, s, _ = q.shape
    last_halo = s // ATT_REACH - 1
    halo_per_tile = ATT_TILE // ATT_REACH
    cur = pl.BlockSpec((None, ATT_TILE, LANES), lambda bi, t, hp: (bi, t, hp))
    prev = pl.BlockSpec((None, ATT_REACH, LANES),
                        lambda bi, t, hp: (bi, jnp.maximum(t * halo_per_tile - 1, 0), hp))
    nxt = pl.BlockSpec((None, ATT_REACH, LANES),
                       lambda bi, t, hp: (bi, jnp.minimum((t + 1) * halo_per_tile, last_halo), hp))
    bias_spec = pl.BlockSpec((len(ATT_PATTERNS), 2, ATT_BLOCK, ATT_KW), lambda bi, t, hp: (0, hp, 0, 0))
    window = ATT_TILE + 2 * ATT_REACH
    return pl.pallas_call(
        functools.partial(_attn_body, seq_len=s),
        grid=(b, s // ATT_TILE, N_HEADS // 2),
        in_specs=[cur, cur, prev, nxt, cur, prev, nxt, bias_spec],
        out_specs=cur,
        out_shape=jax.ShapeDtypeStruct((b, s, D_ATT), BF16),
        scratch_shapes=[pltpu.VMEM((window, LANES), F32), pltpu.VMEM((window, LANES), F32),
                        pltpu.VMEM((len(ATT_PATTERNS), ATT_TILE, LANES), F32),
                        pltpu.VMEM((len(ATT_PATTERNS), ATT_TILE, LANES), F32)],
        compiler_params=_params("parallel", "parallel", "parallel"),
        name="attn",
    )(q, k, k, k, v, v, v, bias)


def _conv_silu(xc_ref, xp_ref, xn_ref, cw_ref, cb_ref, ext, c, nc):
    ext[0:HALO_ROWS, :] = jnp.where(c > 0, xp_ref[...], 0.0)
    ext[HALO_ROWS:HALO_ROWS + CHUNK, :] = xc_ref[...]
    ext[HALO_ROWS + CHUNK:, :] = jnp.where(c < nc - 1, xn_ref[...], 0.0)
    pad = D_CONV // 2
    acc = cb_ref[...] + ext[HALO_ROWS - pad:HALO_ROWS - pad + CHUNK, :] * cw_ref[0:1, :]
    for j in range(1, D_CONV):
        acc = acc + ext[HALO_ROWS - pad + j:HALO_ROWS - pad + j + CHUNK, :] * cw_ref[j:j + 1, :]
    return _silu(acc)


def _softplus(x):
    return jnp.maximum(x, 0.0) + jnp.log(1.0 + jnp.exp(-jnp.abs(x)))


def _tri(lower):
    r = lax.broadcasted_iota(jnp.int32, (CHUNK, CHUNK), 0)
    c = lax.broadcasted_iota(jnp.int32, (CHUNK, CHUNK), 1)
    return (r >= c) if lower else (r <= c)


def _cumsum_rows(a, lower):
    return jnp.dot(_tri(lower).astype(F32), a, precision=lax.Precision.HIGHEST,
                   preferred_element_type=F32)


def _ssd_bwd_body(xc_ref, xp_ref, xn_ref, dt_ref, cw_ref, cb_ref, dtb_ref, a_ref, e_ref,
                  yoff_ref, ext, hst, *, nc):
    i = pl.program_id(1)
    c = nc - 1 - i

    @pl.when(i == 0)
    def _():
        hst[...] = jnp.zeros_like(hst)

    xa = _conv_silu(xc_ref, xp_ref, xn_ref, cw_ref, cb_ref, ext, c, nc)
    dt = _softplus(dt_ref[...] + dtb_ref[...])
    rcum = _cumsum_rows(dt * a_ref[...], lower=False)
    decay_in = _expand(jnp.exp(rcum), e_ref)
    w_in = _expand(jnp.exp(rcum[0:1, :] - rcum) * dt, e_ref)
    chunk_decay = _expand(jnp.broadcast_to(jnp.exp(rcum[0:1, :]), (HALO_ROWS, LANES)), e_ref)[0:1, :]

    xs = xa[:, :D_SSM]
    xw = (xs * w_in).astype(BF16)
    for g in range(N_GROUPS):
        gl = slice(g * HEADS_PER_GROUP * HEAD_DIM, (g + 1) * HEADS_PER_GROUP * HEAD_DIM)
        bg = xa[:, D_SSM + g * D_STATE:D_SSM + (g + 1) * D_STATE].astype(BF16)
        cg = xa[:, D_SSM + (N_GROUPS + g) * D_STATE:D_SSM + (N_GROUPS + g + 1) * D_STATE].astype(BF16)
        h_in = hst[:, gl]
        yoff_ref[:, gl] = _dot(cg, h_in.astype(BF16)) * decay_in[:, gl]
        upd = lax.dot_general(bg, xw[:, gl], (((0,), (0,)), ((), ())), preferred_element_type=F32)
        hst[:, gl] = h_in * chunk_decay[:, gl] + upd


def _ssd_fwd_body(xc_ref, xp_ref, xn_ref, dt_ref, z_ref, yb_ref, cw_ref, cb_ref, dtb_ref, a_ref,
                  dsk_ref, og_ref, e_ref, y_ref, ext, hst, *, nc):
    c = pl.program_id(1)

    @pl.when(c == 0)
    def _():
        hst[...] = jnp.zeros_like(hst)

    xa = _conv_silu(xc_ref, xp_ref, xn_ref, cw_ref, cb_ref, ext, c, nc)
    dt = _softplus(dt_ref[...] + dtb_ref[...])
    a = dt * a_ref[...]
    fcum = _cumsum_rows(a, lower=True)
    rcum = _cumsum_rows(a, lower=False)
    dt_t, fcum_t, rcum_t = dt.T, fcum.T, rcum.T
    decay_in = _expand(jnp.exp(fcum), e_ref)
    w_in = _expand(jnp.exp(fcum[CHUNK - 1:CHUNK, :] - fcum) * dt, e_ref)
    chunk_decay = _expand(jnp.broadcast_to(jnp.exp(fcum[CHUNK - 1:CHUNK, :]), (HALO_ROWS, LANES)),
                          e_ref)[0:1, :]
    lower, upper = _tri(True), _tri(False)
    lane = lax.broadcasted_iota(jnp.int32, (CHUNK, LANES), 1)

    xs = xa[:, :D_SSM]
    xsb = xs.astype(BF16)
    xw = (xs * w_in).astype(BF16)
    for g in range(N_GROUPS):
        gl = slice(g * HEADS_PER_GROUP * HEAD_DIM, (g + 1) * HEADS_PER_GROUP * HEAD_DIM)
        bg = xa[:, D_SSM + g * D_STATE:D_SSM + (g + 1) * D_STATE].astype(BF16)
        cg = xa[:, D_SSM + (N_GROUPS + g) * D_STATE:D_SSM + (N_GROUPS + g + 1) * D_STATE].astype(BF16)
        cb = lax.dot_general(cg, bg, (((1,), (1,)), ((), ())), preferred_element_type=F32)
        h_in = hst[:, gl]
        y_off = _dot(cg, h_in.astype(BF16)) * decay_in[:, gl]
        upd = lax.dot_general(bg, xw[:, gl], (((0,), (0,)), ((), ())), preferred_element_type=F32)
        hst[:, gl] = h_in * chunk_decay[:, gl] + upd
        gated = []
        for pair in range(HEADS_PER_GROUP // 2):
            pl_ = slice(g * 256 + pair * LANES, g * 256 + (pair + 1) * LANES)
            x_pair = xsb[:, pl_]
            halves = []
            for sub in range(2):
                h = g * HEADS_PER_GROUP + 2 * pair + sub
                hb = N_HEADS + h
                seg_f = jnp.where(lower, fcum[:, h:h + 1] - fcum_t[h:h + 1, :], NEG_INF)
                seg_b = jnp.where(upper, rcum[:, hb:hb + 1] - rcum_t[hb:hb + 1, :], NEG_INF)
                mix = jnp.exp(seg_f) * dt_t[h:h + 1, :] + jnp.exp(seg_b) * dt_t[hb:hb + 1, :]
                halves.append(_dot((cb * mix).astype(BF16), x_pair))
            y_diag = jnp.where(lane < HEAD_DIM, halves[0], halves[1])
            po = slice(pair * LANES, (pair + 1) * LANES)
            y = (y_diag + y_off[:, po] + yb_ref[:, pl_] + dsk_ref[:, pl_] * xs[:, pl_])
            gated.append(y * _silu(z_ref[:, pl_]))
        y_ref[:, gl] = _rms(jnp.concatenate(gated, axis=1), og_ref[:, gl]).astype(BF16)


def _ssd(xbc, dt, z, conv_w, conv_b, dt_bias, a_neg, d_skip, out_g, e_fwd, e_bwd):
    b, s, _ = xbc.shape
    nc = s // CHUNK
    hpc = CHUNK // HALO_ROWS
    last_halo = s // HALO_ROWS - 1

    def specs(cidx):
        chunk = lambda w: pl.BlockSpec((None, CHUNK, w), lambda bi, i: (bi, cidx(i), 0))
        prev = pl.BlockSpec((None, HALO_ROWS, CONV_DIM),
                            lambda bi, i: (bi, jnp.maximum(cidx(i) * hpc - 1, 0), 0))
        nxt = pl.BlockSpec((None, HALO_ROWS, CONV_DIM),
                           lambda bi, i: (bi, jnp.minimum((cidx(i) + 1) * hpc, last_halo), 0))
        return chunk, prev, nxt

    scratch = [pltpu.VMEM((CHUNK + 2 * HALO_ROWS, CONV_DIM), F32), pltpu.VMEM((D_STATE, D_SSM), F32)]
    small = [_const_spec(conv_w.shape), _const_spec(conv_b.shape), _const_spec(dt_bias.shape),
             _const_spec(a_neg.shape)]

    chunk, prev, nxt = specs(lambda i: nc - 1 - i)
    yb_off = pl.pallas_call(
        functools.partial(_ssd_bwd_body, nc=nc),
        grid=(b, nc),
        in_specs=[chunk(CONV_DIM), prev, nxt, chunk(LANES)] + small + [_const_spec(e_bwd.shape)],
        out_specs=chunk(D_SSM),
        out_shape=jax.ShapeDtypeStruct((b, s, D_SSM), F32),
        scratch_shapes=scratch,
        compiler_params=_params("parallel", "arbitrary"),
        name="ssd_bwd",
    )(xbc, xbc, xbc, dt, conv_w, conv_b, dt_bias, a_neg, e_bwd)

    chunk, prev, nxt = specs(lambda i: i)
    return pl.pallas_call(
        functools.partial(_ssd_fwd_body, nc=nc),
        grid=(b, nc),
        in_specs=[chunk(CONV_DIM), prev, nxt, chunk(LANES), chunk(D_SSM), chunk(D_SSM)] + small
                 + [_const_spec(d_skip.shape), _const_spec(out_g.shape), _const_spec(e_fwd.shape)],
        out_specs=chunk(D_SSM),
        out_shape=jax.ShapeDtypeStruct((b, s, D_SSM), BF16),
        scratch_shapes=scratch,
        compiler_params=_params("parallel", "arbitrary"),
        name="ssd_fwd",
    )(xbc, xbc, xbc, dt, z, yb_off, conv_w, conv_b, dt_bias, a_neg, d_skip, out_g, e_fwd)


def _outproj_body(x_ref, a_ref, y_ref, g_ref, wa_ref, ws_ref, out_ref):
    attn = _rms(a_ref[...].astype(F32), g_ref[...]).astype(BF16)
    out_ref[...] = x_ref[...] + _dot(attn, wa_ref[...]) + _dot(y_ref[...], ws_ref[...])


def _outproj(x, attn, y, g, wa, ws, tm):
    n = x.shape[0]
    row = lambda w: pl.BlockSpec((tm, w), lambda i: (i, 0))
    sq = _const_spec((D_ATT, D_MODEL))
    return pl.pallas_call(
        _outproj_body,
        grid=(n // tm,),
        in_specs=[row(D_MODEL), row(D_ATT), row(D_SSM), _const_spec((1, D_ATT)), sq, sq],
        out_specs=row(D_MODEL),
        out_shape=jax.ShapeDtypeStruct((n, D_MODEL), F32),
        compiler_params=_params("parallel"),
        name="outproj",
    )(x, attn, y, g, wa, ws)


def _t5_bucket(rel):
    nb = N_REL_BUCKETS // 2
    max_exact = nb // 2
    n = np.abs(rel)
    large = max_exact + (np.log(np.maximum(n, 1) / max_exact)
                         / math.log(REL_MAX_DIST / max_exact) * (nb - max_exact)).astype(np.int32)
    large = np.minimum(large, nb - 1)
    return (np.where(rel > 0, nb, 0) + np.where(n < max_exact, n, large)).astype(np.int32)


def _bias_tables(rel_bias):
    rel_sub = np.arange(ATT_KW)[None, :] - ATT_HALF - np.arange(ATT_BLOCK)[:, None]
    in_window = np.abs(rel_sub) <= ATT_HALF
    tables = []
    for _, dil in ATT_PATTERNS:
        onehot = (_t5_bucket(rel_sub * dil)[:, :, None] == np.arange(N_REL_BUCKETS)).astype(np.float32)
        bias = jnp.einsum("ijk,kh->hij", jnp.asarray(onehot, BF16).astype(F32), rel_bias.astype(F32),
                          precision=lax.Precision.HIGHEST)
        tables.append(jnp.where(in_window[None], bias, NEG_INF))
    return jnp.stack(tables, axis=0)


def _head_expander(first_row, width):
    e = np.zeros((2 * LANES, width), np.float32)
    for h in range(width // HEAD_DIM):
        e[first_row + h, h * HEAD_DIM:(h + 1) * HEAD_DIM] = 1.0
        e[LANES + first_row + h, h * HEAD_DIM:(h + 1) * HEAD_DIM] = 1.0
    return jnp.asarray(e, BF16)


def _block_diag_mean(width):
    i = np.arange(width)
    return jnp.asarray((i[:, None] // HEAD_DIM == i[None, :] // HEAD_DIM) / HEAD_DIM, BF16)


def _layer(x, p, tm):
    b, s, _ = x.shape
    n = b * s
    xf = x.reshape(n, D_MODEL)
    x1 = _ffn(xf, p["ffn1_g"], p["ffn1_wg"], p["ffn1_wu"], p["ffn1_wd"], tm)
    q, k, v, z, xbc, dt = _inproj(x1, p["mix_g"], p["wq"], p["wk"], p["wv"], p["wz"], p["wx"],
                                  p["wdt"], p["qg"], p["kg"], p["bd"], tm)
    seq = lambda a: a.reshape(b, s, a.shape[-1])
    attn = _attention(seq(q), seq(k), seq(v), p["bias"])
    y = _ssd(seq(xbc), seq(dt), seq(z), p["conv_w"], p["conv_b"], p["dt_bias"], p["a_neg"],
             p["d_skip"], p["ssm_g"], p["e_fwd"], p["e_bwd"])
    x2 = _outproj(x1, attn.reshape(n, D_ATT), y.reshape(n, D_SSM), p["attn_g"], p["wo_att"],
                  p["wo_ssm"], tm)
    out = _ffn(x2, p["ffn2_g"], p["ffn2_wg"], p["ffn2_wu"], p["ffn2_wd"], tm)
    return out.reshape(b, s, D_MODEL)


def _prepare(rel_bias, ffn1_norm_g, ffn1_w_gate, ffn1_w_up, ffn1_w_down, mix_norm_g, w_in,
             q_norm_g, k_norm_g, attn_out_g, conv_w, conv_b, dt_bias, a_log, d_skip, ssm_out_g,
             w_out, ffn2_norm_g, ffn2_w_gate, ffn2_w_up, ffn2_w_down):
    row = lambda a: a.reshape(1, -1).astype(F32)
    w16 = lambda a: a.astype(BF16)
    c0, c1, c2, c3 = D_ATT, 2 * D_ATT, 3 * D_ATT, 3 * D_ATT + D_SSM
    c4 = c3 + CONV_DIM
    pad32 = lambda a: jnp.pad(a.reshape(1, 2 * N_HEADS).astype(F32), ((0, 0), (0, LANES - 2 * N_HEADS)))
    return {
        "ffn1_g": row(ffn1_norm_g), "ffn1_wg": w16(ffn1_w_gate), "ffn1_wu": w16(ffn1_w_up),
        "ffn1_wd": w16(ffn1_w_down),
        "ffn2_g": row(ffn2_norm_g), "ffn2_wg": w16(ffn2_w_gate), "ffn2_wu": w16(ffn2_w_up),
        "ffn2_wd": w16(ffn2_w_down),
        "mix_g": row(mix_norm_g),
        "wq": w16(w_in[:, :c0]), "wk": w16(w_in[:, c0:c1]), "wv": w16(w_in[:, c1:c2]),
        "wz": w16(w_in[:, c2:c3]), "wx": w16(w_in[:, c3:c4]),
        "wdt": w16(jnp.pad(w_in[:, c4:], ((0, 0), (0, LANES - 2 * N_HEADS)))),
        "qg": row(jnp.tile(q_norm_g, N_HEADS)) * HEAD_DIM ** -0.5,
        "kg": row(jnp.tile(k_norm_g, N_HEADS)),
        "bd": _block_diag_mean(256),
        "bias": _bias_tables(rel_bias),
        "attn_g": row(attn_out_g),
        "conv_w": conv_w.astype(F32), "conv_b": row(conv_b),
        "dt_bias": pad32(dt_bias), "a_neg": pad32(-jnp.exp(a_log.astype(F32))),
        "d_skip": row(jnp.repeat(d_skip, HEAD_DIM)), "ssm_g": row(ssm_out_g),
        "e_fwd": _head_expander(0, D_SSM), "e_bwd": _head_expander(N_HEADS, D_SSM),
        "wo_att": w16(w_out[:D_ATT]), "wo_ssm": w16(w_out[D_ATT:]),
    }


def _trunk(x, layers, tm=512):
    for p in layers:
        x = _layer(x, p, tm)
    return x


def kernel(x_prompt, x_sample, rel_bias, ffn1_norm_g, ffn1_w_gate, ffn1_w_up, ffn1_w_down, mix_norm_g, w_in, q_norm_g, k_norm_g, attn_out_g, conv_w, conv_b, dt_bias, a_log, d_skip, ssm_out_g, w_out, ffn2_norm_g, ffn2_w_gate, ffn2_w_up, ffn2_w_down):
    per_layer = (ffn1_norm_g, ffn1_w_gate, ffn1_w_up, ffn1_w_down, mix_norm_g, w_in, q_norm_g,
                 k_norm_g, attn_out_g, conv_w, conv_b, dt_bias, a_log, d_skip, ssm_out_g, w_out,
                 ffn2_norm_g, ffn2_w_gate, ffn2_w_up, ffn2_w_down)
    layers = [_prepare(rel_bias, *(a[l] for a in per_layer)) for l in range(ffn1_norm_g.shape[0])]
    return (_trunk(x_prompt, layers), _trunk(x_sample, layers))
```

```python
import functools
import math

import numpy as np
import jax
import jax.numpy as jnp
from jax import lax
from jax.experimental import pallas as pl
from jax.experimental.pallas import tpu as pltpu

D_MODEL = 1024
D_ATT = 1024
D_SSM = 1024
HEAD_DIM = 64
N_HEADS = 16
ATT_PATTERNS = ((128, 1), (512, 4), (2048, 16))
ATT_BLOCK = 128
ATT_HALF = 64
ATT_KW = ATT_BLOCK + 2 * ATT_HALF
ATT_MAX_DIL = max(d for _, d in ATT_PATTERNS)
ATT_TILE = ATT_BLOCK * ATT_MAX_DIL
ATT_REACH = ATT_HALF * ATT_MAX_DIL
ATT_UNROLL = 8
N_REL_BUCKETS = 32
REL_MAX_DIST = 1024
N_GROUPS = 4
HEADS_PER_GROUP = 4
D_STATE = 128
D_CONV = 5
CHUNK = 128
CONV_DIM = D_SSM + 2 * N_GROUPS * D_STATE
D_FF = 2816
EPS = 1e-6
NEG_INF = -1e30
LOG2_E = 1.4426950408889634

LANES = 128
HALO_ROWS = 8
VMEM_LIMIT = 56 * 1024 * 1024

F32 = jnp.float32
BF16 = jnp.bfloat16


def _params(*sem):
    return pltpu.CompilerParams(dimension_semantics=sem, vmem_limit_bytes=VMEM_LIMIT)


def _const_spec(shape):
    n = len(shape)
    return pl.BlockSpec(shape, lambda *_: (0,) * n, pipeline_mode=pl.Buffered(1))


def _rms(x, g):
    ms = jnp.mean(x * x, axis=-1, keepdims=True)
    return x * lax.rsqrt(ms + EPS) * g


def _silu(x):
    return x * (1.0 / (1.0 + jnp.exp(-x)))


def _dot(a, b):
    return jnp.dot(a, b, preferred_element_type=F32)


def _expand(vals, e_ref):
    hi = vals.astype(BF16)
    lo = (vals - hi.astype(F32)).astype(BF16)
    return _dot(jnp.concatenate([hi, lo], axis=1), e_ref[...])


def _ffn_body(x_ref, g_ref, wg_ref, wu_ref, wd_ref, o_ref):
    x = x_ref[...]
    h = _rms(x, g_ref[...]).astype(BF16)
    gate = _dot(h, wg_ref[...])
    up = _dot(h, wu_ref[...])
    act = (_silu(gate) * up).astype(BF16)
    o_ref[...] = x + 0.5 * _dot(act, wd_ref[...])


def _ffn(x, g, wg, wu, wd, tm):
    n = x.shape[0]
    row = pl.BlockSpec((tm, D_MODEL), lambda i: (i, 0))
    return pl.pallas_call(
        _ffn_body,
        grid=(n // tm,),
        in_specs=[row, _const_spec((1, D_MODEL)), _const_spec((D_MODEL, D_FF)),
                  _const_spec((D_MODEL, D_FF)), _const_spec((D_FF, D_MODEL))],
        out_specs=row,
        out_shape=jax.ShapeDtypeStruct((n, D_MODEL), F32),
        compiler_params=_params("parallel"),
        name="ffn",
    )(x, g, wg, wu, wd)


def _inproj_body(x_ref, g_ref, wq_ref, wk_ref, wv_ref, wz_ref, wx_ref, wdt_ref, qg_ref, kg_ref,
                 bd_ref, q_ref, k_ref, v_ref, z_ref, xbc_ref, dt_ref):
    h = _rms(x_ref[...], g_ref[...]).astype(BF16)

    def head_norm(t, gain):
        t2 = (t * t).astype(BF16)
        w = bd_ref.shape[0]
        ms = jnp.concatenate([_dot(t2[:, j * w:(j + 1) * w], bd_ref[...])
                              for j in range(D_ATT // w)], axis=1)
        return t * lax.rsqrt(ms + EPS) * gain

    q_ref[...] = head_norm(_dot(h, wq_ref[...]), qg_ref[...])
    k_ref[...] = head_norm(_dot(h, wk_ref[...]), kg_ref[...])
    v_ref[...] = _dot(h, wv_ref[...])
    z_ref[...] = _dot(h, wz_ref[...])
    xbc_ref[...] = _dot(h, wx_ref[...])
    dt_ref[...] = _dot(h, wdt_ref[...])


def _inproj(x, g, wq, wk, wv, wz, wx, wdt, qg, kg, bd, tm):
    n = x.shape[0]
    row = lambda w: pl.BlockSpec((tm, w), lambda i: (i, 0))
    sq = _const_spec((D_MODEL, D_ATT))
    out = lambda w: jax.ShapeDtypeStruct((n, w), F32)
    return pl.pallas_call(
        _inproj_body,
        grid=(n // tm,),
        in_specs=[row(D_MODEL), _const_spec((1, D_MODEL)), sq, sq, sq, sq,
                  _const_spec((D_MODEL, CONV_DIM)), _const_spec((D_MODEL, LANES)),
                  _const_spec((1, D_ATT)), _const_spec((1, D_ATT)), _const_spec(bd.shape)],
        out_specs=[row(D_ATT), row(D_ATT), row(D_ATT), row(D_SSM), row(CONV_DIM), row(LANES)],
        out_shape=[out(D_ATT), out(D_ATT), out(D_ATT), out(D_SSM), out(CONV_DIM), out(LANES)],
        compiler_params=_params("parallel"),
        name="inproj",
    )(x, g, wq, wk, wv, wz, wx, wdt, qg, kg, bd)


def _rows(start, size, dil):
    return pl.ds(start, size) if dil == 1 else pl.ds(start, size, stride=dil)


def _attn_body(q_ref, kc_ref, kp_ref, kn_ref, vc_ref, vp_ref, vn_ref, bias_ref, out_ref,
               kbuf, vbuf, o_scr, lse_scr, *, seq_len):
    t0 = pl.program_id(1) * ATT_TILE
    for buf, prev, cur, nxt in ((kbuf, kp_ref, kc_ref, kn_ref), (vbuf, vp_ref, vc_ref, vn_ref)):
        buf[0:ATT_REACH, :] = prev[...]
        buf[ATT_REACH:ATT_REACH + ATT_TILE, :] = cur[...]
        buf[ATT_REACH + ATT_TILE:, :] = nxt[...]

    lane = lax.broadcasted_iota(jnp.int32, (ATT_BLOCK, LANES), 1)
    first_head = lane < HEAD_DIM
    col = lax.broadcasted_iota(jnp.int32, (1, ATT_KW), 1)

    for p, (_, dil) in enumerate(ATT_PATTERNS):
        blocks_per_residue = ATT_TILE // (dil * ATT_BLOCK)

        def block(idx, carry, p=p, dil=dil, blocks_per_residue=blocks_per_residue):
            r = idx // blocks_per_residue
            j = idx % blocks_per_residue
            q0 = j * (ATT_BLOCK * dil) + r
            k0 = q0 + ATT_REACH - ATT_HALF * dil
            key_tok = t0 + q0 + (col - ATT_HALF) * dil
            valid = (key_tok >= 0) & (key_tok < seq_len)
            qp = q_ref[_rows(q0, ATT_BLOCK, dil), :].astype(BF16)
            kp = kbuf[_rows(k0, ATT_KW, dil), :].astype(BF16)
            vp = vbuf[_rows(k0, ATT_KW, dil), :].astype(BF16)
            outs, lses = [], []
            for sub in range(2):
                qh = jnp.where(first_head if sub == 0 else ~first_head, qp, jnp.zeros_like(qp))
                s = lax.dot_general(qh, kp, (((1,), (1,)), ((), ())), preferred_element_type=F32)
                s = jnp.where(valid, s + bias_ref[p, sub], NEG_INF)
                m = jnp.max(s, axis=-1, keepdims=True)
                e = jnp.exp(s - m)
                l = jnp.sum(e, axis=-1, keepdims=True)
                outs.append(_dot(e.astype(BF16), vp) * (1.0 / l))
                lses.append(m + jnp.log(l))
            o_scr[p, _rows(q0, ATT_BLOCK, dil), :] = jnp.where(first_head, outs[0], outs[1])
            lse_scr[p, _rows(q0, ATT_BLOCK, dil), :] = jnp.where(first_head, lses[0], lses[1])
            return carry

        lax.fori_loop(0, ATT_TILE // ATT_BLOCK, block, 0, unroll=ATT_UNROLL)

    def merge(i, carry):
        rows = pl.ds(pl.multiple_of(i * ATT_BLOCK, ATT_BLOCK), ATT_BLOCK)
        l1, l2, l3 = lse_scr[0, rows, :], lse_scr[1, rows, :], lse_scr[2, rows, :]
        m = jnp.maximum(jnp.maximum(l1, l2), l3)
        e1, e2, e3 = jnp.exp(l1 - m), jnp.exp(l2 - m), jnp.exp(l3 - m)
        mixed = e1 * o_scr[0, rows, :] + e2 * o_scr[1, rows, :] + e3 * o_scr[2, rows, :]
        out_ref[rows, :] = (mixed * (1.0 / (e1 + e2 + e3))).astype(BF16)
        return carry

    lax.fori_loop(0, ATT_TILE // ATT_BLOCK, merge, 0)


def _attention(q, k, v, bias):
    b, s, _ = q.shape
    last_halo = s // ATT_REACH - 1
    halo_per_tile = ATT_TILE // ATT_REACH
    cur = pl.BlockSpec((None, ATT_TILE, LANES), lambda bi, t, hp: (bi, t, hp))
    prev = pl.BlockSpec((None, ATT_REACH, LANES),
                        lambda bi, t, hp: (bi, jnp.maximum(t * halo_per_tile - 1, 0), hp))
    nxt = pl.BlockSpec((None, ATT_REACH, LANES),
                       lambda bi, t, hp: (bi, jnp.minimum((t + 1) * halo_per_tile, last_halo), hp))
    bias_spec = pl.BlockSpec((len(ATT_PATTERNS), 2, ATT_BLOCK, ATT_KW), lambda bi, t, hp: (0, hp, 0, 0))
    window = ATT_TILE + 2 * ATT_REACH
    return pl.pallas_call(
        functools.partial(_attn_body, seq_len=s),
        grid=(b, s // ATT_TILE, N_HEADS // 2),
        in_specs=[cur, cur, prev, nxt, cur, prev, nxt, bias_spec],
        out_specs=cur,
        out_shape=jax.ShapeDtypeStruct((b, s, D_ATT), BF16),
        scratch_shapes=[pltpu.VMEM((window, LANES), F32), pltpu.VMEM((window, LANES), F32),
                        pltpu.VMEM((len(ATT_PATTERNS), ATT_TILE, LANES), F32),
                        pltpu.VMEM((len(ATT_PATTERNS), ATT_TILE, LANES), F32)],
        compiler_params=_params("parallel", "parallel", "parallel"),
        name="attn",
    )(q, k, k, k, v, v, v, bias)


def _conv_silu(xc_ref, xp_ref, xn_ref, cw_ref, cb_ref, ext, xa_ref, c, nc):
    pad = D_CONV // 2
    for sl in range(CONV_DIM // LANES):
        lanes = slice(sl * LANES, (sl + 1) * LANES)
        ext[sl, 0:HALO_ROWS, :] = jnp.where(c > 0, xp_ref[:, lanes], 0.0)
        ext[sl, HALO_ROWS:HALO_ROWS + CHUNK, :] = xc_ref[:, lanes]
        ext[sl, HALO_ROWS + CHUNK:, :] = jnp.where(c < nc - 1, xn_ref[:, lanes], 0.0)
        acc = cb_ref[:, lanes]
        for j in range(D_CONV):
            tap = ext[sl, pl.ds(HALO_ROWS - pad + j, CHUNK, stride=1), :]
            acc = acc + tap * cw_ref[j:j + 1, lanes]
        xa_ref[:, lanes] = _silu(acc)


def _softplus(x):
    return jnp.maximum(x, 0.0) + jnp.log(1.0 + jnp.exp(-jnp.abs(x)))


def _tri(lower):
    r = lax.broadcasted_iota(jnp.int32, (CHUNK, CHUNK), 0)
    c = lax.broadcasted_iota(jnp.int32, (CHUNK, CHUNK), 1)
    return (r >= c) if lower else (r <= c)


def _cumsum_rows(a, lower):
    return jnp.dot(_tri(lower).astype(F32), a, precision=lax.Precision.HIGHEST,
                   preferred_element_type=F32)


def _ssd_bwd_body(xc_ref, xp_ref, xn_ref, dt_ref, cw_ref, cb_ref, dtb_ref, a_ref, e_ref,
                  xa_ref, yoff_ref, ext, hst, *, nc):
    i = pl.program_id(1)
    c = nc - 1 - i

    @pl.when(i == 0)
    def _():
        hst[...] = jnp.zeros_like(hst)

    _conv_silu(xc_ref, xp_ref, xn_ref, cw_ref, cb_ref, ext, xa_ref, c, nc)
    dt = _softplus(dt_ref[...] + dtb_ref[...])
    rcum = _cumsum_rows(dt * a_ref[...], lower=False)
    decay_in = _expand(jnp.exp(rcum), e_ref)
    w_in = _expand(jnp.exp(rcum[0:1, :] - rcum) * dt, e_ref)
    chunk_decay = _expand(jnp.broadcast_to(jnp.exp(rcum[0:1, :]), (HALO_ROWS, LANES)), e_ref)[0:1, :]

    for g in range(N_GROUPS):
        gl = slice(g * HEADS_PER_GROUP * HEAD_DIM, (g + 1) * HEADS_PER_GROUP * HEAD_DIM)
        bg = xa_ref[:, D_SSM + g * D_STATE:D_SSM + (g + 1) * D_STATE].astype(BF16)
        cg = xa_ref[:, D_SSM + (N_GROUPS + g) * D_STATE:D_SSM + (N_GROUPS + g + 1) * D_STATE].astype(BF16)
        xw = (xa_ref[:, gl] * w_in[:, gl]).astype(BF16)
        h_in = hst[:, gl]
        yoff_ref[:, gl] = _dot(cg, h_in.astype(BF16)) * decay_in[:, gl]
        upd = lax.dot_general(bg, xw, (((0,), (0,)), ((), ())), preferred_element_type=F32)
        hst[:, gl] = h_in * chunk_decay[:, gl] + upd


def _ssd_fwd_body(xa_ref, dt_ref, z_ref, yb_ref, dtb_ref, a_ref, dsk_ref, og_ref, e_ref, y_ref, hst):
    c = pl.program_id(1)

    @pl.when(c == 0)
    def _():
        hst[...] = jnp.zeros_like(hst)

    dt = _softplus(dt_ref[...] + dtb_ref[...])
    a = dt * a_ref[...]
    fcum = _cumsum_rows(a, lower=True)
    rcum = _cumsum_rows(a, lower=False)
    decay_in = _expand(jnp.exp(fcum), e_ref)
    w_in = _expand(jnp.exp(fcum[CHUNK - 1:CHUNK, :] - fcum) * dt, e_ref)
    chunk_decay = _expand(jnp.broadcast_to(jnp.exp(fcum[CHUNK - 1:CHUNK, :]), (HALO_ROWS, LANES)),
                          e_ref)[0:1, :]
    log2_dt = jnp.log(dt) * LOG2_E
    fcol, rcol = fcum * LOG2_E, rcum * LOG2_E
    frow, rrow = (fcol - log2_dt).T, (rcol - log2_dt).T
    lower, upper = _tri(True), _tri(False)
    lane = lax.broadcasted_iota(jnp.int32, (CHUNK, LANES), 1)

    for g in range(N_GROUPS):
        gl = slice(g * HEADS_PER_GROUP * HEAD_DIM, (g + 1) * HEADS_PER_GROUP * HEAD_DIM)
        bg = xa_ref[:, D_SSM + g * D_STATE:D_SSM + (g + 1) * D_STATE].astype(BF16)
        cg = xa_ref[:, D_SSM + (N_GROUPS + g) * D_STATE:D_SSM + (N_GROUPS + g + 1) * D_STATE].astype(BF16)
        cb = lax.dot_general(cg, bg, (((1,), (1,)), ((), ())), preferred_element_type=F32)
        xw = (xa_ref[:, gl] * w_in[:, gl]).astype(BF16)
        h_in = hst[:, gl]
        y_off = _dot(cg, h_in.astype(BF16)) * decay_in[:, gl]
        upd = lax.dot_general(bg, xw, (((0,), (0,)), ((), ())), preferred_element_type=F32)
        hst[:, gl] = h_in * chunk_decay[:, gl] + upd
        gated = []
        for pair in range(HEADS_PER_GROUP // 2):
            pl_ = slice(g * 256 + pair * LANES, g * 256 + (pair + 1) * LANES)
            xs = xa_ref[:, pl_]
            x_pair = xs.astype(BF16)
            halves = []
            for sub in range(2):
                h = g * HEADS_PER_GROUP + 2 * pair + sub
                hb = N_HEADS + h
                seg_f = jnp.where(lower, fcol[:, h:h + 1] - frow[h:h + 1, :], NEG_INF)
                seg_b = jnp.where(upper, rcol[:, hb:hb + 1] - rrow[hb:hb + 1, :], NEG_INF)
                mix = jnp.exp2(seg_f) + jnp.exp2(seg_b)
                halves.append(_dot((cb * mix).astype(BF16), x_pair))
            y_diag = jnp.where(lane < HEAD_DIM, halves[0], halves[1])
            po = slice(pair * LANES, (pair + 1) * LANES)
            y = y_diag + y_off[:, po] + yb_ref[:, pl_] + dsk_ref[:, pl_] * xs
            gated.append(y * _silu(z_ref[:, pl_]))
        y_ref[:, gl] = _rms(jnp.concatenate(gated, axis=1), og_ref[:, gl]).astype(BF16)


def _ssd(xbc, dt, z, conv_w, conv_b, dt_bias, a_neg, d_skip, out_g, e_fwd, e_bwd):
    b, s, _ = xbc.shape
    nc = s // CHUNK
    hpc = CHUNK // HALO_ROWS
    last_halo = s // HALO_ROWS - 1

    def specs(cidx):
        chunk = lambda w: pl.BlockSpec((None, CHUNK, w), lambda bi, i: (bi, cidx(i), 0))
        prev = pl.BlockSpec((None, HALO_ROWS, CONV_DIM),
                            lambda bi, i: (bi, jnp.maximum(cidx(i) * hpc - 1, 0), 0))
        nxt = pl.BlockSpec((None, HALO_ROWS, CONV_DIM),
                           lambda bi, i: (bi, jnp.minimum((cidx(i) + 1) * hpc, last_halo), 0))
        return chunk, prev, nxt

    state = pltpu.VMEM((D_STATE, D_SSM), F32)
    conv_ext = pltpu.VMEM((CONV_DIM // LANES, CHUNK + 2 * HALO_ROWS, LANES), F32)
    small = [_const_spec(dt_bias.shape), _const_spec(a_neg.shape)]

    chunk, prev, nxt = specs(lambda i: nc - 1 - i)
    xa, yb_off = pl.pallas_call(
        functools.partial(_ssd_bwd_body, nc=nc),
        grid=(b, nc),
        in_specs=[chunk(CONV_DIM), prev, nxt, chunk(LANES), _const_spec(conv_w.shape),
                  _const_spec(conv_b.shape)] + small + [_const_spec(e_bwd.shape)],
        out_specs=[chunk(CONV_DIM), chunk(D_SSM)],
        out_shape=[jax.ShapeDtypeStruct((b, s, CONV_DIM), F32), jax.ShapeDtypeStruct((b, s, D_SSM), F32)],
        scratch_shapes=[conv_ext, state],
        compiler_params=_params("parallel", "arbitrary"),
        name="ssd_bwd",
    )(xbc, xbc, xbc, dt, conv_w, conv_b, dt_bias, a_neg, e_bwd)

    chunk, _, _ = specs(lambda i: i)
    return pl.pallas_call(
        _ssd_fwd_body,
        grid=(b, nc),
        in_specs=[chunk(CONV_DIM), chunk(LANES), chunk(D_SSM), chunk(D_SSM)] + small
                 + [_const_spec(d_skip.shape), _const_spec(out_g.shape), _const_spec(e_fwd.shape)],
        out_specs=chunk(D_SSM),
        out_shape=jax.ShapeDtypeStruct((b, s, D_SSM), BF16),
        scratch_shapes=[state],
        compiler_params=_params("parallel", "arbitrary"),
        name="ssd_fwd",
    )(xa, dt, z, yb_off, dt_bias, a_neg, d_skip, out_g, e_fwd)


def _outproj_body(x_ref, a_ref, y_ref, g_ref, wa_ref, ws_ref, out_ref):
    attn = _rms(a_ref[...].astype(F32), g_ref[...]).astype(BF16)
    out_ref[...] = x_ref[...] + _dot(attn, wa_ref[...]) + _dot(y_ref[...], ws_ref[...])


def _outproj(x, attn, y, g, wa, ws, tm):
    n = x.shape[0]
    row = lambda w: pl.BlockSpec((tm, w), lambda i: (i, 0))
    sq = _const_spec((D_ATT, D_MODEL))
    return pl.pallas_call(
        _outproj_body,
        grid=(n // tm,),
        in_specs=[row(D_MODEL), row(D_ATT), row(D_SSM), _const_spec((1, D_ATT)), sq, sq],
        out_specs=row(D_MODEL),
        out_shape=jax.ShapeDtypeStruct((n, D_MODEL), F32),
        compiler_params=_params("parallel"),
        name="outproj",
    )(x, attn, y, g, wa, ws)


def _t5_bucket(rel):
    nb = N_REL_BUCKETS // 2
    max_exact = nb // 2
    n = np.abs(rel)
    large = max_exact + (np.log(np.maximum(n, 1) / max_exact)
                         / math.log(REL_MAX_DIST / max_exact) * (nb - max_exact)).astype(np.int32)
    large = np.minimum(large, nb - 1)
    return (np.where(rel > 0, nb, 0) + np.where(n < max_exact, n, large)).astype(np.int32)


def _bias_tables(rel_bias):
    rel_sub = np.arange(ATT_KW)[None, :] - ATT_HALF - np.arange(ATT_BLOCK)[:, None]
    in_window = np.abs(rel_sub) <= ATT_HALF
    tables = []
    for _, dil in ATT_PATTERNS:
        onehot = (_t5_bucket(rel_sub * dil)[:, :, None] == np.arange(N_REL_BUCKETS)).astype(np.float32)
        bias = jnp.einsum("ijk,kh->hij", jnp.asarray(onehot, BF16).astype(F32), rel_bias.astype(F32),
                          precision=lax.Precision.HIGHEST)
        tables.append(jnp.where(in_window[None], bias, NEG_INF))
    return jnp.stack(tables, axis=0)


def _head_expander(first_row, width):
    e = np.zeros((2 * LANES, width), np.float32)
    for h in range(width // HEAD_DIM):
        e[first_row + h, h * HEAD_DIM:(h + 1) * HEAD_DIM] = 1.0
        e[LANES + first_row + h, h * HEAD_DIM:(h + 1) * HEAD_DIM] = 1.0
    return jnp.asarray(e, BF16)


def _block_diag_mean(width):
    i = np.arange(width)
    return jnp.asarray((i[:, None] // HEAD_DIM == i[None, :] // HEAD_DIM) / HEAD_DIM, BF16)


def _layer(x, p, tm):
    b, s, _ = x.shape
    n = b * s
    xf = x.reshape(n, D_MODEL)
    x1 = _ffn(xf, p["ffn1_g"], p["ffn1_wg"], p["ffn1_wu"], p["ffn1_wd"], tm)
    q, k, v, z, xbc, dt = _inproj(x1, p["mix_g"], p["wq"], p["wk"], p["wv"], p["wz"], p["wx"],
                                  p["wdt"], p["qg"], p["kg"], p["bd"], tm)
    seq = lambda a: a.reshape(b, s, a.shape[-1])
    attn = _attention(seq(q), seq(k), seq(v), p["bias"])
    y = _ssd(seq(xbc), seq(dt), seq(z), p["conv_w"], p["conv_b"], p["dt_bias"], p["a_neg"],
             p["d_skip"], p["ssm_g"], p["e_fwd"], p["e_bwd"])
    x2 = _outproj(x1, attn.reshape(n, D_ATT), y.reshape(n, D_SSM), p["attn_g"], p["wo_att"],
                  p["wo_ssm"], tm)
    out = _ffn(x2, p["ffn2_g"], p["ffn2_wg"], p["ffn2_wu"], p["ffn2_wd"], tm)
    return out.reshape(b, s, D_MODEL)


def _prepare(rel_bias, ffn1_norm_g, ffn1_w_gate, ffn1_w_up, ffn1_w_down, mix_norm_g, w_in,
             q_norm_g, k_norm_g, attn_out_g, conv_w, conv_b, dt_bias, a_log, d_skip, ssm_out_g,
             w_out, ffn2_norm_g, ffn2_w_gate, ffn2_w_up, ffn2_w_down):
    row = lambda a: a.reshape(1, -1).astype(F32)
    w16 = lambda a: a.astype(BF16)
    c0, c1, c2, c3 = D_ATT, 2 * D_ATT, 3 * D_ATT, 3 * D_ATT + D_SSM
    c4 = c3 + CONV_DIM
    pad32 = lambda a: jnp.pad(a.reshape(1, 2 * N_HEADS).astype(F32), ((0, 0), (0, LANES - 2 * N_HEADS)))
    return {
        "ffn1_g": row(ffn1_norm_g), "ffn1_wg": w16(ffn1_w_gate), "ffn1_wu": w16(ffn1_w_up),
        "ffn1_wd": w16(ffn1_w_down),
        "ffn2_g": row(ffn2_norm_g), "ffn2_wg": w16(ffn2_w_gate), "ffn2_wu": w16(ffn2_w_up),
        "ffn2_wd": w16(ffn2_w_down),
        "mix_g": row(mix_norm_g),
        "wq": w16(w_in[:, :c0]), "wk": w16(w_in[:, c0:c1]), "wv": w16(w_in[:, c1:c2]),
        "wz": w16(w_in[:, c2:c3]), "wx": w16(w_in[:, c3:c4]),
        "wdt": w16(jnp.pad(w_in[:, c4:], ((0, 0), (0, LANES - 2 * N_HEADS)))),
        "qg": row(jnp.tile(q_norm_g, N_HEADS)) * HEAD_DIM ** -0.5,
        "kg": row(jnp.tile(k_norm_g, N_HEADS)),
        "bd": _block_diag_mean(256),
        "bias": _bias_tables(rel_bias),
        "attn_g": row(attn_out_g),
        "conv_w": conv_w.astype(F32), "conv_b": row(conv_b),
        "dt_bias": pad32(dt_bias), "a_neg": pad32(-jnp.exp(a_log.astype(F32))),
        "d_skip": row(jnp.repeat(d_skip, HEAD_DIM)), "ssm_g": row(ssm_out_g),
        "e_fwd": _head_expander(0, D_SSM), "e_bwd": _head_expander(N_HEADS, D_SSM),
        "wo_att": w16(w_out[:D_ATT]), "wo_ssm": w16(w_out[D_ATT:]),
    }


def _trunk(x, layers, tm=512):
    for p in layers:
        x = _layer(x, p, tm)
    return x


def kernel(x_prompt, x_sample, rel_bias, ffn1_norm_g, ffn1_w_gate, ffn1_w_up, ffn1_w_down, mix_norm_g, w_in, q_norm_g, k_norm_g, attn_out_g, conv_w, conv_b, dt_bias, a_log, d_skip, ssm_out_g, w_out, ffn2_norm_g, ffn2_w_gate, ffn2_w_up, ffn2_w_down):
    per_layer = (ffn1_norm_g, ffn1_w_gate, ffn1_w_up, ffn1_w_down, mix_norm_g, w_in, q_norm_g,
                 k_norm_g, attn_out_g, conv_w, conv_b, dt_bias, a_log, d_skip, ssm_out_g, w_out,
                 ffn2_norm_g, ffn2_w_gate, ffn2_w_up, ffn2_w_down)
    layers = [_prepare(rel_bias, *(a[l] for a in per_layer)) for l in range(ffn1_norm_g.shape[0])]
    return (_trunk(x_prompt, layers), _trunk(x_sample, layers))
```

```python
import functools
import math

import numpy as np
import jax
import jax.numpy as jnp
from jax import lax
from jax.experimental import pallas as pl
from jax.experimental.pallas import tpu as pltpu

D_MODEL = 1024
D_ATT = 1024
D_SSM = 1024
HEAD_DIM = 64
N_HEADS = 16
ATT_PATTERNS = ((128, 1), (512, 4), (2048, 16))
ATT_BLOCK = 128
ATT_HALF = 64
ATT_KW = ATT_BLOCK + 2 * ATT_HALF
ATT_MAX_DIL = max(d for _, d in ATT_PATTERNS)
ATT_TILE = ATT_BLOCK * ATT_MAX_DIL
ATT_REACH = ATT_HALF * ATT_MAX_DIL
ATT_WINDOW = ATT_TILE + 2 * ATT_REACH
ATT_STAGE = 4
assert tuple(d for _, d in ATT_PATTERNS) == (1, ATT_STAGE, ATT_STAGE * ATT_STAGE)
ATT_UNROLL = 8
N_REL_BUCKETS = 32
REL_MAX_DIST = 1024
N_GROUPS = 4
HEADS_PER_GROUP = 4
D_STATE = 128
D_CONV = 5
CHUNK = 128
CONV_DIM = D_SSM + 2 * N_GROUPS * D_STATE
D_FF = 2816
EPS = 1e-6
NEG_INF = -1e30
LOG2_E = 1.4426950408889634

LANES = 128
HALO_ROWS = 8
VMEM_LIMIT = 56 * 1024 * 1024

F32 = jnp.float32
BF16 = jnp.bfloat16


def _params(*sem):
    return pltpu.CompilerParams(dimension_semantics=sem, vmem_limit_bytes=VMEM_LIMIT)


def _const_spec(shape):
    n = len(shape)
    return pl.BlockSpec(shape, lambda *_: (0,) * n, pipeline_mode=pl.Buffered(1))


def _rms(x, g):
    ms = jnp.mean(x * x, axis=-1, keepdims=True)
    return x * lax.rsqrt(ms + EPS) * g


def _silu(x):
    return x * (1.0 / (1.0 + jnp.exp(-x)))


def _dot(a, b):
    return jnp.dot(a, b, preferred_element_type=F32)


def _expand(vals, e_ref):
    hi = vals.astype(BF16)
    lo = (vals - hi.astype(F32)).astype(BF16)
    return _dot(jnp.concatenate([hi, lo], axis=1), e_ref[...])


def _ffn_body(x_ref, g_ref, wg_ref, wu_ref, wd_ref, o_ref):
    x = x_ref[...]
    h = _rms(x, g_ref[...]).astype(BF16)
    gate = _dot(h, wg_ref[...])
    up = _dot(h, wu_ref[...])
    act = (_silu(gate) * up).astype(BF16)
    o_ref[...] = x + 0.5 * _dot(act, wd_ref[...])


def _ffn(x, g, wg, wu, wd, tm):
    n = x.shape[0]
    row = pl.BlockSpec((tm, D_MODEL), lambda i: (i, 0))
    return pl.pallas_call(
        _ffn_body,
        grid=(n // tm,),
        in_specs=[row, _const_spec((1, D_MODEL)), _const_spec((D_MODEL, D_FF)),
                  _const_spec((D_MODEL, D_FF)), _const_spec((D_FF, D_MODEL))],
        out_specs=row,
        out_shape=jax.ShapeDtypeStruct((n, D_MODEL), F32),
        compiler_params=_params("parallel"),
        name="ffn",
    )(x, g, wg, wu, wd)


def _inproj_body(x_ref, g_ref, wq_ref, wk_ref, wv_ref, wz_ref, wx_ref, wdt_ref, qg_ref, kg_ref,
                 bd_ref, q_ref, k_ref, v_ref, z_ref, xbc_ref, dt_ref):
    h = _rms(x_ref[...], g_ref[...]).astype(BF16)

    def head_norm(t, gain):
        t2 = (t * t).astype(BF16)
        w = bd_ref.shape[0]
        ms = jnp.concatenate([_dot(t2[:, j * w:(j + 1) * w], bd_ref[...])
                              for j in range(D_ATT // w)], axis=1)
        return t * lax.rsqrt(ms + EPS) * gain

    q_ref[...] = head_norm(_dot(h, wq_ref[...]), qg_ref[...])
    k_ref[...] = head_norm(_dot(h, wk_ref[...]), kg_ref[...])
    v_ref[...] = _dot(h, wv_ref[...])
    z_ref[...] = _dot(h, wz_ref[...])
    xbc_ref[...] = _dot(h, wx_ref[...])
    dt_ref[...] = _dot(h, wdt_ref[...])


def _inproj(x, g, wq, wk, wv, wz, wx, wdt, qg, kg, bd, tm):
    n = x.shape[0]
    row = lambda w: pl.BlockSpec((tm, w), lambda i: (i, 0))
    sq = _const_spec((D_MODEL, D_ATT))
    out = lambda w: jax.ShapeDtypeStruct((n, w), F32)
    return pl.pallas_call(
        _inproj_body,
        grid=(n // tm,),
        in_specs=[row(D_MODEL), _const_spec((1, D_MODEL)), sq, sq, sq, sq,
                  _const_spec((D_MODEL, CONV_DIM)), _const_spec((D_MODEL, LANES)),
                  _const_spec((1, D_ATT)), _const_spec((1, D_ATT)), _const_spec(bd.shape)],
        out_specs=[row(D_ATT), row(D_ATT), row(D_ATT), row(D_SSM), row(CONV_DIM), row(LANES)],
        out_shape=[out(D_ATT), out(D_ATT), out(D_ATT), out(D_SSM), out(CONV_DIM), out(LANES)],
        compiler_params=_params("parallel"),
        name="inproj",
    )(x, g, wq, wk, wv, wz, wx, wdt, qg, kg, bd)


def _rows(start, size, dil):
    return pl.ds(start, size) if dil == 1 else pl.ds(start, size, stride=dil)


def _regroup_keys(prev, cur, nxt, tmp, dst):
    s4 = ATT_STAGE
    lo, hi = ATT_REACH - ATT_HALF, ATT_REACH + ATT_TILE + ATT_HALF
    dst[0, lo:ATT_REACH, :] = prev[lo:ATT_REACH, :].astype(BF16)
    dst[0, ATT_REACH:ATT_REACH + ATT_TILE, :] = cur[...].astype(BF16)
    dst[0, ATT_REACH + ATT_TILE:hi, :] = nxt[0:ATT_HALF, :].astype(BF16)
    h4, t4 = ATT_REACH // s4, ATT_TILE // s4
    for r in range(s4):
        tmp[r, 0:h4, :] = prev[pl.ds(r, h4, stride=s4), :]
        tmp[r, h4:h4 + t4, :] = cur[pl.ds(r, t4, stride=s4), :]
        tmp[r, h4 + t4:, :] = nxt[pl.ds(r, h4, stride=s4), :]
    lo, hi = h4 - ATT_HALF, h4 + t4 + ATT_HALF
    for r in range(s4):
        dst[1, r * (ATT_WINDOW // s4) + lo:r * (ATT_WINDOW // s4) + hi, :] = tmp[r, lo:hi, :].astype(BF16)
    per = ATT_WINDOW // (s4 * s4)
    for r in range(s4 * s4):
        dst[2, r * per:(r + 1) * per, :] = tmp[r % s4, pl.ds(r // s4, per, stride=s4), :].astype(BF16)


def _attn_body(q_ref, kc_ref, kp_ref, kn_ref, vc_ref, vp_ref, vn_ref, bias_ref, out_ref,
               kd, vd, tmp, o_scr, lse_scr, *, seq_len):
    t0 = pl.program_id(1) * ATT_TILE
    _regroup_keys(kp_ref, kc_ref, kn_ref, tmp, kd)
    _regroup_keys(vp_ref, vc_ref, vn_ref, tmp, vd)

    first_head = lax.broadcasted_iota(jnp.int32, (ATT_BLOCK, LANES), 1) < HEAD_DIM
    first_head_kw = lax.broadcasted_iota(jnp.int32, (ATT_KW, LANES), 1) < HEAD_DIM
    head_ones = (first_head_kw.astype(BF16), (~first_head_kw).astype(BF16))

    for p, (_, dil) in enumerate(ATT_PATTERNS):
        blocks_per_residue = ATT_TILE // (dil * ATT_BLOCK)

        def block(idx, carry, p=p, dil=dil, blocks_per_residue=blocks_per_residue):
            r = idx // blocks_per_residue
            j = idx % blocks_per_residue
            q0 = j * (ATT_BLOCK * dil) + r
            k0 = pl.multiple_of(r * (ATT_WINDOW // dil) + ATT_REACH // dil - ATT_HALF + j * ATT_BLOCK,
                                ATT_HALF)
            at_start = jnp.logical_and(t0 == 0, j == 0)
            at_end = jnp.logical_and(t0 + ATT_TILE == seq_len, j == blocks_per_residue - 1)
            variant = at_start.astype(jnp.int32) + 2 * at_end.astype(jnp.int32)
            qp = q_ref[_rows(q0, ATT_BLOCK, dil), :].astype(BF16)
            kp = kd[p, pl.ds(k0, ATT_KW), :]
            vp = vd[p, pl.ds(k0, ATT_KW), :]
            zq, zv = jnp.zeros_like(qp), jnp.zeros_like(vp)
            q2 = jnp.concatenate([jnp.where(first_head, qp, zq), jnp.where(first_head, zq, qp)], axis=0)
            s2 = lax.dot_general(q2, kp, (((1,), (1,)), ((), ())), preferred_element_type=F32)
            es, ms = [], []
            for sub in range(2):
                s = s2[sub * ATT_BLOCK:(sub + 1) * ATT_BLOCK, :] + bias_ref[variant, p, sub]
                m = jnp.max(s, axis=-1, keepdims=True)
                es.append(jnp.exp2(s - m).astype(BF16))
                ms.append(m)
            v2 = jnp.concatenate([
                jnp.concatenate([jnp.where(first_head_kw, vp, zv), head_ones[0]], axis=1),
                jnp.concatenate([jnp.where(first_head_kw, zv, vp), head_ones[1]], axis=1)], axis=0)
            ol = _dot(jnp.concatenate(es, axis=1), v2)
            l = ol[:, LANES:]
            o_scr[p, _rows(q0, ATT_BLOCK, dil), :] = ol[:, :LANES] * (1.0 / l)
            lse_scr[p, _rows(q0, ATT_BLOCK, dil), :] = jnp.where(first_head, ms[0], ms[1]) + jnp.log2(l)
            return carry

        lax.fori_loop(0, ATT_TILE // ATT_BLOCK, block, 0, unroll=ATT_UNROLL)

    def merge(i, carry):
        rows = pl.ds(pl.multiple_of(i * ATT_BLOCK, ATT_BLOCK), ATT_BLOCK)
        l1, l2, l3 = lse_scr[0, rows, :], lse_scr[1, rows, :], lse_scr[2, rows, :]
        m = jnp.maximum(jnp.maximum(l1, l2), l3)
        e1, e2, e3 = jnp.exp2(l1 - m), jnp.exp2(l2 - m), jnp.exp2(l3 - m)
        mixed = e1 * o_scr[0, rows, :] + e2 * o_scr[1, rows, :] + e3 * o_scr[2, rows, :]
        out_ref[rows, :] = (mixed * (1.0 / (e1 + e2 + e3))).astype(BF16)
        return carry

    lax.fori_loop(0, ATT_TILE // ATT_BLOCK, merge, 0)


def _attention(q, k, v, bias):
    b, s, _ = q.shape
    n_pat = len(ATT_PATTERNS)
    last_halo = s // ATT_REACH - 1
    halo_per_tile = ATT_TILE // ATT_REACH
    cur = pl.BlockSpec((None, ATT_TILE, LANES), lambda bi, t, hp: (bi, t, hp))
    prev = pl.BlockSpec((None, ATT_REACH, LANES),
                        lambda bi, t, hp: (bi, jnp.maximum(t * halo_per_tile - 1, 0), hp))
    nxt = pl.BlockSpec((None, ATT_REACH, LANES),
                       lambda bi, t, hp: (bi, jnp.minimum((t + 1) * halo_per_tile, last_halo), hp))
    bias_spec = pl.BlockSpec((bias.shape[0], n_pat, 2, ATT_BLOCK, ATT_KW),
                             lambda bi, t, hp: (0, 0, hp, 0, 0))
    return pl.pallas_call(
        functools.partial(_attn_body, seq_len=s),
        grid=(b, s // ATT_TILE, N_HEADS // 2),
        in_specs=[cur, cur, prev, nxt, cur, prev, nxt, bias_spec],
        out_specs=cur,
        out_shape=jax.ShapeDtypeStruct((b, s, D_ATT), BF16),
        scratch_shapes=[pltpu.VMEM((n_pat, ATT_WINDOW, LANES), BF16),
                        pltpu.VMEM((n_pat, ATT_WINDOW, LANES), BF16),
                        pltpu.VMEM((ATT_STAGE, ATT_WINDOW // ATT_STAGE, LANES), F32),
                        pltpu.VMEM((n_pat, ATT_TILE, LANES), F32),
                        pltpu.VMEM((n_pat, ATT_TILE, LANES), F32)],
        compiler_params=_params("parallel", "parallel", "parallel"),
        name="attn",
    )(q, k, k, k, v, v, v, bias)


def _conv_silu(xc_ref, xp_ref, xn_ref, cw_ref, cb_ref, ext, xa_ref, c, nc):
    pad = D_CONV // 2
    for sl in range(CONV_DIM // LANES):
        lanes = slice(sl * LANES, (sl + 1) * LANES)
        ext[sl, 0:HALO_ROWS, :] = jnp.where(c > 0, xp_ref[:, lanes], 0.0)
        ext[sl, HALO_ROWS:HALO_ROWS + CHUNK, :] = xc_ref[:, lanes]
        ext[sl, HALO_ROWS + CHUNK:, :] = jnp.where(c < nc - 1, xn_ref[:, lanes], 0.0)
        acc = cb_ref[:, lanes]
        for j in range(D_CONV):
            tap = ext[sl, pl.ds(HALO_ROWS - pad + j, CHUNK, stride=1), :]
            acc = acc + tap * cw_ref[j:j + 1, lanes]
        xa_ref[:, lanes] = _silu(acc)


def _softplus(x):
    return jnp.maximum(x, 0.0) + jnp.log(1.0 + jnp.exp(-jnp.abs(x)))


def _tri(lower):
    r = lax.broadcasted_iota(jnp.int32, (CHUNK, CHUNK), 0)
    c = lax.broadcasted_iota(jnp.int32, (CHUNK, CHUNK), 1)
    return (r >= c) if lower else (r <= c)


def _cumsum_rows(a, lower):
    return jnp.dot(_tri(lower).astype(F32), a, precision=lax.Precision.HIGHEST,
                   preferred_element_type=F32)


def _ssd_bwd_body(xc_ref, xp_ref, xn_ref, dt_ref, cw_ref, cb_ref, dtb_ref, a_ref, e_ref,
                  xa_ref, yoff_ref, ext, hst, *, nc):
    i = pl.program_id(1)
    c = nc - 1 - i

    @pl.when(i == 0)
    def _():
        hst[...] = jnp.zeros_like(hst)

    _conv_silu(xc_ref, xp_ref, xn_ref, cw_ref, cb_ref, ext, xa_ref, c, nc)
    dt = _softplus(dt_ref[...] + dtb_ref[...])
    rcum = _cumsum_rows(dt * a_ref[...], lower=False)
    decay_in = _expand(jnp.exp(rcum), e_ref)
    w_in = _expand(jnp.exp(rcum[0:1, :] - rcum) * dt, e_ref)
    chunk_decay = _expand(jnp.broadcast_to(jnp.exp(rcum[0:1, :]), (HALO_ROWS, LANES)), e_ref)[0:1, :]

    for g in range(N_GROUPS):
        gl = slice(g * HEADS_PER_GROUP * HEAD_DIM, (g + 1) * HEADS_PER_GROUP * HEAD_DIM)
        bg = xa_ref[:, D_SSM + g * D_STATE:D_SSM + (g + 1) * D_STATE].astype(BF16)
        cg = xa_ref[:, D_SSM + (N_GROUPS + g) * D_STATE:D_SSM + (N_GROUPS + g + 1) * D_STATE].astype(BF16)
        xw = (xa_ref[:, gl] * w_in[:, gl]).astype(BF16)
        h_in = hst[:, gl]
        yoff_ref[:, gl] = _dot(cg, h_in.astype(BF16)) * decay_in[:, gl]
        upd = lax.dot_general(bg, xw, (((0,), (0,)), ((), ())), preferred_element_type=F32)
        hst[:, gl] = h_in * chunk_decay[:, gl] + upd


def _ssd_fwd_body(xa_ref, dt_ref, z_ref, yb_ref, dtb_ref, a_ref, dsk_ref, og_ref, e_ref, y_ref, hst):
    c = pl.program_id(1)

    @pl.when(c == 0)
    def _():
        hst[...] = jnp.zeros_like(hst)

    dt = _softplus(dt_ref[...] + dtb_ref[...])
    a = dt * a_ref[...]
    fcum = _cumsum_rows(a, lower=True)
    rcum = _cumsum_rows(a, lower=False)
    decay_in = _expand(jnp.exp(fcum), e_ref)
    w_in = _expand(jnp.exp(fcum[CHUNK - 1:CHUNK, :] - fcum) * dt, e_ref)
    chunk_decay = _expand(jnp.broadcast_to(jnp.exp(fcum[CHUNK - 1:CHUNK, :]), (HALO_ROWS, LANES)),
                          e_ref)[0:1, :]
    log2_dt = jnp.log(dt) * LOG2_E
    fcol, rcol = fcum * LOG2_E, rcum * LOG2_E
    frow, rrow = (fcol - log2_dt).T, (rcol - log2_dt).T
    lower, upper = _tri(True), _tri(False)
    lane = lax.broadcasted_iota(jnp.int32, (CHUNK, LANES), 1)

    for g in range(N_GROUPS):
        gl = slice(g * HEADS_PER_GROUP * HEAD_DIM, (g + 1) * HEADS_PER_GROUP * HEAD_DIM)
        bg = xa_ref[:, D_SSM + g * D_STATE:D_SSM + (g + 1) * D_STATE].astype(BF16)
        cg = xa_ref[:, D_SSM + (N_GROUPS + g) * D_STATE:D_SSM + (N_GROUPS + g + 1) * D_STATE].astype(BF16)
        cb = lax.dot_general(cg, bg, (((1,), (1,)), ((), ())), preferred_element_type=F32)
        xw = (xa_ref[:, gl] * w_in[:, gl]).astype(BF16)
        h_in = hst[:, gl]
        y_off = _dot(cg, h_in.astype(BF16)) * decay_in[:, gl]
        upd = lax.dot_general(bg, xw, (((0,), (0,)), ((), ())), preferred_element_type=F32)
        hst[:, gl] = h_in * chunk_decay[:, gl] + upd
        gated = []
        for pair in range(HEADS_PER_GROUP // 2):
            pl_ = slice(g * 256 + pair * LANES, g * 256 + (pair + 1) * LANES)
            xs = xa_ref[:, pl_]
            x_pair = xs.astype(BF16)
            halves = []
            for sub in range(2):
                h = g * HEADS_PER_GROUP + 2 * pair + sub
                hb = N_HEADS + h
                seg_f = jnp.where(lower, fcol[:, h:h + 1] - frow[h:h + 1, :], NEG_INF)
                seg_b = jnp.where(upper, rcol[:, hb:hb + 1] - rrow[hb:hb + 1, :], NEG_INF)
                mix = jnp.exp2(seg_f) + jnp.exp2(seg_b)
                halves.append(_dot((cb * mix).astype(BF16), x_pair))
            y_diag = jnp.where(lane < HEAD_DIM, halves[0], halves[1])
            po = slice(pair * LANES, (pair + 1) * LANES)
            y = y_diag + y_off[:, po] + yb_ref[:, pl_] + dsk_ref[:, pl_] * xs
            gated.append(y * _silu(z_ref[:, pl_]))
        y_ref[:, gl] = _rms(jnp.concatenate(gated, axis=1), og_ref[:, gl]).astype(BF16)


def _ssd(xbc, dt, z, conv_w, conv_b, dt_bias, a_neg, d_skip, out_g, e_fwd, e_bwd):
    b, s, _ = xbc.shape
    nc = s // CHUNK
    hpc = CHUNK // HALO_ROWS
    last_halo = s // HALO_ROWS - 1

    def specs(cidx):
        chunk = lambda w: pl.BlockSpec((None, CHUNK, w), lambda bi, i: (bi, cidx(i), 0))
        prev = pl.BlockSpec((None, HALO_ROWS, CONV_DIM),
                            lambda bi, i: (bi, jnp.maximum(cidx(i) * hpc - 1, 0), 0))
        nxt = pl.BlockSpec((None, HALO_ROWS, CONV_DIM),
                           lambda bi, i: (bi, jnp.minimum((cidx(i) + 1) * hpc, last_halo), 0))
        return chunk, prev, nxt

    state = pltpu.VMEM((D_STATE, D_SSM), F32)
    conv_ext = pltpu.VMEM((CONV_DIM // LANES, CHUNK + 2 * HALO_ROWS, LANES), F32)
    small = [_const_spec(dt_bias.shape), _const_spec(a_neg.shape)]

    chunk, prev, nxt = specs(lambda i: nc - 1 - i)
    xa, yb_off = pl.pallas_call(
        functools.partial(_ssd_bwd_body, nc=nc),
        grid=(b, nc),
        in_specs=[chunk(CONV_DIM), prev, nxt, chunk(LANES), _const_spec(conv_w.shape),
                  _const_spec(conv_b.shape)] + small + [_const_spec(e_bwd.shape)],
        out_specs=[chunk(CONV_DIM), chunk(D_SSM)],
        out_shape=[jax.ShapeDtypeStruct((b, s, CONV_DIM), F32), jax.ShapeDtypeStruct((b, s, D_SSM), F32)],
        scratch_shapes=[conv_ext, state],
        compiler_params=_params("parallel", "arbitrary"),
        name="ssd_bwd",
    )(xbc, xbc, xbc, dt, conv_w, conv_b, dt_bias, a_neg, e_bwd)

    chunk, _, _ = specs(lambda i: i)
    return pl.pallas_call(
        _ssd_fwd_body,
        grid=(b, nc),
        in_specs=[chunk(CONV_DIM), chunk(LANES), chunk(D_SSM), chunk(D_SSM)] + small
                 + [_const_spec(d_skip.shape), _const_spec(out_g.shape), _const_spec(e_fwd.shape)],
        out_specs=chunk(D_SSM),
        out_shape=jax.ShapeDtypeStruct((b, s, D_SSM), BF16),
        scratch_shapes=[state],
        compiler_params=_params("parallel", "arbitrary"),
        name="ssd_fwd",
    )(xa, dt, z, yb_off, dt_bias, a_neg, d_skip, out_g, e_fwd)


def _outproj_body(x_ref, a_ref, y_ref, g_ref, wa_ref, ws_ref, out_ref):
    attn = _rms(a_ref[...].astype(F32), g_ref[...]).astype(BF16)
    out_ref[...] = x_ref[...] + _dot(attn, wa_ref[...]) + _dot(y_ref[...], ws_ref[...])


def _outproj(x, attn, y, g, wa, ws, tm):
    n = x.shape[0]
    row = lambda w: pl.BlockSpec((tm, w), lambda i: (i, 0))
    sq = _const_spec((D_ATT, D_MODEL))
    return pl.pallas_call(
        _outproj_body,
        grid=(n // tm,),
        in_specs=[row(D_MODEL), row(D_ATT), row(D_SSM), _const_spec((1, D_ATT)), sq, sq],
        out_specs=row(D_MODEL),
        out_shape=jax.ShapeDtypeStruct((n, D_MODEL), F32),
        compiler_params=_params("parallel"),
        name="outproj",
    )(x, attn, y, g, wa, ws)


def _t5_bucket(rel):
    nb = N_REL_BUCKETS // 2
    max_exact = nb // 2
    n = np.abs(rel)
    large = max_exact + (np.log(np.maximum(n, 1) / max_exact)
                         / math.log(REL_MAX_DIST / max_exact) * (nb - max_exact)).astype(np.int32)
    large = np.minimum(large, nb - 1)
    return (np.where(rel > 0, nb, 0) + np.where(n < max_exact, n, large)).astype(np.int32)


def _bias_tables(rel_bias):
    rel_sub = np.arange(ATT_KW)[None, :] - ATT_HALF - np.arange(ATT_BLOCK)[:, None]
    in_window = np.abs(rel_sub) <= ATT_HALF
    col = np.arange(ATT_KW)[None, :]
    tables = []
    for _, dil in ATT_PATTERNS:
        onehot = (_t5_bucket(rel_sub * dil)[:, :, None] == np.arange(N_REL_BUCKETS)).astype(np.float32)
        bias = jnp.einsum("ijk,kh->hij", jnp.asarray(onehot, BF16).astype(F32), rel_bias.astype(F32),
                          precision=lax.Precision.HIGHEST)
        tables.append(bias * LOG2_E)
    bias = jnp.stack(tables, axis=0)
    variants = []
    for v in range(4):
        keep = in_window & ((col >= ATT_HALF) | (v & 1 == 0)) & ((col < ATT_BLOCK + ATT_HALF) | (v & 2 == 0))
        variants.append(jnp.where(keep[None, None], bias, NEG_INF))
    return jnp.stack(variants, axis=0)


def _head_expander(first_row, width):
    e = np.zeros((2 * LANES, width), np.float32)
    for h in range(width // HEAD_DIM):
        e[first_row + h, h * HEAD_DIM:(h + 1) * HEAD_DIM] = 1.0
        e[LANES + first_row + h, h * HEAD_DIM:(h + 1) * HEAD_DIM] = 1.0
    return jnp.asarray(e, BF16)


def _block_diag_mean(width):
    i = np.arange(width)
    return jnp.asarray((i[:, None] // HEAD_DIM == i[None, :] // HEAD_DIM) / HEAD_DIM, BF16)


def _layer(x, p, tm):
    b, s, _ = x.shape
    n = b * s
    xf = x.reshape(n, D_MODEL)
    x1 = _ffn(xf, p["ffn1_g"], p["ffn1_wg"], p["ffn1_wu"], p["ffn1_wd"], tm)
    q, k, v, z, xbc, dt = _inproj(x1, p["mix_g"], p["wq"], p["wk"], p["wv"], p["wz"], p["wx"],
                                  p["wdt"], p["qg"], p["kg"], p["bd"], tm)
    seq = lambda a: a.reshape(b, s, a.shape[-1])
    attn = _attention(seq(q), seq(k), seq(v), p["bias"])
    y = _ssd(seq(xbc), seq(dt), seq(z), p["conv_w"], p["conv_b"], p["dt_bias"], p["a_neg"],
             p["d_skip"], p["ssm_g"], p["e_fwd"], p["e_bwd"])
    x2 = _outproj(x1, attn.reshape(n, D_ATT), y.reshape(n, D_SSM), p["attn_g"], p["wo_att"],
                  p["wo_ssm"], tm)
    out = _ffn(x2, p["ffn2_g"], p["ffn2_wg"], p["ffn2_wu"], p["ffn2_wd"], tm)
    return out.reshape(b, s, D_MODEL)


def _prepare(rel_bias, ffn1_norm_g, ffn1_w_gate, ffn1_w_up, ffn1_w_down, mix_norm_g, w_in,
             q_norm_g, k_norm_g, attn_out_g, conv_w, conv_b, dt_bias, a_log, d_skip, ssm_out_g,
             w_out, ffn2_norm_g, ffn2_w_gate, ffn2_w_up, ffn2_w_down):
    row = lambda a: a.reshape(1, -1).astype(F32)
    w16 = lambda a: a.astype(BF16)
    c0, c1, c2, c3 = D_ATT, 2 * D_ATT, 3 * D_ATT, 3 * D_ATT + D_SSM
    c4 = c3 + CONV_DIM
    pad32 = lambda a: jnp.pad(a.reshape(1, 2 * N_HEADS).astype(F32), ((0, 0), (0, LANES - 2 * N_HEADS)))
    return {
        "ffn1_g": row(ffn1_norm_g), "ffn1_wg": w16(ffn1_w_gate), "ffn1_wu": w16(ffn1_w_up),
        "ffn1_wd": w16(ffn1_w_down),
        "ffn2_g": row(ffn2_norm_g), "ffn2_wg": w16(ffn2_w_gate), "ffn2_wu": w16(ffn2_w_up),
        "ffn2_wd": w16(ffn2_w_down),
        "mix_g": row(mix_norm_g),
        "wq": w16(w_in[:, :c0]), "wk": w16(w_in[:, c0:c1]), "wv": w16(w_in[:, c1:c2]),
        "wz": w16(w_in[:, c2:c3]), "wx": w16(w_in[:, c3:c4]),
        "wdt": w16(jnp.pad(w_in[:, c4:], ((0, 0), (0, LANES - 2 * N_HEADS)))),
        "qg": row(jnp.tile(q_norm_g, N_HEADS)) * (HEAD_DIM ** -0.5 * LOG2_E),
        "kg": row(jnp.tile(k_norm_g, N_HEADS)),
        "bd": _block_diag_mean(256),
        "bias": _bias_tables(rel_bias),
        "attn_g": row(attn_out_g),
        "conv_w": conv_w.astype(F32), "conv_b": row(conv_b),
        "dt_bias": pad32(dt_bias), "a_neg": pad32(-jnp.exp(a_log.astype(F32))),
        "d_skip": row(jnp.repeat(d_skip, HEAD_DIM)), "ssm_g": row(ssm_out_g),
        "e_fwd": _head_expander(0, D_SSM), "e_bwd": _head_expander(N_HEADS, D_SSM),
        "wo_att": w16(w_out[:D_ATT]), "wo_ssm": w16(w_out[D_ATT:]),
    }


def _trunk(x, layers, tm=512):
    for p in layers:
        x = _layer(x, p, tm)
    return x


def kernel(x_prompt, x_sample, rel_bias, ffn1_norm_g, ffn1_w_gate, ffn1_w_up, ffn1_w_down, mix_norm_g, w_in, q_norm_g, k_norm_g, attn_out_g, conv_w, conv_b, dt_bias, a_log, d_skip, ssm_out_g, w_out, ffn2_norm_g, ffn2_w_gate, ffn2_w_up, ffn2_w_down):
    per_layer = (ffn1_norm_g, ffn1_w_gate, ffn1_w_up, ffn1_w_down, mix_norm_g, w_in, q_norm_g,
                 k_norm_g, attn_out_g, conv_w, conv_b, dt_bias, a_log, d_skip, ssm_out_g, w_out,
                 ffn2_norm_g, ffn2_w_gate, ffn2_w_up, ffn2_w_down)
    layers = [_prepare(rel_bias, *(a[l] for a in per_layer)) for l in range(ffn1_norm_g.shape[0])]
    return (_trunk(x_prompt, layers), _trunk(x_sample, layers))
```

```python
import functools
import math

import numpy as np
import jax
import jax.numpy as jnp
from jax import lax
from jax.experimental import pallas as pl
from jax.experimental.pallas import tpu as pltpu

D_MODEL = 1024
D_ATT = 1024
D_SSM = 1024
HEAD_DIM = 64
N_HEADS = 16
ATT_PATTERNS = ((128, 1), (512, 4), (2048, 16))
ATT_BLOCK = 128
ATT_HALF = 64
ATT_KW = ATT_BLOCK + 2 * ATT_HALF
ATT_MAX_DIL = max(d for _, d in ATT_PATTERNS)
ATT_TILE = ATT_BLOCK * ATT_MAX_DIL
ATT_REACH = ATT_HALF * ATT_MAX_DIL
ATT_WINDOW = ATT_TILE + 2 * ATT_REACH
ATT_STAGE = 4
assert tuple(d for _, d in ATT_PATTERNS) == (1, ATT_STAGE, ATT_STAGE * ATT_STAGE)
ATT_UNROLL = 8
N_REL_BUCKETS = 32
REL_MAX_DIST = 1024
N_GROUPS = 4
HEADS_PER_GROUP = 4
D_STATE = 128
D_CONV = 5
CHUNK = 128
SSD_STEP_CHUNKS = 4
CONV_DIM = D_SSM + 2 * N_GROUPS * D_STATE
D_FF = 2816
EPS = 1e-6
NEG_INF = -1e30
LOG2_E = 1.4426950408889634

LANES = 128
HALO_ROWS = 8
VMEM_LIMIT = 56 * 1024 * 1024

F32 = jnp.float32
BF16 = jnp.bfloat16


def _params(*sem):
    return pltpu.CompilerParams(dimension_semantics=sem, vmem_limit_bytes=VMEM_LIMIT)


def _const_spec(shape):
    n = len(shape)
    return pl.BlockSpec(shape, lambda *_: (0,) * n, pipeline_mode=pl.Buffered(1))


def _rms(x, g):
    ms = jnp.mean(x * x, axis=-1, keepdims=True)
    return x * lax.rsqrt(ms + EPS) * g


def _silu(x):
    h = 0.5 * x
    return h + h * jnp.tanh(h)


def _dot(a, b):
    return jnp.dot(a, b, preferred_element_type=F32)


def _hi_lo(vals):
    hi = vals.astype(BF16)
    lo = (vals - hi.astype(F32)).astype(BF16)
    return jnp.concatenate([hi, lo], axis=1)


def _expand(split, e_ref, cols=slice(None)):
    return _dot(split, e_ref[:, cols])


def _ffn_body(x_ref, g_ref, wg_ref, wu_ref, wd_ref, o_ref):
    x = x_ref[...]
    h = _rms(x, g_ref[...]).astype(BF16)
    gate = _dot(h, wg_ref[...])
    up = _dot(h, wu_ref[...])
    act = (_silu(gate) * up).astype(BF16)
    o_ref[...] = x + 0.5 * _dot(act, wd_ref[...])


def _ffn(x, g, wg, wu, wd, tm):
    n = x.shape[0]
    row = pl.BlockSpec((tm, D_MODEL), lambda i: (i, 0))
    return pl.pallas_call(
        _ffn_body,
        grid=(n // tm,),
        in_specs=[row, _const_spec((1, D_MODEL)), _const_spec((D_MODEL, D_FF)),
                  _const_spec((D_MODEL, D_FF)), _const_spec((D_FF, D_MODEL))],
        out_specs=row,
        out_shape=jax.ShapeDtypeStruct((n, D_MODEL), F32),
        compiler_params=_params("parallel"),
        name="ffn",
    )(x, g, wg, wu, wd)


def _inproj_body(x_ref, g_ref, wq_ref, wk_ref, wv_ref, wz_ref, wx_ref, wdt_ref, qg_ref, kg_ref,
                 bd_ref, q_ref, k_ref, v_ref, z_ref, xbc_ref, dt_ref):
    h = _rms(x_ref[...], g_ref[...]).astype(BF16)

    def head_norm(t, gain):
        t2 = (t * t).astype(BF16)
        w = bd_ref.shape[0]
        ms = jnp.concatenate([_dot(t2[:, j * w:(j + 1) * w], bd_ref[...])
                              for j in range(D_ATT // w)], axis=1)
        return t * lax.rsqrt(ms + EPS) * gain

    q_ref[...] = head_norm(_dot(h, wq_ref[...]), qg_ref[...])
    k_ref[...] = head_norm(_dot(h, wk_ref[...]), kg_ref[...])
    v_ref[...] = _dot(h, wv_ref[...])
    z_ref[...] = _dot(h, wz_ref[...])
    xbc_ref[...] = _dot(h, wx_ref[...])
    dt_ref[...] = _dot(h, wdt_ref[...])


def _inproj(x, g, wq, wk, wv, wz, wx, wdt, qg, kg, bd, tm):
    n = x.shape[0]
    row = lambda w: pl.BlockSpec((tm, w), lambda i: (i, 0))
    sq = _const_spec((D_MODEL, D_ATT))
    out = lambda w: jax.ShapeDtypeStruct((n, w), F32)
    return pl.pallas_call(
        _inproj_body,
        grid=(n // tm,),
        in_specs=[row(D_MODEL), _const_spec((1, D_MODEL)), sq, sq, sq, sq,
                  _const_spec((D_MODEL, CONV_DIM)), _const_spec((D_MODEL, LANES)),
                  _const_spec((1, D_ATT)), _const_spec((1, D_ATT)), _const_spec(bd.shape)],
        out_specs=[row(D_ATT), row(D_ATT), row(D_ATT), row(D_SSM), row(CONV_DIM), row(LANES)],
        out_shape=[out(D_ATT), out(D_ATT), out(D_ATT), out(D_SSM), out(CONV_DIM), out(LANES)],
        compiler_params=_params("parallel"),
        name="inproj",
    )(x, g, wq, wk, wv, wz, wx, wdt, qg, kg, bd)


def _rows(start, size, dil):
    return pl.ds(start, size) if dil == 1 else pl.ds(start, size, stride=dil)


def _regroup_keys(prev, cur, nxt, tmp, dst):
    s4 = ATT_STAGE
    lo, hi = ATT_REACH - ATT_HALF, ATT_REACH + ATT_TILE + ATT_HALF
    dst[0, lo:ATT_REACH, :] = prev[lo:ATT_REACH, :].astype(BF16)
    dst[0, ATT_REACH:ATT_REACH + ATT_TILE, :] = cur[...].astype(BF16)
    dst[0, ATT_REACH + ATT_TILE:hi, :] = nxt[0:ATT_HALF, :].astype(BF16)
    h4, t4 = ATT_REACH // s4, ATT_TILE // s4
    for r in range(s4):
        tmp[r, 0:h4, :] = prev[pl.ds(r, h4, stride=s4), :]
        tmp[r, h4:h4 + t4, :] = cur[pl.ds(r, t4, stride=s4), :]
        tmp[r, h4 + t4:, :] = nxt[pl.ds(r, h4, stride=s4), :]
    lo, hi = h4 - ATT_HALF, h4 + t4 + ATT_HALF
    for r in range(s4):
        dst[1, r * (ATT_WINDOW // s4) + lo:r * (ATT_WINDOW // s4) + hi, :] = tmp[r, lo:hi, :].astype(BF16)
    per = ATT_WINDOW // (s4 * s4)
    for r in range(s4 * s4):
        dst[2, r * per:(r + 1) * per, :] = tmp[r % s4, pl.ds(r // s4, per, stride=s4), :].astype(BF16)


def _attn_body(q_ref, kc_ref, kp_ref, kn_ref, vc_ref, vp_ref, vn_ref, bias_ref, out_ref,
               kd, vd, tmp, o_scr, lse_scr, *, seq_len):
    t0 = pl.program_id(1) * ATT_TILE
    _regroup_keys(kp_ref, kc_ref, kn_ref, tmp, kd)
    _regroup_keys(vp_ref, vc_ref, vn_ref, tmp, vd)

    first_head = lax.broadcasted_iota(jnp.int32, (ATT_BLOCK, LANES), 1) < HEAD_DIM
    first_head_kw = lax.broadcasted_iota(jnp.int32, (ATT_KW, LANES), 1) < HEAD_DIM
    head_ones = (first_head_kw.astype(BF16), (~first_head_kw).astype(BF16))

    for p, (_, dil) in enumerate(ATT_PATTERNS):
        blocks_per_residue = ATT_TILE // (dil * ATT_BLOCK)

        def block(idx, carry, p=p, dil=dil, blocks_per_residue=blocks_per_residue):
            r = idx // blocks_per_residue
            j = idx % blocks_per_residue
            q0 = j * (ATT_BLOCK * dil) + r
            k0 = pl.multiple_of(r * (ATT_WINDOW // dil) + ATT_REACH // dil - ATT_HALF + j * ATT_BLOCK,
                                ATT_HALF)
            at_start = jnp.logical_and(t0 == 0, j == 0)
            at_end = jnp.logical_and(t0 + ATT_TILE == seq_len, j == blocks_per_residue - 1)
            variant = at_start.astype(jnp.int32) + 2 * at_end.astype(jnp.int32)
            qp = q_ref[_rows(q0, ATT_BLOCK, dil), :].astype(BF16)
            kp = kd[p, pl.ds(k0, ATT_KW), :]
            vp = vd[p, pl.ds(k0, ATT_KW), :]
            zq, zv = jnp.zeros_like(qp), jnp.zeros_like(vp)
            q2 = jnp.concatenate([jnp.where(first_head, qp, zq), jnp.where(first_head, zq, qp)], axis=0)
            s2 = lax.dot_general(q2, kp, (((1,), (1,)), ((), ())), preferred_element_type=F32)
            es, ms = [], []
            for sub in range(2):
                s = s2[sub * ATT_BLOCK:(sub + 1) * ATT_BLOCK, :] + bias_ref[variant, p, sub]
                m = jnp.max(s, axis=-1, keepdims=True)
                es.append(jnp.exp2(s - m).astype(BF16))
                ms.append(m)
            v2 = jnp.concatenate([
                jnp.concatenate([jnp.where(first_head_kw, vp, zv), head_ones[0]], axis=1),
                jnp.concatenate([jnp.where(first_head_kw, zv, vp), head_ones[1]], axis=1)], axis=0)
            ol = _dot(jnp.concatenate(es, axis=1), v2)
            l = ol[:, LANES:]
            o_scr[p, _rows(q0, ATT_BLOCK, dil), :] = ol[:, :LANES] * (1.0 / l)
            lse_scr[p, _rows(q0, ATT_BLOCK, dil), :] = jnp.where(first_head, ms[0], ms[1]) + jnp.log2(l)
            return carry

        lax.fori_loop(0, ATT_TILE // ATT_BLOCK, block, 0, unroll=ATT_UNROLL)

    def merge(i, carry):
        rows = pl.ds(pl.multiple_of(i * ATT_BLOCK, ATT_BLOCK), ATT_BLOCK)
        l1, l2, l3 = lse_scr[0, rows, :], lse_scr[1, rows, :], lse_scr[2, rows, :]
        m = jnp.maximum(jnp.maximum(l1, l2), l3)
        e1, e2, e3 = jnp.exp2(l1 - m), jnp.exp2(l2 - m), jnp.exp2(l3 - m)
        mixed = e1 * o_scr[0, rows, :] + e2 * o_scr[1, rows, :] + e3 * o_scr[2, rows, :]
        out_ref[rows, :] = (mixed * (1.0 / (e1 + e2 + e3))).astype(BF16)
        return carry

    lax.fori_loop(0, ATT_TILE // ATT_BLOCK, merge, 0)


def _attention(q, k, v, bias):
    b, s, _ = q.shape
    n_pat = len(ATT_PATTERNS)
    last_halo = s // ATT_REACH - 1
    halo_per_tile = ATT_TILE // ATT_REACH
    cur = pl.BlockSpec((None, ATT_TILE, LANES), lambda bi, t, hp: (bi, t, hp))
    prev = pl.BlockSpec((None, ATT_REACH, LANES),
                        lambda bi, t, hp: (bi, jnp.maximum(t * halo_per_tile - 1, 0), hp))
    nxt = pl.BlockSpec((None, ATT_REACH, LANES),
                       lambda bi, t, hp: (bi, jnp.minimum((t + 1) * halo_per_tile, last_halo), hp))
    bias_spec = pl.BlockSpec((bias.shape[0], n_pat, 2, ATT_BLOCK, ATT_KW),
                             lambda bi, t, hp: (0, 0, hp, 0, 0))
    return pl.pallas_call(
        functools.partial(_attn_body, seq_len=s),
        grid=(b, s // ATT_TILE, N_HEADS // 2),
        in_specs=[cur, cur, prev, nxt, cur, prev, nxt, bias_spec],
        out_specs=cur,
        out_shape=jax.ShapeDtypeStruct((b, s, D_ATT), BF16),
        scratch_shapes=[pltpu.VMEM((n_pat, ATT_WINDOW, LANES), BF16),
                        pltpu.VMEM((n_pat, ATT_WINDOW, LANES), BF16),
                        pltpu.VMEM((ATT_STAGE, ATT_WINDOW // ATT_STAGE, LANES), F32),
                        pltpu.VMEM((n_pat, ATT_TILE, LANES), F32),
                        pltpu.VMEM((n_pat, ATT_TILE, LANES), F32)],
        compiler_params=_params("parallel", "parallel", "parallel"),
        name="attn",
    )(q, k, k, k, v, v, v, bias)


def _stage_conv_input(xc_ref, xp_ref, xn_ref, ext, has_prev, has_next):
    rows = xc_ref.shape[0]
    for sl in range(CONV_DIM // LANES):
        lanes = slice(sl * LANES, (sl + 1) * LANES)
        ext[sl, 0:HALO_ROWS, :] = jnp.where(has_prev, xp_ref[:, lanes], 0.0)
        ext[sl, HALO_ROWS:HALO_ROWS + rows, :] = xc_ref[:, lanes]
        ext[sl, HALO_ROWS + rows:, :] = jnp.where(has_next, xn_ref[:, lanes], 0.0)


def _conv_silu(ext, cw_ref, cb_ref, xa_ref, r0):
    pad = D_CONV // 2
    for sl in range(CONV_DIM // LANES):
        lanes = slice(sl * LANES, (sl + 1) * LANES)
        acc = cb_ref[:, lanes]
        for j in range(D_CONV):
            tap = ext[sl, pl.ds(r0 + (HALO_ROWS - pad + j), CHUNK, stride=1), :]
            acc = acc + tap * cw_ref[j:j + 1, lanes]
        xa_ref[pl.ds(r0, CHUNK), lanes] = _silu(acc)


def _softplus(x):
    return jnp.maximum(x, 0.0) + jnp.log(1.0 + jnp.exp(-jnp.abs(x)))


def _tri(lower):
    r = lax.broadcasted_iota(jnp.int32, (CHUNK, CHUNK), 0)
    c = lax.broadcasted_iota(jnp.int32, (CHUNK, CHUNK), 1)
    return (r >= c) if lower else (r <= c)


def _cumsum_rows(a, lower):
    return jnp.dot(_tri(lower).astype(F32), a, precision=lax.Precision.HIGHEST,
                   preferred_element_type=F32)


def _chunk_rows(i):
    return pl.ds(pl.multiple_of(i * CHUNK, CHUNK), CHUNK)


def _ssd_bwd_body(xc_ref, xp_ref, xn_ref, dt_ref, cw_ref, cb_ref, dtb_ref, a_ref, e_ref,
                  xa_ref, yoff_ref, ext, hst, *, n_steps):
    i = pl.program_id(1)
    step = n_steps - 1 - i

    @pl.when(i == 0)
    def _():
        hst[...] = jnp.zeros_like(hst)

    _stage_conv_input(xc_ref, xp_ref, xn_ref, ext, step > 0, step < n_steps - 1)

    def chunk(ci, carry):
        rs = _chunk_rows(SSD_STEP_CHUNKS - 1 - ci)
        _conv_silu(ext, cw_ref, cb_ref, xa_ref, rs.start)
        dt = _softplus(dt_ref[rs, :] + dtb_ref[...])
        rcum = _cumsum_rows(dt * a_ref[...], lower=False)
        decay_in = _expand(_hi_lo(jnp.exp(rcum)), e_ref)
        w_in = _expand(_hi_lo(jnp.exp(rcum[0:1, :] - rcum) * dt), e_ref)
        chunk_decay = _expand(_hi_lo(jnp.broadcast_to(jnp.exp(rcum[0:1, :]), (HALO_ROWS, LANES))),
                              e_ref)[0:1, :]
        for g in range(N_GROUPS):
            gl = slice(g * HEADS_PER_GROUP * HEAD_DIM, (g + 1) * HEADS_PER_GROUP * HEAD_DIM)
            bg = xa_ref[rs, D_SSM + g * D_STATE:D_SSM + (g + 1) * D_STATE].astype(BF16)
            cg = xa_ref[rs, D_SSM + (N_GROUPS + g) * D_STATE:D_SSM + (N_GROUPS + g + 1) * D_STATE].astype(BF16)
            xw = (xa_ref[rs, gl] * w_in[:, gl]).astype(BF16)
            h_in = hst[:, gl]
            yoff_ref[rs, gl] = _dot(cg, h_in.astype(BF16)) * decay_in[:, gl]
            upd = lax.dot_general(bg, xw, (((0,), (0,)), ((), ())), preferred_element_type=F32)
            hst[:, gl] = h_in * chunk_decay[:, gl] + upd
        return carry

    lax.fori_loop(0, SSD_STEP_CHUNKS, chunk, 0, unroll=True)


def _ssd_fwd_body(xa_ref, dt_ref, z_ref, yb_ref, dtb_ref, a_ref, dsk_ref, og_ref, e_ref, y_ref, hst):
    @pl.when(pl.program_id(1) == 0)
    def _():
        hst[...] = jnp.zeros_like(hst)

    lower, upper = _tri(True), _tri(False)
    lane = lax.broadcasted_iota(jnp.int32, (CHUNK, LANES), 1)

    def chunk(ci, carry):
        rs = _chunk_rows(ci)
        dt = _softplus(dt_ref[rs, :] + dtb_ref[...])
        a = dt * a_ref[...]
        fcum = _cumsum_rows(a, lower=True)
        rcum = _cumsum_rows(a, lower=False)
        decay_in = _expand(_hi_lo(jnp.exp(fcum)), e_ref)
        w_in = _expand(_hi_lo(jnp.exp(fcum[CHUNK - 1:CHUNK, :] - fcum) * dt), e_ref)
        chunk_decay = _expand(_hi_lo(jnp.broadcast_to(jnp.exp(fcum[CHUNK - 1:CHUNK, :]),
                                                       (HALO_ROWS, LANES))), e_ref)[0:1, :]
        log2_dt = jnp.log(dt) * LOG2_E
        fcol, rcol = fcum * LOG2_E, rcum * LOG2_E
        frow, rrow = (fcol - log2_dt).T, (rcol - log2_dt).T

        for g in range(N_GROUPS):
            gl = slice(g * HEADS_PER_GROUP * HEAD_DIM, (g + 1) * HEADS_PER_GROUP * HEAD_DIM)
            bg = xa_ref[rs, D_SSM + g * D_STATE:D_SSM + (g + 1) * D_STATE].astype(BF16)
            cg = xa_ref[rs, D_SSM + (N_GROUPS + g) * D_STATE:D_SSM + (N_GROUPS + g + 1) * D_STATE].astype(BF16)
            cb = lax.dot_general(cg, bg, (((1,), (1,)), ((), ())), preferred_element_type=F32)
            xw = (xa_ref[rs, gl] * w_in[:, gl]).astype(BF16)
            h_in = hst[:, gl]
            y_off = _dot(cg, h_in.astype(BF16)) * decay_in[:, gl]
            upd = lax.dot_general(bg, xw, (((0,), (0,)), ((), ())), preferred_element_type=F32)
            hst[:, gl] = h_in * chunk_decay[:, gl] + upd
            gated = []
            for pair in range(HEADS_PER_GROUP // 2):
                pl_ = slice(g * 256 + pair * LANES, g * 256 + (pair + 1) * LANES)
                xs = xa_ref[rs, pl_]
                x_pair = xs.astype(BF16)
                halves = []
                for sub in range(2):
                    h = g * HEADS_PER_GROUP + 2 * pair + sub
                    hb = N_HEADS + h
                    seg_f = jnp.where(lower, fcol[:, h:h + 1] - frow[h:h + 1, :], NEG_INF)
                    seg_b = jnp.where(upper, rcol[:, hb:hb + 1] - rrow[hb:hb + 1, :], NEG_INF)
                    mix = jnp.exp2(seg_f) + jnp.exp2(seg_b)
                    halves.append(_dot((cb * mix).astype(BF16), x_pair))
                y_diag = jnp.where(lane < HEAD_DIM, halves[0], halves[1])
                po = slice(pair * LANES, (pair + 1) * LANES)
                y = y_diag + y_off[:, po] + yb_ref[rs, pl_] + dsk_ref[:, pl_] * xs
                gated.append(y * _silu(z_ref[rs, pl_]))
            y_ref[rs, gl] = _rms(jnp.concatenate(gated, axis=1), og_ref[:, gl]).astype(BF16)
        return carry

    lax.fori_loop(0, SSD_STEP_CHUNKS, chunk, 0, unroll=True)


def _ssd(xbc, dt, z, conv_w, conv_b, dt_bias, a_neg, d_skip, out_g, e_fwd, e_bwd):
    b, s, _ = xbc.shape
    rows = SSD_STEP_CHUNKS * CHUNK
    nc = s // rows
    hpc = rows // HALO_ROWS
    last_halo = s // HALO_ROWS - 1

    def specs(cidx):
        chunk = lambda w: pl.BlockSpec((None, rows, w), lambda bi, i: (bi, cidx(i), 0))
        prev = pl.BlockSpec((None, HALO_ROWS, CONV_DIM),
                            lambda bi, i: (bi, jnp.maximum(cidx(i) * hpc - 1, 0), 0))
        nxt = pl.BlockSpec((None, HALO_ROWS, CONV_DIM),
                           lambda bi, i: (bi, jnp.minimum((cidx(i) + 1) * hpc, last_halo), 0))
        return chunk, prev, nxt

    state = pltpu.VMEM((D_STATE, D_SSM), F32)
    conv_ext = pltpu.VMEM((CONV_DIM // LANES, rows + 2 * HALO_ROWS, LANES), F32)
    small = [_const_spec(dt_bias.shape), _const_spec(a_neg.shape)]

    chunk, prev, nxt = specs(lambda i: nc - 1 - i)
    xa, yb_off = pl.pallas_call(
        functools.partial(_ssd_bwd_body, n_steps=nc),
        grid=(b, nc),
        in_specs=[chunk(CONV_DIM), prev, nxt, chunk(LANES), _const_spec(conv_w.shape),
                  _const_spec(conv_b.shape)] + small + [_const_spec(e_bwd.shape)],
        out_specs=[chunk(CONV_DIM), chunk(D_SSM)],
        out_shape=[jax.ShapeDtypeStruct((b, s, CONV_DIM), F32), jax.ShapeDtypeStruct((b, s, D_SSM), F32)],
        scratch_shapes=[conv_ext, state],
        compiler_params=_params("parallel", "arbitrary"),
        name="ssd_bwd",
    )(xbc, xbc, xbc, dt, conv_w, conv_b, dt_bias, a_neg, e_bwd)

    chunk, _, _ = specs(lambda i: i)
    return pl.pallas_call(
        _ssd_fwd_body,
        grid=(b, nc),
        in_specs=[chunk(CONV_DIM), chunk(LANES), chunk(D_SSM), chunk(D_SSM)] + small
                 + [_const_spec(d_skip.shape), _const_spec(out_g.shape), _const_spec(e_fwd.shape)],
        out_specs=chunk(D_SSM),
        out_shape=jax.ShapeDtypeStruct((b, s, D_SSM), BF16),
        scratch_shapes=[state],
        compiler_params=_params("parallel", "arbitrary"),
        name="ssd_fwd",
    )(xa, dt, z, yb_off, dt_bias, a_neg, d_skip, out_g, e_fwd)


def _outproj_body(x_ref, a_ref, y_ref, g_ref, wa_ref, ws_ref, out_ref):
    attn = _rms(a_ref[...].astype(F32), g_ref[...]).astype(BF16)
    out_ref[...] = x_ref[...] + _dot(attn, wa_ref[...]) + _dot(y_ref[...], ws_ref[...])


def _outproj(x, attn, y, g, wa, ws, tm):
    n = x.shape[0]
    row = lambda w: pl.BlockSpec((tm, w), lambda i: (i, 0))
    sq = _const_spec((D_ATT, D_MODEL))
    return pl.pallas_call(
        _outproj_body,
        grid=(n // tm,),
        in_specs=[row(D_MODEL), row(D_ATT), row(D_SSM), _const_spec((1, D_ATT)), sq, sq],
        out_specs=row(D_MODEL),
        out_shape=jax.ShapeDtypeStruct((n, D_MODEL), F32),
        compiler_params=_params("parallel"),
        name="outproj",
    )(x, attn, y, g, wa, ws)


def _t5_bucket(rel):
    nb = N_REL_BUCKETS // 2
    max_exact = nb // 2
    n = np.abs(rel)
    large = max_exact + (np.log(np.maximum(n, 1) / max_exact)
                         / math.log(REL_MAX_DIST / max_exact) * (nb - max_exact)).astype(np.int32)
    large = np.minimum(large, nb - 1)
    return (np.where(rel > 0, nb, 0) + np.where(n < max_exact, n, large)).astype(np.int32)


def _bias_tables(rel_bias):
    rel_sub = np.arange(ATT_KW)[None, :] - ATT_HALF - np.arange(ATT_BLOCK)[:, None]
    in_window = np.abs(rel_sub) <= ATT_HALF
    col = np.arange(ATT_KW)[None, :]
    tables = []
    for _, dil in ATT_PATTERNS:
        onehot = (_t5_bucket(rel_sub * dil)[:, :, None] == np.arange(N_REL_BUCKETS)).astype(np.float32)
        bias = jnp.einsum("ijk,kh->hij", jnp.asarray(onehot, BF16).astype(F32), rel_bias.astype(F32),
                          precision=lax.Precision.HIGHEST)
        tables.append(bias * LOG2_E)
    bias = jnp.stack(tables, axis=0)
    variants = []
    for v in range(4):
        keep = in_window & ((col >= ATT_HALF) | (v & 1 == 0)) & ((col < ATT_BLOCK + ATT_HALF) | (v & 2 == 0))
        variants.append(jnp.where(keep[None, None], bias, NEG_INF))
    return jnp.stack(variants, axis=0)


def _head_expander(first_row, width):
    e = np.zeros((2 * LANES, width), np.float32)
    for h in range(width // HEAD_DIM):
        e[first_row + h, h * HEAD_DIM:(h + 1) * HEAD_DIM] = 1.0
        e[LANES + first_row + h, h * HEAD_DIM:(h + 1) * HEAD_DIM] = 1.0
    return jnp.asarray(e, BF16)


def _block_diag_mean(width):
    i = np.arange(width)
    return jnp.asarray((i[:, None] // HEAD_DIM == i[None, :] // HEAD_DIM) / HEAD_DIM, BF16)


def _layer(x, p, tm):
    b, s, _ = x.shape
    n = b * s
    xf = x.reshape(n, D_MODEL)
    x1 = _ffn(xf, p["ffn1_g"], p["ffn1_wg"], p["ffn1_wu"], p["ffn1_wd"], tm)
    q, k, v, z, xbc, dt = _inproj(x1, p["mix_g"], p["wq"], p["wk"], p["wv"], p["wz"], p["wx"],
                                  p["wdt"], p["qg"], p["kg"], p["bd"], tm)
    seq = lambda a: a.reshape(b, s, a.shape[-1])
    attn = _attention(seq(q), seq(k), seq(v), p["bias"])
    y = _ssd(seq(xbc), seq(dt), seq(z), p["conv_w"], p["conv_b"], p["dt_bias"], p["a_neg"],
             p["d_skip"], p["ssm_g"], p["e_fwd"], p["e_bwd"])
    x2 = _outproj(x1, attn.reshape(n, D_ATT), y.reshape(n, D_SSM), p["attn_g"], p["wo_att"],
                  p["wo_ssm"], tm)
    out = _ffn(x2, p["ffn2_g"], p["ffn2_wg"], p["ffn2_wu"], p["ffn2_wd"], tm)
    return out.reshape(b, s, D_MODEL)


def _prepare(rel_bias, ffn1_norm_g, ffn1_w_gate, ffn1_w_up, ffn1_w_down, mix_norm_g, w_in,
             q_norm_g, k_norm_g, attn_out_g, conv_w, conv_b, dt_bias, a_log, d_skip, ssm_out_g,
             w_out, ffn2_norm_g, ffn2_w_gate, ffn2_w_up, ffn2_w_down):
    row = lambda a: a.reshape(1, -1).astype(F32)
    w16 = lambda a: a.astype(BF16)
    c0, c1, c2, c3 = D_ATT, 2 * D_ATT, 3 * D_ATT, 3 * D_ATT + D_SSM
    c4 = c3 + CONV_DIM
    pad32 = lambda a: jnp.pad(a.reshape(1, 2 * N_HEADS).astype(F32), ((0, 0), (0, LANES - 2 * N_HEADS)))
    return {
        "ffn1_g": row(ffn1_norm_g), "ffn1_wg": w16(ffn1_w_gate), "ffn1_wu": w16(ffn1_w_up),
        "ffn1_wd": w16(ffn1_w_down),
        "ffn2_g": row(ffn2_norm_g), "ffn2_wg": w16(ffn2_w_gate), "ffn2_wu": w16(ffn2_w_up),
        "ffn2_wd": w16(ffn2_w_down),
        "mix_g": row(mix_norm_g),
        "wq": w16(w_in[:, :c0]), "wk": w16(w_in[:, c0:c1]), "wv": w16(w_in[:, c1:c2]),
        "wz": w16(w_in[:, c2:c3]), "wx": w16(w_in[:, c3:c4]),
        "wdt": w16(jnp.pad(w_in[:, c4:], ((0, 0), (0, LANES - 2 * N_HEADS)))),
        "qg": row(jnp.tile(q_norm_g, N_HEADS)) * (HEAD_DIM ** -0.5 * LOG2_E),
        "kg": row(jnp.tile(k_norm_g, N_HEADS)),
        "bd": _block_diag_mean(256),
        "bias": _bias_tables(rel_bias),
        "attn_g": row(attn_out_g),
        "conv_w": conv_w.astype(F32), "conv_b": row(conv_b),
        "dt_bias": pad32(dt_bias), "a_neg": pad32(-jnp.exp(a_log.astype(F32))),
        "d_skip": row(jnp.repeat(d_skip, HEAD_DIM)), "ssm_g": row(ssm_out_g),
        "e_fwd": _head_expander(0, D_SSM), "e_bwd": _head_expander(N_HEADS, D_SSM),
        "wo_att": w16(w_out[:D_ATT]), "wo_ssm": w16(w_out[D_ATT:]),
    }


def _trunk(x, layers, tm=512):
    for p in layers:
        x = _layer(x, p, tm)
    return x


def kernel(x_prompt, x_sample, rel_bias, ffn1_norm_g, ffn1_w_gate, ffn1_w_up, ffn1_w_down, mix_norm_g, w_in, q_norm_g, k_norm_g, attn_out_g, conv_w, conv_b, dt_bias, a_log, d_skip, ssm_out_g, w_out, ffn2_norm_g, ffn2_w_gate, ffn2_w_up, ffn2_w_down):
    per_layer = (ffn1_norm_g, ffn1_w_gate, ffn1_w_up, ffn1_w_down, mix_norm_g, w_in, q_norm_g,
                 k_norm_g, attn_out_g, conv_w, conv_b, dt_bias, a_log, d_skip, ssm_out_g, w_out,
                 ffn2_norm_g, ffn2_w_gate, ffn2_w_up, ffn2_w_down)
    layers = [_prepare(rel_bias, *(a[l] for a in per_layer)) for l in range(ffn1_norm_g.shape[0])]
    return (_trunk(x_prompt, layers), _trunk(x_sample, layers))
```

```python
import functools
import math

import numpy as np
import jax
import jax.numpy as jnp
from jax import lax
from jax.experimental import pallas as pl
from jax.experimental.pallas import tpu as pltpu

D_MODEL = 1024
D_ATT = 1024
D_SSM = 1024
HEAD_DIM = 64
N_HEADS = 16
ATT_PATTERNS = ((128, 1), (512, 4), (2048, 16))
ATT_BLOCK = 128
ATT_HALF = 64
ATT_KW = ATT_BLOCK + 2 * ATT_HALF
ATT_MAX_DIL = max(d for _, d in ATT_PATTERNS)
ATT_TILE = ATT_BLOCK * ATT_MAX_DIL
ATT_REACH = ATT_HALF * ATT_MAX_DIL
ATT_WINDOW = ATT_TILE + 2 * ATT_REACH
ATT_STAGE = 4
assert tuple(d for _, d in ATT_PATTERNS) == (1, ATT_STAGE, ATT_STAGE * ATT_STAGE)
N_REL_BUCKETS = 32
REL_MAX_DIST = 1024
N_GROUPS = 4
HEADS_PER_GROUP = 4
D_STATE = 128
D_CONV = 5
CHUNK = 128
SSD_STEP_CHUNKS = 4
CONV_DIM = D_SSM + 2 * N_GROUPS * D_STATE
D_FF = 2816
EPS = 1e-6
NEG_INF = -1e30
LOG2_E = 1.4426950408889634

LANES = 128
HALO_ROWS = 8
VMEM_LIMIT = 56 * 1024 * 1024

F32 = jnp.float32
BF16 = jnp.bfloat16


def _params(*sem):
    return pltpu.CompilerParams(dimension_semantics=sem, vmem_limit_bytes=VMEM_LIMIT)


def _const_spec(shape):
    n = len(shape)
    return pl.BlockSpec(shape, lambda *_: (0,) * n, pipeline_mode=pl.Buffered(1))


def _rms(x, g):
    ms = jnp.mean(x * x, axis=-1, keepdims=True)
    return x * lax.rsqrt(ms + EPS) * g


def _silu(x):
    h = 0.5 * x
    return h + h * jnp.tanh(h)


def _dot(a, b):
    return jnp.dot(a, b, preferred_element_type=F32)


def _hi_lo(vals):
    hi = vals.astype(BF16)
    lo = (vals - hi.astype(F32)).astype(BF16)
    return jnp.concatenate([hi, lo], axis=1)


def _expand(split, e_ref, cols=slice(None)):
    return _dot(split, e_ref[:, cols])


def _ffn_body(x_ref, g_ref, wg_ref, wu_ref, wd_ref, o_ref):
    x = x_ref[...]
    h = _rms(x, g_ref[...]).astype(BF16)
    gate = _dot(h, wg_ref[...])
    up = _dot(h, wu_ref[...])
    act = (_silu(gate) * up).astype(BF16)
    o_ref[...] = x + 0.5 * _dot(act, wd_ref[...])


def _ffn(x, g, wg, wu, wd, tm):
    n = x.shape[0]
    row = pl.BlockSpec((tm, D_MODEL), lambda i: (i, 0))
    return pl.pallas_call(
        _ffn_body,
        grid=(n // tm,),
        in_specs=[row, _const_spec((1, D_MODEL)), _const_spec((D_MODEL, D_FF)),
                  _const_spec((D_MODEL, D_FF)), _const_spec((D_FF, D_MODEL))],
        out_specs=row,
        out_shape=jax.ShapeDtypeStruct((n, D_MODEL), F32),
        compiler_params=_params("parallel"),
        name="ffn",
    )(x, g, wg, wu, wd)


def _inproj_body(x_ref, g_ref, wq_ref, wk_ref, wv_ref, wz_ref, wx_ref, wdt_ref, qg_ref, kg_ref,
                 bd_ref, q_ref, k_ref, v_ref, z_ref, xbc_ref, dt_ref):
    h = _rms(x_ref[...], g_ref[...]).astype(BF16)

    def head_norm(t, gain):
        t2 = (t * t).astype(BF16)
        w = bd_ref.shape[0]
        ms = jnp.concatenate([_dot(t2[:, j * w:(j + 1) * w], bd_ref[...])
                              for j in range(D_ATT // w)], axis=1)
        return t * lax.rsqrt(ms + EPS) * gain

    q_ref[...] = head_norm(_dot(h, wq_ref[...]), qg_ref[...])
    k_ref[...] = head_norm(_dot(h, wk_ref[...]), kg_ref[...])
    v_ref[...] = _dot(h, wv_ref[...])
    z_ref[...] = _dot(h, wz_ref[...])
    xbc_ref[...] = _dot(h, wx_ref[...])
    dt_ref[...] = _dot(h, wdt_ref[...])


def _inproj(x, g, wq, wk, wv, wz, wx, wdt, qg, kg, bd, tm):
    n = x.shape[0]
    row = lambda w: pl.BlockSpec((tm, w), lambda i: (i, 0))
    sq = _const_spec((D_MODEL, D_ATT))
    out = lambda w: jax.ShapeDtypeStruct((n, w), F32)
    return pl.pallas_call(
        _inproj_body,
        grid=(n // tm,),
        in_specs=[row(D_MODEL), _const_spec((1, D_MODEL)), sq, sq, sq, sq,
                  _const_spec((D_MODEL, CONV_DIM)), _const_spec((D_MODEL, LANES)),
                  _const_spec((1, D_ATT)), _const_spec((1, D_ATT)), _const_spec(bd.shape)],
        out_specs=[row(D_ATT), row(D_ATT), row(D_ATT), row(D_SSM), row(CONV_DIM), row(LANES)],
        out_shape=[out(D_ATT), out(D_ATT), out(D_ATT), out(D_SSM), out(CONV_DIM), out(LANES)],
        compiler_params=_params("parallel"),
        name="inproj",
    )(x, g, wq, wk, wv, wz, wx, wdt, qg, kg, bd)


def _rows(start, size, dil):
    return pl.ds(start, size) if dil == 1 else pl.ds(start, size, stride=dil)


def _regroup_keys(prev, cur, nxt, tmp, dst):
    s4 = ATT_STAGE
    lo, hi = ATT_REACH - ATT_HALF, ATT_REACH + ATT_TILE + ATT_HALF
    dst[0, lo:ATT_REACH, :] = prev[lo:ATT_REACH, :].astype(BF16)
    dst[0, ATT_REACH:ATT_REACH + ATT_TILE, :] = cur[...].astype(BF16)
    dst[0, ATT_REACH + ATT_TILE:hi, :] = nxt[0:ATT_HALF, :].astype(BF16)
    h4, t4 = ATT_REACH // s4, ATT_TILE // s4
    for r in range(s4):
        tmp[r, 0:h4, :] = prev[pl.ds(r, h4, stride=s4), :]
        tmp[r, h4:h4 + t4, :] = cur[pl.ds(r, t4, stride=s4), :]
        tmp[r, h4 + t4:, :] = nxt[pl.ds(r, h4, stride=s4), :]
    lo, hi = h4 - ATT_HALF, h4 + t4 + ATT_HALF
    for r in range(s4):
        dst[1, r * (ATT_WINDOW // s4) + lo:r * (ATT_WINDOW // s4) + hi, :] = tmp[r, lo:hi, :].astype(BF16)
    per = ATT_WINDOW // (s4 * s4)
    for r in range(s4 * s4):
        dst[2, r * per:(r + 1) * per, :] = tmp[r % s4, pl.ds(r // s4, per, stride=s4), :].astype(BF16)


def _attn_body(q_ref, kc_ref, kp_ref, kn_ref, vc_ref, vp_ref, vn_ref, bias_ref, out_ref,
               kd, vd, tmp, o_scr, lse_scr, *, seq_len):
    t0 = pl.program_id(2) * ATT_TILE
    _regroup_keys(kp_ref, kc_ref, kn_ref, tmp, kd)
    _regroup_keys(vp_ref, vc_ref, vn_ref, tmp, vd)

    first_head = lax.broadcasted_iota(jnp.int32, (ATT_BLOCK, LANES), 1) < HEAD_DIM
    first_head_kw = lax.broadcasted_iota(jnp.int32, (ATT_KW, LANES), 1) < HEAD_DIM
    head_ones = (first_head_kw.astype(BF16), (~first_head_kw).astype(BF16))

    def block(p, dil, r, j):
        blocks_per_residue = ATT_TILE // (dil * ATT_BLOCK)
        q0 = j * (ATT_BLOCK * dil) + r
        k0 = r * (ATT_WINDOW // dil) + ATT_REACH // dil - ATT_HALF + j * ATT_BLOCK
        variant = 0
        if j == 0:
            variant = variant + (t0 == 0).astype(jnp.int32)
        if j == blocks_per_residue - 1:
            variant = variant + 2 * (t0 + ATT_TILE == seq_len).astype(jnp.int32)
        qp = q_ref[_rows(q0, ATT_BLOCK, dil), :].astype(BF16)
        kp = kd[p, k0:k0 + ATT_KW, :]
        vp = vd[p, k0:k0 + ATT_KW, :]
        zq, zv = jnp.zeros_like(qp), jnp.zeros_like(vp)
        q2 = jnp.concatenate([jnp.where(first_head, qp, zq), jnp.where(first_head, zq, qp)], axis=0)
        s2 = lax.dot_general(q2, kp, (((1,), (1,)), ((), ())), preferred_element_type=F32)
        es, ms = [], []
        for sub in range(2):
            s = s2[sub * ATT_BLOCK:(sub + 1) * ATT_BLOCK, :] + bias_ref[variant, p, sub]
            m = jnp.max(s, axis=-1, keepdims=True)
            es.append(jnp.exp2(s - m).astype(BF16))
            ms.append(m)
        v2 = jnp.concatenate([
            jnp.concatenate([jnp.where(first_head_kw, vp, zv), head_ones[0]], axis=1),
            jnp.concatenate([jnp.where(first_head_kw, zv, vp), head_ones[1]], axis=1)], axis=0)
        ol = _dot(jnp.concatenate(es, axis=1), v2)
        l = ol[:, LANES:]
        return ol[:, :LANES] * (1.0 / l), jnp.where(first_head, ms[0], ms[1]) + jnp.log2(l)

    for p in range(len(ATT_PATTERNS) - 1, 0, -1):
        dil = ATT_PATTERNS[p][1]
        blocks_per_residue = ATT_TILE // (dil * ATT_BLOCK)
        for idx in range(ATT_TILE // ATT_BLOCK):
            r, j = idx // blocks_per_residue, idx % blocks_per_residue
            rows = _rows(j * (ATT_BLOCK * dil) + r, ATT_BLOCK, dil)
            o_scr[p - 1, rows, :], lse_scr[p - 1, rows, :] = block(p, dil, r, j)
    for j in range(ATT_TILE // ATT_BLOCK):
        rows = pl.ds(j * ATT_BLOCK, ATT_BLOCK)
        o1, l1 = block(0, 1, 0, j)
        l2, l3 = lse_scr[0, rows, :], lse_scr[1, rows, :]
        m = jnp.maximum(jnp.maximum(l1, l2), l3)
        e1, e2, e3 = jnp.exp2(l1 - m), jnp.exp2(l2 - m), jnp.exp2(l3 - m)
        mixed = e1 * o1 + e2 * o_scr[0, rows, :] + e3 * o_scr[1, rows, :]
        out_ref[rows, :] = (mixed * (1.0 / (e1 + e2 + e3))).astype(BF16)


def _attention(q, k, v, bias):
    b, s, _ = q.shape
    n_pat = len(ATT_PATTERNS)
    last_halo = s // ATT_REACH - 1
    halo_per_tile = ATT_TILE // ATT_REACH
    cur = pl.BlockSpec((None, ATT_TILE, LANES), lambda hp, bi, t: (bi, t, hp))
    prev = pl.BlockSpec((None, ATT_REACH, LANES),
                        lambda hp, bi, t: (bi, jnp.maximum(t * halo_per_tile - 1, 0), hp))
    nxt = pl.BlockSpec((None, ATT_REACH, LANES),
                       lambda hp, bi, t: (bi, jnp.minimum((t + 1) * halo_per_tile, last_halo), hp))
    bias_spec = pl.BlockSpec((bias.shape[0], n_pat, 2, ATT_BLOCK, ATT_KW),
                             lambda hp, bi, t: (0, 0, hp, 0, 0))
    return pl.pallas_call(
        functools.partial(_attn_body, seq_len=s),
        grid=(N_HEADS // 2, b, s // ATT_TILE),
        in_specs=[cur, cur, prev, nxt, cur, prev, nxt, bias_spec],
        out_specs=cur,
        out_shape=jax.ShapeDtypeStruct((b, s, D_ATT), BF16),
        scratch_shapes=[pltpu.VMEM((n_pat, ATT_WINDOW, LANES), BF16),
                        pltpu.VMEM((n_pat, ATT_WINDOW, LANES), BF16),
                        pltpu.VMEM((ATT_STAGE, ATT_WINDOW // ATT_STAGE, LANES), F32),
                        pltpu.VMEM((n_pat - 1, ATT_TILE, LANES), F32),
                        pltpu.VMEM((n_pat - 1, ATT_TILE, LANES), F32)],
        compiler_params=_params("parallel", "parallel", "parallel"),
        name="attn",
    )(q, k, k, k, v, v, v, bias)


def _stage_conv_input(xc_ref, xp_ref, xn_ref, ext, has_prev, has_next):
    rows = xc_ref.shape[0]
    for sl in range(CONV_DIM // LANES):
        lanes = slice(sl * LANES, (sl + 1) * LANES)
        ext[sl, 0:HALO_ROWS, :] = jnp.where(has_prev, xp_ref[:, lanes], 0.0)
        ext[sl, HALO_ROWS:HALO_ROWS + rows, :] = xc_ref[:, lanes]
        ext[sl, HALO_ROWS + rows:, :] = jnp.where(has_next, xn_ref[:, lanes], 0.0)


def _conv_silu(ext, cw_ref, cb_ref, xa_ref, r0):
    pad = D_CONV // 2
    for sl in range(CONV_DIM // LANES):
        lanes = slice(sl * LANES, (sl + 1) * LANES)
        acc = cb_ref[:, lanes]
        for j in range(D_CONV):
            tap = ext[sl, pl.ds(r0 + (HALO_ROWS - pad + j), CHUNK, stride=1), :]
            acc = acc + tap * cw_ref[j:j + 1, lanes]
        xa_ref[pl.ds(r0, CHUNK), lanes] = _silu(acc)


def _softplus(x):
    return jnp.maximum(x, 0.0) + jnp.log(1.0 + jnp.exp(-jnp.abs(x)))


def _tri(lower):
    r = lax.broadcasted_iota(jnp.int32, (CHUNK, CHUNK), 0)
    c = lax.broadcasted_iota(jnp.int32, (CHUNK, CHUNK), 1)
    return (r >= c) if lower else (r <= c)


def _cumsum_rows(a, lower):
    return jnp.dot(_tri(lower).astype(F32), a, precision=lax.Precision.HIGHEST,
                   preferred_element_type=F32)


def _chunk_rows(i):
    return pl.ds(pl.multiple_of(i * CHUNK, CHUNK), CHUNK)


def _ssd_bwd_body(xc_ref, xp_ref, xn_ref, dt_ref, cw_ref, cb_ref, dtb_ref, a_ref, e_ref,
                  xa_ref, yoff_ref, ext, hst, *, n_steps):
    i = pl.program_id(1)
    step = n_steps - 1 - i

    @pl.when(i == 0)
    def _():
        hst[...] = jnp.zeros_like(hst)

    _stage_conv_input(xc_ref, xp_ref, xn_ref, ext, step > 0, step < n_steps - 1)

    def chunk(ci, carry):
        rs = _chunk_rows(SSD_STEP_CHUNKS - 1 - ci)
        _conv_silu(ext, cw_ref, cb_ref, xa_ref, rs.start)
        dt = _softplus(dt_ref[rs, :] + dtb_ref[...])
        rcum = _cumsum_rows(dt * a_ref[...], lower=False)
        decay_in = _expand(_hi_lo(jnp.exp(rcum)), e_ref)
        w_in = _expand(_hi_lo(jnp.exp(rcum[0:1, :] - rcum) * dt), e_ref)
        chunk_decay = _expand(_hi_lo(jnp.broadcast_to(jnp.exp(rcum[0:1, :]), (HALO_ROWS, LANES))),
                              e_ref)[0:1, :]
        for g in range(N_GROUPS):
            gl = slice(g * HEADS_PER_GROUP * HEAD_DIM, (g + 1) * HEADS_PER_GROUP * HEAD_DIM)
            bg = xa_ref[rs, D_SSM + g * D_STATE:D_SSM + (g + 1) * D_STATE].astype(BF16)
            cg = xa_ref[rs, D_SSM + (N_GROUPS + g) * D_STATE:D_SSM + (N_GROUPS + g + 1) * D_STATE].astype(BF16)
            xw = (xa_ref[rs, gl] * w_in[:, gl]).astype(BF16)
            h_in = hst[:, gl]
            yoff_ref[rs, gl] = _dot(cg, h_in.astype(BF16)) * decay_in[:, gl]
            upd = lax.dot_general(bg, xw, (((0,), (0,)), ((), ())), preferred_element_type=F32)
            hst[:, gl] = h_in * chunk_decay[:, gl] + upd
        return carry

    lax.fori_loop(0, SSD_STEP_CHUNKS, chunk, 0, unroll=True)


def _ssd_fwd_body(xa_ref, dt_ref, z_ref, yb_ref, dtb_ref, a_ref, dsk_ref, og_ref, e_ref, y_ref, hst):
    @pl.when(pl.program_id(1) == 0)
    def _():
        hst[...] = jnp.zeros_like(hst)

    lower, upper = _tri(True), _tri(False)
    lane = lax.broadcasted_iota(jnp.int32, (CHUNK, LANES), 1)

    def chunk(ci, carry):
        rs = _chunk_rows(ci)
        dt = _softplus(dt_ref[rs, :] + dtb_ref[...])
        a = dt * a_ref[...]
        fcum = _cumsum_rows(a, lower=True)
        rcum = _cumsum_rows(a, lower=False)
        decay_in = _expand(_hi_lo(jnp.exp(fcum)), e_ref)
        w_in = _expand(_hi_lo(jnp.exp(fcum[CHUNK - 1:CHUNK, :] - fcum) * dt), e_ref)
        chunk_decay = _expand(_hi_lo(jnp.broadcast_to(jnp.exp(fcum[CHUNK - 1:CHUNK, :]),
                                                       (HALO_ROWS, LANES))), e_ref)[0:1, :]
        log2_dt = jnp.log(dt) * LOG2_E
        fcol, rcol = fcum * LOG2_E, rcum * LOG2_E
        frow, rrow = (fcol - log2_dt).T, (rcol - log2_dt).T

        for g in range(N_GROUPS):
            gl = slice(g * HEADS_PER_GROUP * HEAD_DIM, (g + 1) * HEADS_PER_GROUP * HEAD_DIM)
            bg = xa_ref[rs, D_SSM + g * D_STATE:D_SSM + (g + 1) * D_STATE].astype(BF16)
            cg = xa_ref[rs, D_SSM + (N_GROUPS + g) * D_STATE:D_SSM + (N_GROUPS + g + 1) * D_STATE].astype(BF16)
            cb = lax.dot_general(cg, bg, (((1,), (1,)), ((), ())), preferred_element_type=F32)
            xw = (xa_ref[rs, gl] * w_in[:, gl]).astype(BF16)
            h_in = hst[:, gl]
            y_off = _dot(cg, h_in.astype(BF16)) * decay_in[:, gl]
            upd = lax.dot_general(bg, xw, (((0,), (0,)), ((), ())), preferred_element_type=F32)
            hst[:, gl] = h_in * chunk_decay[:, gl] + upd
            gated = []
            for pair in range(HEADS_PER_GROUP // 2):
                pl_ = slice(g * 256 + pair * LANES, g * 256 + (pair + 1) * LANES)
                xs = xa_ref[rs, pl_]
                x_pair = xs.astype(BF16)
                halves = []
                for sub in range(2):
                    h = g * HEADS_PER_GROUP + 2 * pair + sub
                    hb = N_HEADS + h
                    seg_f = jnp.where(lower, fcol[:, h:h + 1] - frow[h:h + 1, :], NEG_INF)
                    seg_b = jnp.where(upper, rcol[:, hb:hb + 1] - rrow[hb:hb + 1, :], NEG_INF)
                    mix = jnp.exp2(seg_f) + jnp.exp2(seg_b)
                    halves.append(_dot((cb * mix).astype(BF16), x_pair))
                y_diag = jnp.where(lane < HEAD_DIM, halves[0], halves[1])
                po = slice(pair * LANES, (pair + 1) * LANES)
                y = y_diag + y_off[:, po] + yb_ref[rs, pl_] + dsk_ref[:, pl_] * xs
                gated.append(y * _silu(z_ref[rs, pl_]))
            y_ref[rs, gl] = _rms(jnp.concatenate(gated, axis=1), og_ref[:, gl]).astype(BF16)
        return carry

    lax.fori_loop(0, SSD_STEP_CHUNKS, chunk, 0, unroll=True)


def _ssd(xbc, dt, z, conv_w, conv_b, dt_bias, a_neg, d_skip, out_g, e_fwd, e_bwd):
    b, s, _ = xbc.shape
    rows = SSD_STEP_CHUNKS * CHUNK
    nc = s // rows
    hpc = rows // HALO_ROWS
    last_halo = s // HALO_ROWS - 1

    def specs(cidx):
        chunk = lambda w: pl.BlockSpec((None, rows, w), lambda bi, i: (bi, cidx(i), 0))
        prev = pl.BlockSpec((None, HALO_ROWS, CONV_DIM),
                            lambda bi, i: (bi, jnp.maximum(cidx(i) * hpc - 1, 0), 0))
        nxt = pl.BlockSpec((None, HALO_ROWS, CONV_DIM),
                           lambda bi, i: (bi, jnp.minimum((cidx(i) + 1) * hpc, last_halo), 0))
        return chunk, prev, nxt

    state = pltpu.VMEM((D_STATE, D_SSM), F32)
    conv_ext = pltpu.VMEM((CONV_DIM // LANES, rows + 2 * HALO_ROWS, LANES), F32)
    small = [_const_spec(dt_bias.shape), _const_spec(a_neg.shape)]

    chunk, prev, nxt = specs(lambda i: nc - 1 - i)
    xa, yb_off = pl.pallas_call(
        functools.partial(_ssd_bwd_body, n_steps=nc),
        grid=(b, nc),
        in_specs=[chunk(CONV_DIM), prev, nxt, chunk(LANES), _const_spec(conv_w.shape),
                  _const_spec(conv_b.shape)] + small + [_const_spec(e_bwd.shape)],
        out_specs=[chunk(CONV_DIM), chunk(D_SSM)],
        out_shape=[jax.ShapeDtypeStruct((b, s, CONV_DIM), F32), jax.ShapeDtypeStruct((b, s, D_SSM), F32)],
        scratch_shapes=[conv_ext, state],
        compiler_params=_params("parallel", "arbitrary"),
        name="ssd_bwd",
    )(xbc, xbc, xbc, dt, conv_w, conv_b, dt_bias, a_neg, e_bwd)

    chunk, _, _ = specs(lambda i: i)
    return pl.pallas_call(
        _ssd_fwd_body,
        grid=(b, nc),
        in_specs=[chunk(CONV_DIM), chunk(LANES), chunk(D_SSM), chunk(D_SSM)] + small
                 + [_const_spec(d_skip.shape), _const_spec(out_g.shape), _const_spec(e_fwd.shape)],
        out_specs=chunk(D_SSM),
        out_shape=jax.ShapeDtypeStruct((b, s, D_SSM), BF16),
        scratch_shapes=[state],
        compiler_params=_params("parallel", "arbitrary"),
        name="ssd_fwd",
    )(xa, dt, z, yb_off, dt_bias, a_neg, d_skip, out_g, e_fwd)


def _outproj_body(x_ref, a_ref, y_ref, g_ref, wa_ref, ws_ref, out_ref):
    attn = _rms(a_ref[...].astype(F32), g_ref[...]).astype(BF16)
    out_ref[...] = x_ref[...] + _dot(attn, wa_ref[...]) + _dot(y_ref[...], ws_ref[...])


def _outproj(x, attn, y, g, wa, ws, tm):
    n = x.shape[0]
    row = lambda w: pl.BlockSpec((tm, w), lambda i: (i, 0))
    sq = _const_spec((D_ATT, D_MODEL))
    return pl.pallas_call(
        _outproj_body,
        grid=(n // tm,),
        in_specs=[row(D_MODEL), row(D_ATT), row(D_SSM), _const_spec((1, D_ATT)), sq, sq],
        out_specs=row(D_MODEL),
        out_shape=jax.ShapeDtypeStruct((n, D_MODEL), F32),
        compiler_params=_params("parallel"),
        name="outproj",
    )(x, attn, y, g, wa, ws)


def _t5_bucket(rel):
    nb = N_REL_BUCKETS // 2
    max_exact = nb // 2
    n = np.abs(rel)
    large = max_exact + (np.log(np.maximum(n, 1) / max_exact)
                         / math.log(REL_MAX_DIST / max_exact) * (nb - max_exact)).astype(np.int32)
    large = np.minimum(large, nb - 1)
    return (np.where(rel > 0, nb, 0) + np.where(n < max_exact, n, large)).astype(np.int32)


def _bias_tables(rel_bias):
    rel_sub = np.arange(ATT_KW)[None, :] - ATT_HALF - np.arange(ATT_BLOCK)[:, None]
    in_window = np.abs(rel_sub) <= ATT_HALF
    col = np.arange(ATT_KW)[None, :]
    tables = []
    for _, dil in ATT_PATTERNS:
        onehot = (_t5_bucket(rel_sub * dil)[:, :, None] == np.arange(N_REL_BUCKETS)).astype(np.float32)
        bias = jnp.einsum("ijk,kh->hij", jnp.asarray(onehot, BF16).astype(F32), rel_bias.astype(F32),
                          precision=lax.Precision.HIGHEST)
        tables.append(bias * LOG2_E)
    bias = jnp.stack(tables, axis=0)
    variants = []
    for v in range(4):
        keep = in_window & ((col >= ATT_HALF) | (v & 1 == 0)) & ((col < ATT_BLOCK + ATT_HALF) | (v & 2 == 0))
        variants.append(jnp.where(keep[None, None], bias, NEG_INF))
    return jnp.stack(variants, axis=0)


def _head_expander(first_row, width):
    e = np.zeros((2 * LANES, width), np.float32)
    for h in range(width // HEAD_DIM):
        e[first_row + h, h * HEAD_DIM:(h + 1) * HEAD_DIM] = 1.0
        e[LANES + first_row + h, h * HEAD_DIM:(h + 1) * HEAD_DIM] = 1.0
    return jnp.asarray(e, BF16)


def _block_diag_mean(width):
    i = np.arange(width)
    return jnp.asarray((i[:, None] // HEAD_DIM == i[None, :] // HEAD_DIM) / HEAD_DIM, BF16)


def _layer(x, p, tm):
    b, s, _ = x.shape
    n = b * s
    xf = x.reshape(n, D_MODEL)
    x1 = _ffn(xf, p["ffn1_g"], p["ffn1_wg"], p["ffn1_wu"], p["ffn1_wd"], tm)
    q, k, v, z, xbc, dt = _inproj(x1, p["mix_g"], p["wq"], p["wk"], p["wv"], p["wz"], p["wx"],
                                  p["wdt"], p["qg"], p["kg"], p["bd"], tm)
    seq = lambda a: a.reshape(b, s, a.shape[-1])
    attn = _attention(seq(q), seq(k), seq(v), p["bias"])
    y = _ssd(seq(xbc), seq(dt), seq(z), p["conv_w"], p["conv_b"], p["dt_bias"], p["a_neg"],
             p["d_skip"], p["ssm_g"], p["e_fwd"], p["e_bwd"])
    x2 = _outproj(x1, attn.reshape(n, D_ATT), y.reshape(n, D_SSM), p["attn_g"], p["wo_att"],
                  p["wo_ssm"], tm)
    out = _ffn(x2, p["ffn2_g"], p["ffn2_wg"], p["ffn2_wu"], p["ffn2_wd"], tm)
    return out.reshape(b, s, D_MODEL)


def _prepare(rel_bias, ffn1_norm_g, ffn1_w_gate, ffn1_w_up, ffn1_w_down, mix_norm_g, w_in,
             q_norm_g, k_norm_g, attn_out_g, conv_w, conv_b, dt_bias, a_log, d_skip, ssm_out_g,
             w_out, ffn2_norm_g, ffn2_w_gate, ffn2_w_up, ffn2_w_down):
    row = lambda a: a.reshape(1, -1).astype(F32)
    w16 = lambda a: a.astype(BF16)
    c0, c1, c2, c3 = D_ATT, 2 * D_ATT, 3 * D_ATT, 3 * D_ATT + D_SSM
    c4 = c3 + CONV_DIM
    pad32 = lambda a: jnp.pad(a.reshape(1, 2 * N_HEADS).astype(F32), ((0, 0), (0, LANES - 2 * N_HEADS)))
    return {
        "ffn1_g": row(ffn1_norm_g), "ffn1_wg": w16(ffn1_w_gate), "ffn1_wu": w16(ffn1_w_up),
        "ffn1_wd": w16(ffn1_w_down),
        "ffn2_g": row(ffn2_norm_g), "ffn2_wg": w16(ffn2_w_gate), "ffn2_wu": w16(ffn2_w_up),
        "ffn2_wd": w16(ffn2_w_down),
        "mix_g": row(mix_norm_g),
        "wq": w16(w_in[:, :c0]), "wk": w16(w_in[:, c0:c1]), "wv": w16(w_in[:, c1:c2]),
        "wz": w16(w_in[:, c2:c3]), "wx": w16(w_in[:, c3:c4]),
        "wdt": w16(jnp.pad(w_in[:, c4:], ((0, 0), (0, LANES - 2 * N_HEADS)))),
        "qg": row(jnp.tile(q_norm_g, N_HEADS)) * (HEAD_DIM ** -0.5 * LOG2_E),
        "kg": row(jnp.tile(k_norm_g, N_HEADS)),
        "bd": _block_diag_mean(256),
        "bias": _bias_tables(rel_bias),
        "attn_g": row(attn_out_g),
        "conv_w": conv_w.astype(F32), "conv_b": row(conv_b),
        "dt_bias": pad32(dt_bias), "a_neg": pad32(-jnp.exp(a_log.astype(F32))),
        "d_skip": row(jnp.repeat(d_skip, HEAD_DIM)), "ssm_g": row(ssm_out_g),
        "e_fwd": _head_expander(0, D_SSM), "e_bwd": _head_expander(N_HEADS, D_SSM),
        "wo_att": w16(w_out[:D_ATT]), "wo_ssm": w16(w_out[D_ATT:]),
    }


def _trunk(x, layers, tm=512):
    for p in layers:
        x = _layer(x, p, tm)
    return x


def kernel(x_prompt, x_sample, rel_bias, ffn1_norm_g, ffn1_w_gate, ffn1_w_up, ffn1_w_down, mix_norm_g, w_in, q_norm_g, k_norm_g, attn_out_g, conv_w, conv_b, dt_bias, a_log, d_skip, ssm_out_g, w_out, ffn2_norm_g, ffn2_w_gate, ffn2_w_up, ffn2_w_down):
    per_layer = (ffn1_norm_g, ffn1_w_gate, ffn1_w_up, ffn1_w_down, mix_norm_g, w_in, q_norm_g,
                 k_norm_g, attn_out_g, conv_w, conv_b, dt_bias, a_log, d_skip, ssm_out_g, w_out,
                 ffn2_norm_g, ffn2_w_gate, ffn2_w_up, ffn2_w_down)
    layers = [_prepare(rel_bias, *(a[l] for a in per_layer)) for l in range(ffn1_norm_g.shape[0])]
    return (_trunk(x_prompt, layers), _trunk(x_sample, layers))
```

```python
import functools
import math

import numpy as np
import jax
import jax.numpy as jnp
from jax import lax
from jax.experimental import pallas as pl
from jax.experimental.pallas import tpu as pltpu

D_MODEL = 1024
D_ATT = 1024
D_SSM = 1024
HEAD_DIM = 64
N_HEADS = 16
ATT_PATTERNS = ((128, 1), (512, 4), (2048, 16))
ATT_BLOCK = 128
ATT_HALF = 64
ATT_KW = ATT_BLOCK + 2 * ATT_HALF
ATT_MAX_DIL = max(d for _, d in ATT_PATTERNS)
ATT_TILE = ATT_BLOCK * ATT_MAX_DIL
ATT_REACH = ATT_HALF * ATT_MAX_DIL
ATT_WINDOW = ATT_TILE + 2 * ATT_REACH
ATT_STAGE = 4
assert tuple(d for _, d in ATT_PATTERNS) == (1, ATT_STAGE, ATT_STAGE * ATT_STAGE)
N_REL_BUCKETS = 32
REL_MAX_DIST = 1024
N_GROUPS = 4
HEADS_PER_GROUP = 4
D_STATE = 128
D_CONV = 5
CHUNK = 128
SSD_STEP_CHUNKS = 4
CONV_DIM = D_SSM + 2 * N_GROUPS * D_STATE
D_FF = 2816
EPS = 1e-6
NEG_INF = -1e30
LOG2_E = 1.4426950408889634

LANES = 128
HALO_ROWS = 8
VMEM_LIMIT = 56 * 1024 * 1024

F32 = jnp.float32
BF16 = jnp.bfloat16


def _params(*sem):
    return pltpu.CompilerParams(dimension_semantics=sem, vmem_limit_bytes=VMEM_LIMIT)


def _const_spec(shape):
    n = len(shape)
    return pl.BlockSpec(shape, lambda *_: (0,) * n, pipeline_mode=pl.Buffered(1))


def _rms(x, g):
    ms = jnp.mean(x * x, axis=-1, keepdims=True)
    return x * lax.rsqrt(ms + EPS) * g


def _silu(x):
    h = 0.5 * x
    return h + h * jnp.tanh(h)


def _dot(a, b):
    return jnp.dot(a, b, preferred_element_type=F32)


def _hi_lo(vals):
    hi = vals.astype(BF16)
    lo = (vals - hi.astype(F32)).astype(BF16)
    return jnp.concatenate([hi, lo], axis=1)


def _expand(split, e_ref, cols=slice(None)):
    return _dot(split, e_ref[:, cols])


def _half_step_ffn(x, g_ref, wg_ref, wu_ref, wd_ref):
    h = _rms(x, g_ref[...]).astype(BF16)
    gate = _dot(h, wg_ref[...])
    up = _dot(h, wu_ref[...])
    act = (_silu(gate) * up).astype(BF16)
    return x + 0.5 * _dot(act, wd_ref[...])


def _ffn_body(x_ref, g_ref, wg_ref, wu_ref, wd_ref, o_ref):
    o_ref[...] = _half_step_ffn(x_ref[...], g_ref, wg_ref, wu_ref, wd_ref)


def _ffn(x, g, wg, wu, wd, tm):
    n = x.shape[0]
    row = pl.BlockSpec((tm, D_MODEL), lambda i: (i, 0))
    return pl.pallas_call(
        _ffn_body,
        grid=(n // tm,),
        in_specs=[row, _const_spec((1, D_MODEL)), _const_spec((D_MODEL, D_FF)),
                  _const_spec((D_MODEL, D_FF)), _const_spec((D_FF, D_MODEL))],
        out_specs=row,
        out_shape=jax.ShapeDtypeStruct((n, D_MODEL), F32),
        compiler_params=_params("parallel"),
        name="ffn",
    )(x, g, wg, wu, wd)


def _inproj_body(x_ref, g_ref, wq_ref, wk_ref, wv_ref, wz_ref, wx_ref, wdt_ref, qg_ref, kg_ref,
                 bd_ref, q_ref, k_ref, v_ref, z_ref, xbc_ref, dt_ref):
    h = _rms(x_ref[...], g_ref[...]).astype(BF16)

    def head_norm(t, gain):
        t2 = (t * t).astype(BF16)
        w = bd_ref.shape[0]
        ms = jnp.concatenate([_dot(t2[:, j * w:(j + 1) * w], bd_ref[...])
                              for j in range(D_ATT // w)], axis=1)
        return t * lax.rsqrt(ms + EPS) * gain

    def put_pairs(ref, val):
        for hp in range(N_HEADS // 2):
            ref[hp] = val[:, hp * LANES:(hp + 1) * LANES]

    put_pairs(q_ref, head_norm(_dot(h, wq_ref[...]), qg_ref[...]))
    put_pairs(k_ref, head_norm(_dot(h, wk_ref[...]), kg_ref[...]))
    put_pairs(v_ref, _dot(h, wv_ref[...]))
    z_ref[...] = _dot(h, wz_ref[...])
    xbc_ref[...] = _dot(h, wx_ref[...])
    dt_ref[...] = _dot(h, wdt_ref[...])


def _inproj(x, g, wq, wk, wv, wz, wx, wdt, qg, kg, bd, tm):
    n = x.shape[0]
    row = lambda w: pl.BlockSpec((tm, w), lambda i: (i, 0))
    sq = _const_spec((D_MODEL, D_ATT))
    out = lambda w: jax.ShapeDtypeStruct((n, w), F32)
    pairs = pl.BlockSpec((N_HEADS // 2, tm, LANES), lambda i: (0, i, 0))
    pairs_out = jax.ShapeDtypeStruct((N_HEADS // 2, n, LANES), F32)
    return pl.pallas_call(
        _inproj_body,
        grid=(n // tm,),
        in_specs=[row(D_MODEL), _const_spec((1, D_MODEL)), sq, sq, sq, sq,
                  _const_spec((D_MODEL, CONV_DIM)), _const_spec((D_MODEL, LANES)),
                  _const_spec((1, D_ATT)), _const_spec((1, D_ATT)), _const_spec(bd.shape)],
        out_specs=[pairs, pairs, pairs, row(D_SSM), row(CONV_DIM), row(LANES)],
        out_shape=[pairs_out, pairs_out, pairs_out, out(D_SSM), out(CONV_DIM), out(LANES)],
        compiler_params=_params("parallel"),
        name="inproj",
    )(x, g, wq, wk, wv, wz, wx, wdt, qg, kg, bd)


def _rows(start, size, dil):
    return pl.ds(start, size) if dil == 1 else pl.ds(start, size, stride=dil)


def _regroup_keys(prev, cur, nxt, tmp, dst):
    s4 = ATT_STAGE
    lo, hi = ATT_REACH - ATT_HALF, ATT_REACH + ATT_TILE + ATT_HALF
    dst[0, lo:ATT_REACH, :] = prev[lo:ATT_REACH, :].astype(BF16)
    dst[0, ATT_REACH:ATT_REACH + ATT_TILE, :] = cur[...].astype(BF16)
    dst[0, ATT_REACH + ATT_TILE:hi, :] = nxt[0:ATT_HALF, :].astype(BF16)
    h4, t4 = ATT_REACH // s4, ATT_TILE // s4
    for r in range(s4):
        tmp[r, 0:h4, :] = prev[pl.ds(r, h4, stride=s4), :]
        tmp[r, h4:h4 + t4, :] = cur[pl.ds(r, t4, stride=s4), :]
        tmp[r, h4 + t4:, :] = nxt[pl.ds(r, h4, stride=s4), :]
    lo, hi = h4 - ATT_HALF, h4 + t4 + ATT_HALF
    for r in range(s4):
        dst[1, r * (ATT_WINDOW // s4) + lo:r * (ATT_WINDOW // s4) + hi, :] = tmp[r, lo:hi, :].astype(BF16)
    per = ATT_WINDOW // (s4 * s4)
    for r in range(s4 * s4):
        dst[2, r * per:(r + 1) * per, :] = tmp[r % s4, pl.ds(r // s4, per, stride=s4), :].astype(BF16)


def _attn_body(q_ref, kc_ref, kp_ref, kn_ref, vc_ref, vp_ref, vn_ref, bias_ref, out_ref,
               kd, vd, tmp, o_scr, lse_scr, *, seq_len):
    t0 = pl.program_id(2) * ATT_TILE
    _regroup_keys(kp_ref, kc_ref, kn_ref, tmp, kd)
    _regroup_keys(vp_ref, vc_ref, vn_ref, tmp, vd)

    first_head = lax.broadcasted_iota(jnp.int32, (ATT_BLOCK, LANES), 1) < HEAD_DIM
    first_head_kw = lax.broadcasted_iota(jnp.int32, (ATT_KW, LANES), 1) < HEAD_DIM
    head_ones = (first_head_kw.astype(BF16), (~first_head_kw).astype(BF16))

    def block(p, dil, r, j):
        blocks_per_residue = ATT_TILE // (dil * ATT_BLOCK)
        q0 = j * (ATT_BLOCK * dil) + r
        k0 = r * (ATT_WINDOW // dil) + ATT_REACH // dil - ATT_HALF + j * ATT_BLOCK
        variant = 0
        if j == 0:
            variant = variant + (t0 == 0).astype(jnp.int32)
        if j == blocks_per_residue - 1:
            variant = variant + 2 * (t0 + ATT_TILE == seq_len).astype(jnp.int32)
        qp = q_ref[_rows(q0, ATT_BLOCK, dil), :].astype(BF16)
        kp = kd[p, k0:k0 + ATT_KW, :]
        vp = vd[p, k0:k0 + ATT_KW, :]
        zq, zv = jnp.zeros_like(qp), jnp.zeros_like(vp)
        q2 = jnp.concatenate([jnp.where(first_head, qp, zq), jnp.where(first_head, zq, qp)], axis=0)
        s2 = lax.dot_general(q2, kp, (((1,), (1,)), ((), ())), preferred_element_type=F32)
        es, ms = [], []
        for sub in range(2):
            s = s2[sub * ATT_BLOCK:(sub + 1) * ATT_BLOCK, :] + bias_ref[variant, p, sub]
            m = jnp.max(s, axis=-1, keepdims=True)
            es.append(jnp.exp2(s - m).astype(BF16))
            ms.append(m)
        v2 = jnp.concatenate([
            jnp.concatenate([jnp.where(first_head_kw, vp, zv), head_ones[0]], axis=1),
            jnp.concatenate([jnp.where(first_head_kw, zv, vp), head_ones[1]], axis=1)], axis=0)
        ol = _dot(jnp.concatenate(es, axis=1), v2)
        l = ol[:, LANES:]
        return ol[:, :LANES] * (1.0 / l), jnp.where(first_head, ms[0], ms[1]) + jnp.log2(l)

    for p in range(len(ATT_PATTERNS) - 1, 0, -1):
        dil = ATT_PATTERNS[p][1]
        blocks_per_residue = ATT_TILE // (dil * ATT_BLOCK)
        for idx in range(ATT_TILE // ATT_BLOCK):
            r, j = idx // blocks_per_residue, idx % blocks_per_residue
            rows = _rows(j * (ATT_BLOCK * dil) + r, ATT_BLOCK, dil)
            o_scr[p - 1, rows, :], lse_scr[p - 1, rows, :] = block(p, dil, r, j)
    for j in range(ATT_TILE // ATT_BLOCK):
        rows = pl.ds(j * ATT_BLOCK, ATT_BLOCK)
        o1, l1 = block(0, 1, 0, j)
        l2, l3 = lse_scr[0, rows, :], lse_scr[1, rows, :]
        m = jnp.maximum(jnp.maximum(l1, l2), l3)
        e1, e2, e3 = jnp.exp2(l1 - m), jnp.exp2(l2 - m), jnp.exp2(l3 - m)
        mixed = e1 * o1 + e2 * o_scr[0, rows, :] + e3 * o_scr[1, rows, :]
        out_ref[rows, :] = (mixed * (1.0 / (e1 + e2 + e3))).astype(BF16)


def _attention(q, k, v, bias):
    n_pairs, b, s, _ = q.shape
    n_pat = len(ATT_PATTERNS)
    last_halo = s // ATT_REACH - 1
    halo_per_tile = ATT_TILE // ATT_REACH
    cur = pl.BlockSpec((None, None, ATT_TILE, LANES), lambda hp, bi, t: (hp, bi, t, 0))
    prev = pl.BlockSpec((None, None, ATT_REACH, LANES),
                        lambda hp, bi, t: (hp, bi, jnp.maximum(t * halo_per_tile - 1, 0), 0))
    nxt = pl.BlockSpec((None, None, ATT_REACH, LANES),
                       lambda hp, bi, t: (hp, bi, jnp.minimum((t + 1) * halo_per_tile, last_halo), 0))
    bias_spec = pl.BlockSpec((bias.shape[0], n_pat, 2, ATT_BLOCK, ATT_KW),
                             lambda hp, bi, t: (0, 0, hp, 0, 0))
    return pl.pallas_call(
        functools.partial(_attn_body, seq_len=s),
        grid=(n_pairs, b, s // ATT_TILE),
        in_specs=[cur, cur, prev, nxt, cur, prev, nxt, bias_spec],
        out_specs=cur,
        out_shape=jax.ShapeDtypeStruct((n_pairs, b, s, LANES), BF16),
        scratch_shapes=[pltpu.VMEM((n_pat, ATT_WINDOW, LANES), BF16),
                        pltpu.VMEM((n_pat, ATT_WINDOW, LANES), BF16),
                        pltpu.VMEM((ATT_STAGE, ATT_WINDOW // ATT_STAGE, LANES), F32),
                        pltpu.VMEM((n_pat - 1, ATT_TILE, LANES), F32),
                        pltpu.VMEM((n_pat - 1, ATT_TILE, LANES), F32)],
        compiler_params=_params("parallel", "parallel", "parallel"),
        name="attn",
    )(q, k, k, k, v, v, v, bias)


def _stage_conv_input(xc_ref, xp_ref, xn_ref, ext, has_prev, has_next):
    rows = xc_ref.shape[0]
    for sl in range(CONV_DIM // LANES):
        lanes = slice(sl * LANES, (sl + 1) * LANES)
        ext[sl, 0:HALO_ROWS, :] = jnp.where(has_prev, xp_ref[:, lanes], 0.0)
        ext[sl, HALO_ROWS:HALO_ROWS + rows, :] = xc_ref[:, lanes]
        ext[sl, HALO_ROWS + rows:, :] = jnp.where(has_next, xn_ref[:, lanes], 0.0)


def _conv_silu(ext, cw_ref, cb_ref, xa_ref, r0):
    pad = D_CONV // 2
    for sl in range(CONV_DIM // LANES):
        lanes = slice(sl * LANES, (sl + 1) * LANES)
        acc = cb_ref[:, lanes]
        for j in range(D_CONV):
            tap = ext[sl, pl.ds(r0 + (HALO_ROWS - pad + j), CHUNK, stride=1), :]
            acc = acc + tap * cw_ref[j:j + 1, lanes]
        xa_ref[pl.ds(r0, CHUNK), lanes] = _silu(acc)


def _softplus(x):
    return jnp.maximum(x, 0.0) + jnp.log(1.0 + jnp.exp(-jnp.abs(x)))


def _tri(lower):
    r = lax.broadcasted_iota(jnp.int32, (CHUNK, CHUNK), 0)
    c = lax.broadcasted_iota(jnp.int32, (CHUNK, CHUNK), 1)
    return (r >= c) if lower else (r <= c)


def _cumsum_rows(a, lower):
    return jnp.dot(_tri(lower).astype(F32), a, precision=lax.Precision.HIGHEST,
                   preferred_element_type=F32)


def _chunk_rows(i):
    return pl.ds(pl.multiple_of(i * CHUNK, CHUNK), CHUNK)


def _ssd_bwd_body(xc_ref, xp_ref, xn_ref, dt_ref, cw_ref, cb_ref, dtb_ref, a_ref, e_ref,
                  xa_ref, yoff_ref, ext, hst, *, n_steps):
    i = pl.program_id(1)
    step = n_steps - 1 - i

    @pl.when(i == 0)
    def _():
        hst[...] = jnp.zeros_like(hst)

    _stage_conv_input(xc_ref, xp_ref, xn_ref, ext, step > 0, step < n_steps - 1)

    def chunk(ci, carry):
        rs = _chunk_rows(SSD_STEP_CHUNKS - 1 - ci)
        _conv_silu(ext, cw_ref, cb_ref, xa_ref, rs.start)
        dt = _softplus(dt_ref[rs, :] + dtb_ref[...])
        rcum = _cumsum_rows(dt * a_ref[...], lower=False)
        decay_in = _expand(_hi_lo(jnp.exp(rcum)), e_ref)
        w_in = _expand(_hi_lo(jnp.exp(rcum[0:1, :] - rcum) * dt), e_ref)
        chunk_decay = _expand(_hi_lo(jnp.broadcast_to(jnp.exp(rcum[0:1, :]), (HALO_ROWS, LANES))),
                              e_ref)[0:1, :]
        for g in range(N_GROUPS):
            gl = slice(g * HEADS_PER_GROUP * HEAD_DIM, (g + 1) * HEADS_PER_GROUP * HEAD_DIM)
            bg = xa_ref[rs, D_SSM + g * D_STATE:D_SSM + (g + 1) * D_STATE].astype(BF16)
            cg = xa_ref[rs, D_SSM + (N_GROUPS + g) * D_STATE:D_SSM + (N_GROUPS + g + 1) * D_STATE].astype(BF16)
            xw = (xa_ref[rs, gl] * w_in[:, gl]).astype(BF16)
            h_in = hst[:, gl]
            yoff_ref[rs, gl] = _dot(cg, h_in.astype(BF16)) * decay_in[:, gl]
            upd = lax.dot_general(bg, xw, (((0,), (0,)), ((), ())), preferred_element_type=F32)
            hst[:, gl] = h_in * chunk_decay[:, gl] + upd
        return carry

    lax.fori_loop(0, SSD_STEP_CHUNKS, chunk, 0, unroll=True)


def _ssd_fwd_body(xa_ref, dt_ref, z_ref, yb_ref, dtb_ref, a_ref, dsk_ref, og_ref, e_ref, y_ref, hst):
    @pl.when(pl.program_id(1) == 0)
    def _():
        hst[...] = jnp.zeros_like(hst)

    lower, upper = _tri(True), _tri(False)
    lane = lax.broadcasted_iota(jnp.int32, (CHUNK, LANES), 1)

    def chunk(ci, carry):
        rs = _chunk_rows(ci)
        dt = _softplus(dt_ref[rs, :] + dtb_ref[...])
        a = dt * a_ref[...]
        fcum = _cumsum_rows(a, lower=True)
        rcum = _cumsum_rows(a, lower=False)
        decay_in = _expand(_hi_lo(jnp.exp(fcum)), e_ref)
        w_in = _expand(_hi_lo(jnp.exp(fcum[CHUNK - 1:CHUNK, :] - fcum) * dt), e_ref)
        chunk_decay = _expand(_hi_lo(jnp.broadcast_to(jnp.exp(fcum[CHUNK - 1:CHUNK, :]),
                                                       (HALO_ROWS, LANES))), e_ref)[0:1, :]
        log2_dt = jnp.log(dt) * LOG2_E
        fcol, rcol = fcum * LOG2_E, rcum * LOG2_E
        frow, rrow = (fcol - log2_dt).T, (rcol - log2_dt).T

        for g in range(N_GROUPS):
            gl = slice(g * HEADS_PER_GROUP * HEAD_DIM, (g + 1) * HEADS_PER_GROUP * HEAD_DIM)
            bg = xa_ref[rs, D_SSM + g * D_STATE:D_SSM + (g + 1) * D_STATE].astype(BF16)
            cg = xa_ref[rs, D_SSM + (N_GROUPS + g) * D_STATE:D_SSM + (N_GROUPS + g + 1) * D_STATE].astype(BF16)
            cb = lax.dot_general(cg, bg, (((1,), (1,)), ((), ())), preferred_element_type=F32)
            xw = (xa_ref[rs, gl] * w_in[:, gl]).astype(BF16)
            h_in = hst[:, gl]
            y_off = _dot(cg, h_in.astype(BF16)) * decay_in[:, gl]
            upd = lax.dot_general(bg, xw, (((0,), (0,)), ((), ())), preferred_element_type=F32)
            hst[:, gl] = h_in * chunk_decay[:, gl] + upd
            gated = []
            for pair in range(HEADS_PER_GROUP // 2):
                pl_ = slice(g * 256 + pair * LANES, g * 256 + (pair + 1) * LANES)
                xs = xa_ref[rs, pl_]
                x_pair = xs.astype(BF16)
                halves = []
                for sub in range(2):
                    h = g * HEADS_PER_GROUP + 2 * pair + sub
                    hb = N_HEADS + h
                    seg_f = jnp.where(lower, fcol[:, h:h + 1] - frow[h:h + 1, :], NEG_INF)
                    seg_b = jnp.where(upper, rcol[:, hb:hb + 1] - rrow[hb:hb + 1, :], NEG_INF)
                    mix = jnp.exp2(seg_f) + jnp.exp2(seg_b)
                    halves.append(_dot((cb * mix).astype(BF16), x_pair))
                y_diag = jnp.where(lane < HEAD_DIM, halves[0], halves[1])
                po = slice(pair * LANES, (pair + 1) * LANES)
                y = y_diag + y_off[:, po] + yb_ref[rs, pl_] + dsk_ref[:, pl_] * xs
                gated.append(y * _silu(z_ref[rs, pl_]))
            y_ref[rs, gl] = _rms(jnp.concatenate(gated, axis=1), og_ref[:, gl]).astype(BF16)
        return carry

    lax.fori_loop(0, SSD_STEP_CHUNKS, chunk, 0, unroll=True)


def _ssd(xbc, dt, z, conv_w, conv_b, dt_bias, a_neg, d_skip, out_g, e_fwd, e_bwd):
    b, s, _ = xbc.shape
    rows = SSD_STEP_CHUNKS * CHUNK
    nc = s // rows
    hpc = rows // HALO_ROWS
    last_halo = s // HALO_ROWS - 1

    def specs(cidx):
        chunk = lambda w: pl.BlockSpec((None, rows, w), lambda bi, i: (bi, cidx(i), 0))
        prev = pl.BlockSpec((None, HALO_ROWS, CONV_DIM),
                            lambda bi, i: (bi, jnp.maximum(cidx(i) * hpc - 1, 0), 0))
        nxt = pl.BlockSpec((None, HALO_ROWS, CONV_DIM),
                           lambda bi, i: (bi, jnp.minimum((cidx(i) + 1) * hpc, last_halo), 0))
        return chunk, prev, nxt

    state = pltpu.VMEM((D_STATE, D_SSM), F32)
    conv_ext = pltpu.VMEM((CONV_DIM // LANES, rows + 2 * HALO_ROWS, LANES), F32)
    small = [_const_spec(dt_bias.shape), _const_spec(a_neg.shape)]

    chunk, prev, nxt = specs(lambda i: nc - 1 - i)
    xa, yb_off = pl.pallas_call(
        functools.partial(_ssd_bwd_body, n_steps=nc),
        grid=(b, nc),
        in_specs=[chunk(CONV_DIM), prev, nxt, chunk(LANES), _const_spec(conv_w.shape),
                  _const_spec(conv_b.shape)] + small + [_const_spec(e_bwd.shape)],
        out_specs=[chunk(CONV_DIM), chunk(D_SSM)],
        out_shape=[jax.ShapeDtypeStruct((b, s, CONV_DIM), F32), jax.ShapeDtypeStruct((b, s, D_SSM), F32)],
        scratch_shapes=[conv_ext, state],
        compiler_params=_params("parallel", "arbitrary"),
        name="ssd_bwd",
    )(xbc, xbc, xbc, dt, conv_w, conv_b, dt_bias, a_neg, e_bwd)

    chunk, _, _ = specs(lambda i: i)
    return pl.pallas_call(
        _ssd_fwd_body,
        grid=(b, nc),
        in_specs=[chunk(CONV_DIM), chunk(LANES), chunk(D_SSM), chunk(D_SSM)] + small
                 + [_const_spec(d_skip.shape), _const_spec(out_g.shape), _const_spec(e_fwd.shape)],
        out_specs=chunk(D_SSM),
        out_shape=jax.ShapeDtypeStruct((b, s, D_SSM), BF16),
        scratch_shapes=[state],
        compiler_params=_params("parallel", "arbitrary"),
        name="ssd_fwd",
    )(xa, dt, z, yb_off, dt_bias, a_neg, d_skip, out_g, e_fwd)


def _outproj_ffn_body(x_ref, a_ref, y_ref, g_ref, wa_ref, ws_ref, g2_ref, wg_ref, wu_ref, wd_ref,
                      out_ref):
    attn = jnp.concatenate([a_ref[hp] for hp in range(a_ref.shape[0])], axis=1)
    attn = _rms(attn.astype(F32), g_ref[...]).astype(BF16)
    x2 = x_ref[...] + _dot(attn, wa_ref[...]) + _dot(y_ref[...], ws_ref[...])
    out_ref[...] = _half_step_ffn(x2, g2_ref, wg_ref, wu_ref, wd_ref)


def _outproj_ffn(x, attn, y, g, wa, ws, g2, wg, wu, wd, tm):
    n = x.shape[0]
    row = lambda w: pl.BlockSpec((tm, w), lambda i: (i, 0))
    sq = _const_spec((D_ATT, D_MODEL))
    pairs = pl.BlockSpec((attn.shape[0], tm, LANES), lambda i: (0, i, 0))
    return pl.pallas_call(
        _outproj_ffn_body,
        grid=(n // tm,),
        in_specs=[row(D_MODEL), pairs, row(D_SSM), _const_spec((1, D_ATT)), sq, sq,
                  _const_spec((1, D_MODEL)), _const_spec((D_MODEL, D_FF)), _const_spec((D_MODEL, D_FF)),
                  _const_spec((D_FF, D_MODEL))],
        out_specs=row(D_MODEL),
        out_shape=jax.ShapeDtypeStruct((n, D_MODEL), F32),
        compiler_params=_params("parallel"),
        name="outproj_ffn",
    )(x, attn, y, g, wa, ws, g2, wg, wu, wd)


def _t5_bucket(rel):
    nb = N_REL_BUCKETS // 2
    max_exact = nb // 2
    n = np.abs(rel)
    large = max_exact + (np.log(np.maximum(n, 1) / max_exact)
                         / math.log(REL_MAX_DIST / max_exact) * (nb - max_exact)).astype(np.int32)
    large = np.minimum(large, nb - 1)
    return (np.where(rel > 0, nb, 0) + np.where(n < max_exact, n, large)).astype(np.int32)


def _bias_tables(rel_bias):
    rel_sub = np.arange(ATT_KW)[None, :] - ATT_HALF - np.arange(ATT_BLOCK)[:, None]
    in_window = np.abs(rel_sub) <= ATT_HALF
    col = np.arange(ATT_KW)[None, :]
    tables = []
    for _, dil in ATT_PATTERNS:
        onehot = (_t5_bucket(rel_sub * dil)[:, :, None] == np.arange(N_REL_BUCKETS)).astype(np.float32)
        bias = jnp.einsum("ijk,kh->hij", jnp.asarray(onehot, BF16).astype(F32), rel_bias.astype(F32),
                          precision=lax.Precision.HIGHEST)
        tables.append(bias * LOG2_E)
    bias = jnp.stack(tables, axis=0)
    variants = []
    for v in range(4):
        keep = in_window & ((col >= ATT_HALF) | (v & 1 == 0)) & ((col < ATT_BLOCK + ATT_HALF) | (v & 2 == 0))
        variants.append(jnp.where(keep[None, None], bias, NEG_INF))
    return jnp.stack(variants, axis=0)


def _head_expander(first_row, width):
    e = np.zeros((2 * LANES, width), np.float32)
    for h in range(width // HEAD_DIM):
        e[first_row + h, h * HEAD_DIM:(h + 1) * HEAD_DIM] = 1.0
        e[LANES + first_row + h, h * HEAD_DIM:(h + 1) * HEAD_DIM] = 1.0
    return jnp.asarray(e, BF16)


def _block_diag_mean(width):
    i = np.arange(width)
    return jnp.asarray((i[:, None] // HEAD_DIM == i[None, :] // HEAD_DIM) / HEAD_DIM, BF16)


def _layer(x, p, tm):
    b, s, _ = x.shape
    n = b * s
    xf = x.reshape(n, D_MODEL)
    x1 = _ffn(xf, p["ffn1_g"], p["ffn1_wg"], p["ffn1_wu"], p["ffn1_wd"], tm)
    q, k, v, z, xbc, dt = _inproj(x1, p["mix_g"], p["wq"], p["wk"], p["wv"], p["wz"], p["wx"],
                                  p["wdt"], p["qg"], p["kg"], p["bd"], tm)
    seq = lambda a: a.reshape(b, s, a.shape[-1])
    pair_seq = lambda a: a.reshape(a.shape[0], b, s, LANES)
    attn = _attention(pair_seq(q), pair_seq(k), pair_seq(v), p["bias"])
    y = _ssd(seq(xbc), seq(dt), seq(z), p["conv_w"], p["conv_b"], p["dt_bias"], p["a_neg"],
             p["d_skip"], p["ssm_g"], p["e_fwd"], p["e_bwd"])
    out = _outproj_ffn(x1, attn.reshape(attn.shape[0], n, LANES), y.reshape(n, D_SSM), p["attn_g"],
                       p["wo_att"], p["wo_ssm"], p["ffn2_g"], p["ffn2_wg"], p["ffn2_wu"],
                       p["ffn2_wd"], tm)
    return out.reshape(b, s, D_MODEL)


def _prepare(rel_bias, ffn1_norm_g, ffn1_w_gate, ffn1_w_up, ffn1_w_down, mix_norm_g, w_in,
             q_norm_g, k_norm_g, attn_out_g, conv_w, conv_b, dt_bias, a_log, d_skip, ssm_out_g,
             w_out, ffn2_norm_g, ffn2_w_gate, ffn2_w_up, ffn2_w_down):
    row = lambda a: a.reshape(1, -1).astype(F32)
    w16 = lambda a: a.astype(BF16)
    c0, c1, c2, c3 = D_ATT, 2 * D_ATT, 3 * D_ATT, 3 * D_ATT + D_SSM
    c4 = c3 + CONV_DIM
    pad32 = lambda a: jnp.pad(a.reshape(1, 2 * N_HEADS).astype(F32), ((0, 0), (0, LANES - 2 * N_HEADS)))
    return {
        "ffn1_g": row(ffn1_norm_g), "ffn1_wg": w16(ffn1_w_gate), "ffn1_wu": w16(ffn1_w_up),
        "ffn1_wd": w16(ffn1_w_down),
        "ffn2_g": row(ffn2_norm_g), "ffn2_wg": w16(ffn2_w_gate), "ffn2_wu": w16(ffn2_w_up),
        "ffn2_wd": w16(ffn2_w_down),
        "mix_g": row(mix_norm_g),
        "wq": w16(w_in[:, :c0]), "wk": w16(w_in[:, c0:c1]), "wv": w16(w_in[:, c1:c2]),
        "wz": w16(w_in[:, c2:c3]), "wx": w16(w_in[:, c3:c4]),
        "wdt": w16(jnp.pad(w_in[:, c4:], ((0, 0), (0, LANES - 2 * N_HEADS)))),
        "qg": row(jnp.tile(q_norm_g, N_HEADS)) * (HEAD_DIM ** -0.5 * LOG2_E),
        "kg": row(jnp.tile(k_norm_g, N_HEADS)),
        "bd": _block_diag_mean(256),
        "bias": _bias_tables(rel_bias),
        "attn_g": row(attn_out_g),
        "conv_w": conv_w.astype(F32), "conv_b": row(conv_b),
        "dt_bias": pad32(dt_bias), "a_neg": pad32(-jnp.exp(a_log.astype(F32))),
        "d_skip": row(jnp.repeat(d_skip, HEAD_DIM)), "ssm_g": row(ssm_out_g),
        "e_fwd": _head_expander(0, D_SSM), "e_bwd": _head_expander(N_HEADS, D_SSM),
        "wo_att": w16(w_out[:D_ATT]), "wo_ssm": w16(w_out[D_ATT:]),
    }


def _trunk(x, layers, tm=512):
    for p in layers:
        x = _layer(x, p, tm)
    return x


def kernel(x_prompt, x_sample, rel_bias, ffn1_norm_g, ffn1_w_gate, ffn1_w_up, ffn1_w_down, mix_norm_g, w_in, q_norm_g, k_norm_g, attn_out_g, conv_w, conv_b, dt_bias, a_log, d_skip, ssm_out_g, w_out, ffn2_norm_g, ffn2_w_gate, ffn2_w_up, ffn2_w_down):
    per_layer = (ffn1_norm_g, ffn1_w_gate, ffn1_w_up, ffn1_w_down, mix_norm_g, w_in, q_norm_g,
                 k_norm_g, attn_out_g, conv_w, conv_b, dt_bias, a_log, d_skip, ssm_out_g, w_out,
                 ffn2_norm_g, ffn2_w_gate, ffn2_w_up, ffn2_w_down)
    layers = [_prepare(rel_bias, *(a[l] for a in per_layer)) for l in range(ffn1_norm_g.shape[0])]
    return (_trunk(x_prompt, layers), _trunk(x_sample, layers))
```

```python
import functools
import math

import numpy as np
import jax
import jax.numpy as jnp
from jax import lax
from jax.experimental import pallas as pl
from jax.experimental.pallas import tpu as pltpu

D_MODEL = 1024
D_ATT = 1024
D_SSM = 1024
HEAD_DIM = 64
N_HEADS = 16
ATT_PATTERNS = ((128, 1), (512, 4), (2048, 16))
ATT_BLOCK = 128
ATT_HALF = 64
ATT_KW = ATT_BLOCK + 2 * ATT_HALF
ATT_MAX_DIL = max(d for _, d in ATT_PATTERNS)
ATT_TILE = ATT_BLOCK * ATT_MAX_DIL
ATT_REACH = ATT_HALF * ATT_MAX_DIL
ATT_WINDOW = ATT_TILE + 2 * ATT_REACH
ATT_STAGE = 4
assert tuple(d for _, d in ATT_PATTERNS) == (1, ATT_STAGE, ATT_STAGE * ATT_STAGE)
N_REL_BUCKETS = 32
REL_MAX_DIST = 1024
N_GROUPS = 4
HEADS_PER_GROUP = 4
D_STATE = 128
D_CONV = 5
CHUNK = 128
SSD_STEP_CHUNKS = 4
CONV_DIM = D_SSM + 2 * N_GROUPS * D_STATE
D_FF = 2816
EPS = 1e-6
NEG_INF = -1e30
LOG2_E = 1.4426950408889634
EXP2_SAFE_RANGE = 100.0
QK_BOUND_SLACK = 1.05

LANES = 128
HALO_ROWS = 8
VMEM_LIMIT = 56 * 1024 * 1024

F32 = jnp.float32
BF16 = jnp.bfloat16


def _params(*sem):
    return pltpu.CompilerParams(dimension_semantics=sem, vmem_limit_bytes=VMEM_LIMIT)


def _const_spec(shape):
    n = len(shape)
    return pl.BlockSpec(shape, lambda *_: (0,) * n, pipeline_mode=pl.Buffered(1))


def _rms(x, g):
    ms = jnp.mean(x * x, axis=-1, keepdims=True)
    return x * lax.rsqrt(ms + EPS) * g


def _silu(x):
    h = 0.5 * x
    return h + h * jnp.tanh(h)


def _dot(a, b):
    return jnp.dot(a, b, preferred_element_type=F32)


def _hi_lo(vals):
    hi = vals.astype(BF16)
    lo = (vals - hi.astype(F32)).astype(BF16)
    return jnp.concatenate([hi, lo], axis=1)


def _expand(split, e_ref, cols=slice(None)):
    return _dot(split, e_ref[:, cols])


def _half_step_ffn(x, g_ref, wg_ref, wu_ref, wd_ref):
    h = _rms(x, g_ref[...]).astype(BF16)
    gate = _dot(h, wg_ref[...])
    up = _dot(h, wu_ref[...])
    act = (_silu(gate) * up).astype(BF16)
    return x + 0.5 * _dot(act, wd_ref[...])


def _ffn_body(x_ref, g_ref, wg_ref, wu_ref, wd_ref, o_ref):
    o_ref[...] = _half_step_ffn(x_ref[...], g_ref, wg_ref, wu_ref, wd_ref)


def _ffn(x, g, wg, wu, wd, tm):
    n = x.shape[0]
    row = pl.BlockSpec((tm, D_MODEL), lambda i: (i, 0))
    return pl.pallas_call(
        _ffn_body,
        grid=(n // tm,),
        in_specs=[row, _const_spec((1, D_MODEL)), _const_spec((D_MODEL, D_FF)),
                  _const_spec((D_MODEL, D_FF)), _const_spec((D_FF, D_MODEL))],
        out_specs=row,
        out_shape=jax.ShapeDtypeStruct((n, D_MODEL), F32),
        compiler_params=_params("parallel"),
        name="ffn",
    )(x, g, wg, wu, wd)


def _inproj_body(x_ref, g_ref, wq_ref, wk_ref, wv_ref, wz_ref, wx_ref, wdt_ref, qg_ref, kg_ref,
                 bd_ref, q_ref, k_ref, v_ref, z_ref, xbc_ref, dt_ref):
    h = _rms(x_ref[...], g_ref[...]).astype(BF16)

    def head_norm(t, gain):
        t2 = (t * t).astype(BF16)
        w = bd_ref.shape[0]
        ms = jnp.concatenate([_dot(t2[:, j * w:(j + 1) * w], bd_ref[...])
                              for j in range(D_ATT // w)], axis=1)
        return t * lax.rsqrt(ms + EPS) * gain

    def put_pairs(ref, val):
        for hp in range(N_HEADS // 2):
            ref[hp] = val[:, hp * LANES:(hp + 1) * LANES]

    put_pairs(q_ref, head_norm(_dot(h, wq_ref[...]), qg_ref[...]))
    put_pairs(k_ref, head_norm(_dot(h, wk_ref[...]), kg_ref[...]))
    put_pairs(v_ref, _dot(h, wv_ref[...]))
    z_ref[...] = _dot(h, wz_ref[...])
    xbc_ref[...] = _dot(h, wx_ref[...])
    dt_ref[...] = _dot(h, wdt_ref[...])


def _inproj(x, g, wq, wk, wv, wz, wx, wdt, qg, kg, bd, tm):
    n = x.shape[0]
    row = lambda w: pl.BlockSpec((tm, w), lambda i: (i, 0))
    sq = _const_spec((D_MODEL, D_ATT))
    out = lambda w: jax.ShapeDtypeStruct((n, w), F32)
    pairs = pl.BlockSpec((N_HEADS // 2, tm, LANES), lambda i: (0, i, 0))
    pairs_out = jax.ShapeDtypeStruct((N_HEADS // 2, n, LANES), F32)
    return pl.pallas_call(
        _inproj_body,
        grid=(n // tm,),
        in_specs=[row(D_MODEL), _const_spec((1, D_MODEL)), sq, sq, sq, sq,
                  _const_spec((D_MODEL, CONV_DIM)), _const_spec((D_MODEL, LANES)),
                  _const_spec((1, D_ATT)), _const_spec((1, D_ATT)), _const_spec(bd.shape)],
        out_specs=[pairs, pairs, pairs, row(D_SSM), row(CONV_DIM), row(LANES)],
        out_shape=[pairs_out, pairs_out, pairs_out, out(D_SSM), out(CONV_DIM), out(LANES)],
        compiler_params=_params("parallel"),
        name="inproj",
    )(x, g, wq, wk, wv, wz, wx, wdt, qg, kg, bd)


def _rows(start, size, dil):
    return pl.ds(start, size) if dil == 1 else pl.ds(start, size, stride=dil)


def _regroup_keys(prev, cur, nxt, tmp, dst):
    s4 = ATT_STAGE
    lo, hi = ATT_REACH - ATT_HALF, ATT_REACH + ATT_TILE + ATT_HALF
    dst[0, lo:ATT_REACH, :] = prev[lo:ATT_REACH, :].astype(BF16)
    dst[0, ATT_REACH:ATT_REACH + ATT_TILE, :] = cur[...].astype(BF16)
    dst[0, ATT_REACH + ATT_TILE:hi, :] = nxt[0:ATT_HALF, :].astype(BF16)
    h4, t4 = ATT_REACH // s4, ATT_TILE // s4
    for r in range(s4):
        tmp[r, 0:h4, :] = prev[pl.ds(r, h4, stride=s4), :]
        tmp[r, h4:h4 + t4, :] = cur[pl.ds(r, t4, stride=s4), :]
        tmp[r, h4 + t4:, :] = nxt[pl.ds(r, h4, stride=s4), :]
    lo, hi = h4 - ATT_HALF, h4 + t4 + ATT_HALF
    for r in range(s4):
        dst[1, r * (ATT_WINDOW // s4) + lo:r * (ATT_WINDOW // s4) + hi, :] = tmp[r, lo:hi, :].astype(BF16)
    per = ATT_WINDOW // (s4 * s4)
    for r in range(s4 * s4):
        dst[2, r * per:(r + 1) * per, :] = tmp[r % s4, pl.ds(r // s4, per, stride=s4), :].astype(BF16)


def _attn_body(q_ref, kc_ref, kp_ref, kn_ref, vc_ref, vp_ref, vn_ref, bias_ref, out_ref,
               kd, vd, tmp, o_scr, lse_scr, *, seq_len, row_max):
    t0 = pl.program_id(2) * ATT_TILE
    _regroup_keys(kp_ref, kc_ref, kn_ref, tmp, kd)
    _regroup_keys(vp_ref, vc_ref, vn_ref, tmp, vd)

    first_head = lax.broadcasted_iota(jnp.int32, (ATT_BLOCK, LANES), 1) < HEAD_DIM
    first_head_kw = lax.broadcasted_iota(jnp.int32, (ATT_KW, LANES), 1) < HEAD_DIM
    head_ones = (first_head_kw.astype(BF16), (~first_head_kw).astype(BF16))

    def block(p, dil, r, j):
        blocks_per_residue = ATT_TILE // (dil * ATT_BLOCK)
        q0 = j * (ATT_BLOCK * dil) + r
        k0 = r * (ATT_WINDOW // dil) + ATT_REACH // dil - ATT_HALF + j * ATT_BLOCK
        variant = 0
        if j == 0:
            variant = variant + (t0 == 0).astype(jnp.int32)
        if j == blocks_per_residue - 1:
            variant = variant + 2 * (t0 + ATT_TILE == seq_len).astype(jnp.int32)
        qp = q_ref[_rows(q0, ATT_BLOCK, dil), :].astype(BF16)
        kp = kd[p, k0:k0 + ATT_KW, :]
        vp = vd[p, k0:k0 + ATT_KW, :]
        zq, zv = jnp.zeros_like(qp), jnp.zeros_like(vp)
        q2 = jnp.concatenate([jnp.where(first_head, qp, zq), jnp.where(first_head, zq, qp)], axis=0)
        s2 = lax.dot_general(q2, kp, (((1,), (1,)), ((), ())), preferred_element_type=F32)
        es, ms = [], []
        for sub in range(2):
            s = s2[sub * ATT_BLOCK:(sub + 1) * ATT_BLOCK, :] + bias_ref[variant, p, sub]
            if row_max:
                m = jnp.max(s, axis=-1, keepdims=True)
                s = s - m
                ms.append(m)
            es.append(jnp.exp2(s).astype(BF16))
        v2 = jnp.concatenate([
            jnp.concatenate([jnp.where(first_head_kw, vp, zv), head_ones[0]], axis=1),
            jnp.concatenate([jnp.where(first_head_kw, zv, vp), head_ones[1]], axis=1)], axis=0)
        ol = _dot(jnp.concatenate(es, axis=1), v2)
        l = ol[:, LANES:]
        if not row_max:
            return ol[:, :LANES], l
        return ol[:, :LANES] * (1.0 / l), jnp.where(first_head, ms[0], ms[1]) + jnp.log2(l)

    for p in range(len(ATT_PATTERNS) - 1, 0, -1):
        dil = ATT_PATTERNS[p][1]
        blocks_per_residue = ATT_TILE // (dil * ATT_BLOCK)
        for idx in range(ATT_TILE // ATT_BLOCK):
            r, j = idx // blocks_per_residue, idx % blocks_per_residue
            rows = _rows(j * (ATT_BLOCK * dil) + r, ATT_BLOCK, dil)
            o_scr[p - 1, rows, :], lse_scr[p - 1, rows, :] = block(p, dil, r, j)
    for j in range(ATT_TILE // ATT_BLOCK):
        rows = pl.ds(j * ATT_BLOCK, ATT_BLOCK)
        o1, l1 = block(0, 1, 0, j)
        l2, l3 = lse_scr[0, rows, :], lse_scr[1, rows, :]
        if not row_max:
            mixed = o1 + o_scr[0, rows, :] + o_scr[1, rows, :]
            out_ref[rows, :] = (mixed * (1.0 / (l1 + l2 + l3))).astype(BF16)
            continue
        m = jnp.maximum(jnp.maximum(l1, l2), l3)
        e1, e2, e3 = jnp.exp2(l1 - m), jnp.exp2(l2 - m), jnp.exp2(l3 - m)
        mixed = e1 * o1 + e2 * o_scr[0, rows, :] + e3 * o_scr[1, rows, :]
        out_ref[rows, :] = (mixed * (1.0 / (e1 + e2 + e3))).astype(BF16)


def _attention(q, k, v, bias, logit_bound, bias_lo, bias_hi):
    shift = logit_bound + bias_hi
    narrow = 2.0 * logit_bound + (bias_hi - bias_lo) < EXP2_SAFE_RANGE
    shifted_bias = jnp.where(bias > 0.5 * NEG_INF, bias - shift, NEG_INF)
    return lax.cond(narrow,
                    lambda: _attention_call(q, k, v, shifted_bias, row_max=False),
                    lambda: _attention_call(q, k, v, bias, row_max=True))


def _attention_call(q, k, v, bias, row_max):
    n_pairs, b, s, _ = q.shape
    n_pat = len(ATT_PATTERNS)
    last_halo = s // ATT_REACH - 1
    halo_per_tile = ATT_TILE // ATT_REACH
    cur = pl.BlockSpec((None, None, ATT_TILE, LANES), lambda hp, bi, t: (hp, bi, t, 0))
    prev = pl.BlockSpec((None, None, ATT_REACH, LANES),
                        lambda hp, bi, t: (hp, bi, jnp.maximum(t * halo_per_tile - 1, 0), 0))
    nxt = pl.BlockSpec((None, None, ATT_REACH, LANES),
                       lambda hp, bi, t: (hp, bi, jnp.minimum((t + 1) * halo_per_tile, last_halo), 0))
    bias_spec = pl.BlockSpec((bias.shape[0], n_pat, 2, ATT_BLOCK, ATT_KW),
                             lambda hp, bi, t: (0, 0, hp, 0, 0))
    return pl.pallas_call(
        functools.partial(_attn_body, seq_len=s, row_max=row_max),
        grid=(n_pairs, b, s // ATT_TILE),
        in_specs=[cur, cur, prev, nxt, cur, prev, nxt, bias_spec],
        out_specs=cur,
        out_shape=jax.ShapeDtypeStruct((n_pairs, b, s, LANES), BF16),
        scratch_shapes=[pltpu.VMEM((n_pat, ATT_WINDOW, LANES), BF16),
                        pltpu.VMEM((n_pat, ATT_WINDOW, LANES), BF16),
                        pltpu.VMEM((ATT_STAGE, ATT_WINDOW // ATT_STAGE, LANES), F32),
                        pltpu.VMEM((n_pat - 1, ATT_TILE, LANES), F32),
                        pltpu.VMEM((n_pat - 1, ATT_TILE, LANES), F32)],
        compiler_params=_params("parallel", "parallel", "parallel"),
        name="attn_rowmax" if row_max else "attn",
    )(q, k, k, k, v, v, v, bias)


def _stage_conv_input(xc_ref, xp_ref, xn_ref, ext, has_prev, has_next):
    rows = xc_ref.shape[0]
    for sl in range(CONV_DIM // LANES):
        lanes = slice(sl * LANES, (sl + 1) * LANES)
        ext[sl, 0:HALO_ROWS, :] = jnp.where(has_prev, xp_ref[:, lanes], 0.0)
        ext[sl, HALO_ROWS:HALO_ROWS + rows, :] = xc_ref[:, lanes]
        ext[sl, HALO_ROWS + rows:, :] = jnp.where(has_next, xn_ref[:, lanes], 0.0)


def _conv_silu(ext, cw_ref, cb_ref, xa_ref, r0):
    pad = D_CONV // 2
    for sl in range(CONV_DIM // LANES):
        lanes = slice(sl * LANES, (sl + 1) * LANES)
        acc = cb_ref[:, lanes]
        for j in range(D_CONV):
            tap = ext[sl, pl.ds(r0 + (HALO_ROWS - pad + j), CHUNK, stride=1), :]
            acc = acc + tap * cw_ref[j:j + 1, lanes]
        xa_ref[pl.ds(r0, CHUNK), lanes] = _silu(acc)


def _softplus(x):
    return jnp.maximum(x, 0.0) + jnp.log(1.0 + jnp.exp(-jnp.abs(x)))


def _tri(lower):
    r = lax.broadcasted_iota(jnp.int32, (CHUNK, CHUNK), 0)
    c = lax.broadcasted_iota(jnp.int32, (CHUNK, CHUNK), 1)
    return (r >= c) if lower else (r <= c)


def _cumsum_rows(a, lower):
    return jnp.dot(_tri(lower).astype(F32), a, precision=lax.Precision.HIGHEST,
                   preferred_element_type=F32)


def _chunk_rows(i):
    return pl.ds(pl.multiple_of(i * CHUNK, CHUNK), CHUNK)


def _ssd_bwd_body(xc_ref, xp_ref, xn_ref, dt_ref, cw_ref, cb_ref, dtb_ref, a_ref, e_ref,
                  xa_ref, yoff_ref, ext, hst, *, n_steps):
    i = pl.program_id(1)
    step = n_steps - 1 - i

    @pl.when(i == 0)
    def _():
        hst[...] = jnp.zeros_like(hst)

    _stage_conv_input(xc_ref, xp_ref, xn_ref, ext, step > 0, step < n_steps - 1)

    def chunk(ci, carry):
        rs = _chunk_rows(SSD_STEP_CHUNKS - 1 - ci)
        _conv_silu(ext, cw_ref, cb_ref, xa_ref, rs.start)
        dt = _softplus(dt_ref[rs, :] + dtb_ref[...])
        rcum = _cumsum_rows(dt * a_ref[...], lower=False)
        decay_in = _expand(_hi_lo(jnp.exp(rcum)), e_ref)
        w_in = _expand(_hi_lo(jnp.exp(rcum[0:1, :] - rcum) * dt), e_ref)
        chunk_decay = _expand(_hi_lo(jnp.broadcast_to(jnp.exp(rcum[0:1, :]), (HALO_ROWS, LANES))),
                              e_ref)[0:1, :]
        for g in range(N_GROUPS):
            gl = slice(g * HEADS_PER_GROUP * HEAD_DIM, (g + 1) * HEADS_PER_GROUP * HEAD_DIM)
            bg = xa_ref[rs, D_SSM + g * D_STATE:D_SSM + (g + 1) * D_STATE].astype(BF16)
            cg = xa_ref[rs, D_SSM + (N_GROUPS + g) * D_STATE:D_SSM + (N_GROUPS + g + 1) * D_STATE].astype(BF16)
            xw = (xa_ref[rs, gl] * w_in[:, gl]).astype(BF16)
            h_in = hst[:, gl]
            yoff_ref[rs, gl] = _dot(cg, h_in.astype(BF16)) * decay_in[:, gl]
            upd = lax.dot_general(bg, xw, (((0,), (0,)), ((), ())), preferred_element_type=F32)
            hst[:, gl] = h_in * chunk_decay[:, gl] + upd
        return carry

    lax.fori_loop(0, SSD_STEP_CHUNKS, chunk, 0, unroll=True)


def _ssd_fwd_body(xa_ref, dt_ref, z_ref, yb_ref, dtb_ref, a_ref, dsk_ref, og_ref, e_ref, y_ref, hst):
    @pl.when(pl.program_id(1) == 0)
    def _():
        hst[...] = jnp.zeros_like(hst)

    lower, upper = _tri(True), _tri(False)
    lane = lax.broadcasted_iota(jnp.int32, (CHUNK, LANES), 1)

    def chunk(ci, carry):
        rs = _chunk_rows(ci)
        dt = _softplus(dt_ref[rs, :] + dtb_ref[...])
        a = dt * a_ref[...]
        fcum = _cumsum_rows(a, lower=True)
        rcum = _cumsum_rows(a, lower=False)
        decay_in = _expand(_hi_lo(jnp.exp(fcum)), e_ref)
        w_in = _expand(_hi_lo(jnp.exp(fcum[CHUNK - 1:CHUNK, :] - fcum) * dt), e_ref)
        chunk_decay = _expand(_hi_lo(jnp.broadcast_to(jnp.exp(fcum[CHUNK - 1:CHUNK, :]),
                                                       (HALO_ROWS, LANES))), e_ref)[0:1, :]
        log2_dt = jnp.log(dt) * LOG2_E
        fcol, rcol = fcum * LOG2_E, rcum * LOG2_E
        frow, rrow = (fcol - log2_dt).T, (rcol - log2_dt).T

        for g in range(N_GROUPS):
            gl = slice(g * HEADS_PER_GROUP * HEAD_DIM, (g + 1) * HEADS_PER_GROUP * HEAD_DIM)
            bg = xa_ref[rs, D_SSM + g * D_STATE:D_SSM + (g + 1) * D_STATE].astype(BF16)
            cg = xa_ref[rs, D_SSM + (N_GROUPS + g) * D_STATE:D_SSM + (N_GROUPS + g + 1) * D_STATE].astype(BF16)
            cb = lax.dot_general(cg, bg, (((1,), (1,)), ((), ())), preferred_element_type=F32)
            xw = (xa_ref[rs, gl] * w_in[:, gl]).astype(BF16)
            h_in = hst[:, gl]
            y_off = _dot(cg, h_in.astype(BF16)) * decay_in[:, gl]
            upd = lax.dot_general(bg, xw, (((0,), (0,)), ((), ())), preferred_element_type=F32)
            hst[:, gl] = h_in * chunk_decay[:, gl] + upd
            gated = []
            for pair in range(HEADS_PER_GROUP // 2):
                pl_ = slice(g * 256 + pair * LANES, g * 256 + (pair + 1) * LANES)
                xs = xa_ref[rs, pl_]
                x_pair = xs.astype(BF16)
                halves = []
                for sub in range(2):
                    h = g * HEADS_PER_GROUP + 2 * pair + sub
                    hb = N_HEADS + h
                    seg_f = jnp.where(lower, fcol[:, h:h + 1] - frow[h:h + 1, :], NEG_INF)
                    seg_b = jnp.where(upper, rcol[:, hb:hb + 1] - rrow[hb:hb + 1, :], NEG_INF)
                    mix = jnp.exp2(seg_f) + jnp.exp2(seg_b)
                    halves.append(_dot((cb * mix).astype(BF16), x_pair))
                y_diag = jnp.where(lane < HEAD_DIM, halves[0], halves[1])
                po = slice(pair * LANES, (pair + 1) * LANES)
                y = y_diag + y_off[:, po] + yb_ref[rs, pl_] + dsk_ref[:, pl_] * xs
                gated.append(y * _silu(z_ref[rs, pl_]))
            y_ref[rs, gl] = _rms(jnp.concatenate(gated, axis=1), og_ref[:, gl]).astype(BF16)
        return carry

    lax.fori_loop(0, SSD_STEP_CHUNKS, chunk, 0, unroll=True)


def _ssd(xbc, dt, z, conv_w, conv_b, dt_bias, a_neg, d_skip, out_g, e_fwd, e_bwd):
    b, s, _ = xbc.shape
    rows = SSD_STEP_CHUNKS * CHUNK
    nc = s // rows
    hpc = rows // HALO_ROWS
    last_halo = s // HALO_ROWS - 1

    def specs(cidx):
        chunk = lambda w: pl.BlockSpec((None, rows, w), lambda bi, i: (bi, cidx(i), 0))
        prev = pl.BlockSpec((None, HALO_ROWS, CONV_DIM),
                            lambda bi, i: (bi, jnp.maximum(cidx(i) * hpc - 1, 0), 0))
        nxt = pl.BlockSpec((None, HALO_ROWS, CONV_DIM),
                           lambda bi, i: (bi, jnp.minimum((cidx(i) + 1) * hpc, last_halo), 0))
        return chunk, prev, nxt

    state = pltpu.VMEM((D_STATE, D_SSM), F32)
    conv_ext = pltpu.VMEM((CONV_DIM // LANES, rows + 2 * HALO_ROWS, LANES), F32)
    small = [_const_spec(dt_bias.shape), _const_spec(a_neg.shape)]

    chunk, prev, nxt = specs(lambda i: nc - 1 - i)
    xa, yb_off = pl.pallas_call(
        functools.partial(_ssd_bwd_body, n_steps=nc),
        grid=(b, nc),
        in_specs=[chunk(CONV_DIM), prev, nxt, chunk(LANES), _const_spec(conv_w.shape),
                  _const_spec(conv_b.shape)] + small + [_const_spec(e_bwd.shape)],
        out_specs=[chunk(CONV_DIM), chunk(D_SSM)],
        out_shape=[jax.ShapeDtypeStruct((b, s, CONV_DIM), F32), jax.ShapeDtypeStruct((b, s, D_SSM), F32)],
        scratch_shapes=[conv_ext, state],
        compiler_params=_params("parallel", "arbitrary"),
        name="ssd_bwd",
    )(xbc, xbc, xbc, dt, conv_w, conv_b, dt_bias, a_neg, e_bwd)

    chunk, _, _ = specs(lambda i: i)
    return pl.pallas_call(
        _ssd_fwd_body,
        grid=(b, nc),
        in_specs=[chunk(CONV_DIM), chunk(LANES), chunk(D_SSM), chunk(D_SSM)] + small
                 + [_const_spec(d_skip.shape), _const_spec(out_g.shape), _const_spec(e_fwd.shape)],
        out_specs=chunk(D_SSM),
        out_shape=jax.ShapeDtypeStruct((b, s, D_SSM), BF16),
        scratch_shapes=[state],
        compiler_params=_params("parallel", "arbitrary"),
        name="ssd_fwd",
    )(xa, dt, z, yb_off, dt_bias, a_neg, d_skip, out_g, e_fwd)


def _outproj_ffn_body(x_ref, a_ref, y_ref, g_ref, wa_ref, ws_ref, g2_ref, wg_ref, wu_ref, wd_ref,
                      out_ref):
    attn = jnp.concatenate([a_ref[hp] for hp in range(a_ref.shape[0])], axis=1)
    attn = _rms(attn.astype(F32), g_ref[...]).astype(BF16)
    x2 = x_ref[...] + _dot(attn, wa_ref[...]) + _dot(y_ref[...], ws_ref[...])
    out_ref[...] = _half_step_ffn(x2, g2_ref, wg_ref, wu_ref, wd_ref)


def _outproj_ffn(x, attn, y, g, wa, ws, g2, wg, wu, wd, tm):
    n = x.shape[0]
    row = lambda w: pl.BlockSpec((tm, w), lambda i: (i, 0))
    sq = _const_spec((D_ATT, D_MODEL))
    pairs = pl.BlockSpec((attn.shape[0], tm, LANES), lambda i: (0, i, 0))
    return pl.pallas_call(
        _outproj_ffn_body,
        grid=(n // tm,),
        in_specs=[row(D_MODEL), pairs, row(D_SSM), _const_spec((1, D_ATT)), sq, sq,
                  _const_spec((1, D_MODEL)), _const_spec((D_MODEL, D_FF)), _const_spec((D_MODEL, D_FF)),
                  _const_spec((D_FF, D_MODEL))],
        out_specs=row(D_MODEL),
        out_shape=jax.ShapeDtypeStruct((n, D_MODEL), F32),
        compiler_params=_params("parallel"),
        name="outproj_ffn",
    )(x, attn, y, g, wa, ws, g2, wg, wu, wd)


def _t5_bucket(rel):
    nb = N_REL_BUCKETS // 2
    max_exact = nb // 2
    n = np.abs(rel)
    large = max_exact + (np.log(np.maximum(n, 1) / max_exact)
                         / math.log(REL_MAX_DIST / max_exact) * (nb - max_exact)).astype(np.int32)
    large = np.minimum(large, nb - 1)
    return (np.where(rel > 0, nb, 0) + np.where(n < max_exact, n, large)).astype(np.int32)


def _bias_tables(rel_bias):
    rel_sub = np.arange(ATT_KW)[None, :] - ATT_HALF - np.arange(ATT_BLOCK)[:, None]
    in_window = np.abs(rel_sub) <= ATT_HALF
    col = np.arange(ATT_KW)[None, :]
    tables = []
    for _, dil in ATT_PATTERNS:
        onehot = (_t5_bucket(rel_sub * dil)[:, :, None] == np.arange(N_REL_BUCKETS)).astype(np.float32)
        bias = jnp.einsum("ijk,kh->hij", jnp.asarray(onehot, BF16).astype(F32), rel_bias.astype(F32),
                          precision=lax.Precision.HIGHEST)
        tables.append(bias * LOG2_E)
    bias = jnp.stack(tables, axis=0)
    variants = []
    for v in range(4):
        keep = in_window & ((col >= ATT_HALF) | (v & 1 == 0)) & ((col < ATT_BLOCK + ATT_HALF) | (v & 2 == 0))
        variants.append(jnp.where(keep[None, None], bias, NEG_INF))
    return jnp.stack(variants, axis=0)


def _head_expander(first_row, width):
    e = np.zeros((2 * LANES, width), np.float32)
    for h in range(width // HEAD_DIM):
        e[first_row + h, h * HEAD_DIM:(h + 1) * HEAD_DIM] = 1.0
        e[LANES + first_row + h, h * HEAD_DIM:(h + 1) * HEAD_DIM] = 1.0
    return jnp.asarray(e, BF16)


def _block_diag_mean(width):
    i = np.arange(width)
    return jnp.asarray((i[:, None] // HEAD_DIM == i[None, :] // HEAD_DIM) / HEAD_DIM, BF16)


def _layer(x, p, tm):
    b, s, _ = x.shape
    n = b * s
    xf = x.reshape(n, D_MODEL)
    x1 = _ffn(xf, p["ffn1_g"], p["ffn1_wg"], p["ffn1_wu"], p["ffn1_wd"], tm)
    q, k, v, z, xbc, dt = _inproj(x1, p["mix_g"], p["wq"], p["wk"], p["wv"], p["wz"], p["wx"],
                                  p["wdt"], p["qg"], p["kg"], p["bd"], tm)
    seq = lambda a: a.reshape(b, s, a.shape[-1])
    pair_seq = lambda a: a.reshape(a.shape[0], b, s, LANES)
    attn = _attention(pair_seq(q), pair_seq(k), pair_seq(v), p["bias"], p["logit_bound"], p["bias_lo"],
                      p["bias_hi"])
    y = _ssd(seq(xbc), seq(dt), seq(z), p["conv_w"], p["conv_b"], p["dt_bias"], p["a_neg"],
             p["d_skip"], p["ssm_g"], p["e_fwd"], p["e_bwd"])
    out = _outproj_ffn(x1, attn.reshape(attn.shape[0], n, LANES), y.reshape(n, D_SSM), p["attn_g"],
                       p["wo_att"], p["wo_ssm"], p["ffn2_g"], p["ffn2_wg"], p["ffn2_wu"],
                       p["ffn2_wd"], tm)
    return out.reshape(b, s, D_MODEL)


def _prepare(rel_bias, ffn1_norm_g, ffn1_w_gate, ffn1_w_up, ffn1_w_down, mix_norm_g, w_in,
             q_norm_g, k_norm_g, attn_out_g, conv_w, conv_b, dt_bias, a_log, d_skip, ssm_out_g,
             w_out, ffn2_norm_g, ffn2_w_gate, ffn2_w_up, ffn2_w_down):
    row = lambda a: a.reshape(1, -1).astype(F32)
    w16 = lambda a: a.astype(BF16)
    c0, c1, c2, c3 = D_ATT, 2 * D_ATT, 3 * D_ATT, 3 * D_ATT + D_SSM
    c4 = c3 + CONV_DIM
    pad32 = lambda a: jnp.pad(a.reshape(1, 2 * N_HEADS).astype(F32), ((0, 0), (0, LANES - 2 * N_HEADS)))
    return {
        "ffn1_g": row(ffn1_norm_g), "ffn1_wg": w16(ffn1_w_gate), "ffn1_wu": w16(ffn1_w_up),
        "ffn1_wd": w16(ffn1_w_down),
        "ffn2_g": row(ffn2_norm_g), "ffn2_wg": w16(ffn2_w_gate), "ffn2_wu": w16(ffn2_w_up),
        "ffn2_wd": w16(ffn2_w_down),
        "mix_g": row(mix_norm_g),
        "wq": w16(w_in[:, :c0]), "wk": w16(w_in[:, c0:c1]), "wv": w16(w_in[:, c1:c2]),
        "wz": w16(w_in[:, c2:c3]), "wx": w16(w_in[:, c3:c4]),
        "wdt": w16(jnp.pad(w_in[:, c4:], ((0, 0), (0, LANES - 2 * N_HEADS)))),
        "qg": row(jnp.tile(q_norm_g, N_HEADS)) * (HEAD_DIM ** -0.5 * LOG2_E),
        "kg": row(jnp.tile(k_norm_g, N_HEADS)),
        "bd": _block_diag_mean(256),
        "bias": _bias_tables(rel_bias),
        "logit_bound": (QK_BOUND_SLACK * HEAD_DIM * HEAD_DIM ** -0.5 * LOG2_E
                        * jnp.max(jnp.abs(q_norm_g)) * jnp.max(jnp.abs(k_norm_g))).astype(F32),
        "bias_lo": (jnp.min(rel_bias) * LOG2_E).astype(F32),
        "bias_hi": (jnp.max(rel_bias) * LOG2_E).astype(F32),
        "attn_g": row(attn_out_g),
        "conv_w": conv_w.astype(F32), "conv_b": row(conv_b),
        "dt_bias": pad32(dt_bias), "a_neg": pad32(-jnp.exp(a_log.astype(F32))),
        "d_skip": row(jnp.repeat(d_skip, HEAD_DIM)), "ssm_g": row(ssm_out_g),
        "e_fwd": _head_expander(0, D_SSM), "e_bwd": _head_expander(N_HEADS, D_SSM),
        "wo_att": w16(w_out[:D_ATT]), "wo_ssm": w16(w_out[D_ATT:]),
    }


def _trunk(x, layers, tm=512):
    for p in layers:
        x = _layer(x, p, tm)
    return x


def kernel(x_prompt, x_sample, rel_bias, ffn1_norm_g, ffn1_w_gate, ffn1_w_up, ffn1_w_down, mix_norm_g, w_in, q_norm_g, k_norm_g, attn_out_g, conv_w, conv_b, dt_bias, a_log, d_skip, ssm_out_g, w_out, ffn2_norm_g, ffn2_w_gate, ffn2_w_up, ffn2_w_down):
    per_layer = (ffn1_norm_g, ffn1_w_gate, ffn1_w_up, ffn1_w_down, mix_norm_g, w_in, q_norm_g,
                 k_norm_g, attn_out_g, conv_w, conv_b, dt_bias, a_log, d_skip, ssm_out_g, w_out,
                 ffn2_norm_g, ffn2_w_gate, ffn2_w_up, ffn2_w_down)
    layers = [_prepare(rel_bias, *(a[l] for a in per_layer)) for l in range(ffn1_norm_g.shape[0])]
    return (_trunk(x_prompt, layers), _trunk(x_sample, layers))
```

```python
import functools
import math

import numpy as np
import jax
import jax.numpy as jnp
from jax import lax
from jax.experimental import pallas as pl
from jax.experimental.pallas import tpu as pltpu

D_MODEL = 1024
D_ATT = 1024
D_SSM = 1024
HEAD_DIM = 64
N_HEADS = 16
ATT_PATTERNS = ((128, 1), (512, 4), (2048, 16))
ATT_BLOCK = 128
ATT_HALF = 64
ATT_KW = ATT_BLOCK + 2 * ATT_HALF
ATT_MAX_DIL = max(d for _, d in ATT_PATTERNS)
ATT_TILE = ATT_BLOCK * ATT_MAX_DIL
ATT_REACH = ATT_HALF * ATT_MAX_DIL
ATT_WINDOW = ATT_TILE + 2 * ATT_REACH
ATT_STAGE = 4
assert tuple(d for _, d in ATT_PATTERNS) == (1, ATT_STAGE, ATT_STAGE * ATT_STAGE)
N_REL_BUCKETS = 32
REL_MAX_DIST = 1024
N_GROUPS = 4
HEADS_PER_GROUP = 4
D_STATE = 128
D_CONV = 5
CHUNK = 128
SSD_STEP_CHUNKS = 4
MIXER_CHUNKS = ATT_TILE // CHUNK // (N_HEADS // 2)
CONV_DIM = D_SSM + 2 * N_GROUPS * D_STATE
D_FF = 2816
EPS = 1e-6
NEG_INF = -1e30
LOG2_E = 1.4426950408889634
EXP2_SAFE_RANGE = 100.0
QK_BOUND_SLACK = 1.05

LANES = 128
HALO_ROWS = 8
VMEM_LIMIT = 56 * 1024 * 1024

F32 = jnp.float32
BF16 = jnp.bfloat16


def _params(*sem):
    return pltpu.CompilerParams(dimension_semantics=sem, vmem_limit_bytes=VMEM_LIMIT)


def _const_spec(shape):
    n = len(shape)
    return pl.BlockSpec(shape, lambda *_: (0,) * n, pipeline_mode=pl.Buffered(1))


def _rms(x, g):
    ms = jnp.mean(x * x, axis=-1, keepdims=True)
    return x * lax.rsqrt(ms + EPS) * g


def _silu(x):
    h = 0.5 * x
    return h + h * jnp.tanh(h)


def _dot(a, b):
    return jnp.dot(a, b, preferred_element_type=F32)


def _hi_lo(vals):
    hi = vals.astype(BF16)
    lo = (vals - hi.astype(F32)).astype(BF16)
    return jnp.concatenate([hi, lo], axis=1)


def _expand(split, e_ref, cols=slice(None)):
    return _dot(split, e_ref[:, cols])


def _half_step_ffn(x, g_ref, wg_ref, wu_ref, wd_ref):
    h = _rms(x, g_ref[...]).astype(BF16)
    gate = _dot(h, wg_ref[...])
    up = _dot(h, wu_ref[...])
    act = (_silu(gate) * up).astype(BF16)
    return x + 0.5 * _dot(act, wd_ref[...])


def _ffn_body(x_ref, g_ref, wg_ref, wu_ref, wd_ref, o_ref):
    o_ref[...] = _half_step_ffn(x_ref[...], g_ref, wg_ref, wu_ref, wd_ref)


def _ffn(x, g, wg, wu, wd, tm):
    n = x.shape[0]
    row = pl.BlockSpec((tm, D_MODEL), lambda i: (i, 0))
    return pl.pallas_call(
        _ffn_body,
        grid=(n // tm,),
        in_specs=[row, _const_spec((1, D_MODEL)), _const_spec((D_MODEL, D_FF)),
                  _const_spec((D_MODEL, D_FF)), _const_spec((D_FF, D_MODEL))],
        out_specs=row,
        out_shape=jax.ShapeDtypeStruct((n, D_MODEL), F32),
        compiler_params=_params("parallel"),
        name="ffn",
    )(x, g, wg, wu, wd)


def _inproj_body(x_ref, g_ref, wq_ref, wk_ref, wv_ref, wz_ref, wx_ref, wdt_ref, qg_ref, kg_ref,
                 bd_ref, q_ref, k_ref, v_ref, z_ref, xbc_ref, dt_ref):
    h = _rms(x_ref[...], g_ref[...]).astype(BF16)

    def head_norm(t, gain):
        t2 = (t * t).astype(BF16)
        w = bd_ref.shape[0]
        ms = jnp.concatenate([_dot(t2[:, j * w:(j + 1) * w], bd_ref[...])
                              for j in range(D_ATT // w)], axis=1)
        return t * lax.rsqrt(ms + EPS) * gain

    def put_pairs(ref, val):
        for hp in range(N_HEADS // 2):
            ref[hp] = val[:, hp * LANES:(hp + 1) * LANES]

    put_pairs(q_ref, head_norm(_dot(h, wq_ref[...]), qg_ref[...]))
    put_pairs(k_ref, head_norm(_dot(h, wk_ref[...]), kg_ref[...]))
    put_pairs(v_ref, _dot(h, wv_ref[...]))
    z_ref[...] = _dot(h, wz_ref[...])
    xbc_ref[...] = _dot(h, wx_ref[...])
    dt_ref[...] = _dot(h, wdt_ref[...])


def _inproj(x, g, wq, wk, wv, wz, wx, wdt, qg, kg, bd, tm):
    n = x.shape[0]
    row = lambda w: pl.BlockSpec((tm, w), lambda i: (i, 0))
    sq = _const_spec((D_MODEL, D_ATT))
    out = lambda w: jax.ShapeDtypeStruct((n, w), F32)
    pairs = pl.BlockSpec((N_HEADS // 2, tm, LANES), lambda i: (0, i, 0))
    pairs_out = jax.ShapeDtypeStruct((N_HEADS // 2, n, LANES), F32)
    return pl.pallas_call(
        _inproj_body,
        grid=(n // tm,),
        in_specs=[row(D_MODEL), _const_spec((1, D_MODEL)), sq, sq, sq, sq,
                  _const_spec((D_MODEL, CONV_DIM)), _const_spec((D_MODEL, LANES)),
                  _const_spec((1, D_ATT)), _const_spec((1, D_ATT)), _const_spec(bd.shape)],
        out_specs=[pairs, pairs, pairs, row(D_SSM), row(CONV_DIM), row(LANES)],
        out_shape=[pairs_out, pairs_out, pairs_out, out(D_SSM), out(CONV_DIM), out(LANES)],
        compiler_params=_params("parallel"),
        name="inproj",
    )(x, g, wq, wk, wv, wz, wx, wdt, qg, kg, bd)


def _rows(start, size, dil):
    return pl.ds(start, size) if dil == 1 else pl.ds(start, size, stride=dil)


def _regroup_keys(prev, cur, nxt, tmp, dst):
    s4 = ATT_STAGE
    lo, hi = ATT_REACH - ATT_HALF, ATT_REACH + ATT_TILE + ATT_HALF
    dst[0, lo:ATT_REACH, :] = prev[lo:ATT_REACH, :].astype(BF16)
    dst[0, ATT_REACH:ATT_REACH + ATT_TILE, :] = cur[...].astype(BF16)
    dst[0, ATT_REACH + ATT_TILE:hi, :] = nxt[0:ATT_HALF, :].astype(BF16)
    h4, t4 = ATT_REACH // s4, ATT_TILE // s4
    for r in range(s4):
        tmp[r, 0:h4, :] = prev[pl.ds(r, h4, stride=s4), :]
        tmp[r, h4:h4 + t4, :] = cur[pl.ds(r, t4, stride=s4), :]
        tmp[r, h4 + t4:, :] = nxt[pl.ds(r, h4, stride=s4), :]
    lo, hi = h4 - ATT_HALF, h4 + t4 + ATT_HALF
    for r in range(s4):
        dst[1, r * (ATT_WINDOW // s4) + lo:r * (ATT_WINDOW // s4) + hi, :] = tmp[r, lo:hi, :].astype(BF16)
    per = ATT_WINDOW // (s4 * s4)
    for r in range(s4 * s4):
        dst[2, r * per:(r + 1) * per, :] = tmp[r % s4, pl.ds(r // s4, per, stride=s4), :].astype(BF16)


def _attn_step(q_ref, kc_ref, kp_ref, kn_ref, vc_ref, vp_ref, vn_ref, bias_ref, out_ref,
               kd, vd, tmp, o_scr, lse_scr, *, t0, seq_len, row_max):
    _regroup_keys(kp_ref, kc_ref, kn_ref, tmp, kd)
    _regroup_keys(vp_ref, vc_ref, vn_ref, tmp, vd)

    first_head = lax.broadcasted_iota(jnp.int32, (ATT_BLOCK, LANES), 1) < HEAD_DIM
    first_head_kw = lax.broadcasted_iota(jnp.int32, (ATT_KW, LANES), 1) < HEAD_DIM
    head_ones = (first_head_kw.astype(BF16), (~first_head_kw).astype(BF16))

    def block(p, dil, r, j):
        blocks_per_residue = ATT_TILE // (dil * ATT_BLOCK)
        q0 = j * (ATT_BLOCK * dil) + r
        k0 = r * (ATT_WINDOW // dil) + ATT_REACH // dil - ATT_HALF + j * ATT_BLOCK
        variant = 0
        if j == 0:
            variant = variant + (t0 == 0).astype(jnp.int32)
        if j == blocks_per_residue - 1:
            variant = variant + 2 * (t0 + ATT_TILE == seq_len).astype(jnp.int32)
        qp = q_ref[_rows(q0, ATT_BLOCK, dil), :].astype(BF16)
        kp = kd[p, k0:k0 + ATT_KW, :]
        vp = vd[p, k0:k0 + ATT_KW, :]
        zq, zv = jnp.zeros_like(qp), jnp.zeros_like(vp)
        q2 = jnp.concatenate([jnp.where(first_head, qp, zq), jnp.where(first_head, zq, qp)], axis=0)
        s2 = lax.dot_general(q2, kp, (((1,), (1,)), ((), ())), preferred_element_type=F32)
        es, ms = [], []
        for sub in range(2):
            s = s2[sub * ATT_BLOCK:(sub + 1) * ATT_BLOCK, :] + bias_ref[variant, p, sub]
            if row_max:
                m = jnp.max(s, axis=-1, keepdims=True)
                s = s - m
                ms.append(m)
            es.append(jnp.exp2(s).astype(BF16))
        v2 = jnp.concatenate([
            jnp.concatenate([jnp.where(first_head_kw, vp, zv), head_ones[0]], axis=1),
            jnp.concatenate([jnp.where(first_head_kw, zv, vp), head_ones[1]], axis=1)], axis=0)
        ol = _dot(jnp.concatenate(es, axis=1), v2)
        l = ol[:, LANES:]
        if not row_max:
            return ol[:, :LANES], l
        return ol[:, :LANES] * (1.0 / l), jnp.where(first_head, ms[0], ms[1]) + jnp.log2(l)

    for p in range(len(ATT_PATTERNS) - 1, 0, -1):
        dil = ATT_PATTERNS[p][1]
        blocks_per_residue = ATT_TILE // (dil * ATT_BLOCK)
        for idx in range(ATT_TILE // ATT_BLOCK):
            r, j = idx // blocks_per_residue, idx % blocks_per_residue
            rows = _rows(j * (ATT_BLOCK * dil) + r, ATT_BLOCK, dil)
            o_scr[p - 1, rows, :], lse_scr[p - 1, rows, :] = block(p, dil, r, j)
    for j in range(ATT_TILE // ATT_BLOCK):
        rows = pl.ds(j * ATT_BLOCK, ATT_BLOCK)
        o1, l1 = block(0, 1, 0, j)
        l2, l3 = lse_scr[0, rows, :], lse_scr[1, rows, :]
        if not row_max:
            mixed = o1 + o_scr[0, rows, :] + o_scr[1, rows, :]
            out_ref[rows, :] = (mixed * (1.0 / (l1 + l2 + l3))).astype(BF16)
            continue
        m = jnp.maximum(jnp.maximum(l1, l2), l3)
        e1, e2, e3 = jnp.exp2(l1 - m), jnp.exp2(l2 - m), jnp.exp2(l3 - m)
        mixed = e1 * o1 + e2 * o_scr[0, rows, :] + e3 * o_scr[1, rows, :]
        out_ref[rows, :] = (mixed * (1.0 / (e1 + e2 + e3))).astype(BF16)


def _stage_conv_input(xc_ref, xp_ref, xn_ref, ext, has_prev, has_next):
    rows = xc_ref.shape[0]
    for sl in range(CONV_DIM // LANES):
        lanes = slice(sl * LANES, (sl + 1) * LANES)
        ext[sl, 0:HALO_ROWS, :] = jnp.where(has_prev, xp_ref[:, lanes], 0.0)
        ext[sl, HALO_ROWS:HALO_ROWS + rows, :] = xc_ref[:, lanes]
        ext[sl, HALO_ROWS + rows:, :] = jnp.where(has_next, xn_ref[:, lanes], 0.0)


def _conv_silu(ext, cw_ref, cb_ref, xa_ref, r0):
    pad = D_CONV // 2
    for sl in range(CONV_DIM // LANES):
        lanes = slice(sl * LANES, (sl + 1) * LANES)
        acc = cb_ref[:, lanes]
        for j in range(D_CONV):
            tap = ext[sl, pl.ds(r0 + (HALO_ROWS - pad + j), CHUNK, stride=1), :]
            acc = acc + tap * cw_ref[j:j + 1, lanes]
        xa_ref[pl.ds(r0, CHUNK), lanes] = _silu(acc)


def _softplus(x):
    return jnp.maximum(x, 0.0) + jnp.log(1.0 + jnp.exp(-jnp.abs(x)))


def _tri(lower):
    r = lax.broadcasted_iota(jnp.int32, (CHUNK, CHUNK), 0)
    c = lax.broadcasted_iota(jnp.int32, (CHUNK, CHUNK), 1)
    return (r >= c) if lower else (r <= c)


def _cumsum_rows(a, lower):
    return jnp.dot(_tri(lower).astype(F32), a, precision=lax.Precision.HIGHEST,
                   preferred_element_type=F32)


def _chunk_rows(i):
    return pl.ds(pl.multiple_of(i * CHUNK, CHUNK), CHUNK)


def _ssd_bwd_body(xc_ref, xp_ref, xn_ref, dt_ref, cw_ref, cb_ref, dtb_ref, a_ref, e_ref,
                  xa_ref, yoff_ref, ext, hst, *, n_steps):
    i = pl.program_id(1)
    step = n_steps - 1 - i

    @pl.when(i == 0)
    def _():
        hst[...] = jnp.zeros_like(hst)

    _stage_conv_input(xc_ref, xp_ref, xn_ref, ext, step > 0, step < n_steps - 1)

    def chunk(ci, carry):
        rs = _chunk_rows(SSD_STEP_CHUNKS - 1 - ci)
        _conv_silu(ext, cw_ref, cb_ref, xa_ref, rs.start)
        dt = _softplus(dt_ref[rs, :] + dtb_ref[...])
        rcum = _cumsum_rows(dt * a_ref[...], lower=False)
        decay_in = _expand(_hi_lo(jnp.exp(rcum)), e_ref)
        w_in = _expand(_hi_lo(jnp.exp(rcum[0:1, :] - rcum) * dt), e_ref)
        chunk_decay = _expand(_hi_lo(jnp.broadcast_to(jnp.exp(rcum[0:1, :]), (HALO_ROWS, LANES))),
                              e_ref)[0:1, :]
        for g in range(N_GROUPS):
            gl = slice(g * HEADS_PER_GROUP * HEAD_DIM, (g + 1) * HEADS_PER_GROUP * HEAD_DIM)
            bg = xa_ref[rs, D_SSM + g * D_STATE:D_SSM + (g + 1) * D_STATE].astype(BF16)
            cg = xa_ref[rs, D_SSM + (N_GROUPS + g) * D_STATE:D_SSM + (N_GROUPS + g + 1) * D_STATE].astype(BF16)
            xw = (xa_ref[rs, gl] * w_in[:, gl]).astype(BF16)
            h_in = hst[:, gl]
            yoff_ref[rs, gl] = _dot(cg, h_in.astype(BF16)) * decay_in[:, gl]
            upd = lax.dot_general(bg, xw, (((0,), (0,)), ((), ())), preferred_element_type=F32)
            hst[:, gl] = h_in * chunk_decay[:, gl] + upd
        return carry

    lax.fori_loop(0, SSD_STEP_CHUNKS, chunk, 0, unroll=True)


def _ssd_fwd_chunks(n_chunks, xa_ref, dt_ref, z_ref, yb_ref, dtb_ref, a_ref, dsk_ref, og_ref, e_ref,
                    y_ref, hst):
    lower, upper = _tri(True), _tri(False)
    lane = lax.broadcasted_iota(jnp.int32, (CHUNK, LANES), 1)

    def chunk(ci, carry):
        rs = _chunk_rows(ci)
        dt = _softplus(dt_ref[rs, :] + dtb_ref[...])
        a = dt * a_ref[...]
        fcum = _cumsum_rows(a, lower=True)
        rcum = _cumsum_rows(a, lower=False)
        decay_in = _expand(_hi_lo(jnp.exp(fcum)), e_ref)
        w_in = _expand(_hi_lo(jnp.exp(fcum[CHUNK - 1:CHUNK, :] - fcum) * dt), e_ref)
        chunk_decay = _expand(_hi_lo(jnp.broadcast_to(jnp.exp(fcum[CHUNK - 1:CHUNK, :]),
                                                       (HALO_ROWS, LANES))), e_ref)[0:1, :]
        log2_dt = jnp.log(dt) * LOG2_E
        fcol, rcol = fcum * LOG2_E, rcum * LOG2_E
        frow, rrow = (fcol - log2_dt).T, (rcol - log2_dt).T

        for g in range(N_GROUPS):
            gl = slice(g * HEADS_PER_GROUP * HEAD_DIM, (g + 1) * HEADS_PER_GROUP * HEAD_DIM)
            bg = xa_ref[rs, D_SSM + g * D_STATE:D_SSM + (g + 1) * D_STATE].astype(BF16)
            cg = xa_ref[rs, D_SSM + (N_GROUPS + g) * D_STATE:D_SSM + (N_GROUPS + g + 1) * D_STATE].astype(BF16)
            cb = lax.dot_general(cg, bg, (((1,), (1,)), ((), ())), preferred_element_type=F32)
            xw = (xa_ref[rs, gl] * w_in[:, gl]).astype(BF16)
            h_in = hst[:, gl]
            y_off = _dot(cg, h_in.astype(BF16)) * decay_in[:, gl]
            upd = lax.dot_general(bg, xw, (((0,), (0,)), ((), ())), preferred_element_type=F32)
            hst[:, gl] = h_in * chunk_decay[:, gl] + upd
            gated = []
            for pair in range(HEADS_PER_GROUP // 2):
                pl_ = slice(g * 256 + pair * LANES, g * 256 + (pair + 1) * LANES)
                xs = xa_ref[rs, pl_]
                x_pair = xs.astype(BF16)
                halves = []
                for sub in range(2):
                    h = g * HEADS_PER_GROUP + 2 * pair + sub
                    hb = N_HEADS + h
                    seg_f = jnp.where(lower, fcol[:, h:h + 1] - frow[h:h + 1, :], NEG_INF)
                    seg_b = jnp.where(upper, rcol[:, hb:hb + 1] - rrow[hb:hb + 1, :], NEG_INF)
                    mix = jnp.exp2(seg_f) + jnp.exp2(seg_b)
                    halves.append(_dot((cb * mix).astype(BF16), x_pair))
                y_diag = jnp.where(lane < HEAD_DIM, halves[0], halves[1])
                po = slice(pair * LANES, (pair + 1) * LANES)
                y = y_diag + y_off[:, po] + yb_ref[rs, pl_] + dsk_ref[:, pl_] * xs
                gated.append(y * _silu(z_ref[rs, pl_]))
            y_ref[rs, gl] = _rms(jnp.concatenate(gated, axis=1), og_ref[:, gl]).astype(BF16)
        return carry

    lax.fori_loop(0, n_chunks, chunk, 0, unroll=True)


def _ssd_bwd(xbc, dt, conv_w, conv_b, dt_bias, a_neg, e_bwd):
    b, s, _ = xbc.shape
    rows = SSD_STEP_CHUNKS * CHUNK
    nc = s // rows
    hpc = rows // HALO_ROWS
    last_halo = s // HALO_ROWS - 1

    def specs(cidx):
        chunk = lambda w: pl.BlockSpec((None, rows, w), lambda bi, i: (bi, cidx(i), 0))
        prev = pl.BlockSpec((None, HALO_ROWS, CONV_DIM),
                            lambda bi, i: (bi, jnp.maximum(cidx(i) * hpc - 1, 0), 0))
        nxt = pl.BlockSpec((None, HALO_ROWS, CONV_DIM),
                           lambda bi, i: (bi, jnp.minimum((cidx(i) + 1) * hpc, last_halo), 0))
        return chunk, prev, nxt

    state = pltpu.VMEM((D_STATE, D_SSM), F32)
    conv_ext = pltpu.VMEM((CONV_DIM // LANES, rows + 2 * HALO_ROWS, LANES), F32)
    small = [_const_spec(dt_bias.shape), _const_spec(a_neg.shape)]

    chunk, prev, nxt = specs(lambda i: nc - 1 - i)
    return pl.pallas_call(
        functools.partial(_ssd_bwd_body, n_steps=nc),
        grid=(b, nc),
        in_specs=[chunk(CONV_DIM), prev, nxt, chunk(LANES), _const_spec(conv_w.shape),
                  _const_spec(conv_b.shape)] + small + [_const_spec(e_bwd.shape)],
        out_specs=[chunk(CONV_DIM), chunk(D_SSM)],
        out_shape=[jax.ShapeDtypeStruct((b, s, CONV_DIM), F32), jax.ShapeDtypeStruct((b, s, D_SSM), F32)],
        scratch_shapes=[conv_ext, state],
        compiler_params=_params("parallel", "arbitrary"),
        name="ssd_bwd",
    )(xbc, xbc, xbc, dt, conv_w, conv_b, dt_bias, a_neg, e_bwd)


def _mixer_body(q_ref, kc_ref, kp_ref, kn_ref, vc_ref, vp_ref, vn_ref, bias_ref,
                xa_ref, dt_ref, z_ref, yb_ref, dtb_ref, a_ref, dsk_ref, og_ref, e_ref,
                attn_ref, y_ref, kd, vd, tmp, o_scr, lse_scr, hst, *, seq_len, row_max):
    t, hp = pl.program_id(1), pl.program_id(2)

    @pl.when(jnp.logical_and(t == 0, hp == 0))
    def _():
        hst[...] = jnp.zeros_like(hst)

    _attn_step(q_ref, kc_ref, kp_ref, kn_ref, vc_ref, vp_ref, vn_ref, bias_ref, attn_ref,
               kd, vd, tmp, o_scr, lse_scr, t0=t * ATT_TILE, seq_len=seq_len, row_max=row_max)
    _ssd_fwd_chunks(MIXER_CHUNKS, xa_ref, dt_ref, z_ref, yb_ref, dtb_ref, a_ref, dsk_ref, og_ref, e_ref,
                    y_ref, hst)


def _mixer_call(q, k, v, bias, xa, dt, z, yb, dt_bias, a_neg, d_skip, out_g, e_fwd, row_max):
    n_pairs, b, s, _ = q.shape
    n_pat = len(ATT_PATTERNS)
    last_halo = s // ATT_REACH - 1
    halo_per_tile = ATT_TILE // ATT_REACH
    cur = pl.BlockSpec((None, None, ATT_TILE, LANES), lambda bi, t, hp: (hp, bi, t, 0))
    prev = pl.BlockSpec((None, None, ATT_REACH, LANES),
                        lambda bi, t, hp: (hp, bi, jnp.maximum(t * halo_per_tile - 1, 0), 0))
    nxt = pl.BlockSpec((None, None, ATT_REACH, LANES),
                       lambda bi, t, hp: (hp, bi, jnp.minimum((t + 1) * halo_per_tile, last_halo), 0))
    bias_spec = pl.BlockSpec((bias.shape[0], n_pat, 2, ATT_BLOCK, ATT_KW),
                             lambda bi, t, hp: (0, 0, hp, 0, 0))
    rows = MIXER_CHUNKS * CHUNK
    scan = lambda w: pl.BlockSpec((None, rows, w), lambda bi, t, hp: (bi, t * n_pairs + hp, 0))
    consts = [_const_spec(a.shape) for a in (dt_bias, a_neg, d_skip, out_g, e_fwd)]
    return pl.pallas_call(
        functools.partial(_mixer_body, seq_len=s, row_max=row_max),
        grid=(b, s // ATT_TILE, n_pairs),
        in_specs=[cur, cur, prev, nxt, cur, prev, nxt, bias_spec,
                  scan(CONV_DIM), scan(LANES), scan(D_SSM), scan(D_SSM)] + consts,
        out_specs=[cur, scan(D_SSM)],
        out_shape=[jax.ShapeDtypeStruct((n_pairs, b, s, LANES), BF16),
                   jax.ShapeDtypeStruct((b, s, D_SSM), BF16)],
        scratch_shapes=[pltpu.VMEM((n_pat, ATT_WINDOW, LANES), BF16),
                        pltpu.VMEM((n_pat, ATT_WINDOW, LANES), BF16),
                        pltpu.VMEM((ATT_STAGE, ATT_WINDOW // ATT_STAGE, LANES), F32),
                        pltpu.VMEM((n_pat - 1, ATT_TILE, LANES), F32),
                        pltpu.VMEM((n_pat - 1, ATT_TILE, LANES), F32),
                        pltpu.VMEM((D_STATE, D_SSM), F32)],
        compiler_params=_params("parallel", "arbitrary", "arbitrary"),
        name="mixer_rowmax" if row_max else "mixer",
    )(q, k, k, k, v, v, v, bias, xa, dt, z, yb, dt_bias, a_neg, d_skip, out_g, e_fwd)


def _mixer(q, k, v, att, xa, dt, z, yb, dt_bias, a_neg, d_skip, out_g, e_fwd):
    rest = (xa, dt, z, yb, dt_bias, a_neg, d_skip, out_g, e_fwd)
    return lax.cond(att["narrow"],
                    lambda: _mixer_call(q, k, v, att["bias_shifted"], *rest, row_max=False),
                    lambda: _mixer_call(q, k, v, att["bias"], *rest, row_max=True))


def _outproj_ffn_body(x_ref, a_ref, y_ref, g_ref, wa_ref, ws_ref, g2_ref, wg_ref, wu_ref, wd_ref,
                      out_ref):
    attn = jnp.concatenate([a_ref[hp] for hp in range(a_ref.shape[0])], axis=1)
    attn = _rms(attn.astype(F32), g_ref[...]).astype(BF16)
    x2 = x_ref[...] + _dot(attn, wa_ref[...]) + _dot(y_ref[...], ws_ref[...])
    out_ref[...] = _half_step_ffn(x2, g2_ref, wg_ref, wu_ref, wd_ref)


def _outproj_ffn(x, attn, y, g, wa, ws, g2, wg, wu, wd, tm):
    n = x.shape[0]
    row = lambda w: pl.BlockSpec((tm, w), lambda i: (i, 0))
    sq = _const_spec((D_ATT, D_MODEL))
    pairs = pl.BlockSpec((attn.shape[0], tm, LANES), lambda i: (0, i, 0))
    return pl.pallas_call(
        _outproj_ffn_body,
        grid=(n // tm,),
        in_specs=[row(D_MODEL), pairs, row(D_SSM), _const_spec((1, D_ATT)), sq, sq,
                  _const_spec((1, D_MODEL)), _const_spec((D_MODEL, D_FF)), _const_spec((D_MODEL, D_FF)),
                  _const_spec((D_FF, D_MODEL))],
        out_specs=row(D_MODEL),
        out_shape=jax.ShapeDtypeStruct((n, D_MODEL), F32),
        compiler_params=_params("parallel"),
        name="outproj_ffn",
    )(x, attn, y, g, wa, ws, g2, wg, wu, wd)


def _t5_bucket(rel):
    nb = N_REL_BUCKETS // 2
    max_exact = nb // 2
    n = np.abs(rel)
    large = max_exact + (np.log(np.maximum(n, 1) / max_exact)
                         / math.log(REL_MAX_DIST / max_exact) * (nb - max_exact)).astype(np.int32)
    large = np.minimum(large, nb - 1)
    return (np.where(rel > 0, nb, 0) + np.where(n < max_exact, n, large)).astype(np.int32)


def _bias_tables(rel_bias):
    rel_sub = np.arange(ATT_KW)[None, :] - ATT_HALF - np.arange(ATT_BLOCK)[:, None]
    in_window = np.abs(rel_sub) <= ATT_HALF
    col = np.arange(ATT_KW)[None, :]
    tables = []
    for _, dil in ATT_PATTERNS:
        onehot = (_t5_bucket(rel_sub * dil)[:, :, None] == np.arange(N_REL_BUCKETS)).astype(np.float32)
        bias = jnp.einsum("ijk,kh->hij", jnp.asarray(onehot, BF16).astype(F32), rel_bias.astype(F32),
                          precision=lax.Precision.HIGHEST)
        tables.append(bias * LOG2_E)
    bias = jnp.stack(tables, axis=0)
    variants = []
    for v in range(4):
        keep = in_window & ((col >= ATT_HALF) | (v & 1 == 0)) & ((col < ATT_BLOCK + ATT_HALF) | (v & 2 == 0))
        variants.append(jnp.where(keep[None, None], bias, NEG_INF))
    return jnp.stack(variants, axis=0)


def _attention_tables(rel_bias, q_norm_g, k_norm_g):
    bias = _bias_tables(rel_bias)
    logit_bound = (QK_BOUND_SLACK * HEAD_DIM * HEAD_DIM ** -0.5 * LOG2_E
                   * jnp.max(jnp.abs(q_norm_g)) * jnp.max(jnp.abs(k_norm_g))).astype(F32)
    bias_lo = (jnp.min(rel_bias) * LOG2_E).astype(F32)
    bias_hi = (jnp.max(rel_bias) * LOG2_E).astype(F32)
    shift = logit_bound + bias_hi
    return {
        "bias": bias,
        "bias_shifted": jnp.where(bias > 0.5 * NEG_INF, bias - shift, NEG_INF),
        "narrow": 2.0 * logit_bound + (bias_hi - bias_lo) < EXP2_SAFE_RANGE,
    }


def _head_expander(first_row, width):
    e = np.zeros((2 * LANES, width), np.float32)
    for h in range(width // HEAD_DIM):
        e[first_row + h, h * HEAD_DIM:(h + 1) * HEAD_DIM] = 1.0
        e[LANES + first_row + h, h * HEAD_DIM:(h + 1) * HEAD_DIM] = 1.0
    return jnp.asarray(e, BF16)


def _block_diag_mean(width):
    i = np.arange(width)
    return jnp.asarray((i[:, None] // HEAD_DIM == i[None, :] // HEAD_DIM) / HEAD_DIM, BF16)


def _layer(x, p, tm):
    b, s, _ = x.shape
    n = b * s
    xf = x.reshape(n, D_MODEL)
    x1 = _ffn(xf, p["ffn1_g"], p["ffn1_wg"], p["ffn1_wu"], p["ffn1_wd"], tm)
    q, k, v, z, xbc, dt = _inproj(x1, p["mix_g"], p["wq"], p["wk"], p["wv"], p["wz"], p["wx"],
                                  p["wdt"], p["qg"], p["kg"], p["bd"], tm)
    seq = lambda a: a.reshape(b, s, a.shape[-1])
    pair_seq = lambda a: a.reshape(a.shape[0], b, s, LANES)
    xa, yb = _ssd_bwd(seq(xbc), seq(dt), p["conv_w"], p["conv_b"], p["dt_bias"], p["a_neg"], p["e_bwd"])
    attn, y = _mixer(pair_seq(q), pair_seq(k), pair_seq(v), p["att"], xa, seq(dt), seq(z), yb,
                     p["dt_bias"], p["a_neg"], p["d_skip"], p["ssm_g"], p["e_fwd"])
    out = _outproj_ffn(x1, attn.reshape(attn.shape[0], n, LANES), y.reshape(n, D_SSM), p["attn_g"],
                       p["wo_att"], p["wo_ssm"], p["ffn2_g"], p["ffn2_wg"], p["ffn2_wu"],
                       p["ffn2_wd"], tm)
    return out.reshape(b, s, D_MODEL)


def _prepare(rel_bias, ffn1_norm_g, ffn1_w_gate, ffn1_w_up, ffn1_w_down, mix_norm_g, w_in,
             q_norm_g, k_norm_g, attn_out_g, conv_w, conv_b, dt_bias, a_log, d_skip, ssm_out_g,
             w_out, ffn2_norm_g, ffn2_w_gate, ffn2_w_up, ffn2_w_down):
    row = lambda a: a.reshape(1, -1).astype(F32)
    w16 = lambda a: a.astype(BF16)
    c0, c1, c2, c3 = D_ATT, 2 * D_ATT, 3 * D_ATT, 3 * D_ATT + D_SSM
    c4 = c3 + CONV_DIM
    pad32 = lambda a: jnp.pad(a.reshape(1, 2 * N_HEADS).astype(F32), ((0, 0), (0, LANES - 2 * N_HEADS)))
    return {
        "ffn1_g": row(ffn1_norm_g), "ffn1_wg": w16(ffn1_w_gate), "ffn1_wu": w16(ffn1_w_up),
        "ffn1_wd": w16(ffn1_w_down),
        "ffn2_g": row(ffn2_norm_g), "ffn2_wg": w16(ffn2_w_gate), "ffn2_wu": w16(ffn2_w_up),
        "ffn2_wd": w16(ffn2_w_down),
        "mix_g": row(mix_norm_g),
        "wq": w16(w_in[:, :c0]), "wk": w16(w_in[:, c0:c1]), "wv": w16(w_in[:, c1:c2]),
        "wz": w16(w_in[:, c2:c3]), "wx": w16(w_in[:, c3:c4]),
        "wdt": w16(jnp.pad(w_in[:, c4:], ((0, 0), (0, LANES - 2 * N_HEADS)))),
        "qg": row(jnp.tile(q_norm_g, N_HEADS)) * (HEAD_DIM ** -0.5 * LOG2_E),
        "kg": row(jnp.tile(k_norm_g, N_HEADS)),
        "bd": _block_diag_mean(256),
        "att": _attention_tables(rel_bias, q_norm_g, k_norm_g),
        "attn_g": row(attn_out_g),
        "conv_w": conv_w.astype(F32), "conv_b": row(conv_b),
        "dt_bias": pad32(dt_bias), "a_neg": pad32(-jnp.exp(a_log.astype(F32))),
        "d_skip": row(jnp.repeat(d_skip, HEAD_DIM)), "ssm_g": row(ssm_out_g),
        "e_fwd": _head_expander(0, D_SSM), "e_bwd": _head_expander(N_HEADS, D_SSM),
        "wo_att": w16(w_out[:D_ATT]), "wo_ssm": w16(w_out[D_ATT:]),
    }


def _trunk(x, layers, tm=512):
    for p in layers:
        x = _layer(x, p, tm)
    return x


def kernel(x_prompt, x_sample, rel_bias, ffn1_norm_g, ffn1_w_gate, ffn1_w_up, ffn1_w_down, mix_norm_g, w_in, q_norm_g, k_norm_g, attn_out_g, conv_w, conv_b, dt_bias, a_log, d_skip, ssm_out_g, w_out, ffn2_norm_g, ffn2_w_gate, ffn2_w_up, ffn2_w_down):
    per_layer = (ffn1_norm_g, ffn1_w_gate, ffn1_w_up, ffn1_w_down, mix_norm_g, w_in, q_norm_g,
                 k_norm_g, attn_out_g, conv_w, conv_b, dt_bias, a_log, d_skip, ssm_out_g, w_out,
                 ffn2_norm_g, ffn2_w_gate, ffn2_w_up, ffn2_w_down)
    layers = [_prepare(rel_bias, *(a[l] for a in per_layer)) for l in range(ffn1_norm_g.shape[0])]
    return (_trunk(x_prompt, layers), _trunk(x_sample, layers))
```

```python
import functools
import math

import numpy as np
import jax
import jax.numpy as jnp
from jax import lax
from jax.experimental import pallas as pl
from jax.experimental.pallas import tpu as pltpu

D_MODEL = 1024
D_ATT = 1024
D_SSM = 1024
HEAD_DIM = 64
N_HEADS = 16
ATT_PATTERNS = ((128, 1), (512, 4), (2048, 16))
ATT_BLOCK = 128
ATT_HALF = 64
ATT_KW = ATT_BLOCK + 2 * ATT_HALF
ATT_MAX_DIL = max(d for _, d in ATT_PATTERNS)
ATT_TILE = ATT_BLOCK * ATT_MAX_DIL
ATT_REACH = ATT_HALF * ATT_MAX_DIL
ATT_WINDOW = ATT_TILE + 2 * ATT_REACH
ATT_STAGE = 4
assert tuple(d for _, d in ATT_PATTERNS) == (1, ATT_STAGE, ATT_STAGE * ATT_STAGE)
N_REL_BUCKETS = 32
REL_MAX_DIST = 1024
N_GROUPS = 4
HEADS_PER_GROUP = 4
D_STATE = 128
D_CONV = 5
CHUNK = 128
SSD_STEP_CHUNKS = 4
MIXER_CHUNKS = ATT_TILE // CHUNK // (N_HEADS // 2)
CONV_DIM = D_SSM + 2 * N_GROUPS * D_STATE
D_FF = 2816
EPS = 1e-6
NEG_INF = -1e30
LOG2_E = 1.4426950408889634
EXP2_SAFE_RANGE = 100.0
QK_BOUND_SLACK = 1.05

LANES = 128
HALO_ROWS = 8
VMEM_LIMIT = 56 * 1024 * 1024

F32 = jnp.float32
BF16 = jnp.bfloat16


def _params(*sem):
    return pltpu.CompilerParams(dimension_semantics=sem, vmem_limit_bytes=VMEM_LIMIT)


def _const_spec(shape):
    n = len(shape)
    return pl.BlockSpec(shape, lambda *_: (0,) * n, pipeline_mode=pl.Buffered(1))


def _rms(x, g):
    ms = jnp.mean(x * x, axis=-1, keepdims=True)
    return x * lax.rsqrt(ms + EPS) * g


def _silu(x):
    h = 0.5 * x
    return h + h * jnp.tanh(h)


def _dot(a, b):
    return jnp.dot(a, b, preferred_element_type=F32)


def _hi_lo(vals):
    hi = vals.astype(BF16)
    lo = (vals - hi.astype(F32)).astype(BF16)
    return jnp.concatenate([hi, lo], axis=1)


def _expand(split, e_ref, cols=slice(None)):
    return _dot(split, e_ref[:, cols])


def _half_step_ffn(x, g_ref, wg_ref, wu_ref, wd_ref):
    h = _rms(x, g_ref[...]).astype(BF16)
    gate = _dot(h, wg_ref[...])
    up = _dot(h, wu_ref[...])
    act = (_silu(gate) * up).astype(BF16)
    return x + 0.5 * _dot(act, wd_ref[...])


def _ffn_body(x_ref, g_ref, wg_ref, wu_ref, wd_ref, o_ref):
    o_ref[...] = _half_step_ffn(x_ref[...], g_ref, wg_ref, wu_ref, wd_ref)


def _ffn(x, g, wg, wu, wd, tm):
    n = x.shape[0]
    row = pl.BlockSpec((tm, D_MODEL), lambda i: (i, 0))
    return pl.pallas_call(
        _ffn_body,
        grid=(n // tm,),
        in_specs=[row, _const_spec((1, D_MODEL)), _const_spec((D_MODEL, D_FF)),
                  _const_spec((D_MODEL, D_FF)), _const_spec((D_FF, D_MODEL))],
        out_specs=row,
        out_shape=jax.ShapeDtypeStruct((n, D_MODEL), F32),
        compiler_params=_params("parallel"),
        name="ffn",
    )(x, g, wg, wu, wd)


def _inproj_body(x_ref, g_ref, wq_ref, wk_ref, wv_ref, wz_ref, wx_ref, wdt_ref, qg_ref, kg_ref,
                 bd_ref, q_ref, k_ref, v_ref, z_ref, xbc_ref, dt_ref):
    h = _rms(x_ref[...], g_ref[...]).astype(BF16)

    def head_norm(t, gain):
        t2 = (t * t).astype(BF16)
        w = bd_ref.shape[0]
        ms = jnp.concatenate([_dot(t2[:, j * w:(j + 1) * w], bd_ref[...])
                              for j in range(D_ATT // w)], axis=1)
        return t * lax.rsqrt(ms + EPS) * gain

    def put_pairs(ref, val):
        for hp in range(N_HEADS // 2):
            ref[hp] = val[:, hp * LANES:(hp + 1) * LANES]

    put_pairs(q_ref, head_norm(_dot(h, wq_ref[...]), qg_ref[...]))
    put_pairs(k_ref, head_norm(_dot(h, wk_ref[...]), kg_ref[...]))
    put_pairs(v_ref, _dot(h, wv_ref[...]))
    z_ref[...] = _dot(h, wz_ref[...])
    xbc_ref[...] = _dot(h, wx_ref[...])
    dt_ref[...] = _dot(h, wdt_ref[...])


def _inproj(x, g, wq, wk, wv, wz, wx, wdt, qg, kg, bd, tm):
    n = x.shape[0]
    row = lambda w: pl.BlockSpec((tm, w), lambda i: (i, 0))
    sq = _const_spec((D_MODEL, D_ATT))
    out = lambda w: jax.ShapeDtypeStruct((n, w), F32)
    pairs = pl.BlockSpec((N_HEADS // 2, tm, LANES), lambda i: (0, i, 0))
    pairs_out = jax.ShapeDtypeStruct((N_HEADS // 2, n, LANES), F32)
    return pl.pallas_call(
        _inproj_body,
        grid=(n // tm,),
        in_specs=[row(D_MODEL), _const_spec((1, D_MODEL)), sq, sq, sq, sq,
                  _const_spec((D_MODEL, CONV_DIM)), _const_spec((D_MODEL, LANES)),
                  _const_spec((1, D_ATT)), _const_spec((1, D_ATT)), _const_spec(bd.shape)],
        out_specs=[pairs, pairs, pairs, row(D_SSM), row(CONV_DIM), row(LANES)],
        out_shape=[pairs_out, pairs_out, pairs_out, out(D_SSM), out(CONV_DIM), out(LANES)],
        compiler_params=_params("parallel"),
        name="inproj",
    )(x, g, wq, wk, wv, wz, wx, wdt, qg, kg, bd)


def _rows(start, size, dil):
    return pl.ds(start, size) if dil == 1 else pl.ds(start, size, stride=dil)


def _regroup_keys(prev, cur, nxt, tmp, dst):
    s4 = ATT_STAGE
    lo, hi = ATT_REACH - ATT_HALF, ATT_REACH + ATT_TILE + ATT_HALF
    dst[0, lo:ATT_REACH, :] = prev[lo:ATT_REACH, :].astype(BF16)
    dst[0, ATT_REACH:ATT_REACH + ATT_TILE, :] = cur[...].astype(BF16)
    dst[0, ATT_REACH + ATT_TILE:hi, :] = nxt[0:ATT_HALF, :].astype(BF16)
    h4, t4 = ATT_REACH // s4, ATT_TILE // s4
    for r in range(s4):
        tmp[r, 0:h4, :] = prev[pl.ds(r, h4, stride=s4), :]
        tmp[r, h4:h4 + t4, :] = cur[pl.ds(r, t4, stride=s4), :]
        tmp[r, h4 + t4:, :] = nxt[pl.ds(r, h4, stride=s4), :]
    lo, hi = h4 - ATT_HALF, h4 + t4 + ATT_HALF
    for r in range(s4):
        dst[1, r * (ATT_WINDOW // s4) + lo:r * (ATT_WINDOW // s4) + hi, :] = tmp[r, lo:hi, :].astype(BF16)
    per = ATT_WINDOW // (s4 * s4)
    for r in range(s4 * s4):
        dst[2, r * per:(r + 1) * per, :] = tmp[r % s4, pl.ds(r // s4, per, stride=s4), :].astype(BF16)


def _attn_step(q_ref, kc_ref, kp_ref, kn_ref, vc_ref, vp_ref, vn_ref, bias_ref, out_ref,
               kd, vd, tmp, o_scr, lse_scr, *, t0, seq_len, row_max):
    _regroup_keys(kp_ref, kc_ref, kn_ref, tmp, kd)
    _regroup_keys(vp_ref, vc_ref, vn_ref, tmp, vd)

    first_head = lax.broadcasted_iota(jnp.int32, (ATT_BLOCK, LANES), 1) < HEAD_DIM
    first_head_kw = lax.broadcasted_iota(jnp.int32, (ATT_KW, LANES), 1) < HEAD_DIM
    head_ones = (first_head_kw.astype(BF16), (~first_head_kw).astype(BF16))

    def block(p, dil, r, j):
        blocks_per_residue = ATT_TILE // (dil * ATT_BLOCK)
        q0 = j * (ATT_BLOCK * dil) + r
        k0 = r * (ATT_WINDOW // dil) + ATT_REACH // dil - ATT_HALF + j * ATT_BLOCK
        variant = 0
        if j == 0:
            variant = variant + (t0 == 0).astype(jnp.int32)
        if j == blocks_per_residue - 1:
            variant = variant + 2 * (t0 + ATT_TILE == seq_len).astype(jnp.int32)
        qp = q_ref[_rows(q0, ATT_BLOCK, dil), :].astype(BF16)
        kp = kd[p, k0:k0 + ATT_KW, :]
        vp = vd[p, k0:k0 + ATT_KW, :]
        zq, zv = jnp.zeros_like(qp), jnp.zeros_like(vp)
        q2 = jnp.concatenate([jnp.where(first_head, qp, zq), jnp.where(first_head, zq, qp)], axis=0)
        s2 = lax.dot_general(q2, kp, (((1,), (1,)), ((), ())), preferred_element_type=F32)
        es, ms = [], []
        for sub in range(2):
            s = s2[sub * ATT_BLOCK:(sub + 1) * ATT_BLOCK, :] + bias_ref[variant, p, sub]
            if row_max:
                m = jnp.max(s, axis=-1, keepdims=True)
                s = s - m
                ms.append(m)
            es.append(jnp.exp2(s).astype(BF16))
        v2 = jnp.concatenate([
            jnp.concatenate([jnp.where(first_head_kw, vp, zv), head_ones[0]], axis=1),
            jnp.concatenate([jnp.where(first_head_kw, zv, vp), head_ones[1]], axis=1)], axis=0)
        ol = _dot(jnp.concatenate(es, axis=1), v2)
        l = ol[:, LANES:]
        if not row_max:
            return ol[:, :LANES], l
        return ol[:, :LANES] * (1.0 / l), jnp.where(first_head, ms[0], ms[1]) + jnp.log2(l)

    for p in range(len(ATT_PATTERNS) - 1, 0, -1):
        dil = ATT_PATTERNS[p][1]
        blocks_per_residue = ATT_TILE // (dil * ATT_BLOCK)
        for idx in range(ATT_TILE // ATT_BLOCK):
            r, j = idx // blocks_per_residue, idx % blocks_per_residue
            rows = _rows(j * (ATT_BLOCK * dil) + r, ATT_BLOCK, dil)
            o_scr[p - 1, rows, :], lse_scr[p - 1, rows, :] = block(p, dil, r, j)
    for j in range(ATT_TILE // ATT_BLOCK):
        rows = pl.ds(j * ATT_BLOCK, ATT_BLOCK)
        o1, l1 = block(0, 1, 0, j)
        l2, l3 = lse_scr[0, rows, :], lse_scr[1, rows, :]
        if not row_max:
            mixed = o1 + o_scr[0, rows, :] + o_scr[1, rows, :]
            out_ref[rows, :] = (mixed * (1.0 / (l1 + l2 + l3))).astype(BF16)
            continue
        m = jnp.maximum(jnp.maximum(l1, l2), l3)
        e1, e2, e3 = jnp.exp2(l1 - m), jnp.exp2(l2 - m), jnp.exp2(l3 - m)
        mixed = e1 * o1 + e2 * o_scr[0, rows, :] + e3 * o_scr[1, rows, :]
        out_ref[rows, :] = (mixed * (1.0 / (e1 + e2 + e3))).astype(BF16)


def _stage_conv_input(xc_ref, xp_ref, xn_ref, ext, has_prev, has_next):
    rows = xc_ref.shape[0]
    for sl in range(CONV_DIM // LANES):
        lanes = slice(sl * LANES, (sl + 1) * LANES)
        ext[sl, 0:HALO_ROWS, :] = jnp.where(has_prev, xp_ref[:, lanes], 0.0)
        ext[sl, HALO_ROWS:HALO_ROWS + rows, :] = xc_ref[:, lanes]
        ext[sl, HALO_ROWS + rows:, :] = jnp.where(has_next, xn_ref[:, lanes], 0.0)


def _conv_silu(ext, cw_ref, cb_ref, xa_ref, r0):
    pad = D_CONV // 2
    for sl in range(CONV_DIM // LANES):
        lanes = slice(sl * LANES, (sl + 1) * LANES)
        acc = cb_ref[:, lanes]
        for j in range(D_CONV):
            tap = ext[sl, pl.ds(r0 + (HALO_ROWS - pad + j), CHUNK, stride=1), :]
            acc = acc + tap * cw_ref[j:j + 1, lanes]
        xa_ref[pl.ds(r0, CHUNK), lanes] = _silu(acc)


def _softplus(x):
    return jnp.maximum(x, 0.0) + jnp.log(1.0 + jnp.exp(-jnp.abs(x)))


def _tri(lower):
    r = lax.broadcasted_iota(jnp.int32, (CHUNK, CHUNK), 0)
    c = lax.broadcasted_iota(jnp.int32, (CHUNK, CHUNK), 1)
    return (r >= c) if lower else (r <= c)


def _cumsum_rows(a, lower):
    return jnp.dot(_tri(lower).astype(F32), a, precision=lax.Precision.HIGHEST,
                   preferred_element_type=F32)


def _chunk_rows(i):
    return pl.ds(pl.multiple_of(i * CHUNK, CHUNK), CHUNK)


def _ssd_bwd_body(xc_ref, xp_ref, xn_ref, dt_ref, cw_ref, cb_ref, dtb_ref, a_ref, e_ref,
                  xa_ref, yoff_ref, ext, hst, *, n_steps):
    i = pl.program_id(1)
    step = n_steps - 1 - i

    @pl.when(i == 0)
    def _():
        hst[...] = jnp.zeros_like(hst)

    _stage_conv_input(xc_ref, xp_ref, xn_ref, ext, step > 0, step < n_steps - 1)

    def chunk(ci, carry):
        rs = _chunk_rows(SSD_STEP_CHUNKS - 1 - ci)
        _conv_silu(ext, cw_ref, cb_ref, xa_ref, rs.start)
        dt = _softplus(dt_ref[rs, :] + dtb_ref[...])
        rcum = _cumsum_rows(dt * a_ref[...], lower=False)
        decay_in = _expand(_hi_lo(jnp.exp(rcum)), e_ref)
        w_in = _expand(_hi_lo(jnp.exp(rcum[0:1, :] - rcum) * dt), e_ref)
        chunk_decay = _expand(_hi_lo(jnp.broadcast_to(jnp.exp(rcum[0:1, :]), (HALO_ROWS, LANES))),
                              e_ref)[0:1, :]
        for g in range(N_GROUPS):
            gl = slice(g * HEADS_PER_GROUP * HEAD_DIM, (g + 1) * HEADS_PER_GROUP * HEAD_DIM)
            bg = xa_ref[rs, D_SSM + g * D_STATE:D_SSM + (g + 1) * D_STATE].astype(BF16)
            cg = xa_ref[rs, D_SSM + (N_GROUPS + g) * D_STATE:D_SSM + (N_GROUPS + g + 1) * D_STATE].astype(BF16)
            xw = (xa_ref[rs, gl] * w_in[:, gl]).astype(BF16)
            h_in = hst[:, gl]
            yoff_ref[rs, gl] = _dot(cg, h_in.astype(BF16)) * decay_in[:, gl]
            upd = lax.dot_general(bg, xw, (((0,), (0,)), ((), ())), preferred_element_type=F32)
            hst[:, gl] = h_in * chunk_decay[:, gl] + upd
        return carry

    lax.fori_loop(0, SSD_STEP_CHUNKS, chunk, 0, unroll=True)


def _ssd_fwd_chunks(n_chunks, xa_ref, dt_ref, z_ref, yb_ref, dtb_ref, a_ref, dsk_ref, og_ref, e_ref,
                    y_ref, hst):
    lower, upper = _tri(True), _tri(False)
    lane = lax.broadcasted_iota(jnp.int32, (CHUNK, LANES), 1)

    def chunk(ci, carry):
        rs = _chunk_rows(ci)
        dt = _softplus(dt_ref[rs, :] + dtb_ref[...])
        a = dt * a_ref[...]
        fcum = _cumsum_rows(a, lower=True)
        rcum = _cumsum_rows(a, lower=False)
        decay_in = _expand(_hi_lo(jnp.exp(fcum)), e_ref)
        w_in = _expand(_hi_lo(jnp.exp(fcum[CHUNK - 1:CHUNK, :] - fcum) * dt), e_ref)
        chunk_decay = _expand(_hi_lo(jnp.broadcast_to(jnp.exp(fcum[CHUNK - 1:CHUNK, :]),
                                                       (HALO_ROWS, LANES))), e_ref)[0:1, :]
        log2_dt = jnp.log(dt) * LOG2_E
        fcol, rcol = fcum * LOG2_E, rcum * LOG2_E
        frow, rrow = (fcol - log2_dt).T, (rcol - log2_dt).T

        for g in range(N_GROUPS):
            gl = slice(g * HEADS_PER_GROUP * HEAD_DIM, (g + 1) * HEADS_PER_GROUP * HEAD_DIM)
            bg = xa_ref[rs, D_SSM + g * D_STATE:D_SSM + (g + 1) * D_STATE].astype(BF16)
            cg = xa_ref[rs, D_SSM + (N_GROUPS + g) * D_STATE:D_SSM + (N_GROUPS + g + 1) * D_STATE].astype(BF16)
            cb = lax.dot_general(cg, bg, (((1,), (1,)), ((), ())), preferred_element_type=F32)
            xw = (xa_ref[rs, gl] * w_in[:, gl]).astype(BF16)
            h_in = hst[:, gl]
            y_off = _dot(cg, h_in.astype(BF16)) * decay_in[:, gl]
            upd = lax.dot_general(bg, xw, (((0,), (0,)), ((), ())), preferred_element_type=F32)
            hst[:, gl] = h_in * chunk_decay[:, gl] + upd
            gated = []
            for pair in range(HEADS_PER_GROUP // 2):
                pl_ = slice(g * 256 + pair * LANES, g * 256 + (pair + 1) * LANES)
                xs = xa_ref[rs, pl_]
                x_pair = xs.astype(BF16)
                halves = []
                for sub in range(2):
                    h = g * HEADS_PER_GROUP + 2 * pair + sub
                    hb = N_HEADS + h
                    seg_f = jnp.where(lower, fcol[:, h:h + 1] - frow[h:h + 1, :], NEG_INF)
                    seg_b = jnp.where(upper, rcol[:, hb:hb + 1] - rrow[hb:hb + 1, :], NEG_INF)
                    mix = jnp.exp2(seg_f) + jnp.exp2(seg_b)
                    halves.append(_dot((cb * mix).astype(BF16), x_pair))
                y_diag = jnp.where(lane < HEAD_DIM, halves[0], halves[1])
                po = slice(pair * LANES, (pair + 1) * LANES)
                y = y_diag + y_off[:, po] + yb_ref[rs, pl_] + dsk_ref[:, pl_] * xs
                gated.append(y * _silu(z_ref[rs, pl_]))
            y_ref[rs, gl] = _rms(jnp.concatenate(gated, axis=1), og_ref[:, gl]).astype(BF16)
        return carry

    lax.fori_loop(0, n_chunks, chunk, 0, unroll=True)


def _ssd_bwd(xbc, dt, conv_w, conv_b, dt_bias, a_neg, e_bwd):
    b, s, _ = xbc.shape
    rows = SSD_STEP_CHUNKS * CHUNK
    nc = s // rows
    hpc = rows // HALO_ROWS
    last_halo = s // HALO_ROWS - 1

    def specs(cidx):
        chunk = lambda w: pl.BlockSpec((None, rows, w), lambda bi, i: (bi, cidx(i), 0))
        prev = pl.BlockSpec((None, HALO_ROWS, CONV_DIM),
                            lambda bi, i: (bi, jnp.maximum(cidx(i) * hpc - 1, 0), 0))
        nxt = pl.BlockSpec((None, HALO_ROWS, CONV_DIM),
                           lambda bi, i: (bi, jnp.minimum((cidx(i) + 1) * hpc, last_halo), 0))
        return chunk, prev, nxt

    state = pltpu.VMEM((D_STATE, D_SSM), F32)
    conv_ext = pltpu.VMEM((CONV_DIM // LANES, rows + 2 * HALO_ROWS, LANES), F32)
    small = [_const_spec(dt_bias.shape), _const_spec(a_neg.shape)]

    chunk, prev, nxt = specs(lambda i: nc - 1 - i)
    return pl.pallas_call(
        functools.partial(_ssd_bwd_body, n_steps=nc),
        grid=(b, nc),
        in_specs=[chunk(CONV_DIM), prev, nxt, chunk(LANES), _const_spec(conv_w.shape),
                  _const_spec(conv_b.shape)] + small + [_const_spec(e_bwd.shape)],
        out_specs=[chunk(CONV_DIM), chunk(D_SSM)],
        out_shape=[jax.ShapeDtypeStruct((b, s, CONV_DIM), F32), jax.ShapeDtypeStruct((b, s, D_SSM), F32)],
        scratch_shapes=[conv_ext, state],
        compiler_params=_params("parallel", "arbitrary"),
        name="ssd_bwd",
    )(xbc, xbc, xbc, dt, conv_w, conv_b, dt_bias, a_neg, e_bwd)


def _mixer_body(narrow_ref, q_ref, kc_ref, kp_ref, kn_ref, vc_ref, vp_ref, vn_ref, bias_ref,
                xa_ref, dt_ref, z_ref, yb_ref, dtb_ref, a_ref, dsk_ref, og_ref, e_ref,
                attn_ref, y_ref, kd, vd, tmp, o_scr, lse_scr, hst, *, seq_len):
    t, hp = pl.program_id(1), pl.program_id(2)

    @pl.when(jnp.logical_and(t == 0, hp == 0))
    def _():
        hst[...] = jnp.zeros_like(hst)

    def step(row_max):
        _attn_step(q_ref, kc_ref, kp_ref, kn_ref, vc_ref, vp_ref, vn_ref, bias_ref, attn_ref,
                   kd, vd, tmp, o_scr, lse_scr, t0=t * ATT_TILE, seq_len=seq_len, row_max=row_max)
        _ssd_fwd_chunks(MIXER_CHUNKS, xa_ref, dt_ref, z_ref, yb_ref, dtb_ref, a_ref, dsk_ref, og_ref,
                        e_ref, y_ref, hst)

    pl.when(narrow_ref[0] != 0)(functools.partial(step, False))
    pl.when(narrow_ref[0] == 0)(functools.partial(step, True))


def _mixer(q, k, v, att, xa, dt, z, yb, dt_bias, a_neg, d_skip, out_g, e_fwd):
    bias = att["bias"]
    n_pairs, b, s, _ = q.shape
    n_pat = len(ATT_PATTERNS)
    last_halo = s // ATT_REACH - 1
    halo_per_tile = ATT_TILE // ATT_REACH
    cur = pl.BlockSpec((None, None, ATT_TILE, LANES), lambda bi, t, hp: (hp, bi, t, 0))
    prev = pl.BlockSpec((None, None, ATT_REACH, LANES),
                        lambda bi, t, hp: (hp, bi, jnp.maximum(t * halo_per_tile - 1, 0), 0))
    nxt = pl.BlockSpec((None, None, ATT_REACH, LANES),
                       lambda bi, t, hp: (hp, bi, jnp.minimum((t + 1) * halo_per_tile, last_halo), 0))
    bias_spec = pl.BlockSpec((bias.shape[0], n_pat, 2, ATT_BLOCK, ATT_KW),
                             lambda bi, t, hp: (0, 0, hp, 0, 0))
    rows = MIXER_CHUNKS * CHUNK
    scan = lambda w: pl.BlockSpec((None, rows, w), lambda bi, t, hp: (bi, t * n_pairs + hp, 0))
    consts = [_const_spec(a.shape) for a in (dt_bias, a_neg, d_skip, out_g, e_fwd)]
    return pl.pallas_call(
        functools.partial(_mixer_body, seq_len=s),
        grid=(b, s // ATT_TILE, n_pairs),
        in_specs=[pl.BlockSpec(memory_space=pltpu.SMEM), cur, cur, prev, nxt, cur, prev, nxt, bias_spec,
                  scan(CONV_DIM), scan(LANES), scan(D_SSM), scan(D_SSM)] + consts,
        out_specs=[cur, scan(D_SSM)],
        out_shape=[jax.ShapeDtypeStruct((n_pairs, b, s, LANES), BF16),
                   jax.ShapeDtypeStruct((b, s, D_SSM), BF16)],
        scratch_shapes=[pltpu.VMEM((n_pat, ATT_WINDOW, LANES), BF16),
                        pltpu.VMEM((n_pat, ATT_WINDOW, LANES), BF16),
                        pltpu.VMEM((ATT_STAGE, ATT_WINDOW // ATT_STAGE, LANES), F32),
                        pltpu.VMEM((n_pat - 1, ATT_TILE, LANES), F32),
                        pltpu.VMEM((n_pat - 1, ATT_TILE, LANES), F32),
                        pltpu.VMEM((D_STATE, D_SSM), F32)],
        compiler_params=_params("parallel", "arbitrary", "arbitrary"),
        name="mixer",
    )(att["narrow"], q, k, k, k, v, v, v, bias, xa, dt, z, yb, dt_bias, a_neg, d_skip, out_g, e_fwd)


def _outproj_ffn_body(x_ref, a_ref, y_ref, g_ref, wa_ref, ws_ref, g2_ref, wg_ref, wu_ref, wd_ref,
                      out_ref):
    attn = jnp.concatenate([a_ref[hp] for hp in range(a_ref.shape[0])], axis=1)
    attn = _rms(attn.astype(F32), g_ref[...]).astype(BF16)
    x2 = x_ref[...] + _dot(attn, wa_ref[...]) + _dot(y_ref[...], ws_ref[...])
    out_ref[...] = _half_step_ffn(x2, g2_ref, wg_ref, wu_ref, wd_ref)


def _outproj_ffn(x, attn, y, g, wa, ws, g2, wg, wu, wd, tm):
    n = x.shape[0]
    row = lambda w: pl.BlockSpec((tm, w), lambda i: (i, 0))
    sq = _const_spec((D_ATT, D_MODEL))
    pairs = pl.BlockSpec((attn.shape[0], tm, LANES), lambda i: (0, i, 0))
    return pl.pallas_call(
        _outproj_ffn_body,
        grid=(n // tm,),
        in_specs=[row(D_MODEL), pairs, row(D_SSM), _const_spec((1, D_ATT)), sq, sq,
                  _const_spec((1, D_MODEL)), _const_spec((D_MODEL, D_FF)), _const_spec((D_MODEL, D_FF)),
                  _const_spec((D_FF, D_MODEL))],
        out_specs=row(D_MODEL),
        out_shape=jax.ShapeDtypeStruct((n, D_MODEL), F32),
        compiler_params=_params("parallel"),
        name="outproj_ffn",
    )(x, attn, y, g, wa, ws, g2, wg, wu, wd)


def _t5_bucket(rel):
    nb = N_REL_BUCKETS // 2
    max_exact = nb // 2
    n = np.abs(rel)
    large = max_exact + (np.log(np.maximum(n, 1) / max_exact)
                         / math.log(REL_MAX_DIST / max_exact) * (nb - max_exact)).astype(np.int32)
    large = np.minimum(large, nb - 1)
    return (np.where(rel > 0, nb, 0) + np.where(n < max_exact, n, large)).astype(np.int32)


def _bias_tables(rel_bias, shift):
    rel_sub = np.arange(ATT_KW)[None, :] - ATT_HALF - np.arange(ATT_BLOCK)[:, None]
    in_window = np.abs(rel_sub) <= ATT_HALF
    col = np.arange(ATT_KW)[None, :]
    tables = []
    for _, dil in ATT_PATTERNS:
        onehot = (_t5_bucket(rel_sub * dil)[:, :, None] == np.arange(N_REL_BUCKETS)).astype(np.float32)
        bias = jnp.einsum("ijk,kh->hij", jnp.asarray(onehot, BF16).astype(F32), rel_bias.astype(F32),
                          precision=lax.Precision.HIGHEST)
        tables.append(bias * LOG2_E - shift)
    bias = jnp.stack(tables, axis=0)
    variants = []
    for v in range(4):
        keep = in_window & ((col >= ATT_HALF) | (v & 1 == 0)) & ((col < ATT_BLOCK + ATT_HALF) | (v & 2 == 0))
        variants.append(jnp.where(keep[None, None], bias, NEG_INF))
    return jnp.stack(variants, axis=0)


def _attention_tables(rel_bias, q_norm_g, k_norm_g):
    logit_bound = (QK_BOUND_SLACK * HEAD_DIM * HEAD_DIM ** -0.5 * LOG2_E
                   * jnp.max(jnp.abs(q_norm_g)) * jnp.max(jnp.abs(k_norm_g))).astype(F32)
    bias_lo = (jnp.min(rel_bias) * LOG2_E).astype(F32)
    bias_hi = (jnp.max(rel_bias) * LOG2_E).astype(F32)
    narrow = 2.0 * logit_bound + (bias_hi - bias_lo) < EXP2_SAFE_RANGE
    return {
        "bias": _bias_tables(rel_bias, jnp.where(narrow, logit_bound + bias_hi, 0.0)),
        "narrow": narrow.astype(jnp.int32).reshape(1),
    }


def _head_expander(first_row, width):
    e = np.zeros((2 * LANES, width), np.float32)
    for h in range(width // HEAD_DIM):
        e[first_row + h, h * HEAD_DIM:(h + 1) * HEAD_DIM] = 1.0
        e[LANES + first_row + h, h * HEAD_DIM:(h + 1) * HEAD_DIM] = 1.0
    return jnp.asarray(e, BF16)


def _block_diag_mean(width):
    i = np.arange(width)
    return jnp.asarray((i[:, None] // HEAD_DIM == i[None, :] // HEAD_DIM) / HEAD_DIM, BF16)


def _layer(x, p, tm):
    b, s, _ = x.shape
    n = b * s
    xf = x.reshape(n, D_MODEL)
    x1 = _ffn(xf, p["ffn1_g"], p["ffn1_wg"], p["ffn1_wu"], p["ffn1_wd"], tm)
    q, k, v, z, xbc, dt = _inproj(x1, p["mix_g"], p["wq"], p["wk"], p["wv"], p["wz"], p["wx"],
                                  p["wdt"], p["qg"], p["kg"], p["bd"], tm)
    seq = lambda a: a.reshape(b, s, a.shape[-1])
    pair_seq = lambda a: a.reshape(a.shape[0], b, s, LANES)
    xa, yb = _ssd_bwd(seq(xbc), seq(dt), p["conv_w"], p["conv_b"], p["dt_bias"], p["a_neg"], p["e_bwd"])
    attn, y = _mixer(pair_seq(q), pair_seq(k), pair_seq(v), p["att"], xa, seq(dt), seq(z), yb,
                     p["dt_bias"], p["a_neg"], p["d_skip"], p["ssm_g"], p["e_fwd"])
    out = _outproj_ffn(x1, attn.reshape(attn.shape[0], n, LANES), y.reshape(n, D_SSM), p["attn_g"],
                       p["wo_att"], p["wo_ssm"], p["ffn2_g"], p["ffn2_wg"], p["ffn2_wu"],
                       p["ffn2_wd"], tm)
    return out.reshape(b, s, D_MODEL)


def _prepare(rel_bias, ffn1_norm_g, ffn1_w_gate, ffn1_w_up, ffn1_w_down, mix_norm_g, w_in,
             q_norm_g, k_norm_g, attn_out_g, conv_w, conv_b, dt_bias, a_log, d_skip, ssm_out_g,
             w_out, ffn2_norm_g, ffn2_w_gate, ffn2_w_up, ffn2_w_down):
    row = lambda a: a.reshape(1, -1).astype(F32)
    w16 = lambda a: a.astype(BF16)
    c0, c1, c2, c3 = D_ATT, 2 * D_ATT, 3 * D_ATT, 3 * D_ATT + D_SSM
    c4 = c3 + CONV_DIM
    pad32 = lambda a: jnp.pad(a.reshape(1, 2 * N_HEADS).astype(F32), ((0, 0), (0, LANES - 2 * N_HEADS)))
    return {
        "ffn1_g": row(ffn1_norm_g), "ffn1_wg": w16(ffn1_w_gate), "ffn1_wu": w16(ffn1_w_up),
        "ffn1_wd": w16(ffn1_w_down),
        "ffn2_g": row(ffn2_norm_g), "ffn2_wg": w16(ffn2_w_gate), "ffn2_wu": w16(ffn2_w_up),
        "ffn2_wd": w16(ffn2_w_down),
        "mix_g": row(mix_norm_g),
        "wq": w16(w_in[:, :c0]), "wk": w16(w_in[:, c0:c1]), "wv": w16(w_in[:, c1:c2]),
        "wz": w16(w_in[:, c2:c3]), "wx": w16(w_in[:, c3:c4]),
        "wdt": w16(jnp.pad(w_in[:, c4:], ((0, 0), (0, LANES - 2 * N_HEADS)))),
        "qg": row(jnp.tile(q_norm_g, N_HEADS)) * (HEAD_DIM ** -0.5 * LOG2_E),
        "kg": row(jnp.tile(k_norm_g, N_HEADS)),
        "bd": _block_diag_mean(256),
        "att": _attention_tables(rel_bias, q_norm_g, k_norm_g),
        "attn_g": row(attn_out_g),
        "conv_w": conv_w.astype(F32), "conv_b": row(conv_b),
        "dt_bias": pad32(dt_bias), "a_neg": pad32(-jnp.exp(a_log.astype(F32))),
        "d_skip": row(jnp.repeat(d_skip, HEAD_DIM)), "ssm_g": row(ssm_out_g),
        "e_fwd": _head_expander(0, D_SSM), "e_bwd": _head_expander(N_HEADS, D_SSM),
        "wo_att": w16(w_out[:D_ATT]), "wo_ssm": w16(w_out[D_ATT:]),
    }


def _trunk(x, layers, tm=512):
    for p in layers:
        x = _layer(x, p, tm)
    return x


def kernel(x_prompt, x_sample, rel_bias, ffn1_norm_g, ffn1_w_gate, ffn1_w_up, ffn1_w_down, mix_norm_g, w_in, q_norm_g, k_norm_g, attn_out_g, conv_w, conv_b, dt_bias, a_log, d_skip, ssm_out_g, w_out, ffn2_norm_g, ffn2_w_gate, ffn2_w_up, ffn2_w_down):
    per_layer = (ffn1_norm_g, ffn1_w_gate, ffn1_w_up, ffn1_w_down, mix_norm_g, w_in, q_norm_g,
                 k_norm_g, attn_out_g, conv_w, conv_b, dt_bias, a_log, d_skip, ssm_out_g, w_out,
                 ffn2_norm_g, ffn2_w_gate, ffn2_w_up, ffn2_w_down)
    layers = [_prepare(rel_bias, *(a[l] for a in per_layer)) for l in range(ffn1_norm_g.shape[0])]
    return (_trunk(x_prompt, layers), _trunk(x_sample, layers))
```

```python
import functools
import math

import numpy as np
import jax
import jax.numpy as jnp
from jax import lax
from jax.experimental import pallas as pl
from jax.experimental.pallas import tpu as pltpu

D_MODEL = 1024
D_ATT = 1024
D_SSM = 1024
HEAD_DIM = 64
N_HEADS = 16
ATT_PATTERNS = ((128, 1), (512, 4), (2048, 16))
ATT_BLOCK = 128
ATT_HALF = 64
ATT_KW = ATT_BLOCK + 2 * ATT_HALF
ATT_MAX_DIL = max(d for _, d in ATT_PATTERNS)
ATT_TILE = ATT_BLOCK * ATT_MAX_DIL
ATT_REACH = ATT_HALF * ATT_MAX_DIL
ATT_WINDOW = ATT_TILE + 2 * ATT_REACH
ATT_STAGE = 4
assert tuple(d for _, d in ATT_PATTERNS) == (1, ATT_STAGE, ATT_STAGE * ATT_STAGE)
N_REL_BUCKETS = 32
REL_MAX_DIST = 1024
N_GROUPS = 4
HEADS_PER_GROUP = 4
D_STATE = 128
D_CONV = 5
CHUNK = 128
SSD_STEP_CHUNKS = 4
MIXER_CHUNKS = ATT_TILE // CHUNK // (N_HEADS // 2)
CONV_DIM = D_SSM + 2 * N_GROUPS * D_STATE
D_FF = 2816
EPS = 1e-6
NEG_INF = -1e30
LOG2_E = 1.4426950408889634
EXP2_SAFE_RANGE = 100.0
QK_BOUND_SLACK = 1.05

LANES = 128
MXU_TILE = 256
HALO_ROWS = 8
VMEM_LIMIT = 56 * 1024 * 1024

F32 = jnp.float32
BF16 = jnp.bfloat16


def _params(*sem):
    return pltpu.CompilerParams(dimension_semantics=sem, vmem_limit_bytes=VMEM_LIMIT)


def _const_spec(shape):
    n = len(shape)
    return pl.BlockSpec(shape, lambda *_: (0,) * n, pipeline_mode=pl.Buffered(1))


def _rms(x, g):
    ms = jnp.mean(x * x, axis=-1, keepdims=True)
    return x * lax.rsqrt(ms + EPS) * g


def _silu(x):
    h = 0.5 * x
    return h + h * jnp.tanh(h)


def _dot(a, b):
    return jnp.dot(a, b, preferred_element_type=F32)


def _hi_lo(vals):
    hi = vals.astype(BF16)
    lo = (vals - hi.astype(F32)).astype(BF16)
    return jnp.concatenate([hi, lo], axis=1)


def _expand(split, e_ref, cols=slice(None)):
    return _dot(split, e_ref[:, cols])


def _half_step_ffn(x, g_ref, wg_ref, wu_ref, wd_ref):
    h = _rms(x, g_ref[...]).astype(BF16)
    gate = _dot(h, wg_ref[...])
    up = _dot(h, wu_ref[...])
    act = (_silu(gate) * up).astype(BF16)
    return x + 0.5 * _dot(act, wd_ref[...])


def _ffn_body(x_ref, g_ref, wg_ref, wu_ref, wd_ref, o_ref):
    o_ref[...] = _half_step_ffn(x_ref[...], g_ref, wg_ref, wu_ref, wd_ref)


def _ffn(x, g, wg, wu, wd, tm):
    n = x.shape[0]
    row = pl.BlockSpec((tm, D_MODEL), lambda i: (i, 0))
    return pl.pallas_call(
        _ffn_body,
        grid=(n // tm,),
        in_specs=[row, _const_spec((1, D_MODEL)), _const_spec((D_MODEL, D_FF)),
                  _const_spec((D_MODEL, D_FF)), _const_spec((D_FF, D_MODEL))],
        out_specs=row,
        out_shape=jax.ShapeDtypeStruct((n, D_MODEL), F32),
        compiler_params=_params("parallel"),
        name="ffn",
    )(x, g, wg, wu, wd)


def _inproj_body(x_ref, g_ref, wq_ref, wk_ref, wv_ref, wz_ref, wx_ref, wdt_ref, qg_ref, kg_ref,
                 bd_ref, q_ref, k_ref, v_ref, z_ref, xbc_ref, dt_ref):
    h = _rms(x_ref[...], g_ref[...]).astype(BF16)

    def head_norm(t, gain):
        t2 = (t * t).astype(BF16)
        w = bd_ref.shape[0]
        ms = jnp.concatenate([_dot(t2[:, j * w:(j + 1) * w], bd_ref[...])
                              for j in range(D_ATT // w)], axis=1)
        return t * lax.rsqrt(ms + EPS) * gain

    def put_pairs(ref, val):
        for hp in range(N_HEADS // 2):
            ref[hp] = val[:, hp * LANES:(hp + 1) * LANES]

    put_pairs(q_ref, head_norm(_dot(h, wq_ref[...]), qg_ref[...]))
    put_pairs(k_ref, head_norm(_dot(h, wk_ref[...]), kg_ref[...]))
    put_pairs(v_ref, _dot(h, wv_ref[...]))
    z_ref[...] = _dot(h, wz_ref[...])
    xbc_ref[...] = _dot(h, wx_ref[...])
    dt_ref[...] = _dot(h, wdt_ref[...])


def _inproj(x, g, wq, wk, wv, wz, wx, wdt, qg, kg, bd, tm):
    n = x.shape[0]
    row = lambda w: pl.BlockSpec((tm, w), lambda i: (i, 0))
    sq = _const_spec((D_MODEL, D_ATT))
    out = lambda w: jax.ShapeDtypeStruct((n, w), F32)
    pairs = pl.BlockSpec((N_HEADS // 2, tm, LANES), lambda i: (0, i, 0))
    pairs_out = jax.ShapeDtypeStruct((N_HEADS // 2, n, LANES), F32)
    return pl.pallas_call(
        _inproj_body,
        grid=(n // tm,),
        in_specs=[row(D_MODEL), _const_spec((1, D_MODEL)), sq, sq, sq, sq,
                  _const_spec((D_MODEL, CONV_DIM)), _const_spec((D_MODEL, LANES)),
                  _const_spec((1, D_ATT)), _const_spec((1, D_ATT)), _const_spec(bd.shape)],
        out_specs=[pairs, pairs, pairs, row(D_SSM), row(CONV_DIM), row(LANES)],
        out_shape=[pairs_out, pairs_out, pairs_out, out(D_SSM), out(CONV_DIM), out(LANES)],
        compiler_params=_params("parallel"),
        name="inproj",
    )(x, g, wq, wk, wv, wz, wx, wdt, qg, kg, bd)


def _rows(start, size, dil):
    return pl.ds(start, size) if dil == 1 else pl.ds(start, size, stride=dil)


def _regroup_keys(prev, cur, nxt, tmp, dst):
    s4 = ATT_STAGE
    lo, hi = ATT_REACH - ATT_HALF, ATT_REACH + ATT_TILE + ATT_HALF
    dst[0, lo:ATT_REACH, :] = prev[lo:ATT_REACH, :].astype(BF16)
    dst[0, ATT_REACH:ATT_REACH + ATT_TILE, :] = cur[...].astype(BF16)
    dst[0, ATT_REACH + ATT_TILE:hi, :] = nxt[0:ATT_HALF, :].astype(BF16)
    h4, t4 = ATT_REACH // s4, ATT_TILE // s4
    for r in range(s4):
        tmp[r, 0:h4, :] = prev[pl.ds(r, h4, stride=s4), :]
        tmp[r, h4:h4 + t4, :] = cur[pl.ds(r, t4, stride=s4), :]
        tmp[r, h4 + t4:, :] = nxt[pl.ds(r, h4, stride=s4), :]
    lo, hi = h4 - ATT_HALF, h4 + t4 + ATT_HALF
    for r in range(s4):
        dst[1, r * (ATT_WINDOW // s4) + lo:r * (ATT_WINDOW // s4) + hi, :] = tmp[r, lo:hi, :].astype(BF16)
    per = ATT_WINDOW // (s4 * s4)
    for r in range(s4 * s4):
        dst[2, r * per:(r + 1) * per, :] = tmp[r % s4, pl.ds(r // s4, per, stride=s4), :].astype(BF16)


def _attn_step(q_ref, kc_ref, kp_ref, kn_ref, vc_ref, vp_ref, vn_ref, bias_ref, out_ref,
               kd, vd, tmp, o_scr, lse_scr, *, t0, seq_len, row_max):
    _regroup_keys(kp_ref, kc_ref, kn_ref, tmp, kd)
    _regroup_keys(vp_ref, vc_ref, vn_ref, tmp, vd)

    first_head = lax.broadcasted_iota(jnp.int32, (ATT_BLOCK, LANES), 1) < HEAD_DIM
    first_head_kw = lax.broadcasted_iota(jnp.int32, (ATT_KW, LANES), 1) < HEAD_DIM
    head_ones = (first_head_kw.astype(BF16), (~first_head_kw).astype(BF16))

    def block(p, dil, r, j):
        blocks_per_residue = ATT_TILE // (dil * ATT_BLOCK)
        q0 = j * (ATT_BLOCK * dil) + r
        k0 = r * (ATT_WINDOW // dil) + ATT_REACH // dil - ATT_HALF + j * ATT_BLOCK
        variant = 0
        if j == 0:
            variant = variant + (t0 == 0).astype(jnp.int32)
        if j == blocks_per_residue - 1:
            variant = variant + 2 * (t0 + ATT_TILE == seq_len).astype(jnp.int32)
        qp = q_ref[_rows(q0, ATT_BLOCK, dil), :].astype(BF16)
        kp = kd[p, k0:k0 + ATT_KW, :]
        vp = vd[p, k0:k0 + ATT_KW, :]
        zq, zv = jnp.zeros_like(qp), jnp.zeros_like(vp)
        q2 = jnp.concatenate([jnp.where(first_head, qp, zq), jnp.where(first_head, zq, qp)], axis=0)
        s2 = lax.dot_general(q2, kp, (((1,), (1,)), ((), ())), preferred_element_type=F32)
        es, ms = [], []
        for sub in range(2):
            s = s2[sub * ATT_BLOCK:(sub + 1) * ATT_BLOCK, :] + bias_ref[variant, p, sub]
            if row_max:
                m = jnp.max(s, axis=-1, keepdims=True)
                s = s - m
                ms.append(m)
            es.append(jnp.exp2(s).astype(BF16))
        v2 = jnp.concatenate([
            jnp.concatenate([jnp.where(first_head_kw, vp, zv), head_ones[0]], axis=1),
            jnp.concatenate([jnp.where(first_head_kw, zv, vp), head_ones[1]], axis=1)], axis=0)
        ol = _dot(jnp.concatenate(es, axis=1), v2)
        l = ol[:, LANES:]
        if not row_max:
            return ol[:, :LANES], l
        return ol[:, :LANES] * (1.0 / l), jnp.where(first_head, ms[0], ms[1]) + jnp.log2(l)

    for p in range(len(ATT_PATTERNS) - 1, 0, -1):
        dil = ATT_PATTERNS[p][1]
        blocks_per_residue = ATT_TILE // (dil * ATT_BLOCK)
        for idx in range(ATT_TILE // ATT_BLOCK):
            r, j = idx // blocks_per_residue, idx % blocks_per_residue
            rows = _rows(j * (ATT_BLOCK * dil) + r, ATT_BLOCK, dil)
            o_scr[p - 1, rows, :], lse_scr[p - 1, rows, :] = block(p, dil, r, j)
    for j in range(ATT_TILE // ATT_BLOCK):
        rows = pl.ds(j * ATT_BLOCK, ATT_BLOCK)
        o1, l1 = block(0, 1, 0, j)
        l2, l3 = lse_scr[0, rows, :], lse_scr[1, rows, :]
        if not row_max:
            mixed = o1 + o_scr[0, rows, :] + o_scr[1, rows, :]
            out_ref[rows, :] = (mixed * (1.0 / (l1 + l2 + l3))).astype(BF16)
            continue
        m = jnp.maximum(jnp.maximum(l1, l2), l3)
        e1, e2, e3 = jnp.exp2(l1 - m), jnp.exp2(l2 - m), jnp.exp2(l3 - m)
        mixed = e1 * o1 + e2 * o_scr[0, rows, :] + e3 * o_scr[1, rows, :]
        out_ref[rows, :] = (mixed * (1.0 / (e1 + e2 + e3))).astype(BF16)


def _stage_conv_input(xc_ref, xp_ref, xn_ref, ext, has_prev, has_next):
    rows = xc_ref.shape[0]
    for sl in range(CONV_DIM // LANES):
        lanes = slice(sl * LANES, (sl + 1) * LANES)
        ext[sl, 0:HALO_ROWS, :] = jnp.where(has_prev, xp_ref[:, lanes], 0.0)
        ext[sl, HALO_ROWS:HALO_ROWS + rows, :] = xc_ref[:, lanes]
        ext[sl, HALO_ROWS + rows:, :] = jnp.where(has_next, xn_ref[:, lanes], 0.0)


def _conv_silu(ext, cw_ref, cb_ref, xa_ref, r0):
    pad = D_CONV // 2
    for sl in range(CONV_DIM // LANES):
        lanes = slice(sl * LANES, (sl + 1) * LANES)
        acc = cb_ref[:, lanes]
        for j in range(D_CONV):
            tap = ext[sl, pl.ds(r0 + (HALO_ROWS - pad + j), CHUNK, stride=1), :]
            acc = acc + tap * cw_ref[j:j + 1, lanes]
        xa_ref[pl.ds(r0, CHUNK), lanes] = _silu(acc)


def _softplus(x):
    return jnp.maximum(x, 0.0) + jnp.log(1.0 + jnp.exp(-jnp.abs(x)))


def _tri(lower):
    r = lax.broadcasted_iota(jnp.int32, (CHUNK, CHUNK), 0)
    c = lax.broadcasted_iota(jnp.int32, (CHUNK, CHUNK), 1)
    return (r >= c) if lower else (r <= c)


def _cumsum_rows(a, lower):
    return jnp.dot(_tri(lower).astype(F32), a, precision=lax.Precision.HIGHEST,
                   preferred_element_type=F32)


def _chunk_rows(i):
    return pl.ds(pl.multiple_of(i * CHUNK, CHUNK), CHUNK)


def _ssd_bwd_body(xc_ref, xp_ref, xn_ref, dt_ref, cw_ref, cb_ref, dtb_ref, a_ref, e_ref,
                  xa_ref, yoff_ref, ext, hst, *, n_steps):
    i = pl.program_id(1)
    step = n_steps - 1 - i

    @pl.when(i == 0)
    def _():
        hst[...] = jnp.zeros_like(hst)

    _stage_conv_input(xc_ref, xp_ref, xn_ref, ext, step > 0, step < n_steps - 1)

    def chunk(ci, carry):
        rs = _chunk_rows(SSD_STEP_CHUNKS - 1 - ci)
        _conv_silu(ext, cw_ref, cb_ref, xa_ref, rs.start)
        dt = _softplus(dt_ref[rs, :] + dtb_ref[...])
        rcum = _cumsum_rows(dt * a_ref[...], lower=False)
        decay_in = _expand(_hi_lo(jnp.exp(rcum)), e_ref)
        w_in = _expand(_hi_lo(jnp.exp(rcum[0:1, :] - rcum) * dt), e_ref)
        chunk_decay = _expand(_hi_lo(jnp.broadcast_to(jnp.exp(rcum[0:1, :]), (HALO_ROWS, LANES))),
                              e_ref)[0:1, :]
        for g in range(N_GROUPS):
            gl = slice(g * HEADS_PER_GROUP * HEAD_DIM, (g + 1) * HEADS_PER_GROUP * HEAD_DIM)
            bg = xa_ref[rs, D_SSM + g * D_STATE:D_SSM + (g + 1) * D_STATE].astype(BF16)
            cg = xa_ref[rs, D_SSM + (N_GROUPS + g) * D_STATE:D_SSM + (N_GROUPS + g + 1) * D_STATE].astype(BF16)
            xw = (xa_ref[rs, gl] * w_in[:, gl]).astype(BF16)
            h_in = hst[:, gl]
            yoff_ref[rs, gl] = _dot(cg, h_in.astype(BF16)) * decay_in[:, gl]
            upd = lax.dot_general(bg, xw, (((0,), (0,)), ((), ())), preferred_element_type=F32)
            hst[:, gl] = h_in * chunk_decay[:, gl] + upd
        return carry

    lax.fori_loop(0, SSD_STEP_CHUNKS, chunk, 0, unroll=True)


def _ssd_fwd_chunks(n_chunks, xa_ref, dt_ref, z_ref, yb_ref, dtb_ref, a_ref, dsk_ref, og_ref, e_ref,
                    y_ref, hst):
    lower, upper = _tri(True), _tri(False)
    lane = lax.broadcasted_iota(jnp.int32, (CHUNK, LANES), 1)

    def chunk(ci, carry):
        rs = _chunk_rows(ci)
        dt = _softplus(dt_ref[rs, :] + dtb_ref[...])
        a = dt * a_ref[...]
        fcum = _cumsum_rows(a, lower=True)
        rcum = _cumsum_rows(a, lower=False)
        decay_in = _expand(_hi_lo(jnp.exp(fcum)), e_ref)
        w_in = _expand(_hi_lo(jnp.exp(fcum[CHUNK - 1:CHUNK, :] - fcum) * dt), e_ref)
        chunk_decay = _expand(_hi_lo(jnp.broadcast_to(jnp.exp(fcum[CHUNK - 1:CHUNK, :]),
                                                       (HALO_ROWS, LANES))), e_ref)[0:1, :]
        log2_dt = jnp.log(dt) * LOG2_E
        fcol, rcol = fcum * LOG2_E, rcum * LOG2_E
        frow, rrow = (fcol - log2_dt).T, (rcol - log2_dt).T

        for g in range(N_GROUPS):
            gl = slice(g * HEADS_PER_GROUP * HEAD_DIM, (g + 1) * HEADS_PER_GROUP * HEAD_DIM)
            bg = xa_ref[rs, D_SSM + g * D_STATE:D_SSM + (g + 1) * D_STATE].astype(BF16)
            cg = xa_ref[rs, D_SSM + (N_GROUPS + g) * D_STATE:D_SSM + (N_GROUPS + g + 1) * D_STATE].astype(BF16)
            cb = lax.dot_general(cg, bg, (((1,), (1,)), ((), ())), preferred_element_type=F32)
            xw = (xa_ref[rs, gl] * w_in[:, gl]).astype(BF16)
            h_in = hst[:, gl]
            y_off = _dot(cg, h_in.astype(BF16)) * decay_in[:, gl]
            upd = lax.dot_general(bg, xw, (((0,), (0,)), ((), ())), preferred_element_type=F32)
            hst[:, gl] = h_in * chunk_decay[:, gl] + upd
            gated = []
            for pair in range(HEADS_PER_GROUP // 2):
                pl_ = slice(gl.start + pair * LANES, gl.start + (pair + 1) * LANES)
                xs = xa_ref[rs, pl_]
                x_pair = xs.astype(BF16)
                halves = []
                for sub in range(2):
                    h = g * HEADS_PER_GROUP + 2 * pair + sub
                    hb = N_HEADS + h
                    seg_f = jnp.where(lower, fcol[:, h:h + 1] - frow[h:h + 1, :], NEG_INF)
                    seg_b = jnp.where(upper, rcol[:, hb:hb + 1] - rrow[hb:hb + 1, :], NEG_INF)
                    mix = jnp.exp2(seg_f) + jnp.exp2(seg_b)
                    halves.append(_dot((cb * mix).astype(BF16), x_pair))
                y_diag = jnp.where(lane < HEAD_DIM, halves[0], halves[1])
                po = slice(pair * LANES, (pair + 1) * LANES)
                y = y_diag + y_off[:, po] + yb_ref[rs, pl_] + dsk_ref[:, pl_] * xs
                gated.append(y * _silu(z_ref[rs, pl_]))
            y_ref[rs, gl] = _rms(jnp.concatenate(gated, axis=1), og_ref[:, gl]).astype(BF16)
        return carry

    lax.fori_loop(0, n_chunks, chunk, 0, unroll=True)


def _ssd_bwd(xbc, dt, conv_w, conv_b, dt_bias, a_neg, e_bwd):
    b, s, _ = xbc.shape
    rows = SSD_STEP_CHUNKS * CHUNK
    nc = s // rows
    hpc = rows // HALO_ROWS
    last_halo = s // HALO_ROWS - 1

    def specs(cidx):
        chunk = lambda w: pl.BlockSpec((None, rows, w), lambda bi, i: (bi, cidx(i), 0))
        prev = pl.BlockSpec((None, HALO_ROWS, CONV_DIM),
                            lambda bi, i: (bi, jnp.maximum(cidx(i) * hpc - 1, 0), 0))
        nxt = pl.BlockSpec((None, HALO_ROWS, CONV_DIM),
                           lambda bi, i: (bi, jnp.minimum((cidx(i) + 1) * hpc, last_halo), 0))
        return chunk, prev, nxt

    state = pltpu.VMEM((D_STATE, D_SSM), F32)
    conv_ext = pltpu.VMEM((CONV_DIM // LANES, rows + 2 * HALO_ROWS, LANES), F32)
    small = [_const_spec(dt_bias.shape), _const_spec(a_neg.shape)]

    chunk, prev, nxt = specs(lambda i: nc - 1 - i)
    return pl.pallas_call(
        functools.partial(_ssd_bwd_body, n_steps=nc),
        grid=(b, nc),
        in_specs=[chunk(CONV_DIM), prev, nxt, chunk(LANES), _const_spec(conv_w.shape),
                  _const_spec(conv_b.shape)] + small + [_const_spec(e_bwd.shape)],
        out_specs=[chunk(CONV_DIM), chunk(D_SSM)],
        out_shape=[jax.ShapeDtypeStruct((b, s, CONV_DIM), F32), jax.ShapeDtypeStruct((b, s, D_SSM), F32)],
        scratch_shapes=[conv_ext, state],
        compiler_params=_params("parallel", "arbitrary"),
        name="ssd_bwd",
    )(xbc, xbc, xbc, dt, conv_w, conv_b, dt_bias, a_neg, e_bwd)


def _mixer_body(narrow_ref, q_ref, kc_ref, kp_ref, kn_ref, vc_ref, vp_ref, vn_ref, bias_ref,
                xa_ref, dt_ref, z_ref, yb_ref, dtb_ref, a_ref, dsk_ref, og_ref, e_ref,
                attn_ref, y_ref, kd, vd, tmp, o_scr, lse_scr, hst, *, seq_len):
    hp, t = pl.program_id(1), pl.program_id(2)

    @pl.when(jnp.logical_and(t == 0, hp == 0))
    def _():
        hst[...] = jnp.zeros_like(hst)

    def step(row_max):
        _attn_step(q_ref, kc_ref, kp_ref, kn_ref, vc_ref, vp_ref, vn_ref, bias_ref, attn_ref,
                   kd, vd, tmp, o_scr, lse_scr, t0=t * ATT_TILE, seq_len=seq_len, row_max=row_max)
        _ssd_fwd_chunks(MIXER_CHUNKS, xa_ref, dt_ref, z_ref, yb_ref, dtb_ref, a_ref, dsk_ref, og_ref,
                        e_ref, y_ref, hst)

    pl.when(narrow_ref[0] != 0)(functools.partial(step, False))
    pl.when(narrow_ref[0] == 0)(functools.partial(step, True))


def _mixer(q, k, v, att, xa, dt, z, yb, dt_bias, a_neg, d_skip, out_g, e_fwd):
    bias = att["bias"]
    n_pairs, b, s, _ = q.shape
    n_pat = len(ATT_PATTERNS)
    last_halo = s // ATT_REACH - 1
    halo_per_tile = ATT_TILE // ATT_REACH
    n_tiles = s // ATT_TILE
    cur = pl.BlockSpec((None, None, ATT_TILE, LANES), lambda bi, hp, t: (hp, bi, t, 0))
    prev = pl.BlockSpec((None, None, ATT_REACH, LANES),
                        lambda bi, hp, t: (hp, bi, jnp.maximum(t * halo_per_tile - 1, 0), 0))
    nxt = pl.BlockSpec((None, None, ATT_REACH, LANES),
                       lambda bi, hp, t: (hp, bi, jnp.minimum((t + 1) * halo_per_tile, last_halo), 0))
    bias_spec = pl.BlockSpec((bias.shape[0], n_pat, 2, ATT_BLOCK, ATT_KW),
                             lambda bi, hp, t: (0, 0, hp, 0, 0))
    rows = MIXER_CHUNKS * CHUNK
    scan = lambda w: pl.BlockSpec((None, rows, w), lambda bi, hp, t: (bi, hp * n_tiles + t, 0))
    consts = [_const_spec(a.shape) for a in (dt_bias, a_neg, d_skip, out_g, e_fwd)]
    return pl.pallas_call(
        functools.partial(_mixer_body, seq_len=s),
        grid=(b, n_pairs, n_tiles),
        in_specs=[pl.BlockSpec(memory_space=pltpu.SMEM), cur, cur, prev, nxt, cur, prev, nxt, bias_spec,
                  scan(CONV_DIM), scan(LANES), scan(D_SSM), scan(D_SSM)] + consts,
        out_specs=[cur, scan(D_SSM)],
        out_shape=[jax.ShapeDtypeStruct((n_pairs, b, s, LANES), BF16),
                   jax.ShapeDtypeStruct((b, s, D_SSM), BF16)],
        scratch_shapes=[pltpu.VMEM((n_pat, ATT_WINDOW, LANES), BF16),
                        pltpu.VMEM((n_pat, ATT_WINDOW, LANES), BF16),
                        pltpu.VMEM((ATT_STAGE, ATT_WINDOW // ATT_STAGE, LANES), F32),
                        pltpu.VMEM((n_pat - 1, ATT_TILE, LANES), F32),
                        pltpu.VMEM((n_pat - 1, ATT_TILE, LANES), F32),
                        pltpu.VMEM((D_STATE, D_SSM), F32)],
        compiler_params=_params("parallel", "arbitrary", "arbitrary"),
        name="mixer",
    )(att["narrow"], q, k, k, k, v, v, v, bias, xa, dt, z, yb, dt_bias, a_neg, d_skip, out_g, e_fwd)


def _outproj_ffn_body(x_ref, a_ref, y_ref, g_ref, wa_ref, ws_ref, g2_ref, wg_ref, wu_ref, wd_ref,
                      out_ref):
    attn = jnp.concatenate([a_ref[hp] for hp in range(a_ref.shape[0])], axis=1)
    attn = _rms(attn.astype(F32), g_ref[...]).astype(BF16)
    x2 = x_ref[...] + _dot(attn, wa_ref[...]) + _dot(y_ref[...], ws_ref[...])
    out_ref[...] = _half_step_ffn(x2, g2_ref, wg_ref, wu_ref, wd_ref)


def _outproj_ffn(x, attn, y, g, wa, ws, g2, wg, wu, wd, tm):
    n = x.shape[0]
    row = lambda w: pl.BlockSpec((tm, w), lambda i: (i, 0))
    sq = _const_spec((D_ATT, D_MODEL))
    pairs = pl.BlockSpec((attn.shape[0], tm, LANES), lambda i: (0, i, 0))
    return pl.pallas_call(
        _outproj_ffn_body,
        grid=(n // tm,),
        in_specs=[row(D_MODEL), pairs, row(D_SSM), _const_spec((1, D_ATT)), sq, sq,
                  _const_spec((1, D_MODEL)), _const_spec((D_MODEL, D_FF)), _const_spec((D_MODEL, D_FF)),
                  _const_spec((D_FF, D_MODEL))],
        out_specs=row(D_MODEL),
        out_shape=jax.ShapeDtypeStruct((n, D_MODEL), F32),
        compiler_params=_params("parallel"),
        name="outproj_ffn",
    )(x, attn, y, g, wa, ws, g2, wg, wu, wd)


def _t5_bucket(rel):
    nb = N_REL_BUCKETS // 2
    max_exact = nb // 2
    n = np.abs(rel)
    large = max_exact + (np.log(np.maximum(n, 1) / max_exact)
                         / math.log(REL_MAX_DIST / max_exact) * (nb - max_exact)).astype(np.int32)
    large = np.minimum(large, nb - 1)
    return (np.where(rel > 0, nb, 0) + np.where(n < max_exact, n, large)).astype(np.int32)


def _bias_tables(rel_bias, shift):
    rel_sub = np.arange(ATT_KW)[None, :] - ATT_HALF - np.arange(ATT_BLOCK)[:, None]
    in_window = np.abs(rel_sub) <= ATT_HALF
    col = np.arange(ATT_KW)[None, :]
    tables = []
    for _, dil in ATT_PATTERNS:
        onehot = (_t5_bucket(rel_sub * dil)[:, :, None] == np.arange(N_REL_BUCKETS)).astype(np.float32)
        bias = jnp.einsum("ijk,kh->hij", jnp.asarray(onehot, BF16).astype(F32), rel_bias.astype(F32),
                          precision=lax.Precision.HIGHEST)
        tables.append(bias * LOG2_E - shift)
    bias = jnp.stack(tables, axis=0)
    variants = []
    for v in range(4):
        keep = in_window & ((col >= ATT_HALF) | (v & 1 == 0)) & ((col < ATT_BLOCK + ATT_HALF) | (v & 2 == 0))
        variants.append(jnp.where(keep[None, None], bias, NEG_INF))
    return jnp.stack(variants, axis=0)


def _attention_tables(rel_bias, q_norm_g, k_norm_g):
    logit_bound = (QK_BOUND_SLACK * HEAD_DIM * HEAD_DIM ** -0.5 * LOG2_E
                   * jnp.max(jnp.abs(q_norm_g)) * jnp.max(jnp.abs(k_norm_g))).astype(F32)
    bias_lo = (jnp.min(rel_bias) * LOG2_E).astype(F32)
    bias_hi = (jnp.max(rel_bias) * LOG2_E).astype(F32)
    narrow = 2.0 * logit_bound + (bias_hi - bias_lo) < EXP2_SAFE_RANGE
    return {
        "bias": _bias_tables(rel_bias, jnp.where(narrow, logit_bound + bias_hi, 0.0)),
        "narrow": narrow.astype(jnp.int32).reshape(1),
    }


def _head_expander(first_row, width):
    e = np.zeros((2 * LANES, width), np.float32)
    for h in range(width // HEAD_DIM):
        e[first_row + h, h * HEAD_DIM:(h + 1) * HEAD_DIM] = 1.0
        e[LANES + first_row + h, h * HEAD_DIM:(h + 1) * HEAD_DIM] = 1.0
    return jnp.asarray(e, BF16)


def _block_diag_mean(width):
    i = np.arange(width)
    return jnp.asarray((i[:, None] // HEAD_DIM == i[None, :] // HEAD_DIM) / HEAD_DIM, BF16)


def _layer(x, p, tm):
    b, s, _ = x.shape
    n = b * s
    xf = x.reshape(n, D_MODEL)
    x1 = _ffn(xf, p["ffn1_g"], p["ffn1_wg"], p["ffn1_wu"], p["ffn1_wd"], tm)
    q, k, v, z, xbc, dt = _inproj(x1, p["mix_g"], p["wq"], p["wk"], p["wv"], p["wz"], p["wx"],
                                  p["wdt"], p["qg"], p["kg"], p["bd"], tm)
    seq = lambda a: a.reshape(b, s, a.shape[-1])
    pair_seq = lambda a: a.reshape(a.shape[0], b, s, LANES)
    xa, yb = _ssd_bwd(seq(xbc), seq(dt), p["conv_w"], p["conv_b"], p["dt_bias"], p["a_neg"], p["e_bwd"])
    attn, y = _mixer(pair_seq(q), pair_seq(k), pair_seq(v), p["att"], xa, seq(dt), seq(z), yb,
                     p["dt_bias"], p["a_neg"], p["d_skip"], p["ssm_g"], p["e_fwd"])
    out = _outproj_ffn(x1, attn.reshape(attn.shape[0], n, LANES), y.reshape(n, D_SSM), p["attn_g"],
                       p["wo_att"], p["wo_ssm"], p["ffn2_g"], p["ffn2_wg"], p["ffn2_wu"],
                       p["ffn2_wd"], tm)
    return out.reshape(b, s, D_MODEL)


def _prepare(rel_bias, ffn1_norm_g, ffn1_w_gate, ffn1_w_up, ffn1_w_down, mix_norm_g, w_in,
             q_norm_g, k_norm_g, attn_out_g, conv_w, conv_b, dt_bias, a_log, d_skip, ssm_out_g,
             w_out, ffn2_norm_g, ffn2_w_gate, ffn2_w_up, ffn2_w_down):
    row = lambda a: a.reshape(1, -1).astype(F32)
    w16 = lambda a: a.astype(BF16)
    c0, c1, c2, c3 = D_ATT, 2 * D_ATT, 3 * D_ATT, 3 * D_ATT + D_SSM
    c4 = c3 + CONV_DIM
    pad32 = lambda a: jnp.pad(a.reshape(1, 2 * N_HEADS).astype(F32), ((0, 0), (0, LANES - 2 * N_HEADS)))
    return {
        "ffn1_g": row(ffn1_norm_g), "ffn1_wg": w16(ffn1_w_gate), "ffn1_wu": w16(ffn1_w_up),
        "ffn1_wd": w16(ffn1_w_down),
        "ffn2_g": row(ffn2_norm_g), "ffn2_wg": w16(ffn2_w_gate), "ffn2_wu": w16(ffn2_w_up),
        "ffn2_wd": w16(ffn2_w_down),
        "mix_g": row(mix_norm_g),
        "wq": w16(w_in[:, :c0]), "wk": w16(w_in[:, c0:c1]), "wv": w16(w_in[:, c1:c2]),
        "wz": w16(w_in[:, c2:c3]), "wx": w16(w_in[:, c3:c4]),
        "wdt": w16(jnp.pad(w_in[:, c4:], ((0, 0), (0, LANES - 2 * N_HEADS)))),
        "qg": row(jnp.tile(q_norm_g, N_HEADS)) * (HEAD_DIM ** -0.5 * LOG2_E),
        "kg": row(jnp.tile(k_norm_g, N_HEADS)),
        "bd": _block_diag_mean(MXU_TILE),
        "att": _attention_tables(rel_bias, q_norm_g, k_norm_g),
        "attn_g": row(attn_out_g),
        "conv_w": conv_w.astype(F32), "conv_b": row(conv_b),
        "dt_bias": pad32(dt_bias), "a_neg": pad32(-jnp.exp(a_log.astype(F32))),
        "d_skip": row(jnp.repeat(d_skip, HEAD_DIM)), "ssm_g": row(ssm_out_g),
        "e_fwd": _head_expander(0, D_SSM), "e_bwd": _head_expander(N_HEADS, D_SSM),
        "wo_att": w16(w_out[:D_ATT]), "wo_ssm": w16(w_out[D_ATT:]),
    }


def _trunk(x, layers, tm=512):
    for p in layers:
        x = _layer(x, p, tm)
    return x


def kernel(x_prompt, x_sample, rel_bias, ffn1_norm_g, ffn1_w_gate, ffn1_w_up, ffn1_w_down, mix_norm_g, w_in, q_norm_g, k_norm_g, attn_out_g, conv_w, conv_b, dt_bias, a_log, d_skip, ssm_out_g, w_out, ffn2_norm_g, ffn2_w_gate, ffn2_w_up, ffn2_w_down):
    per_layer = (ffn1_norm_g, ffn1_w_gate, ffn1_w_up, ffn1_w_down, mix_norm_g, w_in, q_norm_g,
                 k_norm_g, attn_out_g, conv_w, conv_b, dt_bias, a_log, d_skip, ssm_out_g, w_out,
                 ffn2_norm_g, ffn2_w_gate, ffn2_w_up, ffn2_w_down)
    layers = [_prepare(rel_bias, *(a[l] for a in per_layer)) for l in range(ffn1_norm_g.shape[0])]
    return (_trunk(x_prompt, layers), _trunk(x_sample, layers))
```

```python
import functools
import math

import numpy as np
import jax
import jax.numpy as jnp
from jax import lax
from jax.experimental import pallas as pl
from jax.experimental.pallas import tpu as pltpu

D_MODEL = 1024
D_ATT = 1024
D_SSM = 1024
HEAD_DIM = 64
N_HEADS = 16
ATT_PATTERNS = ((128, 1), (512, 4), (2048, 16))
ATT_BLOCK = 128
ATT_HALF = 64
ATT_KW = ATT_BLOCK + 2 * ATT_HALF
ATT_MAX_DIL = max(d for _, d in ATT_PATTERNS)
ATT_TILE = ATT_BLOCK * ATT_MAX_DIL
ATT_REACH = ATT_HALF * ATT_MAX_DIL
ATT_WINDOW = ATT_TILE + 2 * ATT_REACH
ATT_STAGE = 4
assert tuple(d for _, d in ATT_PATTERNS) == (1, ATT_STAGE, ATT_STAGE * ATT_STAGE)
N_REL_BUCKETS = 32
REL_MAX_DIST = 1024
N_GROUPS = 4
HEADS_PER_GROUP = 4
D_STATE = 128
D_CONV = 5
CHUNK = 128
SSD_STEP_CHUNKS = 4
MIXER_CHUNKS = ATT_TILE // CHUNK // (N_HEADS // 2)
CONV_DIM = D_SSM + 2 * N_GROUPS * D_STATE
D_FF = 2816
EPS = 1e-6
NEG_INF = -1e30
LOG2_E = 1.4426950408889634
EXP2_SAFE_RANGE = 100.0
QK_BOUND_SLACK = 1.05

LANES = 128
MXU_TILE = 256
HALO_ROWS = 8
VMEM_LIMIT = 56 * 1024 * 1024

F32 = jnp.float32
BF16 = jnp.bfloat16


def _params(*sem):
    return pltpu.CompilerParams(dimension_semantics=sem, vmem_limit_bytes=VMEM_LIMIT)


def _const_spec(shape):
    n = len(shape)
    return pl.BlockSpec(shape, lambda *_: (0,) * n, pipeline_mode=pl.Buffered(1))


def _rms(x, g):
    ms = jnp.mean(x * x, axis=-1, keepdims=True)
    return x * lax.rsqrt(ms + EPS) * g


def _silu(x):
    h = 0.5 * x
    return h + h * jnp.tanh(h)


def _dot(a, b):
    return jnp.dot(a, b, preferred_element_type=F32)


def _hi_lo(vals):
    hi = vals.astype(BF16)
    lo = (vals - hi.astype(F32)).astype(BF16)
    return jnp.concatenate([hi, lo], axis=1)


def _expand(split, e_ref, cols=slice(None)):
    return _dot(split, e_ref[:, cols])


def _half_step_ffn(x, g_ref, wg_ref, wu_ref, wd_ref):
    h = _rms(x, g_ref[...]).astype(BF16)
    gate = _dot(h, wg_ref[...])
    up = _dot(h, wu_ref[...])
    act = (_silu(gate) * up).astype(BF16)
    return x + 0.5 * _dot(act, wd_ref[...])


def _ffn_body(x_ref, g_ref, wg_ref, wu_ref, wd_ref, o_ref):
    o_ref[...] = _half_step_ffn(x_ref[...], g_ref, wg_ref, wu_ref, wd_ref)


def _ffn(x, g, wg, wu, wd, tm):
    n = x.shape[0]
    row = pl.BlockSpec((tm, D_MODEL), lambda i: (i, 0))
    return pl.pallas_call(
        _ffn_body,
        grid=(n // tm,),
        in_specs=[row, _const_spec((1, D_MODEL)), _const_spec((D_MODEL, D_FF)),
                  _const_spec((D_MODEL, D_FF)), _const_spec((D_FF, D_MODEL))],
        out_specs=row,
        out_shape=jax.ShapeDtypeStruct((n, D_MODEL), F32),
        compiler_params=_params("parallel"),
        name="ffn",
    )(x, g, wg, wu, wd)


def _inproj_body(x_ref, xp_ref, xn_ref, g_ref, wq_ref, wk_ref, wv_ref, wz_ref, wx_ref, wdt_ref,
                 qg_ref, kg_ref, bd_ref, cw_ref, cb_ref, q_ref, k_ref, v_ref, z_ref, xa_ref, dt_ref, ext,
                 *, tiles_per_seq):
    h = _rms(x_ref[...], g_ref[...]).astype(BF16)

    def head_norm(t, gain):
        t2 = (t * t).astype(BF16)
        w = bd_ref.shape[0]
        ms = jnp.concatenate([_dot(t2[:, j * w:(j + 1) * w], bd_ref[...])
                              for j in range(D_ATT // w)], axis=1)
        return t * lax.rsqrt(ms + EPS) * gain

    def put_pairs(ref, val):
        for hp in range(N_HEADS // 2):
            ref[hp] = val[:, hp * LANES:(hp + 1) * LANES]

    put_pairs(q_ref, head_norm(_dot(h, wq_ref[...]), qg_ref[...]))
    put_pairs(k_ref, head_norm(_dot(h, wk_ref[...]), kg_ref[...]))
    put_pairs(v_ref, _dot(h, wv_ref[...]))
    z_ref[...] = _dot(h, wz_ref[...])
    dt_ref[...] = _dot(h, wdt_ref[...])

    tile = pl.program_id(0) % tiles_per_seq
    halo = jnp.concatenate([xp_ref[...], xn_ref[...]], axis=0)
    xbc_halo = _dot(_rms(halo, g_ref[...]).astype(BF16), wx_ref[...])
    xbc = _dot(h, wx_ref[...])
    rows = xbc.shape[0]
    pad = D_CONV // 2
    for sl in range(CONV_DIM // LANES):
        lanes = slice(sl * LANES, (sl + 1) * LANES)
        ext[sl, 0:HALO_ROWS, :] = jnp.where(tile > 0, xbc_halo[0:HALO_ROWS, lanes], 0.0)
        ext[sl, HALO_ROWS:HALO_ROWS + rows, :] = xbc[:, lanes]
        ext[sl, HALO_ROWS + rows:, :] = jnp.where(tile < tiles_per_seq - 1, xbc_halo[HALO_ROWS:, lanes], 0.0)
        acc = cb_ref[:, lanes]
        for j in range(D_CONV):
            acc = acc + ext[sl, pl.ds(HALO_ROWS - pad + j, rows, stride=1), :] * cw_ref[j:j + 1, lanes]
        xa_ref[:, lanes] = _silu(acc)


def _inproj(x, g, wq, wk, wv, wz, wx, wdt, qg, kg, bd, conv_w, conv_b, s, tm):
    n = x.shape[0]
    halos_per_tile = tm // HALO_ROWS
    halo_prev = pl.BlockSpec((HALO_ROWS, D_MODEL), lambda i: (jnp.maximum(i * halos_per_tile - 1, 0), 0))
    halo_next = pl.BlockSpec((HALO_ROWS, D_MODEL),
                             lambda i: (jnp.minimum((i + 1) * halos_per_tile, n // HALO_ROWS - 1), 0))
    row = lambda w: pl.BlockSpec((tm, w), lambda i: (i, 0))
    sq = _const_spec((D_MODEL, D_ATT))
    out = lambda w: jax.ShapeDtypeStruct((n, w), F32)
    pairs = pl.BlockSpec((N_HEADS // 2, tm, LANES), lambda i: (0, i, 0))
    pairs_out = jax.ShapeDtypeStruct((N_HEADS // 2, n, LANES), F32)
    return pl.pallas_call(
        functools.partial(_inproj_body, tiles_per_seq=s // tm),
        grid=(n // tm,),
        in_specs=[row(D_MODEL), halo_prev, halo_next, _const_spec((1, D_MODEL)), sq, sq, sq, sq,
                  _const_spec((D_MODEL, CONV_DIM)), _const_spec((D_MODEL, LANES)),
                  _const_spec((1, D_ATT)), _const_spec((1, D_ATT)), _const_spec(bd.shape),
                  _const_spec(conv_w.shape), _const_spec(conv_b.shape)],
        out_specs=[pairs, pairs, pairs, row(D_SSM), row(CONV_DIM), row(LANES)],
        out_shape=[pairs_out, pairs_out, pairs_out, out(D_SSM), out(CONV_DIM), out(LANES)],
        scratch_shapes=[pltpu.VMEM((CONV_DIM // LANES, tm + 2 * HALO_ROWS, LANES), F32)],
        compiler_params=_params("parallel"),
        name="inproj",
    )(x, x, x, g, wq, wk, wv, wz, wx, wdt, qg, kg, bd, conv_w, conv_b)


def _rows(start, size, dil):
    return pl.ds(start, size) if dil == 1 else pl.ds(start, size, stride=dil)


def _regroup_keys(prev, cur, nxt, tmp, dst):
    s4 = ATT_STAGE
    lo, hi = ATT_REACH - ATT_HALF, ATT_REACH + ATT_TILE + ATT_HALF
    dst[0, lo:ATT_REACH, :] = prev[lo:ATT_REACH, :].astype(BF16)
    dst[0, ATT_REACH:ATT_REACH + ATT_TILE, :] = cur[...].astype(BF16)
    dst[0, ATT_REACH + ATT_TILE:hi, :] = nxt[0:ATT_HALF, :].astype(BF16)
    h4, t4 = ATT_REACH // s4, ATT_TILE // s4
    for r in range(s4):
        tmp[r, 0:h4, :] = prev[pl.ds(r, h4, stride=s4), :]
        tmp[r, h4:h4 + t4, :] = cur[pl.ds(r, t4, stride=s4), :]
        tmp[r, h4 + t4:, :] = nxt[pl.ds(r, h4, stride=s4), :]
    lo, hi = h4 - ATT_HALF, h4 + t4 + ATT_HALF
    for r in range(s4):
        dst[1, r * (ATT_WINDOW // s4) + lo:r * (ATT_WINDOW // s4) + hi, :] = tmp[r, lo:hi, :].astype(BF16)
    per = ATT_WINDOW // (s4 * s4)
    for r in range(s4 * s4):
        dst[2, r * per:(r + 1) * per, :] = tmp[r % s4, pl.ds(r // s4, per, stride=s4), :].astype(BF16)


def _attn_step(q_ref, kc_ref, kp_ref, kn_ref, vc_ref, vp_ref, vn_ref, bias_ref, out_ref,
               kd, vd, tmp, o_scr, lse_scr, *, t0, seq_len, row_max):
    _regroup_keys(kp_ref, kc_ref, kn_ref, tmp, kd)
    _regroup_keys(vp_ref, vc_ref, vn_ref, tmp, vd)

    first_head = lax.broadcasted_iota(jnp.int32, (ATT_BLOCK, LANES), 1) < HEAD_DIM
    first_head_kw = lax.broadcasted_iota(jnp.int32, (ATT_KW, LANES), 1) < HEAD_DIM
    head_ones = (first_head_kw.astype(BF16), (~first_head_kw).astype(BF16))

    def block(p, dil, r, j):
        blocks_per_residue = ATT_TILE // (dil * ATT_BLOCK)
        q0 = j * (ATT_BLOCK * dil) + r
        k0 = r * (ATT_WINDOW // dil) + ATT_REACH // dil - ATT_HALF + j * ATT_BLOCK
        variant = 0
        if j == 0:
            variant = variant + (t0 == 0).astype(jnp.int32)
        if j == blocks_per_residue - 1:
            variant = variant + 2 * (t0 + ATT_TILE == seq_len).astype(jnp.int32)
        qp = q_ref[_rows(q0, ATT_BLOCK, dil), :].astype(BF16)
        kp = kd[p, k0:k0 + ATT_KW, :]
        vp = vd[p, k0:k0 + ATT_KW, :]
        zq, zv = jnp.zeros_like(qp), jnp.zeros_like(vp)
        q2 = jnp.concatenate([jnp.where(first_head, qp, zq), jnp.where(first_head, zq, qp)], axis=0)
        s2 = lax.dot_general(q2, kp, (((1,), (1,)), ((), ())), preferred_element_type=F32)
        es, ms = [], []
        for sub in range(2):
            s = s2[sub * ATT_BLOCK:(sub + 1) * ATT_BLOCK, :] + bias_ref[variant, p, sub]
            if row_max:
                m = jnp.max(s, axis=-1, keepdims=True)
                s = s - m
                ms.append(m)
            es.append(jnp.exp2(s).astype(BF16))
        v2 = jnp.concatenate([
            jnp.concatenate([jnp.where(first_head_kw, vp, zv), head_ones[0]], axis=1),
            jnp.concatenate([jnp.where(first_head_kw, zv, vp), head_ones[1]], axis=1)], axis=0)
        ol = _dot(jnp.concatenate(es, axis=1), v2)
        l = ol[:, LANES:]
        if not row_max:
            return ol[:, :LANES], l
        return ol[:, :LANES] * (1.0 / l), jnp.where(first_head, ms[0], ms[1]) + jnp.log2(l)

    for p in range(len(ATT_PATTERNS) - 1, 0, -1):
        dil = ATT_PATTERNS[p][1]
        blocks_per_residue = ATT_TILE // (dil * ATT_BLOCK)
        for idx in range(ATT_TILE // ATT_BLOCK):
            r, j = idx // blocks_per_residue, idx % blocks_per_residue
            rows = _rows(j * (ATT_BLOCK * dil) + r, ATT_BLOCK, dil)
            o_scr[p - 1, rows, :], lse_scr[p - 1, rows, :] = block(p, dil, r, j)
    for j in range(ATT_TILE // ATT_BLOCK):
        rows = pl.ds(j * ATT_BLOCK, ATT_BLOCK)
        o1, l1 = block(0, 1, 0, j)
        l2, l3 = lse_scr[0, rows, :], lse_scr[1, rows, :]
        if not row_max:
            mixed = o1 + o_scr[0, rows, :] + o_scr[1, rows, :]
            out_ref[rows, :] = (mixed * (1.0 / (l1 + l2 + l3))).astype(BF16)
            continue
        m = jnp.maximum(jnp.maximum(l1, l2), l3)
        e1, e2, e3 = jnp.exp2(l1 - m), jnp.exp2(l2 - m), jnp.exp2(l3 - m)
        mixed = e1 * o1 + e2 * o_scr[0, rows, :] + e3 * o_scr[1, rows, :]
        out_ref[rows, :] = (mixed * (1.0 / (e1 + e2 + e3))).astype(BF16)


def _softplus(x):
    return jnp.maximum(x, 0.0) + jnp.log(1.0 + jnp.exp(-jnp.abs(x)))


def _tri(lower):
    r = lax.broadcasted_iota(jnp.int32, (CHUNK, CHUNK), 0)
    c = lax.broadcasted_iota(jnp.int32, (CHUNK, CHUNK), 1)
    return (r >= c) if lower else (r <= c)


def _cumsum_rows(a, lower):
    return jnp.dot(_tri(lower).astype(F32), a, precision=lax.Precision.HIGHEST,
                   preferred_element_type=F32)


def _chunk_rows(i):
    return pl.ds(pl.multiple_of(i * CHUNK, CHUNK), CHUNK)


def _ssd_bwd_body(xa_ref, dt_ref, dtb_ref, a_ref, e_ref, yoff_ref, hst):
    @pl.when(pl.program_id(1) == 0)
    def _():
        hst[...] = jnp.zeros_like(hst)

    def chunk(ci, carry):
        rs = _chunk_rows(SSD_STEP_CHUNKS - 1 - ci)
        dt = _softplus(dt_ref[rs, :] + dtb_ref[...])
        rcum = _cumsum_rows(dt * a_ref[...], lower=False)
        decay_in = _expand(_hi_lo(jnp.exp(rcum)), e_ref)
        w_in = _expand(_hi_lo(jnp.exp(rcum[0:1, :] - rcum) * dt), e_ref)
        chunk_decay = _expand(_hi_lo(jnp.broadcast_to(jnp.exp(rcum[0:1, :]), (HALO_ROWS, LANES))),
                              e_ref)[0:1, :]
        for g in range(N_GROUPS):
            gl = slice(g * HEADS_PER_GROUP * HEAD_DIM, (g + 1) * HEADS_PER_GROUP * HEAD_DIM)
            bg = xa_ref[rs, D_SSM + g * D_STATE:D_SSM + (g + 1) * D_STATE].astype(BF16)
            cg = xa_ref[rs, D_SSM + (N_GROUPS + g) * D_STATE:D_SSM + (N_GROUPS + g + 1) * D_STATE].astype(BF16)
            xw = (xa_ref[rs, gl] * w_in[:, gl]).astype(BF16)
            h_in = hst[:, gl]
            yoff_ref[rs, gl] = _dot(cg, h_in.astype(BF16)) * decay_in[:, gl]
            upd = lax.dot_general(bg, xw, (((0,), (0,)), ((), ())), preferred_element_type=F32)
            hst[:, gl] = h_in * chunk_decay[:, gl] + upd
        return carry

    lax.fori_loop(0, SSD_STEP_CHUNKS, chunk, 0, unroll=True)


def _ssd_fwd_chunks(n_chunks, xa_ref, dt_ref, z_ref, yb_ref, dtb_ref, a_ref, dsk_ref, og_ref, e_ref,
                    y_ref, hst):
    lower, upper = _tri(True), _tri(False)
    lane = lax.broadcasted_iota(jnp.int32, (CHUNK, LANES), 1)

    def chunk(ci, carry):
        rs = _chunk_rows(ci)
        dt = _softplus(dt_ref[rs, :] + dtb_ref[...])
        a = dt * a_ref[...]
        fcum = _cumsum_rows(a, lower=True)
        rcum = _cumsum_rows(a, lower=False)
        decay_in = _expand(_hi_lo(jnp.exp(fcum)), e_ref)
        w_in = _expand(_hi_lo(jnp.exp(fcum[CHUNK - 1:CHUNK, :] - fcum) * dt), e_ref)
        chunk_decay = _expand(_hi_lo(jnp.broadcast_to(jnp.exp(fcum[CHUNK - 1:CHUNK, :]),
                                                       (HALO_ROWS, LANES))), e_ref)[0:1, :]
        log2_dt = jnp.log(dt) * LOG2_E
        fcol, rcol = fcum * LOG2_E, rcum * LOG2_E
        frow, rrow = (fcol - log2_dt).T, (rcol - log2_dt).T

        for g in range(N_GROUPS):
            gl = slice(g * HEADS_PER_GROUP * HEAD_DIM, (g + 1) * HEADS_PER_GROUP * HEAD_DIM)
            bg = xa_ref[rs, D_SSM + g * D_STATE:D_SSM + (g + 1) * D_STATE].astype(BF16)
            cg = xa_ref[rs, D_SSM + (N_GROUPS + g) * D_STATE:D_SSM + (N_GROUPS + g + 1) * D_STATE].astype(BF16)
            cb = lax.dot_general(cg, bg, (((1,), (1,)), ((), ())), preferred_element_type=F32)
            xw = (xa_ref[rs, gl] * w_in[:, gl]).astype(BF16)
            h_in = hst[:, gl]
            y_off = _dot(cg, h_in.astype(BF16)) * decay_in[:, gl]
            upd = lax.dot_general(bg, xw, (((0,), (0,)), ((), ())), preferred_element_type=F32)
            hst[:, gl] = h_in * chunk_decay[:, gl] + upd
            gated = []
            for pair in range(HEADS_PER_GROUP // 2):
                pl_ = slice(gl.start + pair * LANES, gl.start + (pair + 1) * LANES)
                xs = xa_ref[rs, pl_]
                x_pair = xs.astype(BF16)
                halves = []
                for sub in range(2):
                    h = g * HEADS_PER_GROUP + 2 * pair + sub
                    hb = N_HEADS + h
                    seg_f = jnp.where(lower, fcol[:, h:h + 1] - frow[h:h + 1, :], NEG_INF)
                    seg_b = jnp.where(upper, rcol[:, hb:hb + 1] - rrow[hb:hb + 1, :], NEG_INF)
                    mix = jnp.exp2(seg_f) + jnp.exp2(seg_b)
                    halves.append(_dot((cb * mix).astype(BF16), x_pair))
                y_diag = jnp.where(lane < HEAD_DIM, halves[0], halves[1])
                po = slice(pair * LANES, (pair + 1) * LANES)
                y = y_diag + y_off[:, po] + yb_ref[rs, pl_] + dsk_ref[:, pl_] * xs
                gated.append(y * _silu(z_ref[rs, pl_]))
            y_ref[rs, gl] = _rms(jnp.concatenate(gated, axis=1), og_ref[:, gl]).astype(BF16)
        return carry

    lax.fori_loop(0, n_chunks, chunk, 0, unroll=True)


def _ssd_bwd(xa, dt, dt_bias, a_neg, e_bwd):
    b, s, _ = xa.shape
    rows = SSD_STEP_CHUNKS * CHUNK
    nc = s // rows
    step = lambda w: pl.BlockSpec((None, rows, w), lambda bi, i: (bi, nc - 1 - i, 0))
    return pl.pallas_call(
        _ssd_bwd_body,
        grid=(b, nc),
        in_specs=[step(CONV_DIM), step(LANES), _const_spec(dt_bias.shape), _const_spec(a_neg.shape),
                  _const_spec(e_bwd.shape)],
        out_specs=step(D_SSM),
        out_shape=jax.ShapeDtypeStruct((b, s, D_SSM), F32),
        scratch_shapes=[pltpu.VMEM((D_STATE, D_SSM), F32)],
        compiler_params=_params("parallel", "arbitrary"),
        name="ssd_bwd",
    )(xa, dt, dt_bias, a_neg, e_bwd)


def _mixer_body(narrow_ref, q_ref, kc_ref, kp_ref, kn_ref, vc_ref, vp_ref, vn_ref, bias_ref,
                xa_ref, dt_ref, z_ref, yb_ref, dtb_ref, a_ref, dsk_ref, og_ref, e_ref,
                attn_ref, y_ref, kd, vd, tmp, o_scr, lse_scr, hst, *, seq_len):
    hp, t = pl.program_id(1), pl.program_id(2)

    @pl.when(jnp.logical_and(t == 0, hp == 0))
    def _():
        hst[...] = jnp.zeros_like(hst)

    def step(row_max):
        _attn_step(q_ref, kc_ref, kp_ref, kn_ref, vc_ref, vp_ref, vn_ref, bias_ref, attn_ref,
                   kd, vd, tmp, o_scr, lse_scr, t0=t * ATT_TILE, seq_len=seq_len, row_max=row_max)
        _ssd_fwd_chunks(MIXER_CHUNKS, xa_ref, dt_ref, z_ref, yb_ref, dtb_ref, a_ref, dsk_ref, og_ref,
                        e_ref, y_ref, hst)

    pl.when(narrow_ref[0] != 0)(functools.partial(step, False))
    pl.when(narrow_ref[0] == 0)(functools.partial(step, True))


def _mixer(q, k, v, att, xa, dt, z, yb, dt_bias, a_neg, d_skip, out_g, e_fwd):
    bias = att["bias"]
    n_pairs, b, s, _ = q.shape
    n_pat = len(ATT_PATTERNS)
    last_halo = s // ATT_REACH - 1
    halo_per_tile = ATT_TILE // ATT_REACH
    n_tiles = s // ATT_TILE
    cur = pl.BlockSpec((None, None, ATT_TILE, LANES), lambda bi, hp, t: (hp, bi, t, 0))
    prev = pl.BlockSpec((None, None, ATT_REACH, LANES),
                        lambda bi, hp, t: (hp, bi, jnp.maximum(t * halo_per_tile - 1, 0), 0))
    nxt = pl.BlockSpec((None, None, ATT_REACH, LANES),
                       lambda bi, hp, t: (hp, bi, jnp.minimum((t + 1) * halo_per_tile, last_halo), 0))
    bias_spec = pl.BlockSpec((bias.shape[0], n_pat, 2, ATT_BLOCK, ATT_KW),
                             lambda bi, hp, t: (0, 0, hp, 0, 0))
    rows = MIXER_CHUNKS * CHUNK
    scan = lambda w: pl.BlockSpec((None, rows, w), lambda bi, hp, t: (bi, hp * n_tiles + t, 0))
    consts = [_const_spec(a.shape) for a in (dt_bias, a_neg, d_skip, out_g, e_fwd)]
    return pl.pallas_call(
        functools.partial(_mixer_body, seq_len=s),
        grid=(b, n_pairs, n_tiles),
        in_specs=[pl.BlockSpec(memory_space=pltpu.SMEM), cur, cur, prev, nxt, cur, prev, nxt, bias_spec,
                  scan(CONV_DIM), scan(LANES), scan(D_SSM), scan(D_SSM)] + consts,
        out_specs=[cur, scan(D_SSM)],
        out_shape=[jax.ShapeDtypeStruct((n_pairs, b, s, LANES), BF16),
                   jax.ShapeDtypeStruct((b, s, D_SSM), BF16)],
        scratch_shapes=[pltpu.VMEM((n_pat, ATT_WINDOW, LANES), BF16),
                        pltpu.VMEM((n_pat, ATT_WINDOW, LANES), BF16),
                        pltpu.VMEM((ATT_STAGE, ATT_WINDOW // ATT_STAGE, LANES), F32),
                        pltpu.VMEM((n_pat - 1, ATT_TILE, LANES), F32),
                        pltpu.VMEM((n_pat - 1, ATT_TILE, LANES), F32),
                        pltpu.VMEM((D_STATE, D_SSM), F32)],
        compiler_params=_params("parallel", "arbitrary", "arbitrary"),
        name="mixer",
    )(att["narrow"], q, k, k, k, v, v, v, bias, xa, dt, z, yb, dt_bias, a_neg, d_skip, out_g, e_fwd)


def _outproj_ffn_body(x_ref, a_ref, y_ref, g_ref, wa_ref, ws_ref, g2_ref, wg_ref, wu_ref, wd_ref,
                      out_ref):
    attn = jnp.concatenate([a_ref[hp] for hp in range(a_ref.shape[0])], axis=1)
    attn = _rms(attn.astype(F32), g_ref[...]).astype(BF16)
    x2 = x_ref[...] + _dot(attn, wa_ref[...]) + _dot(y_ref[...], ws_ref[...])
    out_ref[...] = _half_step_ffn(x2, g2_ref, wg_ref, wu_ref, wd_ref)


def _outproj_ffn(x, attn, y, g, wa, ws, g2, wg, wu, wd, tm):
    n = x.shape[0]
    row = lambda w: pl.BlockSpec((tm, w), lambda i: (i, 0))
    sq = _const_spec((D_ATT, D_MODEL))
    pairs = pl.BlockSpec((attn.shape[0], tm, LANES), lambda i: (0, i, 0))
    return pl.pallas_call(
        _outproj_ffn_body,
        grid=(n // tm,),
        in_specs=[row(D_MODEL), pairs, row(D_SSM), _const_spec((1, D_ATT)), sq, sq,
                  _const_spec((1, D_MODEL)), _const_spec((D_MODEL, D_FF)), _const_spec((D_MODEL, D_FF)),
                  _const_spec((D_FF, D_MODEL))],
        out_specs=row(D_MODEL),
        out_shape=jax.ShapeDtypeStruct((n, D_MODEL), F32),
        compiler_params=_params("parallel"),
        name="outproj_ffn",
    )(x, attn, y, g, wa, ws, g2, wg, wu, wd)


def _t5_bucket(rel):
    nb = N_REL_BUCKETS // 2
    max_exact = nb // 2
    n = np.abs(rel)
    large = max_exact + (np.log(np.maximum(n, 1) / max_exact)
                         / math.log(REL_MAX_DIST / max_exact) * (nb - max_exact)).astype(np.int32)
    large = np.minimum(large, nb - 1)
    return (np.where(rel > 0, nb, 0) + np.where(n < max_exact, n, large)).astype(np.int32)


def _bias_tables(rel_bias, shift):
    rel_sub = np.arange(ATT_KW)[None, :] - ATT_HALF - np.arange(ATT_BLOCK)[:, None]
    in_window = np.abs(rel_sub) <= ATT_HALF
    col = np.arange(ATT_KW)[None, :]
    tables = []
    for _, dil in ATT_PATTERNS:
        onehot = (_t5_bucket(rel_sub * dil)[:, :, None] == np.arange(N_REL_BUCKETS)).astype(np.float32)
        bias = jnp.einsum("ijk,kh->hij", jnp.asarray(onehot, BF16).astype(F32), rel_bias.astype(F32),
                          precision=lax.Precision.HIGHEST)
        tables.append(bias * LOG2_E - shift)
    bias = jnp.stack(tables, axis=0)
    variants = []
    for v in range(4):
        keep = in_window & ((col >= ATT_HALF) | (v & 1 == 0)) & ((col < ATT_BLOCK + ATT_HALF) | (v & 2 == 0))
        variants.append(jnp.where(keep[None, None], bias, NEG_INF))
    return jnp.stack(variants, axis=0)


def _attention_tables(rel_bias, q_norm_g, k_norm_g):
    logit_bound = (QK_BOUND_SLACK * HEAD_DIM * HEAD_DIM ** -0.5 * LOG2_E
                   * jnp.max(jnp.abs(q_norm_g)) * jnp.max(jnp.abs(k_norm_g))).astype(F32)
    bias_lo = (jnp.min(rel_bias) * LOG2_E).astype(F32)
    bias_hi = (jnp.max(rel_bias) * LOG2_E).astype(F32)
    narrow = 2.0 * logit_bound + (bias_hi - bias_lo) < EXP2_SAFE_RANGE
    return {
        "bias": _bias_tables(rel_bias, jnp.where(narrow, logit_bound + bias_hi, 0.0)),
        "narrow": narrow.astype(jnp.int32).reshape(1),
    }


def _head_expander(first_row, width):
    e = np.zeros((2 * LANES, width), np.float32)
    for h in range(width // HEAD_DIM):
        e[first_row + h, h * HEAD_DIM:(h + 1) * HEAD_DIM] = 1.0
        e[LANES + first_row + h, h * HEAD_DIM:(h + 1) * HEAD_DIM] = 1.0
    return jnp.asarray(e, BF16)


def _block_diag_mean(width):
    i = np.arange(width)
    return jnp.asarray((i[:, None] // HEAD_DIM == i[None, :] // HEAD_DIM) / HEAD_DIM, BF16)


def _layer(x, p, tm):
    b, s, _ = x.shape
    n = b * s
    xf = x.reshape(n, D_MODEL)
    x1 = _ffn(xf, p["ffn1_g"], p["ffn1_wg"], p["ffn1_wu"], p["ffn1_wd"], tm)
    q, k, v, z, xa, dt = _inproj(x1, p["mix_g"], p["wq"], p["wk"], p["wv"], p["wz"], p["wx"],
                                 p["wdt"], p["qg"], p["kg"], p["bd"], p["conv_w"], p["conv_b"], s, tm)
    seq = lambda a: a.reshape(b, s, a.shape[-1])
    pair_seq = lambda a: a.reshape(a.shape[0], b, s, LANES)
    yb = _ssd_bwd(seq(xa), seq(dt), p["dt_bias"], p["a_neg"], p["e_bwd"])
    attn, y = _mixer(pair_seq(q), pair_seq(k), pair_seq(v), p["att"], seq(xa), seq(dt), seq(z), yb,
                     p["dt_bias"], p["a_neg"], p["d_skip"], p["ssm_g"], p["e_fwd"])
    out = _outproj_ffn(x1, attn.reshape(attn.shape[0], n, LANES), y.reshape(n, D_SSM), p["attn_g"],
                       p["wo_att"], p["wo_ssm"], p["ffn2_g"], p["ffn2_wg"], p["ffn2_wu"],
                       p["ffn2_wd"], tm)
    return out.reshape(b, s, D_MODEL)


def _prepare(rel_bias, ffn1_norm_g, ffn1_w_gate, ffn1_w_up, ffn1_w_down, mix_norm_g, w_in,
             q_norm_g, k_norm_g, attn_out_g, conv_w, conv_b, dt_bias, a_log, d_skip, ssm_out_g,
             w_out, ffn2_norm_g, ffn2_w_gate, ffn2_w_up, ffn2_w_down):
    row = lambda a: a.reshape(1, -1).astype(F32)
    w16 = lambda a: a.astype(BF16)
    c0, c1, c2, c3 = D_ATT, 2 * D_ATT, 3 * D_ATT, 3 * D_ATT + D_SSM
    c4 = c3 + CONV_DIM
    pad32 = lambda a: jnp.pad(a.reshape(1, 2 * N_HEADS).astype(F32), ((0, 0), (0, LANES - 2 * N_HEADS)))
    return {
        "ffn1_g": row(ffn1_norm_g), "ffn1_wg": w16(ffn1_w_gate), "ffn1_wu": w16(ffn1_w_up),
        "ffn1_wd": w16(ffn1_w_down),
        "ffn2_g": row(ffn2_norm_g), "ffn2_wg": w16(ffn2_w_gate), "ffn2_wu": w16(ffn2_w_up),
        "ffn2_wd": w16(ffn2_w_down),
        "mix_g": row(mix_norm_g),
        "wq": w16(w_in[:, :c0]), "wk": w16(w_in[:, c0:c1]), "wv": w16(w_in[:, c1:c2]),
        "wz": w16(w_in[:, c2:c3]), "wx": w16(w_in[:, c3:c4]),
        "wdt": w16(jnp.pad(w_in[:, c4:], ((0, 0), (0, LANES - 2 * N_HEADS)))),
        "qg": row(jnp.tile(q_norm_g, N_HEADS)) * (HEAD_DIM ** -0.5 * LOG2_E),
        "kg": row(jnp.tile(k_norm_g, N_HEADS)),
        "bd": _block_diag_mean(MXU_TILE),
        "att": _attention_tables(rel_bias, q_norm_g, k_norm_g),
        "attn_g": row(attn_out_g),
        "conv_w": conv_w.astype(F32), "conv_b": row(conv_b),
        "dt_bias": pad32(dt_bias), "a_neg": pad32(-jnp.exp(a_log.astype(F32))),
        "d_skip": row(jnp.repeat(d_skip, HEAD_DIM)), "ssm_g": row(ssm_out_g),
        "e_fwd": _head_expander(0, D_SSM), "e_bwd": _head_expander(N_HEADS, D_SSM),
        "wo_att": w16(w_out[:D_ATT]), "wo_ssm": w16(w_out[D_ATT:]),
    }


def _trunk(x, layers, tm=512):
    for p in layers:
        x = _layer(x, p, tm)
    return x


def kernel(x_prompt, x_sample, rel_bias, ffn1_norm_g, ffn1_w_gate, ffn1_w_up, ffn1_w_down, mix_norm_g, w_in, q_norm_g, k_norm_g, attn_out_g, conv_w, conv_b, dt_bias, a_log, d_skip, ssm_out_g, w_out, ffn2_norm_g, ffn2_w_gate, ffn2_w_up, ffn2_w_down):
    per_layer = (ffn1_norm_g, ffn1_w_gate, ffn1_w_up, ffn1_w_down, mix_norm_g, w_in, q_norm_g,
                 k_norm_g, attn_out_g, conv_w, conv_b, dt_bias, a_log, d_skip, ssm_out_g, w_out,
                 ffn2_norm_g, ffn2_w_gate, ffn2_w_up, ffn2_w_down)
    layers = [_prepare(rel_bias, *(a[l] for a in per_layer)) for l in range(ffn1_norm_g.shape[0])]
    return (_trunk(x_prompt, layers), _trunk(x_sample, layers))
```

```python
import functools
import math

import numpy as np
import jax
import jax.numpy as jnp
from jax import lax
from jax.experimental import pallas as pl
from jax.experimental.pallas import tpu as pltpu

D_MODEL = 1024
D_ATT = 1024
D_SSM = 1024
HEAD_DIM = 64
N_HEADS = 16
ATT_PATTERNS = ((128, 1), (512, 4), (2048, 16))
ATT_BLOCK = 128
ATT_HALF = 64
ATT_KW = ATT_BLOCK + 2 * ATT_HALF
ATT_MAX_DIL = max(d for _, d in ATT_PATTERNS)
ATT_TILE = ATT_BLOCK * ATT_MAX_DIL
ATT_REACH = ATT_HALF * ATT_MAX_DIL
ATT_WINDOW = ATT_TILE + 2 * ATT_REACH
ATT_STAGE = 4
assert tuple(d for _, d in ATT_PATTERNS) == (1, ATT_STAGE, ATT_STAGE * ATT_STAGE)
N_REL_BUCKETS = 32
REL_MAX_DIST = 1024
N_GROUPS = 4
HEADS_PER_GROUP = 4
D_STATE = 128
D_CONV = 5
CHUNK = 128
SSD_STEP_CHUNKS = 4
MIXER_CHUNKS = ATT_TILE // CHUNK // (N_HEADS // 2)
CONV_DIM = D_SSM + 2 * N_GROUPS * D_STATE
D_FF = 2816
EPS = 1e-6
NEG_INF = -1e30
LOG2_E = 1.4426950408889634
EXP2_SAFE_RANGE = 100.0
QK_BOUND_SLACK = 1.05

LANES = 128
MXU_TILE = 256
HALO_ROWS = 8
VMEM_LIMIT = 56 * 1024 * 1024

F32 = jnp.float32
BF16 = jnp.bfloat16


def _params(*sem):
    return pltpu.CompilerParams(dimension_semantics=sem, vmem_limit_bytes=VMEM_LIMIT)


def _const_spec(shape):
    n = len(shape)
    return pl.BlockSpec(shape, lambda *_: (0,) * n, pipeline_mode=pl.Buffered(1))


def _rms(x, g):
    ms = jnp.mean(x * x, axis=-1, keepdims=True)
    return x * lax.rsqrt(ms + EPS) * g


def _silu(x):
    h = 0.5 * x
    return h + h * jnp.tanh(h)


def _dot(a, b):
    return jnp.dot(a, b, preferred_element_type=F32)


def _hi_lo(vals):
    hi = vals.astype(BF16)
    lo = (vals - hi.astype(F32)).astype(BF16)
    return jnp.concatenate([hi, lo], axis=1)


def _expand(split, e_ref, cols=slice(None)):
    return _dot(split, e_ref[:, cols])


def _half_step_ffn(x, g_ref, wg_ref, wu_ref, wd_ref):
    h = _rms(x, g_ref[...]).astype(BF16)
    gate = _dot(h, wg_ref[...])
    up = _dot(h, wu_ref[...])
    act = (_silu(gate) * up).astype(BF16)
    return x + 0.5 * _dot(act, wd_ref[...])


def _ffn_body(x_ref, g_ref, wg_ref, wu_ref, wd_ref, o_ref):
    o_ref[...] = _half_step_ffn(x_ref[...], g_ref, wg_ref, wu_ref, wd_ref)


def _ffn(x, g, wg, wu, wd, tm):
    n = x.shape[0]
    row = pl.BlockSpec((tm, D_MODEL), lambda i: (i, 0))
    return pl.pallas_call(
        _ffn_body,
        grid=(n // tm,),
        in_specs=[row, _const_spec((1, D_MODEL)), _const_spec((D_MODEL, D_FF)),
                  _const_spec((D_MODEL, D_FF)), _const_spec((D_FF, D_MODEL))],
        out_specs=row,
        out_shape=jax.ShapeDtypeStruct((n, D_MODEL), F32),
        compiler_params=_params("parallel"),
        name="ffn",
    )(x, g, wg, wu, wd)


def _inproj_body(x_ref, xp_ref, xn_ref, g_ref, wq_ref, wk_ref, wv_ref, wz_ref, wx_ref, wdt_ref,
                 qg_ref, kg_ref, bd_ref, cw_ref, cb_ref, q_ref, k_ref, v_ref, z_ref, xa_ref, dt_ref, ext,
                 *, tiles_per_seq):
    h = _rms(x_ref[...], g_ref[...]).astype(BF16)

    def head_norm(t, gain):
        t2 = (t * t).astype(BF16)
        w = bd_ref.shape[0]
        ms = jnp.concatenate([_dot(t2[:, j * w:(j + 1) * w], bd_ref[...])
                              for j in range(D_ATT // w)], axis=1)
        return t * lax.rsqrt(ms + EPS) * gain

    def put_pairs(ref, val):
        for hp in range(N_HEADS // 2):
            ref[hp] = val[:, hp * LANES:(hp + 1) * LANES]

    tile = pl.program_id(0) % tiles_per_seq
    halo = jnp.concatenate([xp_ref[...], xn_ref[...]], axis=0)
    xbc_all = _dot(jnp.concatenate([h, _rms(halo, g_ref[...]).astype(BF16)], axis=0), wx_ref[...])
    rows = h.shape[0]
    xbc, xbc_halo = xbc_all[:rows], xbc_all[rows:]
    pad = D_CONV // 2
    for sl in range(CONV_DIM // LANES):
        lanes = slice(sl * LANES, (sl + 1) * LANES)
        ext[sl, 0:HALO_ROWS, :] = jnp.where(tile > 0, xbc_halo[0:HALO_ROWS, lanes], 0.0)
        ext[sl, HALO_ROWS:HALO_ROWS + rows, :] = xbc[:, lanes]
        ext[sl, HALO_ROWS + rows:, :] = jnp.where(tile < tiles_per_seq - 1, xbc_halo[HALO_ROWS:, lanes], 0.0)
        acc = cb_ref[:, lanes]
        for j in range(D_CONV):
            acc = acc + ext[sl, pl.ds(HALO_ROWS - pad + j, rows, stride=1), :] * cw_ref[j:j + 1, lanes]
        xa_ref[:, lanes] = _silu(acc)

    put_pairs(q_ref, head_norm(_dot(h, wq_ref[...]), qg_ref[...]))
    put_pairs(k_ref, head_norm(_dot(h, wk_ref[...]), kg_ref[...]))
    put_pairs(v_ref, _dot(h, wv_ref[...]))
    z_ref[...] = _dot(h, wz_ref[...])
    dt_ref[...] = _dot(h, wdt_ref[...])


def _inproj(x, g, wq, wk, wv, wz, wx, wdt, qg, kg, bd, conv_w, conv_b, s, tm):
    n = x.shape[0]
    halos_per_tile = tm // HALO_ROWS
    halo_prev = pl.BlockSpec((HALO_ROWS, D_MODEL), lambda i: (jnp.maximum(i * halos_per_tile - 1, 0), 0))
    halo_next = pl.BlockSpec((HALO_ROWS, D_MODEL),
                             lambda i: (jnp.minimum((i + 1) * halos_per_tile, n // HALO_ROWS - 1), 0))
    row = lambda w: pl.BlockSpec((tm, w), lambda i: (i, 0))
    sq = _const_spec((D_MODEL, D_ATT))
    out = lambda w: jax.ShapeDtypeStruct((n, w), F32)
    pairs = pl.BlockSpec((N_HEADS // 2, tm, LANES), lambda i: (0, i, 0))
    pairs_out = jax.ShapeDtypeStruct((N_HEADS // 2, n, LANES), F32)
    return pl.pallas_call(
        functools.partial(_inproj_body, tiles_per_seq=s // tm),
        grid=(n // tm,),
        in_specs=[row(D_MODEL), halo_prev, halo_next, _const_spec((1, D_MODEL)), sq, sq, sq, sq,
                  _const_spec((D_MODEL, CONV_DIM)), _const_spec((D_MODEL, LANES)),
                  _const_spec((1, D_ATT)), _const_spec((1, D_ATT)), _const_spec(bd.shape),
                  _const_spec(conv_w.shape), _const_spec(conv_b.shape)],
        out_specs=[pairs, pairs, pairs, row(D_SSM), row(CONV_DIM), row(LANES)],
        out_shape=[pairs_out, pairs_out, pairs_out, out(D_SSM), out(CONV_DIM), out(LANES)],
        scratch_shapes=[pltpu.VMEM((CONV_DIM // LANES, tm + 2 * HALO_ROWS, LANES), F32)],
        compiler_params=_params("parallel"),
        name="inproj",
    )(x, x, x, g, wq, wk, wv, wz, wx, wdt, qg, kg, bd, conv_w, conv_b)


def _rows(start, size, dil):
    return pl.ds(start, size) if dil == 1 else pl.ds(start, size, stride=dil)


def _regroup_keys(prev, cur, nxt, tmp, dst):
    s4 = ATT_STAGE
    lo, hi = ATT_REACH - ATT_HALF, ATT_REACH + ATT_TILE + ATT_HALF
    dst[0, lo:ATT_REACH, :] = prev[lo:ATT_REACH, :].astype(BF16)
    dst[0, ATT_REACH:ATT_REACH + ATT_TILE, :] = cur[...].astype(BF16)
    dst[0, ATT_REACH + ATT_TILE:hi, :] = nxt[0:ATT_HALF, :].astype(BF16)
    h4, t4 = ATT_REACH // s4, ATT_TILE // s4
    for r in range(s4):
        tmp[r, 0:h4, :] = prev[pl.ds(r, h4, stride=s4), :]
        tmp[r, h4:h4 + t4, :] = cur[pl.ds(r, t4, stride=s4), :]
        tmp[r, h4 + t4:, :] = nxt[pl.ds(r, h4, stride=s4), :]
    lo, hi = h4 - ATT_HALF, h4 + t4 + ATT_HALF
    for r in range(s4):
        dst[1, r * (ATT_WINDOW // s4) + lo:r * (ATT_WINDOW // s4) + hi, :] = tmp[r, lo:hi, :].astype(BF16)
    per = ATT_WINDOW // (s4 * s4)
    for r in range(s4 * s4):
        dst[2, r * per:(r + 1) * per, :] = tmp[r % s4, pl.ds(r // s4, per, stride=s4), :].astype(BF16)


def _attn_step(q_ref, kc_ref, kp_ref, kn_ref, vc_ref, vp_ref, vn_ref, bias_ref, out_ref,
               kd, vd, tmp, o_scr, lse_scr, *, t0, seq_len, row_max):
    _regroup_keys(kp_ref, kc_ref, kn_ref, tmp, kd)
    _regroup_keys(vp_ref, vc_ref, vn_ref, tmp, vd)

    first_head = lax.broadcasted_iota(jnp.int32, (ATT_BLOCK, LANES), 1) < HEAD_DIM
    first_head_kw = lax.broadcasted_iota(jnp.int32, (ATT_KW, LANES), 1) < HEAD_DIM
    head_ones = (first_head_kw.astype(BF16), (~first_head_kw).astype(BF16))

    def block(p, dil, r, j):
        blocks_per_residue = ATT_TILE // (dil * ATT_BLOCK)
        q0 = j * (ATT_BLOCK * dil) + r
        k0 = r * (ATT_WINDOW // dil) + ATT_REACH // dil - ATT_HALF + j * ATT_BLOCK
        variant = 0
        if j == 0:
            variant = variant + (t0 == 0).astype(jnp.int32)
        if j == blocks_per_residue - 1:
            variant = variant + 2 * (t0 + ATT_TILE == seq_len).astype(jnp.int32)
        qp = q_ref[_rows(q0, ATT_BLOCK, dil), :].astype(BF16)
        kp = kd[p, k0:k0 + ATT_KW, :]
        vp = vd[p, k0:k0 + ATT_KW, :]
        zq, zv = jnp.zeros_like(qp), jnp.zeros_like(vp)
        q2 = jnp.concatenate([jnp.where(first_head, qp, zq), jnp.where(first_head, zq, qp)], axis=0)
        s2 = lax.dot_general(q2, kp, (((1,), (1,)), ((), ())), preferred_element_type=F32)
        es, ms = [], []
        for sub in range(2):
            s = s2[sub * ATT_BLOCK:(sub + 1) * ATT_BLOCK, :] + bias_ref[variant, p, sub]
            if row_max:
                m = jnp.max(s, axis=-1, keepdims=True)
                s = s - m
                ms.append(m)
            es.append(jnp.exp2(s).astype(BF16))
        v2 = jnp.concatenate([
            jnp.concatenate([jnp.where(first_head_kw, vp, zv), head_ones[0]], axis=1),
            jnp.concatenate([jnp.where(first_head_kw, zv, vp), head_ones[1]], axis=1)], axis=0)
        ol = _dot(jnp.concatenate(es, axis=1), v2)
        l = ol[:, LANES:]
        if not row_max:
            return ol[:, :LANES], l
        return ol[:, :LANES] * (1.0 / l), jnp.where(first_head, ms[0], ms[1]) + jnp.log2(l)

    for p in range(len(ATT_PATTERNS) - 1, 0, -1):
        dil = ATT_PATTERNS[p][1]
        blocks_per_residue = ATT_TILE // (dil * ATT_BLOCK)
        for idx in range(ATT_TILE // ATT_BLOCK):
            r, j = idx // blocks_per_residue, idx % blocks_per_residue
            rows = _rows(j * (ATT_BLOCK * dil) + r, ATT_BLOCK, dil)
            o_scr[p - 1, rows, :], lse_scr[p - 1, rows, :] = block(p, dil, r, j)
    for j in range(ATT_TILE // ATT_BLOCK):
        rows = pl.ds(j * ATT_BLOCK, ATT_BLOCK)
        o1, l1 = block(0, 1, 0, j)
        l2, l3 = lse_scr[0, rows, :], lse_scr[1, rows, :]
        if not row_max:
            mixed = o1 + o_scr[0, rows, :] + o_scr[1, rows, :]
            out_ref[rows, :] = (mixed * (1.0 / (l1 + l2 + l3))).astype(BF16)
            continue
        m = jnp.maximum(jnp.maximum(l1, l2), l3)
        e1, e2, e3 = jnp.exp2(l1 - m), jnp.exp2(l2 - m), jnp.exp2(l3 - m)
        mixed = e1 * o1 + e2 * o_scr[0, rows, :] + e3 * o_scr[1, rows, :]
        out_ref[rows, :] = (mixed * (1.0 / (e1 + e2 + e3))).astype(BF16)


def _softplus(x):
    return jnp.maximum(x, 0.0) + jnp.log(1.0 + jnp.exp(-jnp.abs(x)))


def _tri(lower):
    r = lax.broadcasted_iota(jnp.int32, (CHUNK, CHUNK), 0)
    c = lax.broadcasted_iota(jnp.int32, (CHUNK, CHUNK), 1)
    return (r >= c) if lower else (r <= c)


def _cumsum_rows(a, lower):
    return jnp.dot(_tri(lower).astype(F32), a, precision=lax.Precision.HIGHEST,
                   preferred_element_type=F32)


def _chunk_rows(i):
    return pl.ds(pl.multiple_of(i * CHUNK, CHUNK), CHUNK)


def _ssd_bwd_body(xa_ref, dt_ref, dtb_ref, a_ref, e_ref, yoff_ref, hst):
    @pl.when(pl.program_id(1) == 0)
    def _():
        hst[...] = jnp.zeros_like(hst)

    def chunk(ci, carry):
        rs = _chunk_rows(SSD_STEP_CHUNKS - 1 - ci)
        dt = _softplus(dt_ref[rs, :] + dtb_ref[...])
        rcum = _cumsum_rows(dt * a_ref[...], lower=False)
        decay_in = _expand(_hi_lo(jnp.exp(rcum)), e_ref)
        w_in = _expand(_hi_lo(jnp.exp(rcum[0:1, :] - rcum) * dt), e_ref)
        chunk_decay = _expand(_hi_lo(jnp.broadcast_to(jnp.exp(rcum[0:1, :]), (HALO_ROWS, LANES))),
                              e_ref)[0:1, :]
        for g in range(N_GROUPS):
            gl = slice(g * HEADS_PER_GROUP * HEAD_DIM, (g + 1) * HEADS_PER_GROUP * HEAD_DIM)
            bg = xa_ref[rs, D_SSM + g * D_STATE:D_SSM + (g + 1) * D_STATE].astype(BF16)
            cg = xa_ref[rs, D_SSM + (N_GROUPS + g) * D_STATE:D_SSM + (N_GROUPS + g + 1) * D_STATE].astype(BF16)
            xw = (xa_ref[rs, gl] * w_in[:, gl]).astype(BF16)
            h_in = hst[:, gl]
            yoff_ref[rs, gl] = _dot(cg, h_in.astype(BF16)) * decay_in[:, gl]
            upd = lax.dot_general(bg, xw, (((0,), (0,)), ((), ())), preferred_element_type=F32)
            hst[:, gl] = h_in * chunk_decay[:, gl] + upd
        return carry

    lax.fori_loop(0, SSD_STEP_CHUNKS, chunk, 0, unroll=True)


def _ssd_fwd_chunks(n_chunks, xa_ref, dt_ref, z_ref, yb_ref, dtb_ref, a_ref, dsk_ref, og_ref, e_ref,
                    y_ref, hst):
    lower, upper = _tri(True), _tri(False)
    lane = lax.broadcasted_iota(jnp.int32, (CHUNK, LANES), 1)

    def chunk(ci, carry):
        rs = _chunk_rows(ci)
        dt = _softplus(dt_ref[rs, :] + dtb_ref[...])
        a = dt * a_ref[...]
        fcum = _cumsum_rows(a, lower=True)
        rcum = _cumsum_rows(a, lower=False)
        decay_in = _expand(_hi_lo(jnp.exp(fcum)), e_ref)
        w_in = _expand(_hi_lo(jnp.exp(fcum[CHUNK - 1:CHUNK, :] - fcum) * dt), e_ref)
        chunk_decay = _expand(_hi_lo(jnp.broadcast_to(jnp.exp(fcum[CHUNK - 1:CHUNK, :]),
                                                       (HALO_ROWS, LANES))), e_ref)[0:1, :]
        log2_dt = jnp.log(dt) * LOG2_E
        fcol, rcol = fcum * LOG2_E, rcum * LOG2_E
        frow, rrow = (fcol - log2_dt).T, (rcol - log2_dt).T

        for g in range(N_GROUPS):
            gl = slice(g * HEADS_PER_GROUP * HEAD_DIM, (g + 1) * HEADS_PER_GROUP * HEAD_DIM)
            bg = xa_ref[rs, D_SSM + g * D_STATE:D_SSM + (g + 1) * D_STATE].astype(BF16)
            cg = xa_ref[rs, D_SSM + (N_GROUPS + g) * D_STATE:D_SSM + (N_GROUPS + g + 1) * D_STATE].astype(BF16)
            cb = lax.dot_general(cg, bg, (((1,), (1,)), ((), ())), preferred_element_type=F32)
            xw = (xa_ref[rs, gl] * w_in[:, gl]).astype(BF16)
            h_in = hst[:, gl]
            y_off = _dot(cg, h_in.astype(BF16)) * decay_in[:, gl]
            upd = lax.dot_general(bg, xw, (((0,), (0,)), ((), ())), preferred_element_type=F32)
            hst[:, gl] = h_in * chunk_decay[:, gl] + upd
            gated = []
            for pair in range(HEADS_PER_GROUP // 2):
                pl_ = slice(gl.start + pair * LANES, gl.start + (pair + 1) * LANES)
                xs = xa_ref[rs, pl_]
                x_pair = xs.astype(BF16)
                halves = []
                for sub in range(2):
                    h = g * HEADS_PER_GROUP + 2 * pair + sub
                    hb = N_HEADS + h
                    seg_f = jnp.where(lower, fcol[:, h:h + 1] - frow[h:h + 1, :], NEG_INF)
                    seg_b = jnp.where(upper, rcol[:, hb:hb + 1] - rrow[hb:hb + 1, :], NEG_INF)
                    mix = jnp.exp2(seg_f) + jnp.exp2(seg_b)
                    halves.append(_dot((cb * mix).astype(BF16), x_pair))
                y_diag = jnp.where(lane < HEAD_DIM, halves[0], halves[1])
                po = slice(pair * LANES, (pair + 1) * LANES)
                y = y_diag + y_off[:, po] + yb_ref[rs, pl_] + dsk_ref[:, pl_] * xs
                gated.append(y * _silu(z_ref[rs, pl_]))
            y_ref[rs, gl] = _rms(jnp.concatenate(gated, axis=1), og_ref[:, gl]).astype(BF16)
        return carry

    lax.fori_loop(0, n_chunks, chunk, 0, unroll=True)


def _ssd_bwd(xa, dt, dt_bias, a_neg, e_bwd):
    b, s, _ = xa.shape
    rows = SSD_STEP_CHUNKS * CHUNK
    nc = s // rows
    step = lambda w: pl.BlockSpec((None, rows, w), lambda bi, i: (bi, nc - 1 - i, 0))
    return pl.pallas_call(
        _ssd_bwd_body,
        grid=(b, nc),
        in_specs=[step(CONV_DIM), step(LANES), _const_spec(dt_bias.shape), _const_spec(a_neg.shape),
                  _const_spec(e_bwd.shape)],
        out_specs=step(D_SSM),
        out_shape=jax.ShapeDtypeStruct((b, s, D_SSM), F32),
        scratch_shapes=[pltpu.VMEM((D_STATE, D_SSM), F32)],
        compiler_params=_params("parallel", "arbitrary"),
        name="ssd_bwd",
    )(xa, dt, dt_bias, a_neg, e_bwd)


def _mixer_body(narrow_ref, q_ref, kc_ref, kp_ref, kn_ref, vc_ref, vp_ref, vn_ref, bias_ref,
                xa_ref, dt_ref, z_ref, yb_ref, dtb_ref, a_ref, dsk_ref, og_ref, e_ref,
                attn_ref, y_ref, kd, vd, tmp, o_scr, lse_scr, hst, *, seq_len):
    hp, t = pl.program_id(1), pl.program_id(2)

    @pl.when(jnp.logical_and(t == 0, hp == 0))
    def _():
        hst[...] = jnp.zeros_like(hst)

    def step(row_max):
        _attn_step(q_ref, kc_ref, kp_ref, kn_ref, vc_ref, vp_ref, vn_ref, bias_ref, attn_ref,
                   kd, vd, tmp, o_scr, lse_scr, t0=t * ATT_TILE, seq_len=seq_len, row_max=row_max)
        _ssd_fwd_chunks(MIXER_CHUNKS, xa_ref, dt_ref, z_ref, yb_ref, dtb_ref, a_ref, dsk_ref, og_ref,
                        e_ref, y_ref, hst)

    pl.when(narrow_ref[0] != 0)(functools.partial(step, False))
    pl.when(narrow_ref[0] == 0)(functools.partial(step, True))


def _mixer(q, k, v, att, xa, dt, z, yb, dt_bias, a_neg, d_skip, out_g, e_fwd):
    bias = att["bias"]
    n_pairs, b, s, _ = q.shape
    n_pat = len(ATT_PATTERNS)
    last_halo = s // ATT_REACH - 1
    halo_per_tile = ATT_TILE // ATT_REACH
    n_tiles = s // ATT_TILE
    cur = pl.BlockSpec((None, None, ATT_TILE, LANES), lambda bi, hp, t: (hp, bi, t, 0))
    prev = pl.BlockSpec((None, None, ATT_REACH, LANES),
                        lambda bi, hp, t: (hp, bi, jnp.maximum(t * halo_per_tile - 1, 0), 0))
    nxt = pl.BlockSpec((None, None, ATT_REACH, LANES),
                       lambda bi, hp, t: (hp, bi, jnp.minimum((t + 1) * halo_per_tile, last_halo), 0))
    bias_spec = pl.BlockSpec((bias.shape[0], n_pat, 2, ATT_BLOCK, ATT_KW),
                             lambda bi, hp, t: (0, 0, hp, 0, 0))
    rows = MIXER_CHUNKS * CHUNK
    scan = lambda w: pl.BlockSpec((None, rows, w), lambda bi, hp, t: (bi, hp * n_tiles + t, 0))
    consts = [_const_spec(a.shape) for a in (dt_bias, a_neg, d_skip, out_g, e_fwd)]
    return pl.pallas_call(
        functools.partial(_mixer_body, seq_len=s),
        grid=(b, n_pairs, n_tiles),
        in_specs=[pl.BlockSpec(memory_space=pltpu.SMEM), cur, cur, prev, nxt, cur, prev, nxt, bias_spec,
                  scan(CONV_DIM), scan(LANES), scan(D_SSM), scan(D_SSM)] + consts,
        out_specs=[cur, scan(D_SSM)],
        out_shape=[jax.ShapeDtypeStruct((n_pairs, b, s, LANES), BF16),
                   jax.ShapeDtypeStruct((b, s, D_SSM), BF16)],
        scratch_shapes=[pltpu.VMEM((n_pat, ATT_WINDOW, LANES), BF16),
                        pltpu.VMEM((n_pat, ATT_WINDOW, LANES), BF16),
                        pltpu.VMEM((ATT_STAGE, ATT_WINDOW // ATT_STAGE, LANES), F32),
                        pltpu.VMEM((n_pat - 1, ATT_TILE, LANES), F32),
                        pltpu.VMEM((n_pat - 1, ATT_TILE, LANES), F32),
                        pltpu.VMEM((D_STATE, D_SSM), F32)],
        compiler_params=_params("parallel", "arbitrary", "arbitrary"),
        name="mixer",
    )(att["narrow"], q, k, k, k, v, v, v, bias, xa, dt, z, yb, dt_bias, a_neg, d_skip, out_g, e_fwd)


def _outproj_ffn_body(x_ref, a_ref, y_ref, g_ref, wa_ref, ws_ref, g2_ref, wg_ref, wu_ref, wd_ref,
                      out_ref):
    attn = jnp.concatenate([a_ref[hp] for hp in range(a_ref.shape[0])], axis=1)
    attn = _rms(attn.astype(F32), g_ref[...]).astype(BF16)
    x2 = x_ref[...] + _dot(attn, wa_ref[...]) + _dot(y_ref[...], ws_ref[...])
    out_ref[...] = _half_step_ffn(x2, g2_ref, wg_ref, wu_ref, wd_ref)


def _outproj_ffn(x, attn, y, g, wa, ws, g2, wg, wu, wd, tm):
    n = x.shape[0]
    row = lambda w: pl.BlockSpec((tm, w), lambda i: (i, 0))
    sq = _const_spec((D_ATT, D_MODEL))
    pairs = pl.BlockSpec((attn.shape[0], tm, LANES), lambda i: (0, i, 0))
    return pl.pallas_call(
        _outproj_ffn_body,
        grid=(n // tm,),
        in_specs=[row(D_MODEL), pairs, row(D_SSM), _const_spec((1, D_ATT)), sq, sq,
                  _const_spec((1, D_MODEL)), _const_spec((D_MODEL, D_FF)), _const_spec((D_MODEL, D_FF)),
                  _const_spec((D_FF, D_MODEL))],
        out_specs=row(D_MODEL),
        out_shape=jax.ShapeDtypeStruct((n, D_MODEL), F32),
        compiler_params=_params("parallel"),
        name="outproj_ffn",
    )(x, attn, y, g, wa, ws, g2, wg, wu, wd)


def _t5_bucket(rel):
    nb = N_REL_BUCKETS // 2
    max_exact = nb // 2
    n = np.abs(rel)
    large = max_exact + (np.log(np.maximum(n, 1) / max_exact)
                         / math.log(REL_MAX_DIST / max_exact) * (nb - max_exact)).astype(np.int32)
    large = np.minimum(large, nb - 1)
    return (np.where(rel > 0, nb, 0) + np.where(n < max_exact, n, large)).astype(np.int32)


def _bias_tables(rel_bias, shift):
    rel_sub = np.arange(ATT_KW)[None, :] - ATT_HALF - np.arange(ATT_BLOCK)[:, None]
    in_window = np.abs(rel_sub) <= ATT_HALF
    col = np.arange(ATT_KW)[None, :]
    tables = []
    for _, dil in ATT_PATTERNS:
        onehot = (_t5_bucket(rel_sub * dil)[:, :, None] == np.arange(N_REL_BUCKETS)).astype(np.float32)
        bias = jnp.einsum("ijk,kh->hij", jnp.asarray(onehot, BF16).astype(F32), rel_bias.astype(F32),
                          precision=lax.Precision.HIGHEST)
        tables.append(bias * LOG2_E - shift)
    bias = jnp.stack(tables, axis=0)
    variants = []
    for v in range(4):
        keep = in_window & ((col >= ATT_HALF) | (v & 1 == 0)) & ((col < ATT_BLOCK + ATT_HALF) | (v & 2 == 0))
        variants.append(jnp.where(keep[None, None], bias, NEG_INF))
    return jnp.stack(variants, axis=0)


def _attention_tables(rel_bias, q_norm_g, k_norm_g):
    logit_bound = (QK_BOUND_SLACK * HEAD_DIM * HEAD_DIM ** -0.5 * LOG2_E
                   * jnp.max(jnp.abs(q_norm_g)) * jnp.max(jnp.abs(k_norm_g))).astype(F32)
    bias_lo = (jnp.min(rel_bias) * LOG2_E).astype(F32)
    bias_hi = (jnp.max(rel_bias) * LOG2_E).astype(F32)
    narrow = 2.0 * logit_bound + (bias_hi - bias_lo) < EXP2_SAFE_RANGE
    return {
        "bias": _bias_tables(rel_bias, jnp.where(narrow, logit_bound + bias_hi, 0.0)),
        "narrow": narrow.astype(jnp.int32).reshape(1),
    }


def _head_expander(first_row, width):
    e = np.zeros((2 * LANES, width), np.float32)
    for h in range(width // HEAD_DIM):
        e[first_row + h, h * HEAD_DIM:(h + 1) * HEAD_DIM] = 1.0
        e[LANES + first_row + h, h * HEAD_DIM:(h + 1) * HEAD_DIM] = 1.0
    return jnp.asarray(e, BF16)


def _block_diag_mean(width):
    i = np.arange(width)
    return jnp.asarray((i[:, None] // HEAD_DIM == i[None, :] // HEAD_DIM) / HEAD_DIM, BF16)


def _layer(x, p, tm):
    b, s, _ = x.shape
    n = b * s
    xf = x.reshape(n, D_MODEL)
    x1 = _ffn(xf, p["ffn1_g"], p["ffn1_wg"], p["ffn1_wu"], p["ffn1_wd"], tm)
    q, k, v, z, xa, dt = _inproj(x1, p["mix_g"], p["wq"], p["wk"], p["wv"], p["wz"], p["wx"],
                                 p["wdt"], p["qg"], p["kg"], p["bd"], p["conv_w"], p["conv_b"], s, tm)
    seq = lambda a: a.reshape(b, s, a.shape[-1])
    pair_seq = lambda a: a.reshape(a.shape[0], b, s, LANES)
    yb = _ssd_bwd(seq(xa), seq(dt), p["dt_bias"], p["a_neg"], p["e_bwd"])
    attn, y = _mixer(pair_seq(q), pair_seq(k), pair_seq(v), p["att"], seq(xa), seq(dt), seq(z), yb,
                     p["dt_bias"], p["a_neg"], p["d_skip"], p["ssm_g"], p["e_fwd"])
    out = _outproj_ffn(x1, attn.reshape(attn.shape[0], n, LANES), y.reshape(n, D_SSM), p["attn_g"],
                       p["wo_att"], p["wo_ssm"], p["ffn2_g"], p["ffn2_wg"], p["ffn2_wu"],
                       p["ffn2_wd"], tm)
    return out.reshape(b, s, D_MODEL)


def _prepare(rel_bias, ffn1_norm_g, ffn1_w_gate, ffn1_w_up, ffn1_w_down, mix_norm_g, w_in,
             q_norm_g, k_norm_g, attn_out_g, conv_w, conv_b, dt_bias, a_log, d_skip, ssm_out_g,
             w_out, ffn2_norm_g, ffn2_w_gate, ffn2_w_up, ffn2_w_down):
    row = lambda a: a.reshape(1, -1).astype(F32)
    w16 = lambda a: a.astype(BF16)
    c0, c1, c2, c3 = D_ATT, 2 * D_ATT, 3 * D_ATT, 3 * D_ATT + D_SSM
    c4 = c3 + CONV_DIM
    pad32 = lambda a: jnp.pad(a.reshape(1, 2 * N_HEADS).astype(F32), ((0, 0), (0, LANES - 2 * N_HEADS)))
    return {
        "ffn1_g": row(ffn1_norm_g), "ffn1_wg": w16(ffn1_w_gate), "ffn1_wu": w16(ffn1_w_up),
        "ffn1_wd": w16(ffn1_w_down),
        "ffn2_g": row(ffn2_norm_g), "ffn2_wg": w16(ffn2_w_gate), "ffn2_wu": w16(ffn2_w_up),
        "ffn2_wd": w16(ffn2_w_down),
        "mix_g": row(mix_norm_g),
        "wq": w16(w_in[:, :c0]), "wk": w16(w_in[:, c0:c1]), "wv": w16(w_in[:, c1:c2]),
        "wz": w16(w_in[:, c2:c3]), "wx": w16(w_in[:, c3:c4]),
        "wdt": w16(jnp.pad(w_in[:, c4:], ((0, 0), (0, LANES - 2 * N_HEADS)))),
        "qg": row(jnp.tile(q_norm_g, N_HEADS)) * (HEAD_DIM ** -0.5 * LOG2_E),
        "kg": row(jnp.tile(k_norm_g, N_HEADS)),
        "bd": _block_diag_mean(MXU_TILE),
        "att": _attention_tables(rel_bias, q_norm_g, k_norm_g),
        "attn_g": row(attn_out_g),
        "conv_w": conv_w.astype(F32), "conv_b": row(conv_b),
        "dt_bias": pad32(dt_bias), "a_neg": pad32(-jnp.exp(a_log.astype(F32))),
        "d_skip": row(jnp.repeat(d_skip, HEAD_DIM)), "ssm_g": row(ssm_out_g),
        "e_fwd": _head_expander(0, D_SSM), "e_bwd": _head_expander(N_HEADS, D_SSM),
        "wo_att": w16(w_out[:D_ATT]), "wo_ssm": w16(w_out[D_ATT:]),
    }


def _trunk(x, layers, tm=512):
    for p in layers:
        x = _layer(x, p, tm)
    return x


def kernel(x_prompt, x_sample, rel_bias, ffn1_norm_g, ffn1_w_gate, ffn1_w_up, ffn1_w_down, mix_norm_g, w_in, q_norm_g, k_norm_g, attn_out_g, conv_w, conv_b, dt_bias, a_log, d_skip, ssm_out_g, w_out, ffn2_norm_g, ffn2_w_gate, ffn2_w_up, ffn2_w_down):
    per_layer = (ffn1_norm_g, ffn1_w_gate, ffn1_w_up, ffn1_w_down, mix_norm_g, w_in, q_norm_g,
                 k_norm_g, attn_out_g, conv_w, conv_b, dt_bias, a_log, d_skip, ssm_out_g, w_out,
                 ffn2_norm_g, ffn2_w_gate, ffn2_w_up, ffn2_w_down)
    layers = [_prepare(rel_bias, *(a[l] for a in per_layer)) for l in range(ffn1_norm_g.shape[0])]
    return (_trunk(x_prompt, layers), _trunk(x_sample, layers))
```

```python
import functools
import math

import numpy as np
import jax
import jax.numpy as jnp
from jax import lax
from jax.experimental import pallas as pl
from jax.experimental.pallas import tpu as pltpu

D_MODEL = 1024
D_ATT = 1024
D_SSM = 1024
HEAD_DIM = 64
N_HEADS = 16
ATT_PATTERNS = ((128, 1), (512, 4), (2048, 16))
ATT_BLOCK = 128
ATT_HALF = 64
ATT_KW = ATT_BLOCK + 2 * ATT_HALF
ATT_MAX_DIL = max(d for _, d in ATT_PATTERNS)
ATT_TILE = ATT_BLOCK * ATT_MAX_DIL
ATT_REACH = ATT_HALF * ATT_MAX_DIL
ATT_WINDOW = ATT_TILE + 2 * ATT_REACH
ATT_STAGE = 4
assert tuple(d for _, d in ATT_PATTERNS) == (1, ATT_STAGE, ATT_STAGE * ATT_STAGE)
N_REL_BUCKETS = 32
REL_MAX_DIST = 1024
N_GROUPS = 4
HEADS_PER_GROUP = 4
D_STATE = 128
D_CONV = 5
CHUNK = 128
SSD_STEP_CHUNKS = 4
MIXER_CHUNKS = ATT_TILE // CHUNK // (N_HEADS // 2)
CONV_DIM = D_SSM + 2 * N_GROUPS * D_STATE
D_FF = 2816
EPS = 1e-6
NEG_INF = -1e30
LOG2_E = 1.4426950408889634
EXP2_SAFE_RANGE = 100.0
QK_BOUND_SLACK = 1.05

LANES = 128
MXU_TILE = 256
HALO_ROWS = 8
VMEM_LIMIT = 56 * 1024 * 1024

F32 = jnp.float32
BF16 = jnp.bfloat16


def _params(*sem):
    return pltpu.CompilerParams(dimension_semantics=sem, vmem_limit_bytes=VMEM_LIMIT)


def _const_spec(shape):
    n = len(shape)
    return pl.BlockSpec(shape, lambda *_: (0,) * n, pipeline_mode=pl.Buffered(1))


def _rms(x, g):
    ms = jnp.mean(x * x, axis=-1, keepdims=True)
    return x * lax.rsqrt(ms + EPS) * g


def _silu(x):
    h = 0.5 * x
    return h + h * jnp.tanh(h)


def _dot(a, b):
    return jnp.dot(a, b, preferred_element_type=F32)


def _hi_lo(vals):
    hi = vals.astype(BF16)
    lo = (vals - hi.astype(F32)).astype(BF16)
    return jnp.concatenate([hi, lo], axis=1)


def _expand(split, e_ref, cols=slice(None)):
    return _dot(split, e_ref[:, cols])


def _half_step_ffn(x, g_ref, wg_ref, wu_ref, wd_ref):
    h = _rms(x, g_ref[...]).astype(BF16)
    gate = _dot(h, wg_ref[...])
    up = _dot(h, wu_ref[...])
    act = (_silu(gate) * up).astype(BF16)
    return x + 0.5 * _dot(act, wd_ref[...])


def _ffn_body(x_ref, g_ref, wg_ref, wu_ref, wd_ref, o_ref):
    o_ref[...] = _half_step_ffn(x_ref[...], g_ref, wg_ref, wu_ref, wd_ref)


def _ffn(x, g, wg, wu, wd, tm):
    n = x.shape[0]
    row = pl.BlockSpec((tm, D_MODEL), lambda i: (i, 0))
    return pl.pallas_call(
        _ffn_body,
        grid=(n // tm,),
        in_specs=[row, _const_spec((1, D_MODEL)), _const_spec((D_MODEL, D_FF)),
                  _const_spec((D_MODEL, D_FF)), _const_spec((D_FF, D_MODEL))],
        out_specs=row,
        out_shape=jax.ShapeDtypeStruct((n, D_MODEL), F32),
        compiler_params=_params("parallel"),
        name="ffn",
    )(x, g, wg, wu, wd)


def _inproj_body(x_ref, xp_ref, xn_ref, g_ref, wq_ref, wk_ref, wv_ref, wz_ref, wx_ref, wdt_ref,
                 qg_ref, kg_ref, bd_ref, cw_ref, cb_ref, q_ref, k_ref, v_ref, z_ref, xa_ref, dt_ref, ext,
                 *, tiles_per_seq):
    h = _rms(x_ref[...], g_ref[...]).astype(BF16)

    def head_norm(t, gain):
        t2 = (t * t).astype(BF16)
        w = bd_ref.shape[0]
        ms = jnp.concatenate([_dot(t2[:, j * w:(j + 1) * w], bd_ref[...])
                              for j in range(D_ATT // w)], axis=1)
        return t * lax.rsqrt(ms + EPS) * gain

    def put_pairs(ref, val):
        for hp in range(N_HEADS // 2):
            ref[hp] = val[:, hp * LANES:(hp + 1) * LANES]

    put_pairs(q_ref, head_norm(_dot(h, wq_ref[...]), qg_ref[...]))
    put_pairs(k_ref, head_norm(_dot(h, wk_ref[...]), kg_ref[...]))
    put_pairs(v_ref, _dot(h, wv_ref[...]))
    z_ref[...] = _dot(h, wz_ref[...])
    dt_ref[...] = _dot(h, wdt_ref[...])

    tile = pl.program_id(0) % tiles_per_seq
    halo = jnp.concatenate([xp_ref[...], xn_ref[...]], axis=0)
    xbc_halo = _dot(_rms(halo, g_ref[...]).astype(BF16), wx_ref[...])
    xbc = _dot(h, wx_ref[...])
    rows = xbc.shape[0]
    pad = D_CONV // 2
    for sl in range(CONV_DIM // LANES):
        lanes = slice(sl * LANES, (sl + 1) * LANES)
        ext[sl, 0:HALO_ROWS, :] = jnp.where(tile > 0, xbc_halo[0:HALO_ROWS, lanes], 0.0)
        ext[sl, HALO_ROWS:HALO_ROWS + rows, :] = xbc[:, lanes]
        ext[sl, HALO_ROWS + rows:, :] = jnp.where(tile < tiles_per_seq - 1, xbc_halo[HALO_ROWS:, lanes], 0.0)
        acc = cb_ref[:, lanes]
        for j in range(D_CONV):
            acc = acc + ext[sl, pl.ds(HALO_ROWS - pad + j, rows, stride=1), :] * cw_ref[j:j + 1, lanes]
        xa_ref[:, lanes] = _silu(acc)


def _inproj(x, g, wq, wk, wv, wz, wx, wdt, qg, kg, bd, conv_w, conv_b, s, tm):
    n = x.shape[0]
    halos_per_tile = tm // HALO_ROWS
    halo_prev = pl.BlockSpec((HALO_ROWS, D_MODEL), lambda i: (jnp.maximum(i * halos_per_tile - 1, 0), 0))
    halo_next = pl.BlockSpec((HALO_ROWS, D_MODEL),
                             lambda i: (jnp.minimum((i + 1) * halos_per_tile, n // HALO_ROWS - 1), 0))
    row = lambda w: pl.BlockSpec((tm, w), lambda i: (i, 0))
    sq = _const_spec((D_MODEL, D_ATT))
    out = lambda w: jax.ShapeDtypeStruct((n, w), F32)
    pairs = pl.BlockSpec((N_HEADS // 2, tm, LANES), lambda i: (0, i, 0))
    pairs_out = jax.ShapeDtypeStruct((N_HEADS // 2, n, LANES), F32)
    return pl.pallas_call(
        functools.partial(_inproj_body, tiles_per_seq=s // tm),
        grid=(n // tm,),
        in_specs=[row(D_MODEL), halo_prev, halo_next, _const_spec((1, D_MODEL)), sq, sq, sq, sq,
                  _const_spec((D_MODEL, CONV_DIM)), _const_spec((D_MODEL, LANES)),
                  _const_spec((1, D_ATT)), _const_spec((1, D_ATT)), _const_spec(bd.shape),
                  _const_spec(conv_w.shape), _const_spec(conv_b.shape)],
        out_specs=[pairs, pairs, pairs, row(D_SSM), row(CONV_DIM), row(LANES)],
        out_shape=[pairs_out, pairs_out, pairs_out, out(D_SSM), out(CONV_DIM), out(LANES)],
        scratch_shapes=[pltpu.VMEM((CONV_DIM // LANES, tm + 2 * HALO_ROWS, LANES), F32)],
        compiler_params=_params("parallel"),
        name="inproj",
    )(x, x, x, g, wq, wk, wv, wz, wx, wdt, qg, kg, bd, conv_w, conv_b)


def _rows(start, size, dil):
    return pl.ds(start, size) if dil == 1 else pl.ds(start, size, stride=dil)


def _regroup_keys(prev, cur, nxt, tmp, dst):
    s4 = ATT_STAGE
    lo, hi = ATT_REACH - ATT_HALF, ATT_REACH + ATT_TILE + ATT_HALF
    dst[0, lo:ATT_REACH, :] = prev[lo:ATT_REACH, :].astype(BF16)
    dst[0, ATT_REACH:ATT_REACH + ATT_TILE, :] = cur[...].astype(BF16)
    dst[0, ATT_REACH + ATT_TILE:hi, :] = nxt[0:ATT_HALF, :].astype(BF16)
    h4, t4 = ATT_REACH // s4, ATT_TILE // s4
    for r in range(s4):
        tmp[r, 0:h4, :] = prev[pl.ds(r, h4, stride=s4), :]
        tmp[r, h4:h4 + t4, :] = cur[pl.ds(r, t4, stride=s4), :]
        tmp[r, h4 + t4:, :] = nxt[pl.ds(r, h4, stride=s4), :]
    lo, hi = h4 - ATT_HALF, h4 + t4 + ATT_HALF
    for r in range(s4):
        dst[1, r * (ATT_WINDOW // s4) + lo:r * (ATT_WINDOW // s4) + hi, :] = tmp[r, lo:hi, :].astype(BF16)
    per = ATT_WINDOW // (s4 * s4)
    for r in range(s4 * s4):
        dst[2, r * per:(r + 1) * per, :] = tmp[r % s4, pl.ds(r // s4, per, stride=s4), :].astype(BF16)


def _attn_step(q_ref, kc_ref, kp_ref, kn_ref, vc_ref, vp_ref, vn_ref, bias_ref, out_ref,
               kd, vd, tmp, o_scr, lse_scr, *, t0, seq_len, row_max):
    _regroup_keys(kp_ref, kc_ref, kn_ref, tmp, kd)
    _regroup_keys(vp_ref, vc_ref, vn_ref, tmp, vd)

    first_head = lax.broadcasted_iota(jnp.int32, (ATT_BLOCK, LANES), 1) < HEAD_DIM
    first_head_kw = lax.broadcasted_iota(jnp.int32, (ATT_KW, LANES), 1) < HEAD_DIM
    head_ones = (first_head_kw.astype(BF16), (~first_head_kw).astype(BF16))

    def block(p, dil, r, j):
        blocks_per_residue = ATT_TILE // (dil * ATT_BLOCK)
        q0 = j * (ATT_BLOCK * dil) + r
        k0 = r * (ATT_WINDOW // dil) + ATT_REACH // dil - ATT_HALF + j * ATT_BLOCK
        variant = 0
        if j == 0:
            variant = variant + (t0 == 0).astype(jnp.int32)
        if j == blocks_per_residue - 1:
            variant = variant + 2 * (t0 + ATT_TILE == seq_len).astype(jnp.int32)
        qp = q_ref[_rows(q0, ATT_BLOCK, dil), :].astype(BF16)
        kp = kd[p, k0:k0 + ATT_KW, :]
        vp = vd[p, k0:k0 + ATT_KW, :]
        zq, zv = jnp.zeros_like(qp), jnp.zeros_like(vp)
        q2 = jnp.concatenate([jnp.where(first_head, qp, zq), jnp.where(first_head, zq, qp)], axis=0)
        s2 = lax.dot_general(q2, kp, (((1,), (1,)), ((), ())), preferred_element_type=F32)
        es, ms = [], []
        for sub in range(2):
            s = s2[sub * ATT_BLOCK:(sub + 1) * ATT_BLOCK, :] + bias_ref[variant, p, sub]
            if row_max:
                m = jnp.max(s, axis=-1, keepdims=True)
                s = s - m
                ms.append(m)
            es.append(jnp.exp2(s).astype(BF16))
        v2 = jnp.concatenate([
            jnp.concatenate([jnp.where(first_head_kw, vp, zv), head_ones[0]], axis=1),
            jnp.concatenate([jnp.where(first_head_kw, zv, vp), head_ones[1]], axis=1)], axis=0)
        ol = _dot(jnp.concatenate(es, axis=1), v2)
        l = ol[:, LANES:]
        if not row_max:
            return ol[:, :LANES], l
        return ol[:, :LANES] * (1.0 / l), jnp.where(first_head, ms[0], ms[1]) + jnp.log2(l)

    for p in range(len(ATT_PATTERNS) - 1, 0, -1):
        dil = ATT_PATTERNS[p][1]
        blocks_per_residue = ATT_TILE // (dil * ATT_BLOCK)
        for idx in range(ATT_TILE // ATT_BLOCK):
            r, j = idx // blocks_per_residue, idx % blocks_per_residue
            rows = _rows(j * (ATT_BLOCK * dil) + r, ATT_BLOCK, dil)
            o_scr[p - 1, rows, :], lse_scr[p - 1, rows, :] = block(p, dil, r, j)
    for j in range(ATT_TILE // ATT_BLOCK):
        rows = pl.ds(j * ATT_BLOCK, ATT_BLOCK)
        o1, l1 = block(0, 1, 0, j)
        l2, l3 = lse_scr[0, rows, :], lse_scr[1, rows, :]
        if not row_max:
            mixed = o1 + o_scr[0, rows, :] + o_scr[1, rows, :]
            out_ref[rows, :] = (mixed * (1.0 / (l1 + l2 + l3))).astype(BF16)
            continue
        m = jnp.maximum(jnp.maximum(l1, l2), l3)
        e1, e2, e3 = jnp.exp2(l1 - m), jnp.exp2(l2 - m), jnp.exp2(l3 - m)
        mixed = e1 * o1 + e2 * o_scr[0, rows, :] + e3 * o_scr[1, rows, :]
        out_ref[rows, :] = (mixed * (1.0 / (e1 + e2 + e3))).astype(BF16)


def _softplus(x):
    return jnp.maximum(x, 0.0) + jnp.log(1.0 + jnp.exp(-jnp.abs(x)))


def _tri(lower):
    r = lax.broadcasted_iota(jnp.int32, (CHUNK, CHUNK), 0)
    c = lax.broadcasted_iota(jnp.int32, (CHUNK, CHUNK), 1)
    return (r >= c) if lower else (r <= c)


def _cumsum_rows(a, lower):
    a1 = a.astype(BF16)
    r1 = a - a1.astype(F32)
    a2 = r1.astype(BF16)
    a3 = (r1 - a2.astype(F32)).astype(BF16)
    tri = _tri(lower).astype(BF16)
    return _dot(jnp.concatenate([tri, tri, tri], axis=1), jnp.concatenate([a1, a2, a3], axis=0))


def _chunk_rows(i):
    return pl.ds(pl.multiple_of(i * CHUNK, CHUNK), CHUNK)


def _ssd_bwd_body(xa_ref, dt_ref, dtb_ref, a_ref, e_ref, yoff_ref, hst):
    @pl.when(pl.program_id(1) == 0)
    def _():
        hst[...] = jnp.zeros_like(hst)

    def chunk(ci, carry):
        rs = _chunk_rows(SSD_STEP_CHUNKS - 1 - ci)
        dt = _softplus(dt_ref[rs, :] + dtb_ref[...])
        rcum = _cumsum_rows(dt * a_ref[...], lower=False)
        decay_in = _expand(_hi_lo(jnp.exp(rcum)), e_ref)
        w_in = _expand(_hi_lo(jnp.exp(rcum[0:1, :] - rcum) * dt), e_ref)
        chunk_decay = _expand(_hi_lo(jnp.broadcast_to(jnp.exp(rcum[0:1, :]), (HALO_ROWS, LANES))),
                              e_ref)[0:1, :]
        for g in range(N_GROUPS):
            gl = slice(g * HEADS_PER_GROUP * HEAD_DIM, (g + 1) * HEADS_PER_GROUP * HEAD_DIM)
            bg = xa_ref[rs, D_SSM + g * D_STATE:D_SSM + (g + 1) * D_STATE].astype(BF16)
            cg = xa_ref[rs, D_SSM + (N_GROUPS + g) * D_STATE:D_SSM + (N_GROUPS + g + 1) * D_STATE].astype(BF16)
            xw = (xa_ref[rs, gl] * w_in[:, gl]).astype(BF16)
            h_in = hst[:, gl]
            yoff_ref[rs, gl] = _dot(cg, h_in.astype(BF16)) * decay_in[:, gl]
            upd = lax.dot_general(bg, xw, (((0,), (0,)), ((), ())), preferred_element_type=F32)
            hst[:, gl] = h_in * chunk_decay[:, gl] + upd
        return carry

    lax.fori_loop(0, SSD_STEP_CHUNKS, chunk, 0, unroll=True)


def _ssd_fwd_chunks(n_chunks, xa_ref, dt_ref, z_ref, yb_ref, dtb_ref, a_ref, dsk_ref, og_ref, e_ref,
                    y_ref, hst):
    lower, upper = _tri(True), _tri(False)
    lane = lax.broadcasted_iota(jnp.int32, (CHUNK, LANES), 1)

    def chunk(ci, carry):
        rs = _chunk_rows(ci)
        dt = _softplus(dt_ref[rs, :] + dtb_ref[...])
        a = dt * a_ref[...]
        fcum = _cumsum_rows(a, lower=True)
        rcum = _cumsum_rows(a, lower=False)
        decay_in = _expand(_hi_lo(jnp.exp(fcum)), e_ref)
        w_in = _expand(_hi_lo(jnp.exp(fcum[CHUNK - 1:CHUNK, :] - fcum) * dt), e_ref)
        chunk_decay = _expand(_hi_lo(jnp.broadcast_to(jnp.exp(fcum[CHUNK - 1:CHUNK, :]),
                                                       (HALO_ROWS, LANES))), e_ref)[0:1, :]
        log2_dt = jnp.log(dt) * LOG2_E
        fcol, rcol = fcum * LOG2_E, rcum * LOG2_E
        frow, rrow = (fcol - log2_dt).T, (rcol - log2_dt).T

        for g in range(N_GROUPS):
            gl = slice(g * HEADS_PER_GROUP * HEAD_DIM, (g + 1) * HEADS_PER_GROUP * HEAD_DIM)
            bg = xa_ref[rs, D_SSM + g * D_STATE:D_SSM + (g + 1) * D_STATE].astype(BF16)
            cg = xa_ref[rs, D_SSM + (N_GROUPS + g) * D_STATE:D_SSM + (N_GROUPS + g + 1) * D_STATE].astype(BF16)
            cb = lax.dot_general(cg, bg, (((1,), (1,)), ((), ())), preferred_element_type=F32)
            xw = (xa_ref[rs, gl] * w_in[:, gl]).astype(BF16)
            h_in = hst[:, gl]
            y_off = _dot(cg, h_in.astype(BF16)) * decay_in[:, gl]
            upd = lax.dot_general(bg, xw, (((0,), (0,)), ((), ())), preferred_element_type=F32)
            hst[:, gl] = h_in * chunk_decay[:, gl] + upd
            gated = []
            for pair in range(HEADS_PER_GROUP // 2):
                pl_ = slice(gl.start + pair * LANES, gl.start + (pair + 1) * LANES)
                xs = xa_ref[rs, pl_]
                x_pair = xs.astype(BF16)
                halves = []
                for sub in range(2):
                    h = g * HEADS_PER_GROUP + 2 * pair + sub
                    hb = N_HEADS + h
                    seg_f = jnp.where(lower, fcol[:, h:h + 1] - frow[h:h + 1, :], NEG_INF)
                    seg_b = jnp.where(upper, rcol[:, hb:hb + 1] - rrow[hb:hb + 1, :], NEG_INF)
                    mix = jnp.exp2(seg_f) + jnp.exp2(seg_b)
                    halves.append(_dot((cb * mix).astype(BF16), x_pair))
                y_diag = jnp.where(lane < HEAD_DIM, halves[0], halves[1])
                po = slice(pair * LANES, (pair + 1) * LANES)
                y = y_diag + y_off[:, po] + yb_ref[rs, pl_] + dsk_ref[:, pl_] * xs
                gated.append(y * _silu(z_ref[rs, pl_]))
            y_ref[rs, gl] = _rms(jnp.concatenate(gated, axis=1), og_ref[:, gl]).astype(BF16)
        return carry

    lax.fori_loop(0, n_chunks, chunk, 0, unroll=True)


def _ssd_bwd(xa, dt, dt_bias, a_neg, e_bwd):
    b, s, _ = xa.shape
    rows = SSD_STEP_CHUNKS * CHUNK
    nc = s // rows
    step = lambda w: pl.BlockSpec((None, rows, w), lambda bi, i: (bi, nc - 1 - i, 0))
    return pl.pallas_call(
        _ssd_bwd_body,
        grid=(b, nc),
        in_specs=[step(CONV_DIM), step(LANES), _const_spec(dt_bias.shape), _const_spec(a_neg.shape),
                  _const_spec(e_bwd.shape)],
        out_specs=step(D_SSM),
        out_shape=jax.ShapeDtypeStruct((b, s, D_SSM), F32),
        scratch_shapes=[pltpu.VMEM((D_STATE, D_SSM), F32)],
        compiler_params=_params("parallel", "arbitrary"),
        name="ssd_bwd",
    )(xa, dt, dt_bias, a_neg, e_bwd)


def _mixer_body(narrow_ref, q_ref, kc_ref, kp_ref, kn_ref, vc_ref, vp_ref, vn_ref, bias_ref,
                xa_ref, dt_ref, z_ref, yb_ref, dtb_ref, a_ref, dsk_ref, og_ref, e_ref,
                attn_ref, y_ref, kd, vd, tmp, o_scr, lse_scr, hst, *, seq_len):
    hp, t = pl.program_id(1), pl.program_id(2)

    @pl.when(jnp.logical_and(t == 0, hp == 0))
    def _():
        hst[...] = jnp.zeros_like(hst)

    def step(row_max):
        _attn_step(q_ref, kc_ref, kp_ref, kn_ref, vc_ref, vp_ref, vn_ref, bias_ref, attn_ref,
                   kd, vd, tmp, o_scr, lse_scr, t0=t * ATT_TILE, seq_len=seq_len, row_max=row_max)
        _ssd_fwd_chunks(MIXER_CHUNKS, xa_ref, dt_ref, z_ref, yb_ref, dtb_ref, a_ref, dsk_ref, og_ref,
                        e_ref, y_ref, hst)

    pl.when(narrow_ref[0] != 0)(functools.partial(step, False))
    pl.when(narrow_ref[0] == 0)(functools.partial(step, True))


def _mixer(q, k, v, att, xa, dt, z, yb, dt_bias, a_neg, d_skip, out_g, e_fwd):
    bias = att["bias"]
    n_pairs, b, s, _ = q.shape
    n_pat = len(ATT_PATTERNS)
    last_halo = s // ATT_REACH - 1
    halo_per_tile = ATT_TILE // ATT_REACH
    n_tiles = s // ATT_TILE
    cur = pl.BlockSpec((None, None, ATT_TILE, LANES), lambda bi, hp, t: (hp, bi, t, 0))
    prev = pl.BlockSpec((None, None, ATT_REACH, LANES),
                        lambda bi, hp, t: (hp, bi, jnp.maximum(t * halo_per_tile - 1, 0), 0))
    nxt = pl.BlockSpec((None, None, ATT_REACH, LANES),
                       lambda bi, hp, t: (hp, bi, jnp.minimum((t + 1) * halo_per_tile, last_halo), 0))
    bias_spec = pl.BlockSpec((bias.shape[0], n_pat, 2, ATT_BLOCK, ATT_KW),
                             lambda bi, hp, t: (0, 0, hp, 0, 0))
    rows = MIXER_CHUNKS * CHUNK
    scan = lambda w: pl.BlockSpec((None, rows, w), lambda bi, hp, t: (bi, hp * n_tiles + t, 0))
    consts = [_const_spec(a.shape) for a in (dt_bias, a_neg, d_skip, out_g, e_fwd)]
    return pl.pallas_call(
        functools.partial(_mixer_body, seq_len=s),
        grid=(b, n_pairs, n_tiles),
        in_specs=[pl.BlockSpec(memory_space=pltpu.SMEM), cur, cur, prev, nxt, cur, prev, nxt, bias_spec,
                  scan(CONV_DIM), scan(LANES), scan(D_SSM), scan(D_SSM)] + consts,
        out_specs=[cur, scan(D_SSM)],
        out_shape=[jax.ShapeDtypeStruct((n_pairs, b, s, LANES), BF16),
                   jax.ShapeDtypeStruct((b, s, D_SSM), BF16)],
        scratch_shapes=[pltpu.VMEM((n_pat, ATT_WINDOW, LANES), BF16),
                        pltpu.VMEM((n_pat, ATT_WINDOW, LANES), BF16),
                        pltpu.VMEM((ATT_STAGE, ATT_WINDOW // ATT_STAGE, LANES), F32),
                        pltpu.VMEM((n_pat - 1, ATT_TILE, LANES), F32),
                        pltpu.VMEM((n_pat - 1, ATT_TILE, LANES), F32),
                        pltpu.VMEM((D_STATE, D_SSM), F32)],
        compiler_params=_params("parallel", "arbitrary", "arbitrary"),
        name="mixer",
    )(att["narrow"], q, k, k, k, v, v, v, bias, xa, dt, z, yb, dt_bias, a_neg, d_skip, out_g, e_fwd)


def _outproj_ffn_body(x_ref, a_ref, y_ref, g_ref, wa_ref, ws_ref, g2_ref, wg_ref, wu_ref, wd_ref,
                      out_ref):
    attn = jnp.concatenate([a_ref[hp] for hp in range(a_ref.shape[0])], axis=1)
    attn = _rms(attn.astype(F32), g_ref[...]).astype(BF16)
    x2 = x_ref[...] + _dot(attn, wa_ref[...]) + _dot(y_ref[...], ws_ref[...])
    out_ref[...] = _half_step_ffn(x2, g2_ref, wg_ref, wu_ref, wd_ref)


def _outproj_ffn(x, attn, y, g, wa, ws, g2, wg, wu, wd, tm):
    n = x.shape[0]
    row = lambda w: pl.BlockSpec((tm, w), lambda i: (i, 0))
    sq = _const_spec((D_ATT, D_MODEL))
    pairs = pl.BlockSpec((attn.shape[0], tm, LANES), lambda i: (0, i, 0))
    return pl.pallas_call(
        _outproj_ffn_body,
        grid=(n // tm,),
        in_specs=[row(D_MODEL), pairs, row(D_SSM), _const_spec((1, D_ATT)), sq, sq,
                  _const_spec((1, D_MODEL)), _const_spec((D_MODEL, D_FF)), _const_spec((D_MODEL, D_FF)),
                  _const_spec((D_FF, D_MODEL))],
        out_specs=row(D_MODEL),
        out_shape=jax.ShapeDtypeStruct((n, D_MODEL), F32),
        compiler_params=_params("parallel"),
        name="outproj_ffn",
    )(x, attn, y, g, wa, ws, g2, wg, wu, wd)


def _t5_bucket(rel):
    nb = N_REL_BUCKETS // 2
    max_exact = nb // 2
    n = np.abs(rel)
    large = max_exact + (np.log(np.maximum(n, 1) / max_exact)
                         / math.log(REL_MAX_DIST / max_exact) * (nb - max_exact)).astype(np.int32)
    large = np.minimum(large, nb - 1)
    return (np.where(rel > 0, nb, 0) + np.where(n < max_exact, n, large)).astype(np.int32)


def _bias_tables(rel_bias, shift):
    rel_sub = np.arange(ATT_KW)[None, :] - ATT_HALF - np.arange(ATT_BLOCK)[:, None]
    in_window = np.abs(rel_sub) <= ATT_HALF
    col = np.arange(ATT_KW)[None, :]
    tables = []
    for _, dil in ATT_PATTERNS:
        onehot = (_t5_bucket(rel_sub * dil)[:, :, None] == np.arange(N_REL_BUCKETS)).astype(np.float32)
        bias = jnp.einsum("ijk,kh->hij", jnp.asarray(onehot, BF16).astype(F32), rel_bias.astype(F32),
                          precision=lax.Precision.HIGHEST)
        tables.append(bias * LOG2_E - shift)
    bias = jnp.stack(tables, axis=0)
    variants = []
    for v in range(4):
        keep = in_window & ((col >= ATT_HALF) | (v & 1 == 0)) & ((col < ATT_BLOCK + ATT_HALF) | (v & 2 == 0))
        variants.append(jnp.where(keep[None, None], bias, NEG_INF))
    return jnp.stack(variants, axis=0)


def _attention_tables(rel_bias, q_norm_g, k_norm_g):
    logit_bound = (QK_BOUND_SLACK * HEAD_DIM * HEAD_DIM ** -0.5 * LOG2_E
                   * jnp.max(jnp.abs(q_norm_g)) * jnp.max(jnp.abs(k_norm_g))).astype(F32)
    bias_lo = (jnp.min(rel_bias) * LOG2_E).astype(F32)
    bias_hi = (jnp.max(rel_bias) * LOG2_E).astype(F32)
    narrow = 2.0 * logit_bound + (bias_hi - bias_lo) < EXP2_SAFE_RANGE
    return {
        "bias": _bias_tables(rel_bias, jnp.where(narrow, logit_bound + bias_hi, 0.0)),
        "narrow": narrow.astype(jnp.int32).reshape(1),
    }


def _head_expander(first_row, width):
    e = np.zeros((2 * LANES, width), np.float32)
    for h in range(width // HEAD_DIM):
        e[first_row + h, h * HEAD_DIM:(h + 1) * HEAD_DIM] = 1.0
        e[LANES + first_row + h, h * HEAD_DIM:(h + 1) * HEAD_DIM] = 1.0
    return jnp.asarray(e, BF16)


def _block_diag_mean(width):
    i = np.arange(width)
    return jnp.asarray((i[:, None] // HEAD_DIM == i[None, :] // HEAD_DIM) / HEAD_DIM, BF16)


def _layer(x, p, tm):
    b, s, _ = x.shape
    n = b * s
    xf = x.reshape(n, D_MODEL)
    x1 = _ffn(xf, p["ffn1_g"], p["ffn1_wg"], p["ffn1_wu"], p["ffn1_wd"], tm)
    q, k, v, z, xa, dt = _inproj(x1, p["mix_g"], p["wq"], p["wk"], p["wv"], p["wz"], p["wx"],
                                 p["wdt"], p["qg"], p["kg"], p["bd"], p["conv_w"], p["conv_b"], s, tm)
    seq = lambda a: a.reshape(b, s, a.shape[-1])
    pair_seq = lambda a: a.reshape(a.shape[0], b, s, LANES)
    yb = _ssd_bwd(seq(xa), seq(dt), p["dt_bias"], p["a_neg"], p["e_bwd"])
    attn, y = _mixer(pair_seq(q), pair_seq(k), pair_seq(v), p["att"], seq(xa), seq(dt), seq(z), yb,
                     p["dt_bias"], p["a_neg"], p["d_skip"], p["ssm_g"], p["e_fwd"])
    out = _outproj_ffn(x1, attn.reshape(attn.shape[0], n, LANES), y.reshape(n, D_SSM), p["attn_g"],
                       p["wo_att"], p["wo_ssm"], p["ffn2_g"], p["ffn2_wg"], p["ffn2_wu"],
                       p["ffn2_wd"], tm)
    return out.reshape(b, s, D_MODEL)


def _prepare(rel_bias, ffn1_norm_g, ffn1_w_gate, ffn1_w_up, ffn1_w_down, mix_norm_g, w_in,
             q_norm_g, k_norm_g, attn_out_g, conv_w, conv_b, dt_bias, a_log, d_skip, ssm_out_g,
             w_out, ffn2_norm_g, ffn2_w_gate, ffn2_w_up, ffn2_w_down):
    row = lambda a: a.reshape(1, -1).astype(F32)
    w16 = lambda a: a.astype(BF16)
    c0, c1, c2, c3 = D_ATT, 2 * D_ATT, 3 * D_ATT, 3 * D_ATT + D_SSM
    c4 = c3 + CONV_DIM
    pad32 = lambda a: jnp.pad(a.reshape(1, 2 * N_HEADS).astype(F32), ((0, 0), (0, LANES - 2 * N_HEADS)))
    return {
        "ffn1_g": row(ffn1_norm_g), "ffn1_wg": w16(ffn1_w_gate), "ffn1_wu": w16(ffn1_w_up),
        "ffn1_wd": w16(ffn1_w_down),
        "ffn2_g": row(ffn2_norm_g), "ffn2_wg": w16(ffn2_w_gate), "ffn2_wu": w16(ffn2_w_up),
        "ffn2_wd": w16(ffn2_w_down),
        "mix_g": row(mix_norm_g),
        "wq": w16(w_in[:, :c0]), "wk": w16(w_in[:, c0:c1]), "wv": w16(w_in[:, c1:c2]),
        "wz": w16(w_in[:, c2:c3]), "wx": w16(w_in[:, c3:c4]),
        "wdt": w16(jnp.pad(w_in[:, c4:], ((0, 0), (0, LANES - 2 * N_HEADS)))),
        "qg": row(jnp.tile(q_norm_g, N_HEADS)) * (HEAD_DIM ** -0.5 * LOG2_E),
        "kg": row(jnp.tile(k_norm_g, N_HEADS)),
        "bd": _block_diag_mean(MXU_TILE),
        "att": _attention_tables(rel_bias, q_norm_g, k_norm_g),
        "attn_g": row(attn_out_g),
        "conv_w": conv_w.astype(F32), "conv_b": row(conv_b),
        "dt_bias": pad32(dt_bias), "a_neg": pad32(-jnp.exp(a_log.astype(F32))),
        "d_skip": row(jnp.repeat(d_skip, HEAD_DIM)), "ssm_g": row(ssm_out_g),
        "e_fwd": _head_expander(0, D_SSM), "e_bwd": _head_expander(N_HEADS, D_SSM),
        "wo_att": w16(w_out[:D_ATT]), "wo_ssm": w16(w_out[D_ATT:]),
    }


def _trunk(x, layers, tm=512):
    for p in layers:
        x = _layer(x, p, tm)
    return x


def kernel(x_prompt, x_sample, rel_bias, ffn1_norm_g, ffn1_w_gate, ffn1_w_up, ffn1_w_down, mix_norm_g, w_in, q_norm_g, k_norm_g, attn_out_g, conv_w, conv_b, dt_bias, a_log, d_skip, ssm_out_g, w_out, ffn2_norm_g, ffn2_w_gate, ffn2_w_up, ffn2_w_down):
    per_layer = (ffn1_norm_g, ffn1_w_gate, ffn1_w_up, ffn1_w_down, mix_norm_g, w_in, q_norm_g,
                 k_norm_g, attn_out_g, conv_w, conv_b, dt_bias, a_log, d_skip, ssm_out_g, w_out,
                 ffn2_norm_g, ffn2_w_gate, ffn2_w_up, ffn2_w_down)
    layers = [_prepare(rel_bias, *(a[l] for a in per_layer)) for l in range(ffn1_norm_g.shape[0])]
    return (_trunk(x_prompt, layers), _trunk(x_sample, layers))
```

```python
import functools
import math

import numpy as np
import jax
import jax.numpy as jnp
from jax import lax
from jax.experimental import pallas as pl
from jax.experimental.pallas import tpu as pltpu

D_MODEL = 1024
D_ATT = 1024
D_SSM = 1024
HEAD_DIM = 64
N_HEADS = 16
ATT_PATTERNS = ((128, 1), (512, 4), (2048, 16))
ATT_BLOCK = 128
ATT_HALF = 64
ATT_KW = ATT_BLOCK + 2 * ATT_HALF
ATT_MAX_DIL = max(d for _, d in ATT_PATTERNS)
ATT_TILE = ATT_BLOCK * ATT_MAX_DIL
ATT_REACH = ATT_HALF * ATT_MAX_DIL
ATT_WINDOW = ATT_TILE + 2 * ATT_REACH
ATT_STAGE = 4
assert tuple(d for _, d in ATT_PATTERNS) == (1, ATT_STAGE, ATT_STAGE * ATT_STAGE)
N_REL_BUCKETS = 32
REL_MAX_DIST = 1024
N_GROUPS = 4
HEADS_PER_GROUP = 4
D_STATE = 128
D_CONV = 5
CHUNK = 128
SSD_STEP_CHUNKS = 4
MIXER_CHUNKS = ATT_TILE // CHUNK // (N_HEADS // 2)
CONV_DIM = D_SSM + 2 * N_GROUPS * D_STATE
D_FF = 2816
EPS = 1e-6
NEG_INF = -1e30
LOG2_E = 1.4426950408889634
EXP2_SAFE_RANGE = 100.0
QK_BOUND_SLACK = 1.05

LANES = 128
MXU_TILE = 256
HALO_ROWS = 8
VMEM_LIMIT = 56 * 1024 * 1024

F32 = jnp.float32
BF16 = jnp.bfloat16


def _params(*sem):
    return pltpu.CompilerParams(dimension_semantics=sem, vmem_limit_bytes=VMEM_LIMIT)


def _const_spec(shape):
    n = len(shape)
    return pl.BlockSpec(shape, lambda *_: (0,) * n, pipeline_mode=pl.Buffered(1))


def _rms(x, g):
    ms = jnp.mean(x * x, axis=-1, keepdims=True)
    return x * lax.rsqrt(ms + EPS) * g


def _silu(x):
    h = 0.5 * x
    return h + h * jnp.tanh(h)


def _dot(a, b):
    return jnp.dot(a, b, preferred_element_type=F32)


def _hi_lo(vals):
    hi = vals.astype(BF16)
    lo = (vals - hi.astype(F32)).astype(BF16)
    return jnp.concatenate([hi, lo], axis=1)


def _expand(split, e_ref, cols=slice(None)):
    return _dot(split, e_ref[:, cols])


def _half_step_ffn(x, g_ref, wg_ref, wu_ref, wd_ref):
    h = _rms(x, g_ref[...]).astype(BF16)
    gate = _dot(h, wg_ref[...])
    up = _dot(h, wu_ref[...])
    act = (_silu(gate) * up).astype(BF16)
    return x + 0.5 * _dot(act, wd_ref[...])


def _ffn_body(x_ref, g_ref, wg_ref, wu_ref, wd_ref, o_ref):
    o_ref[...] = _half_step_ffn(x_ref[...], g_ref, wg_ref, wu_ref, wd_ref)


def _ffn(x, g, wg, wu, wd, tm):
    n = x.shape[0]
    row = pl.BlockSpec((tm, D_MODEL), lambda i: (i, 0))
    return pl.pallas_call(
        _ffn_body,
        grid=(n // tm,),
        in_specs=[row, _const_spec((1, D_MODEL)), _const_spec((D_MODEL, D_FF)),
                  _const_spec((D_MODEL, D_FF)), _const_spec((D_FF, D_MODEL))],
        out_specs=row,
        out_shape=jax.ShapeDtypeStruct((n, D_MODEL), F32),
        compiler_params=_params("parallel"),
        name="ffn",
    )(x, g, wg, wu, wd)


def _inproj_body(x_ref, xp_ref, xn_ref, g_ref, wq_ref, wk_ref, wv_ref, wz_ref, wx_ref, wdt_ref,
                 qg_ref, kg_ref, bd_ref, cw_ref, cb_ref, q_ref, k_ref, v_ref, z_ref, xa_ref, dt_ref, ext,
                 *, tiles_per_seq):
    h = _rms(x_ref[...], g_ref[...]).astype(BF16)

    def head_norm(t, gain):
        t2 = (t * t).astype(BF16)
        w = bd_ref.shape[0]
        ms = jnp.concatenate([_dot(t2[:, j * w:(j + 1) * w], bd_ref[...])
                              for j in range(D_ATT // w)], axis=1)
        return t * lax.rsqrt(ms + EPS) * gain

    def put_pairs(ref, val):
        for hp in range(N_HEADS // 2):
            ref[hp] = val[:, hp * LANES:(hp + 1) * LANES]

    put_pairs(q_ref, head_norm(_dot(h, wq_ref[...]), qg_ref[...]))
    put_pairs(k_ref, head_norm(_dot(h, wk_ref[...]), kg_ref[...]))
    put_pairs(v_ref, _dot(h, wv_ref[...]))
    z_ref[...] = _dot(h, wz_ref[...])
    dt_ref[...] = _dot(h, wdt_ref[...])

    tile = pl.program_id(0) % tiles_per_seq
    halo = jnp.concatenate([xp_ref[...], xn_ref[...]], axis=0)
    xbc_halo = _dot(_rms(halo, g_ref[...]).astype(BF16), wx_ref[...])
    xbc = _dot(h, wx_ref[...])
    rows = xbc.shape[0]
    pad = D_CONV // 2
    for sl in range(CONV_DIM // LANES):
        lanes = slice(sl * LANES, (sl + 1) * LANES)
        ext[sl, 0:HALO_ROWS, :] = jnp.where(tile > 0, xbc_halo[0:HALO_ROWS, lanes], 0.0)
        ext[sl, HALO_ROWS:HALO_ROWS + rows, :] = xbc[:, lanes]
        ext[sl, HALO_ROWS + rows:, :] = jnp.where(tile < tiles_per_seq - 1, xbc_halo[HALO_ROWS:, lanes], 0.0)
        acc = cb_ref[:, lanes]
        for j in range(D_CONV):
            acc = acc + ext[sl, pl.ds(HALO_ROWS - pad + j, rows, stride=1), :] * cw_ref[j:j + 1, lanes]
        xa_ref[:, lanes] = _silu(acc)


def _inproj(x, g, wq, wk, wv, wz, wx, wdt, qg, kg, bd, conv_w, conv_b, s, tm):
    n = x.shape[0]
    halos_per_tile = tm // HALO_ROWS
    halo_prev = pl.BlockSpec((HALO_ROWS, D_MODEL), lambda i: (jnp.maximum(i * halos_per_tile - 1, 0), 0))
    halo_next = pl.BlockSpec((HALO_ROWS, D_MODEL),
                             lambda i: (jnp.minimum((i + 1) * halos_per_tile, n // HALO_ROWS - 1), 0))
    row = lambda w: pl.BlockSpec((tm, w), lambda i: (i, 0))
    sq = _const_spec((D_MODEL, D_ATT))
    out = lambda w: jax.ShapeDtypeStruct((n, w), F32)
    pairs = pl.BlockSpec((N_HEADS // 2, tm, LANES), lambda i: (0, i, 0))
    pairs_out = jax.ShapeDtypeStruct((N_HEADS // 2, n, LANES), F32)
    return pl.pallas_call(
        functools.partial(_inproj_body, tiles_per_seq=s // tm),
        grid=(n // tm,),
        in_specs=[row(D_MODEL), halo_prev, halo_next, _const_spec((1, D_MODEL)), sq, sq, sq, sq,
                  _const_spec((D_MODEL, CONV_DIM)), _const_spec((D_MODEL, LANES)),
                  _const_spec((1, D_ATT)), _const_spec((1, D_ATT)), _const_spec(bd.shape),
                  _const_spec(conv_w.shape), _const_spec(conv_b.shape)],
        out_specs=[pairs, pairs, pairs, row(D_SSM), row(CONV_DIM), row(LANES)],
        out_shape=[pairs_out, pairs_out, pairs_out, out(D_SSM), out(CONV_DIM), out(LANES)],
        scratch_shapes=[pltpu.VMEM((CONV_DIM // LANES, tm + 2 * HALO_ROWS, LANES), F32)],
        compiler_params=_params("parallel"),
        name="inproj",
    )(x, x, x, g, wq, wk, wv, wz, wx, wdt, qg, kg, bd, conv_w, conv_b)


def _rows(start, size, dil):
    return pl.ds(start, size) if dil == 1 else pl.ds(start, size, stride=dil)


def _regroup_keys(prev, cur, nxt, tmp, dst):
    s4 = ATT_STAGE
    lo, hi = ATT_REACH - ATT_HALF, ATT_REACH + ATT_TILE + ATT_HALF
    dst[0, lo:ATT_REACH, :] = prev[lo:ATT_REACH, :].astype(BF16)
    dst[0, ATT_REACH:ATT_REACH + ATT_TILE, :] = cur[...].astype(BF16)
    dst[0, ATT_REACH + ATT_TILE:hi, :] = nxt[0:ATT_HALF, :].astype(BF16)
    h4, t4 = ATT_REACH // s4, ATT_TILE // s4
    for r in range(s4):
        tmp[r, 0:h4, :] = prev[pl.ds(r, h4, stride=s4), :]
        tmp[r, h4:h4 + t4, :] = cur[pl.ds(r, t4, stride=s4), :]
        tmp[r, h4 + t4:, :] = nxt[pl.ds(r, h4, stride=s4), :]
    lo, hi = h4 - ATT_HALF, h4 + t4 + ATT_HALF
    for r in range(s4):
        dst[1, r * (ATT_WINDOW // s4) + lo:r * (ATT_WINDOW // s4) + hi, :] = tmp[r, lo:hi, :].astype(BF16)
    per = ATT_WINDOW // (s4 * s4)
    for r in range(s4 * s4):
        dst[2, r * per:(r + 1) * per, :] = tmp[r % s4, pl.ds(r // s4, per, stride=s4), :].astype(BF16)


def _attn_step(q_ref, kc_ref, kp_ref, kn_ref, vc_ref, vp_ref, vn_ref, bias_ref, out_ref,
               kd, vd, tmp, o_scr, lse_scr, *, t0, seq_len, row_max):
    _regroup_keys(kp_ref, kc_ref, kn_ref, tmp, kd)
    _regroup_keys(vp_ref, vc_ref, vn_ref, tmp, vd)

    first_head = lax.broadcasted_iota(jnp.int32, (ATT_BLOCK, LANES), 1) < HEAD_DIM
    first_head_kw = lax.broadcasted_iota(jnp.int32, (ATT_KW, LANES), 1) < HEAD_DIM
    head_ones = (first_head_kw.astype(BF16), (~first_head_kw).astype(BF16))

    def block(p, dil, r, j):
        blocks_per_residue = ATT_TILE // (dil * ATT_BLOCK)
        q0 = j * (ATT_BLOCK * dil) + r
        k0 = r * (ATT_WINDOW // dil) + ATT_REACH // dil - ATT_HALF + j * ATT_BLOCK
        variant = 0
        if j == 0:
            variant = variant + (t0 == 0).astype(jnp.int32)
        if j == blocks_per_residue - 1:
            variant = variant + 2 * (t0 + ATT_TILE == seq_len).astype(jnp.int32)
        qp = q_ref[_rows(q0, ATT_BLOCK, dil), :].astype(BF16)
        kp = kd[p, k0:k0 + ATT_KW, :]
        vp = vd[p, k0:k0 + ATT_KW, :]
        zq, zv = jnp.zeros_like(qp), jnp.zeros_like(vp)
        q2 = jnp.concatenate([jnp.where(first_head, qp, zq), jnp.where(first_head, zq, qp)], axis=0)
        s2 = lax.dot_general(q2, kp, (((1,), (1,)), ((), ())), preferred_element_type=F32)
        es, ms = [], []
        for sub in range(2):
            s = s2[sub * ATT_BLOCK:(sub + 1) * ATT_BLOCK, :] + bias_ref[variant, p, sub]
            if row_max:
                m = jnp.max(s, axis=-1, keepdims=True)
                s = s - m
                ms.append(m)
            es.append(jnp.exp2(s).astype(BF16))
        v2 = jnp.concatenate([
            jnp.concatenate([jnp.where(first_head_kw, vp, zv), head_ones[0]], axis=1),
            jnp.concatenate([jnp.where(first_head_kw, zv, vp), head_ones[1]], axis=1)], axis=0)
        ol = _dot(jnp.concatenate(es, axis=1), v2)
        l = ol[:, LANES:]
        if not row_max:
            return ol[:, :LANES], l
        return ol[:, :LANES] * (1.0 / l), jnp.where(first_head, ms[0], ms[1]) + jnp.log2(l)

    for p in range(len(ATT_PATTERNS) - 1, 0, -1):
        dil = ATT_PATTERNS[p][1]
        blocks_per_residue = ATT_TILE // (dil * ATT_BLOCK)
        for idx in range(ATT_TILE // ATT_BLOCK):
            r, j = idx // blocks_per_residue, idx % blocks_per_residue
            rows = _rows(j * (ATT_BLOCK * dil) + r, ATT_BLOCK, dil)
            o_scr[p - 1, rows, :], lse_scr[p - 1, rows, :] = block(p, dil, r, j)
    for j in range(ATT_TILE // ATT_BLOCK):
        rows = pl.ds(j * ATT_BLOCK, ATT_BLOCK)
        o1, l1 = block(0, 1, 0, j)
        l2, l3 = lse_scr[0, rows, :], lse_scr[1, rows, :]
        if not row_max:
            mixed = o1 + o_scr[0, rows, :] + o_scr[1, rows, :]
            out_ref[rows, :] = (mixed * (1.0 / (l1 + l2 + l3))).astype(BF16)
            continue
        m = jnp.maximum(jnp.maximum(l1, l2), l3)
        e1, e2, e3 = jnp.exp2(l1 - m), jnp.exp2(l2 - m), jnp.exp2(l3 - m)
        mixed = e1 * o1 + e2 * o_scr[0, rows, :] + e3 * o_scr[1, rows, :]
        out_ref[rows, :] = (mixed * (1.0 / (e1 + e2 + e3))).astype(BF16)


def _softplus(x):
    return jnp.maximum(x, 0.0) + jnp.log(1.0 + jnp.exp(-jnp.abs(x)))


def _tri(lower):
    r = lax.broadcasted_iota(jnp.int32, (CHUNK, CHUNK), 0)
    c = lax.broadcasted_iota(jnp.int32, (CHUNK, CHUNK), 1)
    return (r >= c) if lower else (r <= c)


def _cumsum_rows(a, lower):
    a1 = a.astype(BF16)
    r1 = a - a1.astype(F32)
    a2 = r1.astype(BF16)
    a3 = (r1 - a2.astype(F32)).astype(BF16)
    tri = _tri(lower).astype(BF16)
    return _dot(jnp.concatenate([tri, tri, tri], axis=1), jnp.concatenate([a1, a2, a3], axis=0))


def _chunk_rows(i):
    return pl.ds(pl.multiple_of(i * CHUNK, CHUNK), CHUNK)


def _ssd_bwd_body(xa_ref, dt_ref, dtb_ref, a_ref, e_ref, yoff_ref, hst):
    @pl.when(pl.program_id(1) == 0)
    def _():
        hst[...] = jnp.zeros_like(hst)

    def chunk(ci, carry):
        rs = _chunk_rows(SSD_STEP_CHUNKS - 1 - ci)
        dt = _softplus(dt_ref[rs, :] + dtb_ref[...])
        rcum = _cumsum_rows(dt * a_ref[...], lower=False)
        decay_in = _expand(_hi_lo(jnp.exp(rcum)), e_ref)
        w_in = _expand(_hi_lo(jnp.exp(rcum[0:1, :] - rcum) * dt), e_ref)
        chunk_decay = _expand(_hi_lo(jnp.broadcast_to(jnp.exp(rcum[0:1, :]), (HALO_ROWS, LANES))),
                              e_ref)[0:1, :]
        for g in range(N_GROUPS):
            gl = slice(g * HEADS_PER_GROUP * HEAD_DIM, (g + 1) * HEADS_PER_GROUP * HEAD_DIM)
            bg = xa_ref[rs, D_SSM + g * D_STATE:D_SSM + (g + 1) * D_STATE].astype(BF16)
            cg = xa_ref[rs, D_SSM + (N_GROUPS + g) * D_STATE:D_SSM + (N_GROUPS + g + 1) * D_STATE].astype(BF16)
            xw = (xa_ref[rs, gl] * w_in[:, gl]).astype(BF16)
            h_in = hst[:, gl]
            yoff_ref[rs, gl] = _dot(cg, h_in.astype(BF16)) * decay_in[:, gl]
            upd = lax.dot_general(bg, xw, (((0,), (0,)), ((), ())), preferred_element_type=F32)
            hst[:, gl] = h_in * chunk_decay[:, gl] + upd
        return carry

    lax.fori_loop(0, SSD_STEP_CHUNKS, chunk, 0, unroll=True)


def _ssd_fwd_chunks(n_chunks, xa_ref, dt_ref, z_ref, yb_ref, dtb_ref, a_ref, dsk_ref, og_ref, e_ref,
                    y_ref, hst):
    lower, upper = _tri(True), _tri(False)
    lane = lax.broadcasted_iota(jnp.int32, (CHUNK, LANES), 1)

    def chunk(ci, carry):
        rs = _chunk_rows(ci)
        dt = _softplus(dt_ref[rs, :] + dtb_ref[...])
        a = dt * a_ref[...]
        fcum = _cumsum_rows(a, lower=True)
        rcum = _cumsum_rows(a, lower=False)
        decay_in = _expand(_hi_lo(jnp.exp(fcum)), e_ref)
        w_in = _expand(_hi_lo(jnp.exp(fcum[CHUNK - 1:CHUNK, :] - fcum) * dt), e_ref)
        chunk_decay = _expand(_hi_lo(jnp.broadcast_to(jnp.exp(fcum[CHUNK - 1:CHUNK, :]),
                                                       (HALO_ROWS, LANES))), e_ref)[0:1, :]
        log2_dt = jnp.log(dt) * LOG2_E
        fcol, rcol = fcum * LOG2_E, rcum * LOG2_E
        frow, rrow = (fcol - log2_dt).T, (rcol - log2_dt).T

        for g in range(N_GROUPS):
            gl = slice(g * HEADS_PER_GROUP * HEAD_DIM, (g + 1) * HEADS_PER_GROUP * HEAD_DIM)
            bg = xa_ref[rs, D_SSM + g * D_STATE:D_SSM + (g + 1) * D_STATE].astype(BF16)
            cg = xa_ref[rs, D_SSM + (N_GROUPS + g) * D_STATE:D_SSM + (N_GROUPS + g + 1) * D_STATE].astype(BF16)
            cb = lax.dot_general(cg, bg, (((1,), (1,)), ((), ())), preferred_element_type=F32)
            xw = (xa_ref[rs, gl] * w_in[:, gl]).astype(BF16)
            h_in = hst[:, gl]
            y_off = _dot(cg, h_in.astype(BF16)) * decay_in[:, gl]
            upd = lax.dot_general(bg, xw, (((0,), (0,)), ((), ())), preferred_element_type=F32)
            hst[:, gl] = h_in * chunk_decay[:, gl] + upd
            gated = []
            for pair in range(HEADS_PER_GROUP // 2):
                pl_ = slice(gl.start + pair * LANES, gl.start + (pair + 1) * LANES)
                xs = xa_ref[rs, pl_]
                x_pair = xs.astype(BF16)
                zx = jnp.zeros_like(x_pair)
                weights = []
                for sub in range(2):
                    h = g * HEADS_PER_GROUP + 2 * pair + sub
                    hb = N_HEADS + h
                    seg_f = jnp.where(lower, fcol[:, h:h + 1] - frow[h:h + 1, :], NEG_INF)
                    seg_b = jnp.where(upper, rcol[:, hb:hb + 1] - rrow[hb:hb + 1, :], NEG_INF)
                    mix = jnp.exp2(seg_f) + jnp.exp2(seg_b)
                    weights.append((cb * mix).astype(BF16))
                x2 = jnp.concatenate([jnp.where(lane < HEAD_DIM, x_pair, zx),
                                      jnp.where(lane < HEAD_DIM, zx, x_pair)], axis=0)
                y_diag = _dot(jnp.concatenate(weights, axis=1), x2)
                po = slice(pair * LANES, (pair + 1) * LANES)
                y = y_diag + y_off[:, po] + yb_ref[rs, pl_] + dsk_ref[:, pl_] * xs
                gated.append(y * _silu(z_ref[rs, pl_]))
            y_ref[rs, gl] = _rms(jnp.concatenate(gated, axis=1), og_ref[:, gl]).astype(BF16)
        return carry

    lax.fori_loop(0, n_chunks, chunk, 0, unroll=True)


def _ssd_bwd(xa, dt, dt_bias, a_neg, e_bwd):
    b, s, _ = xa.shape
    rows = SSD_STEP_CHUNKS * CHUNK
    nc = s // rows
    step = lambda w: pl.BlockSpec((None, rows, w), lambda bi, i: (bi, nc - 1 - i, 0))
    return pl.pallas_call(
        _ssd_bwd_body,
        grid=(b, nc),
        in_specs=[step(CONV_DIM), step(LANES), _const_spec(dt_bias.shape), _const_spec(a_neg.shape),
                  _const_spec(e_bwd.shape)],
        out_specs=step(D_SSM),
        out_shape=jax.ShapeDtypeStruct((b, s, D_SSM), F32),
        scratch_shapes=[pltpu.VMEM((D_STATE, D_SSM), F32)],
        compiler_params=_params("parallel", "arbitrary"),
        name="ssd_bwd",
    )(xa, dt, dt_bias, a_neg, e_bwd)


def _mixer_body(narrow_ref, q_ref, kc_ref, kp_ref, kn_ref, vc_ref, vp_ref, vn_ref, bias_ref,
                xa_ref, dt_ref, z_ref, yb_ref, dtb_ref, a_ref, dsk_ref, og_ref, e_ref,
                attn_ref, y_ref, kd, vd, tmp, o_scr, lse_scr, hst, *, seq_len):
    hp, t = pl.program_id(1), pl.program_id(2)

    @pl.when(jnp.logical_and(t == 0, hp == 0))
    def _():
        hst[...] = jnp.zeros_like(hst)

    def step(row_max):
        _attn_step(q_ref, kc_ref, kp_ref, kn_ref, vc_ref, vp_ref, vn_ref, bias_ref, attn_ref,
                   kd, vd, tmp, o_scr, lse_scr, t0=t * ATT_TILE, seq_len=seq_len, row_max=row_max)
        _ssd_fwd_chunks(MIXER_CHUNKS, xa_ref, dt_ref, z_ref, yb_ref, dtb_ref, a_ref, dsk_ref, og_ref,
                        e_ref, y_ref, hst)

    pl.when(narrow_ref[0] != 0)(functools.partial(step, False))
    pl.when(narrow_ref[0] == 0)(functools.partial(step, True))


def _mixer(q, k, v, att, xa, dt, z, yb, dt_bias, a_neg, d_skip, out_g, e_fwd):
    bias = att["bias"]
    n_pairs, b, s, _ = q.shape
    n_pat = len(ATT_PATTERNS)
    last_halo = s // ATT_REACH - 1
    halo_per_tile = ATT_TILE // ATT_REACH
    n_tiles = s // ATT_TILE
    cur = pl.BlockSpec((None, None, ATT_TILE, LANES), lambda bi, hp, t: (hp, bi, t, 0))
    prev = pl.BlockSpec((None, None, ATT_REACH, LANES),
                        lambda bi, hp, t: (hp, bi, jnp.maximum(t * halo_per_tile - 1, 0), 0))
    nxt = pl.BlockSpec((None, None, ATT_REACH, LANES),
                       lambda bi, hp, t: (hp, bi, jnp.minimum((t + 1) * halo_per_tile, last_halo), 0))
    bias_spec = pl.BlockSpec((bias.shape[0], n_pat, 2, ATT_BLOCK, ATT_KW),
                             lambda bi, hp, t: (0, 0, hp, 0, 0))
    rows = MIXER_CHUNKS * CHUNK
    scan = lambda w: pl.BlockSpec((None, rows, w), lambda bi, hp, t: (bi, hp * n_tiles + t, 0))
    consts = [_const_spec(a.shape) for a in (dt_bias, a_neg, d_skip, out_g, e_fwd)]
    return pl.pallas_call(
        functools.partial(_mixer_body, seq_len=s),
        grid=(b, n_pairs, n_tiles),
        in_specs=[pl.BlockSpec(memory_space=pltpu.SMEM), cur, cur, prev, nxt, cur, prev, nxt, bias_spec,
                  scan(CONV_DIM), scan(LANES), scan(D_SSM), scan(D_SSM)] + consts,
        out_specs=[cur, scan(D_SSM)],
        out_shape=[jax.ShapeDtypeStruct((n_pairs, b, s, LANES), BF16),
                   jax.ShapeDtypeStruct((b, s, D_SSM), BF16)],
        scratch_shapes=[pltpu.VMEM((n_pat, ATT_WINDOW, LANES), BF16),
                        pltpu.VMEM((n_pat, ATT_WINDOW, LANES), BF16),
                        pltpu.VMEM((ATT_STAGE, ATT_WINDOW // ATT_STAGE, LANES), F32),
                        pltpu.VMEM((n_pat - 1, ATT_TILE, LANES), F32),
                        pltpu.VMEM((n_pat - 1, ATT_TILE, LANES), F32),
                        pltpu.VMEM((D_STATE, D_SSM), F32)],
        compiler_params=_params("parallel", "arbitrary", "arbitrary"),
        name="mixer",
    )(att["narrow"], q, k, k, k, v, v, v, bias, xa, dt, z, yb, dt_bias, a_neg, d_skip, out_g, e_fwd)


def _outproj_ffn_body(x_ref, a_ref, y_ref, g_ref, wa_ref, ws_ref, g2_ref, wg_ref, wu_ref, wd_ref,
                      out_ref):
    attn = jnp.concatenate([a_ref[hp] for hp in range(a_ref.shape[0])], axis=1)
    attn = _rms(attn.astype(F32), g_ref[...]).astype(BF16)
    x2 = x_ref[...] + _dot(attn, wa_ref[...]) + _dot(y_ref[...], ws_ref[...])
    out_ref[...] = _half_step_ffn(x2, g2_ref, wg_ref, wu_ref, wd_ref)


def _outproj_ffn(x, attn, y, g, wa, ws, g2, wg, wu, wd, tm):
    n = x.shape[0]
    row = lambda w: pl.BlockSpec((tm, w), lambda i: (i, 0))
    sq = _const_spec((D_ATT, D_MODEL))
    pairs = pl.BlockSpec((attn.shape[0], tm, LANES), lambda i: (0, i, 0))
    return pl.pallas_call(
        _outproj_ffn_body,
        grid=(n // tm,),
        in_specs=[row(D_MODEL), pairs, row(D_SSM), _const_spec((1, D_ATT)), sq, sq,
                  _const_spec((1, D_MODEL)), _const_spec((D_MODEL, D_FF)), _const_spec((D_MODEL, D_FF)),
                  _const_spec((D_FF, D_MODEL))],
        out_specs=row(D_MODEL),
        out_shape=jax.ShapeDtypeStruct((n, D_MODEL), F32),
        compiler_params=_params("parallel"),
        name="outproj_ffn",
    )(x, attn, y, g, wa, ws, g2, wg, wu, wd)


def _t5_bucket(rel):
    nb = N_REL_BUCKETS // 2
    max_exact = nb // 2
    n = np.abs(rel)
    large = max_exact + (np.log(np.maximum(n, 1) / max_exact)
                         / math.log(REL_MAX_DIST / max_exact) * (nb - max_exact)).astype(np.int32)
    large = np.minimum(large, nb - 1)
    return (np.where(rel > 0, nb, 0) + np.where(n < max_exact, n, large)).astype(np.int32)


def _bias_tables(rel_bias, shift):
    rel_sub = np.arange(ATT_KW)[None, :] - ATT_HALF - np.arange(ATT_BLOCK)[:, None]
    in_window = np.abs(rel_sub) <= ATT_HALF
    col = np.arange(ATT_KW)[None, :]
    tables = []
    for _, dil in ATT_PATTERNS:
        onehot = (_t5_bucket(rel_sub * dil)[:, :, None] == np.arange(N_REL_BUCKETS)).astype(np.float32)
        bias = jnp.einsum("ijk,kh->hij", jnp.asarray(onehot, BF16).astype(F32), rel_bias.astype(F32),
                          precision=lax.Precision.HIGHEST)
        tables.append(bias * LOG2_E - shift)
    bias = jnp.stack(tables, axis=0)
    variants = []
    for v in range(4):
        keep = in_window & ((col >= ATT_HALF) | (v & 1 == 0)) & ((col < ATT_BLOCK + ATT_HALF) | (v & 2 == 0))
        variants.append(jnp.where(keep[None, None], bias, NEG_INF))
    return jnp.stack(variants, axis=0)


def _attention_tables(rel_bias, q_norm_g, k_norm_g):
    logit_bound = (QK_BOUND_SLACK * HEAD_DIM * HEAD_DIM ** -0.5 * LOG2_E
                   * jnp.max(jnp.abs(q_norm_g)) * jnp.max(jnp.abs(k_norm_g))).astype(F32)
    bias_lo = (jnp.min(rel_bias) * LOG2_E).astype(F32)
    bias_hi = (jnp.max(rel_bias) * LOG2_E).astype(F32)
    narrow = 2.0 * logit_bound + (bias_hi - bias_lo) < EXP2_SAFE_RANGE
    return {
        "bias": _bias_tables(rel_bias, jnp.where(narrow, logit_bound + bias_hi, 0.0)),
        "narrow": narrow.astype(jnp.int32).reshape(1),
    }


def _head_expander(first_row, width):
    e = np.zeros((2 * LANES, width), np.float32)
    for h in range(width // HEAD_DIM):
        e[first_row + h, h * HEAD_DIM:(h + 1) * HEAD_DIM] = 1.0
        e[LANES + first_row + h, h * HEAD_DIM:(h + 1) * HEAD_DIM] = 1.0
    return jnp.asarray(e, BF16)


def _block_diag_mean(width):
    i = np.arange(width)
    return jnp.asarray((i[:, None] // HEAD_DIM == i[None, :] // HEAD_DIM) / HEAD_DIM, BF16)


def _layer(x, p, tm):
    b, s, _ = x.shape
    n = b * s
    xf = x.reshape(n, D_MODEL)
    x1 = _ffn(xf, p["ffn1_g"], p["ffn1_wg"], p["ffn1_wu"], p["ffn1_wd"], tm)
    q, k, v, z, xa, dt = _inproj(x1, p["mix_g"], p["wq"], p["wk"], p["wv"], p["wz"], p["wx"],
                                 p["wdt"], p["qg"], p["kg"], p["bd"], p["conv_w"], p["conv_b"], s, tm)
    seq = lambda a: a.reshape(b, s, a.shape[-1])
    pair_seq = lambda a: a.reshape(a.shape[0], b, s, LANES)
    yb = _ssd_bwd(seq(xa), seq(dt), p["dt_bias"], p["a_neg"], p["e_bwd"])
    attn, y = _mixer(pair_seq(q), pair_seq(k), pair_seq(v), p["att"], seq(xa), seq(dt), seq(z), yb,
                     p["dt_bias"], p["a_neg"], p["d_skip"], p["ssm_g"], p["e_fwd"])
    out = _outproj_ffn(x1, attn.reshape(attn.shape[0], n, LANES), y.reshape(n, D_SSM), p["attn_g"],
                       p["wo_att"], p["wo_ssm"], p["ffn2_g"], p["ffn2_wg"], p["ffn2_wu"],
                       p["ffn2_wd"], tm)
    return out.reshape(b, s, D_MODEL)


def _prepare(rel_bias, ffn1_norm_g, ffn1_w_gate, ffn1_w_up, ffn1_w_down, mix_norm_g, w_in,
             q_norm_g, k_norm_g, attn_out_g, conv_w, conv_b, dt_bias, a_log, d_skip, ssm_out_g,
             w_out, ffn2_norm_g, ffn2_w_gate, ffn2_w_up, ffn2_w_down):
    row = lambda a: a.reshape(1, -1).astype(F32)
    w16 = lambda a: a.astype(BF16)
    c0, c1, c2, c3 = D_ATT, 2 * D_ATT, 3 * D_ATT, 3 * D_ATT + D_SSM
    c4 = c3 + CONV_DIM
    pad32 = lambda a: jnp.pad(a.reshape(1, 2 * N_HEADS).astype(F32), ((0, 0), (0, LANES - 2 * N_HEADS)))
    return {
        "ffn1_g": row(ffn1_norm_g), "ffn1_wg": w16(ffn1_w_gate), "ffn1_wu": w16(ffn1_w_up),
        "ffn1_wd": w16(ffn1_w_down),
        "ffn2_g": row(ffn2_norm_g), "ffn2_wg": w16(ffn2_w_gate), "ffn2_wu": w16(ffn2_w_up),
        "ffn2_wd": w16(ffn2_w_down),
        "mix_g": row(mix_norm_g),
        "wq": w16(w_in[:, :c0]), "wk": w16(w_in[:, c0:c1]), "wv": w16(w_in[:, c1:c2]),
        "wz": w16(w_in[:, c2:c3]), "wx": w16(w_in[:, c3:c4]),
        "wdt": w16(jnp.pad(w_in[:, c4:], ((0, 0), (0, LANES - 2 * N_HEADS)))),
        "qg": row(jnp.tile(q_norm_g, N_HEADS)) * (HEAD_DIM ** -0.5 * LOG2_E),
        "kg": row(jnp.tile(k_norm_g, N_HEADS)),
        "bd": _block_diag_mean(MXU_TILE),
        "att": _attention_tables(rel_bias, q_norm_g, k_norm_g),
        "attn_g": row(attn_out_g),
        "conv_w": conv_w.astype(F32), "conv_b": row(conv_b),
        "dt_bias": pad32(dt_bias), "a_neg": pad32(-jnp.exp(a_log.astype(F32))),
        "d_skip": row(jnp.repeat(d_skip, HEAD_DIM)), "ssm_g": row(ssm_out_g),
        "e_fwd": _head_expander(0, D_SSM), "e_bwd": _head_expander(N_HEADS, D_SSM),
        "wo_att": w16(w_out[:D_ATT]), "wo_ssm": w16(w_out[D_ATT:]),
    }


def _trunk(x, layers, tm=512):
    for p in layers:
        x = _layer(x, p, tm)
    return x


def kernel(x_prompt, x_sample, rel_bias, ffn1_norm_g, ffn1_w_gate, ffn1_w_up, ffn1_w_down, mix_norm_g, w_in, q_norm_g, k_norm_g, attn_out_g, conv_w, conv_b, dt_bias, a_log, d_skip, ssm_out_g, w_out, ffn2_norm_g, ffn2_w_gate, ffn2_w_up, ffn2_w_down):
    per_layer = (ffn1_norm_g, ffn1_w_gate, ffn1_w_up, ffn1_w_down, mix_norm_g, w_in, q_norm_g,
                 k_norm_g, attn_out_g, conv_w, conv_b, dt_bias, a_log, d_skip, ssm_out_g, w_out,
                 ffn2_norm_g, ffn2_w_gate, ffn2_w_up, ffn2_w_down)
    layers = [_prepare(rel_bias, *(a[l] for a in per_layer)) for l in range(ffn1_norm_g.shape[0])]
    return (_trunk(x_prompt, layers), _trunk(x_sample, layers))
```

```python
import functools
import math

import numpy as np
import jax
import jax.numpy as jnp
from jax import lax
from jax.experimental import pallas as pl
from jax.experimental.pallas import tpu as pltpu

D_MODEL = 1024
D_ATT = 1024
D_SSM = 1024
HEAD_DIM = 64
N_HEADS = 16
ATT_PATTERNS = ((128, 1), (512, 4), (2048, 16))
ATT_BLOCK = 128
ATT_HALF = 64
ATT_KW = ATT_BLOCK + 2 * ATT_HALF
ATT_MAX_DIL = max(d for _, d in ATT_PATTERNS)
ATT_TILE = ATT_BLOCK * ATT_MAX_DIL
ATT_REACH = ATT_HALF * ATT_MAX_DIL
ATT_WINDOW = ATT_TILE + 2 * ATT_REACH
ATT_STAGE = 4
assert tuple(d for _, d in ATT_PATTERNS) == (1, ATT_STAGE, ATT_STAGE * ATT_STAGE)
N_REL_BUCKETS = 32
REL_MAX_DIST = 1024
N_GROUPS = 4
HEADS_PER_GROUP = 4
D_STATE = 128
D_CONV = 5
CHUNK = 128
SSD_STEP_CHUNKS = 8
MIXER_CHUNKS = ATT_TILE // CHUNK // (N_HEADS // 2)
CONV_DIM = D_SSM + 2 * N_GROUPS * D_STATE
D_FF = 2816
EPS = 1e-6
NEG_INF = -1e30
LOG2_E = 1.4426950408889634
EXP2_SAFE_RANGE = 100.0
QK_BOUND_SLACK = 1.05

LANES = 128
MXU_TILE = 256
HALO_ROWS = 8
VMEM_LIMIT = 56 * 1024 * 1024

F32 = jnp.float32
BF16 = jnp.bfloat16


def _params(*sem):
    return pltpu.CompilerParams(dimension_semantics=sem, vmem_limit_bytes=VMEM_LIMIT)


def _const_spec(shape):
    n = len(shape)
    return pl.BlockSpec(shape, lambda *_: (0,) * n, pipeline_mode=pl.Buffered(1))


def _rms(x, g):
    ms = jnp.mean(x * x, axis=-1, keepdims=True)
    return x * lax.rsqrt(ms + EPS) * g


def _silu(x):
    h = 0.5 * x
    return h + h * jnp.tanh(h)


def _dot(a, b):
    return jnp.dot(a, b, preferred_element_type=F32)


def _hi_lo(vals):
    hi = vals.astype(BF16)
    lo = (vals - hi.astype(F32)).astype(BF16)
    return jnp.concatenate([hi, lo], axis=1)


def _expand(split, e_ref, cols=slice(None)):
    return _dot(split, e_ref[:, cols])


def _half_step_ffn(x, g_ref, wg_ref, wu_ref, wd_ref):
    h = _rms(x, g_ref[...]).astype(BF16)
    gate = _dot(h, wg_ref[...])
    up = _dot(h, wu_ref[...])
    act = (_silu(gate) * up).astype(BF16)
    return x + 0.5 * _dot(act, wd_ref[...])


def _ffn_body(x_ref, g_ref, wg_ref, wu_ref, wd_ref, o_ref):
    o_ref[...] = _half_step_ffn(x_ref[...], g_ref, wg_ref, wu_ref, wd_ref)


def _ffn(x, g, wg, wu, wd, tm):
    n = x.shape[0]
    row = pl.BlockSpec((tm, D_MODEL), lambda i: (i, 0))
    return pl.pallas_call(
        _ffn_body,
        grid=(n // tm,),
        in_specs=[row, _const_spec((1, D_MODEL)), _const_spec((D_MODEL, D_FF)),
                  _const_spec((D_MODEL, D_FF)), _const_spec((D_FF, D_MODEL))],
        out_specs=row,
        out_shape=jax.ShapeDtypeStruct((n, D_MODEL), F32),
        compiler_params=_params("parallel"),
        name="ffn",
    )(x, g, wg, wu, wd)


def _inproj_body(x_ref, xp_ref, xn_ref, g_ref, wq_ref, wk_ref, wv_ref, wz_ref, wx_ref, wdt_ref,
                 qg_ref, kg_ref, bd_ref, cw_ref, cb_ref, q_ref, k_ref, v_ref, z_ref, xa_ref, dt_ref, ext,
                 *, tiles_per_seq):
    h = _rms(x_ref[...], g_ref[...]).astype(BF16)

    def head_norm(t, gain):
        t2 = (t * t).astype(BF16)
        w = bd_ref.shape[0]
        ms = jnp.concatenate([_dot(t2[:, j * w:(j + 1) * w], bd_ref[...])
                              for j in range(D_ATT // w)], axis=1)
        return t * lax.rsqrt(ms + EPS) * gain

    def put_pairs(ref, val):
        for hp in range(N_HEADS // 2):
            ref[hp] = val[:, hp * LANES:(hp + 1) * LANES]

    put_pairs(q_ref, head_norm(_dot(h, wq_ref[...]), qg_ref[...]))
    put_pairs(k_ref, head_norm(_dot(h, wk_ref[...]), kg_ref[...]))
    put_pairs(v_ref, _dot(h, wv_ref[...]))
    z_ref[...] = _dot(h, wz_ref[...])
    dt_ref[...] = _dot(h, wdt_ref[...])

    tile = pl.program_id(0) % tiles_per_seq
    halo = jnp.concatenate([xp_ref[...], xn_ref[...]], axis=0)
    xbc_halo = _dot(_rms(halo, g_ref[...]).astype(BF16), wx_ref[...])
    xbc = _dot(h, wx_ref[...])
    rows = xbc.shape[0]
    pad = D_CONV // 2
    for sl in range(CONV_DIM // LANES):
        lanes = slice(sl * LANES, (sl + 1) * LANES)
        ext[sl, 0:HALO_ROWS, :] = jnp.where(tile > 0, xbc_halo[0:HALO_ROWS, lanes], 0.0)
        ext[sl, HALO_ROWS:HALO_ROWS + rows, :] = xbc[:, lanes]
        ext[sl, HALO_ROWS + rows:, :] = jnp.where(tile < tiles_per_seq - 1, xbc_halo[HALO_ROWS:, lanes], 0.0)
        acc = cb_ref[:, lanes]
        for j in range(D_CONV):
            acc = acc + ext[sl, pl.ds(HALO_ROWS - pad + j, rows, stride=1), :] * cw_ref[j:j + 1, lanes]
        xa_ref[:, lanes] = _silu(acc)


def _inproj(x, g, wq, wk, wv, wz, wx, wdt, qg, kg, bd, conv_w, conv_b, s, tm):
    n = x.shape[0]
    halos_per_tile = tm // HALO_ROWS
    halo_prev = pl.BlockSpec((HALO_ROWS, D_MODEL), lambda i: (jnp.maximum(i * halos_per_tile - 1, 0), 0))
    halo_next = pl.BlockSpec((HALO_ROWS, D_MODEL),
                             lambda i: (jnp.minimum((i + 1) * halos_per_tile, n // HALO_ROWS - 1), 0))
    row = lambda w: pl.BlockSpec((tm, w), lambda i: (i, 0))
    sq = _const_spec((D_MODEL, D_ATT))
    out = lambda w: jax.ShapeDtypeStruct((n, w), F32)
    pairs = pl.BlockSpec((N_HEADS // 2, tm, LANES), lambda i: (0, i, 0))
    pairs_out = jax.ShapeDtypeStruct((N_HEADS // 2, n, LANES), F32)
    return pl.pallas_call(
        functools.partial(_inproj_body, tiles_per_seq=s // tm),
        grid=(n // tm,),
        in_specs=[row(D_MODEL), halo_prev, halo_next, _const_spec((1, D_MODEL)), sq, sq, sq, sq,
                  _const_spec((D_MODEL, CONV_DIM)), _const_spec((D_MODEL, LANES)),
                  _const_spec((1, D_ATT)), _const_spec((1, D_ATT)), _const_spec(bd.shape),
                  _const_spec(conv_w.shape), _const_spec(conv_b.shape)],
        out_specs=[pairs, pairs, pairs, row(D_SSM), row(CONV_DIM), row(LANES)],
        out_shape=[pairs_out, pairs_out, pairs_out, out(D_SSM), out(CONV_DIM), out(LANES)],
        scratch_shapes=[pltpu.VMEM((CONV_DIM // LANES, tm + 2 * HALO_ROWS, LANES), F32)],
        compiler_params=_params("parallel"),
        name="inproj",
    )(x, x, x, g, wq, wk, wv, wz, wx, wdt, qg, kg, bd, conv_w, conv_b)


def _rows(start, size, dil):
    return pl.ds(start, size) if dil == 1 else pl.ds(start, size, stride=dil)


def _regroup_keys(prev, cur, nxt, tmp, dst):
    s4 = ATT_STAGE
    lo, hi = ATT_REACH - ATT_HALF, ATT_REACH + ATT_TILE + ATT_HALF
    dst[0, lo:ATT_REACH, :] = prev[lo:ATT_REACH, :].astype(BF16)
    dst[0, ATT_REACH:ATT_REACH + ATT_TILE, :] = cur[...].astype(BF16)
    dst[0, ATT_REACH + ATT_TILE:hi, :] = nxt[0:ATT_HALF, :].astype(BF16)
    h4, t4 = ATT_REACH // s4, ATT_TILE // s4
    for r in range(s4):
        tmp[r, 0:h4, :] = prev[pl.ds(r, h4, stride=s4), :]
        tmp[r, h4:h4 + t4, :] = cur[pl.ds(r, t4, stride=s4), :]
        tmp[r, h4 + t4:, :] = nxt[pl.ds(r, h4, stride=s4), :]
    lo, hi = h4 - ATT_HALF, h4 + t4 + ATT_HALF
    for r in range(s4):
        dst[1, r * (ATT_WINDOW // s4) + lo:r * (ATT_WINDOW // s4) + hi, :] = tmp[r, lo:hi, :].astype(BF16)
    per = ATT_WINDOW // (s4 * s4)
    for r in range(s4 * s4):
        dst[2, r * per:(r + 1) * per, :] = tmp[r % s4, pl.ds(r // s4, per, stride=s4), :].astype(BF16)


def _attn_step(q_ref, kc_ref, kp_ref, kn_ref, vc_ref, vp_ref, vn_ref, bias_ref, out_ref,
               kd, vd, tmp, o_scr, lse_scr, *, t0, seq_len, row_max):
    _regroup_keys(kp_ref, kc_ref, kn_ref, tmp, kd)
    _regroup_keys(vp_ref, vc_ref, vn_ref, tmp, vd)

    first_head = lax.broadcasted_iota(jnp.int32, (ATT_BLOCK, LANES), 1) < HEAD_DIM
    first_head_kw = lax.broadcasted_iota(jnp.int32, (ATT_KW, LANES), 1) < HEAD_DIM
    head_ones = (first_head_kw.astype(BF16), (~first_head_kw).astype(BF16))

    def block(p, dil, r, j):
        blocks_per_residue = ATT_TILE // (dil * ATT_BLOCK)
        q0 = j * (ATT_BLOCK * dil) + r
        k0 = r * (ATT_WINDOW // dil) + ATT_REACH // dil - ATT_HALF + j * ATT_BLOCK
        variant = 0
        if j == 0:
            variant = variant + (t0 == 0).astype(jnp.int32)
        if j == blocks_per_residue - 1:
            variant = variant + 2 * (t0 + ATT_TILE == seq_len).astype(jnp.int32)
        qp = q_ref[_rows(q0, ATT_BLOCK, dil), :].astype(BF16)
        kp = kd[p, k0:k0 + ATT_KW, :]
        vp = vd[p, k0:k0 + ATT_KW, :]
        zq, zv = jnp.zeros_like(qp), jnp.zeros_like(vp)
        q2 = jnp.concatenate([jnp.where(first_head, qp, zq), jnp.where(first_head, zq, qp)], axis=0)
        s2 = lax.dot_general(q2, kp, (((1,), (1,)), ((), ())), preferred_element_type=F32)
        es, ms = [], []
        for sub in range(2):
            s = s2[sub * ATT_BLOCK:(sub + 1) * ATT_BLOCK, :] + bias_ref[variant, p, sub]
            if row_max:
                m = jnp.max(s, axis=-1, keepdims=True)
                s = s - m
                ms.append(m)
            es.append(jnp.exp2(s).astype(BF16))
        v2 = jnp.concatenate([
            jnp.concatenate([jnp.where(first_head_kw, vp, zv), head_ones[0]], axis=1),
            jnp.concatenate([jnp.where(first_head_kw, zv, vp), head_ones[1]], axis=1)], axis=0)
        ol = _dot(jnp.concatenate(es, axis=1), v2)
        l = ol[:, LANES:]
        if not row_max:
            return ol[:, :LANES], l
        return ol[:, :LANES] * (1.0 / l), jnp.where(first_head, ms[0], ms[1]) + jnp.log2(l)

    for p in range(len(ATT_PATTERNS) - 1, 0, -1):
        dil = ATT_PATTERNS[p][1]
        blocks_per_residue = ATT_TILE // (dil * ATT_BLOCK)
        for idx in range(ATT_TILE // ATT_BLOCK):
            r, j = idx // blocks_per_residue, idx % blocks_per_residue
            rows = _rows(j * (ATT_BLOCK * dil) + r, ATT_BLOCK, dil)
            o_scr[p - 1, rows, :], lse_scr[p - 1, rows, :] = block(p, dil, r, j)
    for j in range(ATT_TILE // ATT_BLOCK):
        rows = pl.ds(j * ATT_BLOCK, ATT_BLOCK)
        o1, l1 = block(0, 1, 0, j)
        l2, l3 = lse_scr[0, rows, :], lse_scr[1, rows, :]
        if not row_max:
            mixed = o1 + o_scr[0, rows, :] + o_scr[1, rows, :]
            out_ref[rows, :] = (mixed * (1.0 / (l1 + l2 + l3))).astype(BF16)
            continue
        m = jnp.maximum(jnp.maximum(l1, l2), l3)
        e1, e2, e3 = jnp.exp2(l1 - m), jnp.exp2(l2 - m), jnp.exp2(l3 - m)
        mixed = e1 * o1 + e2 * o_scr[0, rows, :] + e3 * o_scr[1, rows, :]
        out_ref[rows, :] = (mixed * (1.0 / (e1 + e2 + e3))).astype(BF16)


def _softplus(x):
    return jnp.maximum(x, 0.0) + jnp.log(1.0 + jnp.exp(-jnp.abs(x)))


def _tri(lower):
    r = lax.broadcasted_iota(jnp.int32, (CHUNK, CHUNK), 0)
    c = lax.broadcasted_iota(jnp.int32, (CHUNK, CHUNK), 1)
    return (r >= c) if lower else (r <= c)


def _cumsum_rows(a, lower):
    a1 = a.astype(BF16)
    r1 = a - a1.astype(F32)
    a2 = r1.astype(BF16)
    a3 = (r1 - a2.astype(F32)).astype(BF16)
    tri = _tri(lower).astype(BF16)
    return _dot(jnp.concatenate([tri, tri, tri], axis=1), jnp.concatenate([a1, a2, a3], axis=0))


def _chunk_rows(i):
    return pl.ds(pl.multiple_of(i * CHUNK, CHUNK), CHUNK)


def _ssd_bwd_body(xa_ref, dt_ref, dtb_ref, a_ref, e_ref, yoff_ref, hst):
    @pl.when(pl.program_id(1) == 0)
    def _():
        hst[...] = jnp.zeros_like(hst)

    def chunk(ci, carry):
        rs = _chunk_rows(SSD_STEP_CHUNKS - 1 - ci)
        dt = _softplus(dt_ref[rs, :] + dtb_ref[...])
        rcum = _cumsum_rows(dt * a_ref[...], lower=False)
        decay_in = _expand(_hi_lo(jnp.exp(rcum)), e_ref)
        w_in = _expand(_hi_lo(jnp.exp(rcum[0:1, :] - rcum) * dt), e_ref)
        chunk_decay = _expand(_hi_lo(jnp.broadcast_to(jnp.exp(rcum[0:1, :]), (HALO_ROWS, LANES))),
                              e_ref)[0:1, :]
        for g in range(N_GROUPS):
            gl = slice(g * HEADS_PER_GROUP * HEAD_DIM, (g + 1) * HEADS_PER_GROUP * HEAD_DIM)
            bg = xa_ref[rs, D_SSM + g * D_STATE:D_SSM + (g + 1) * D_STATE].astype(BF16)
            cg = xa_ref[rs, D_SSM + (N_GROUPS + g) * D_STATE:D_SSM + (N_GROUPS + g + 1) * D_STATE].astype(BF16)
            xw = (xa_ref[rs, gl] * w_in[:, gl]).astype(BF16)
            h_in = hst[:, gl]
            yoff_ref[rs, gl] = _dot(cg, h_in.astype(BF16)) * decay_in[:, gl]
            upd = lax.dot_general(bg, xw, (((0,), (0,)), ((), ())), preferred_element_type=F32)
            hst[:, gl] = h_in * chunk_decay[:, gl] + upd
        return carry

    lax.fori_loop(0, SSD_STEP_CHUNKS, chunk, 0, unroll=True)


def _ssd_fwd_chunks(n_chunks, xa_ref, dt_ref, z_ref, yb_ref, dtb_ref, a_ref, dsk_ref, og_ref, e_ref,
                    y_ref, hst):
    lower, upper = _tri(True), _tri(False)
    lane = lax.broadcasted_iota(jnp.int32, (CHUNK, LANES), 1)

    def chunk(ci, carry):
        rs = _chunk_rows(ci)
        dt = _softplus(dt_ref[rs, :] + dtb_ref[...])
        a = dt * a_ref[...]
        fcum = _cumsum_rows(a, lower=True)
        rcum = fcum[CHUNK - 1:CHUNK, :] - fcum + a
        decay_in = _expand(_hi_lo(jnp.exp(fcum)), e_ref)
        w_in = _expand(_hi_lo(jnp.exp(fcum[CHUNK - 1:CHUNK, :] - fcum) * dt), e_ref)
        chunk_decay = _expand(_hi_lo(jnp.broadcast_to(jnp.exp(fcum[CHUNK - 1:CHUNK, :]),
                                                       (HALO_ROWS, LANES))), e_ref)[0:1, :]
        log2_dt = jnp.log(dt) * LOG2_E
        fcol, rcol = fcum * LOG2_E, rcum * LOG2_E
        frow, rrow = (fcol - log2_dt).T, (rcol - log2_dt).T

        for g in range(N_GROUPS):
            gl = slice(g * HEADS_PER_GROUP * HEAD_DIM, (g + 1) * HEADS_PER_GROUP * HEAD_DIM)
            bg = xa_ref[rs, D_SSM + g * D_STATE:D_SSM + (g + 1) * D_STATE].astype(BF16)
            cg = xa_ref[rs, D_SSM + (N_GROUPS + g) * D_STATE:D_SSM + (N_GROUPS + g + 1) * D_STATE].astype(BF16)
            cb = lax.dot_general(cg, bg, (((1,), (1,)), ((), ())), preferred_element_type=F32)
            xw = (xa_ref[rs, gl] * w_in[:, gl]).astype(BF16)
            h_in = hst[:, gl]
            y_off = _dot(cg, h_in.astype(BF16)) * decay_in[:, gl]
            upd = lax.dot_general(bg, xw, (((0,), (0,)), ((), ())), preferred_element_type=F32)
            hst[:, gl] = h_in * chunk_decay[:, gl] + upd
            gated = []
            for pair in range(HEADS_PER_GROUP // 2):
                pl_ = slice(gl.start + pair * LANES, gl.start + (pair + 1) * LANES)
                xs = xa_ref[rs, pl_]
                x_pair = xs.astype(BF16)
                zx = jnp.zeros_like(x_pair)
                weights = []
                for sub in range(2):
                    h = g * HEADS_PER_GROUP + 2 * pair + sub
                    hb = N_HEADS + h
                    seg_f = jnp.where(lower, fcol[:, h:h + 1] - frow[h:h + 1, :], NEG_INF)
                    seg_b = jnp.where(upper, rcol[:, hb:hb + 1] - rrow[hb:hb + 1, :], NEG_INF)
                    mix = jnp.exp2(seg_f) + jnp.exp2(seg_b)
                    weights.append((cb * mix).astype(BF16))
                x2 = jnp.concatenate([jnp.where(lane < HEAD_DIM, x_pair, zx),
                                      jnp.where(lane < HEAD_DIM, zx, x_pair)], axis=0)
                y_diag = _dot(jnp.concatenate(weights, axis=1), x2)
                po = slice(pair * LANES, (pair + 1) * LANES)
                y = y_diag + y_off[:, po] + yb_ref[rs, pl_] + dsk_ref[:, pl_] * xs
                gated.append(y * _silu(z_ref[rs, pl_]))
            y_ref[rs, gl] = _rms(jnp.concatenate(gated, axis=1), og_ref[:, gl]).astype(BF16)
        return carry

    lax.fori_loop(0, n_chunks, chunk, 0, unroll=True)


def _ssd_bwd(xa, dt, dt_bias, a_neg, e_bwd):
    b, s, _ = xa.shape
    rows = SSD_STEP_CHUNKS * CHUNK
    nc = s // rows
    step = lambda w: pl.BlockSpec((None, rows, w), lambda bi, i: (bi, nc - 1 - i, 0))
    return pl.pallas_call(
        _ssd_bwd_body,
        grid=(b, nc),
        in_specs=[step(CONV_DIM), step(LANES), _const_spec(dt_bias.shape), _const_spec(a_neg.shape),
                  _const_spec(e_bwd.shape)],
        out_specs=step(D_SSM),
        out_shape=jax.ShapeDtypeStruct((b, s, D_SSM), F32),
        scratch_shapes=[pltpu.VMEM((D_STATE, D_SSM), F32)],
        compiler_params=_params("parallel", "arbitrary"),
        name="ssd_bwd",
    )(xa, dt, dt_bias, a_neg, e_bwd)


def _mixer_body(narrow_ref, q_ref, kc_ref, kp_ref, kn_ref, vc_ref, vp_ref, vn_ref, bias_ref,
                xa_ref, dt_ref, z_ref, yb_ref, dtb_ref, a_ref, dsk_ref, og_ref, e_ref,
                attn_ref, y_ref, kd, vd, tmp, o_scr, lse_scr, hst, *, seq_len):
    hp, t = pl.program_id(1), pl.program_id(2)

    @pl.when(jnp.logical_and(t == 0, hp == 0))
    def _():
        hst[...] = jnp.zeros_like(hst)

    def step(row_max):
        _attn_step(q_ref, kc_ref, kp_ref, kn_ref, vc_ref, vp_ref, vn_ref, bias_ref, attn_ref,
                   kd, vd, tmp, o_scr, lse_scr, t0=t * ATT_TILE, seq_len=seq_len, row_max=row_max)
        _ssd_fwd_chunks(MIXER_CHUNKS, xa_ref, dt_ref, z_ref, yb_ref, dtb_ref, a_ref, dsk_ref, og_ref,
                        e_ref, y_ref, hst)

    pl.when(narrow_ref[0] != 0)(functools.partial(step, False))
    pl.when(narrow_ref[0] == 0)(functools.partial(step, True))


def _mixer(q, k, v, att, xa, dt, z, yb, dt_bias, a_neg, d_skip, out_g, e_fwd):
    bias = att["bias"]
    n_pairs, b, s, _ = q.shape
    n_pat = len(ATT_PATTERNS)
    last_halo = s // ATT_REACH - 1
    halo_per_tile = ATT_TILE // ATT_REACH
    n_tiles = s // ATT_TILE
    cur = pl.BlockSpec((None, None, ATT_TILE, LANES), lambda bi, hp, t: (hp, bi, t, 0))
    prev = pl.BlockSpec((None, None, ATT_REACH, LANES),
                        lambda bi, hp, t: (hp, bi, jnp.maximum(t * halo_per_tile - 1, 0), 0))
    nxt = pl.BlockSpec((None, None, ATT_REACH, LANES),
                       lambda bi, hp, t: (hp, bi, jnp.minimum((t + 1) * halo_per_tile, last_halo), 0))
    bias_spec = pl.BlockSpec((bias.shape[0], n_pat, 2, ATT_BLOCK, ATT_KW),
                             lambda bi, hp, t: (0, 0, hp, 0, 0))
    rows = MIXER_CHUNKS * CHUNK
    scan = lambda w: pl.BlockSpec((None, rows, w), lambda bi, hp, t: (bi, hp * n_tiles + t, 0))
    consts = [_const_spec(a.shape) for a in (dt_bias, a_neg, d_skip, out_g, e_fwd)]
    return pl.pallas_call(
        functools.partial(_mixer_body, seq_len=s),
        grid=(b, n_pairs, n_tiles),
        in_specs=[pl.BlockSpec(memory_space=pltpu.SMEM), cur, cur, prev, nxt, cur, prev, nxt, bias_spec,
                  scan(CONV_DIM), scan(LANES), scan(D_SSM), scan(D_SSM)] + consts,
        out_specs=[cur, scan(D_SSM)],
        out_shape=[jax.ShapeDtypeStruct((n_pairs, b, s, LANES), BF16),
                   jax.ShapeDtypeStruct((b, s, D_SSM), BF16)],
        scratch_shapes=[pltpu.VMEM((n_pat, ATT_WINDOW, LANES), BF16),
                        pltpu.VMEM((n_pat, ATT_WINDOW, LANES), BF16),
                        pltpu.VMEM((ATT_STAGE, ATT_WINDOW // ATT_STAGE, LANES), F32),
                        pltpu.VMEM((n_pat - 1, ATT_TILE, LANES), F32),
                        pltpu.VMEM((n_pat - 1, ATT_TILE, LANES), F32),
                        pltpu.VMEM((D_STATE, D_SSM), F32)],
        compiler_params=_params("parallel", "arbitrary", "arbitrary"),
        name="mixer",
    )(att["narrow"], q, k, k, k, v, v, v, bias, xa, dt, z, yb, dt_bias, a_neg, d_skip, out_g, e_fwd)


def _outproj_ffn_body(x_ref, a_ref, y_ref, g_ref, wa_ref, ws_ref, g2_ref, wg_ref, wu_ref, wd_ref,
                      out_ref):
    attn = jnp.concatenate([a_ref[hp] for hp in range(a_ref.shape[0])], axis=1)
    attn = _rms(attn.astype(F32), g_ref[...]).astype(BF16)
    x2 = x_ref[...] + _dot(attn, wa_ref[...]) + _dot(y_ref[...], ws_ref[...])
    out_ref[...] = _half_step_ffn(x2, g2_ref, wg_ref, wu_ref, wd_ref)


def _outproj_ffn(x, attn, y, g, wa, ws, g2, wg, wu, wd, tm):
    n = x.shape[0]
    row = lambda w: pl.BlockSpec((tm, w), lambda i: (i, 0))
    sq = _const_spec((D_ATT, D_MODEL))
    pairs = pl.BlockSpec((attn.shape[0], tm, LANES), lambda i: (0, i, 0))
    return pl.pallas_call(
        _outproj_ffn_body,
        grid=(n // tm,),
        in_specs=[row(D_MODEL), pairs, row(D_SSM), _const_spec((1, D_ATT)), sq, sq,
                  _const_spec((1, D_MODEL)), _const_spec((D_MODEL, D_FF)), _const_spec((D_MODEL, D_FF)),
                  _const_spec((D_FF, D_MODEL))],
        out_specs=row(D_MODEL),
        out_shape=jax.ShapeDtypeStruct((n, D_MODEL), F32),
        compiler_params=_params("parallel"),
        name="outproj_ffn",
    )(x, attn, y, g, wa, ws, g2, wg, wu, wd)


def _t5_bucket(rel):
    nb = N_REL_BUCKETS // 2
    max_exact = nb // 2
    n = np.abs(rel)
    large = max_exact + (np.log(np.maximum(n, 1) / max_exact)
                         / math.log(REL_MAX_DIST / max_exact) * (nb - max_exact)).astype(np.int32)
    large = np.minimum(large, nb - 1)
    return (np.where(rel > 0, nb, 0) + np.where(n < max_exact, n, large)).astype(np.int32)


def _bias_tables(rel_bias, shift):
    rel_sub = np.arange(ATT_KW)[None, :] - ATT_HALF - np.arange(ATT_BLOCK)[:, None]
    in_window = np.abs(rel_sub) <= ATT_HALF
    col = np.arange(ATT_KW)[None, :]
    tables = []
    for _, dil in ATT_PATTERNS:
        onehot = (_t5_bucket(rel_sub * dil)[:, :, None] == np.arange(N_REL_BUCKETS)).astype(np.float32)
        bias = jnp.einsum("ijk,kh->hij", jnp.asarray(onehot, BF16).astype(F32), rel_bias.astype(F32),
                          precision=lax.Precision.HIGHEST)
        tables.append(bias * LOG2_E - shift)
    bias = jnp.stack(tables, axis=0)
    variants = []
    for v in range(4):
        keep = in_window & ((col >= ATT_HALF) | (v & 1 == 0)) & ((col < ATT_BLOCK + ATT_HALF) | (v & 2 == 0))
        variants.append(jnp.where(keep[None, None], bias, NEG_INF))
    return jnp.stack(variants, axis=0)


def _attention_tables(rel_bias, q_norm_g, k_norm_g):
    logit_bound = (QK_BOUND_SLACK * HEAD_DIM * HEAD_DIM ** -0.5 * LOG2_E
                   * jnp.max(jnp.abs(q_norm_g)) * jnp.max(jnp.abs(k_norm_g))).astype(F32)
    bias_lo = (jnp.min(rel_bias) * LOG2_E).astype(F32)
    bias_hi = (jnp.max(rel_bias) * LOG2_E).astype(F32)
    narrow = 2.0 * logit_bound + (bias_hi - bias_lo) < EXP2_SAFE_RANGE
    return {
        "bias": _bias_tables(rel_bias, jnp.where(narrow, logit_bound + bias_hi, 0.0)),
        "narrow": narrow.astype(jnp.int32).reshape(1),
    }


def _head_expander(first_row, width):
    e = np.zeros((2 * LANES, width), np.float32)
    for h in range(width // HEAD_DIM):
        e[first_row + h, h * HEAD_DIM:(h + 1) * HEAD_DIM] = 1.0
        e[LANES + first_row + h, h * HEAD_DIM:(h + 1) * HEAD_DIM] = 1.0
    return jnp.asarray(e, BF16)


def _block_diag_mean(width):
    i = np.arange(width)
    return jnp.asarray((i[:, None] // HEAD_DIM == i[None, :] // HEAD_DIM) / HEAD_DIM, BF16)


def _layer(x, p, tm):
    b, s, _ = x.shape
    n = b * s
    xf = x.reshape(n, D_MODEL)
    x1 = _ffn(xf, p["ffn1_g"], p["ffn1_wg"], p["ffn1_wu"], p["ffn1_wd"], tm)
    q, k, v, z, xa, dt = _inproj(x1, p["mix_g"], p["wq"], p["wk"], p["wv"], p["wz"], p["wx"],
                                 p["wdt"], p["qg"], p["kg"], p["bd"], p["conv_w"], p["conv_b"], s, tm)
    seq = lambda a: a.reshape(b, s, a.shape[-1])
    pair_seq = lambda a: a.reshape(a.shape[0], b, s, LANES)
    yb = _ssd_bwd(seq(xa), seq(dt), p["dt_bias"], p["a_neg"], p["e_bwd"])
    attn, y = _mixer(pair_seq(q), pair_seq(k), pair_seq(v), p["att"], seq(xa), seq(dt), seq(z), yb,
                     p["dt_bias"], p["a_neg"], p["d_skip"], p["ssm_g"], p["e_fwd"])
    out = _outproj_ffn(x1, attn.reshape(attn.shape[0], n, LANES), y.reshape(n, D_SSM), p["attn_g"],
                       p["wo_att"], p["wo_ssm"], p["ffn2_g"], p["ffn2_wg"], p["ffn2_wu"],
                       p["ffn2_wd"], tm)
    return out.reshape(b, s, D_MODEL)


def _prepare(rel_bias, ffn1_norm_g, ffn1_w_gate, ffn1_w_up, ffn1_w_down, mix_norm_g, w_in,
             q_norm_g, k_norm_g, attn_out_g, conv_w, conv_b, dt_bias, a_log, d_skip, ssm_out_g,
             w_out, ffn2_norm_g, ffn2_w_gate, ffn2_w_up, ffn2_w_down):
    row = lambda a: a.reshape(1, -1).astype(F32)
    w16 = lambda a: a.astype(BF16)
    c0, c1, c2, c3 = D_ATT, 2 * D_ATT, 3 * D_ATT, 3 * D_ATT + D_SSM
    c4 = c3 + CONV_DIM
    pad32 = lambda a: jnp.pad(a.reshape(1, 2 * N_HEADS).astype(F32), ((0, 0), (0, LANES - 2 * N_HEADS)))
    return {
        "ffn1_g": row(ffn1_norm_g), "ffn1_wg": w16(ffn1_w_gate), "ffn1_wu": w16(ffn1_w_up),
        "ffn1_wd": w16(ffn1_w_down),
        "ffn2_g": row(ffn2_norm_g), "ffn2_wg": w16(ffn2_w_gate), "ffn2_wu": w16(ffn2_w_up),
        "ffn2_wd": w16(ffn2_w_down),
        "mix_g": row(mix_norm_g),
        "wq": w16(w_in[:, :c0]), "wk": w16(w_in[:, c0:c1]), "wv": w16(w_in[:, c1:c2]),
        "wz": w16(w_in[:, c2:c3]), "wx": w16(w_in[:, c3:c4]),
        "wdt": w16(jnp.pad(w_in[:, c4:], ((0, 0), (0, LANES - 2 * N_HEADS)))),
        "qg": row(jnp.tile(q_norm_g, N_HEADS)) * (HEAD_DIM ** -0.5 * LOG2_E),
        "kg": row(jnp.tile(k_norm_g, N_HEADS)),
        "bd": _block_diag_mean(MXU_TILE),
        "att": _attention_tables(rel_bias, q_norm_g, k_norm_g),
        "attn_g": row(attn_out_g),
        "conv_w": conv_w.astype(F32), "conv_b": row(conv_b),
        "dt_bias": pad32(dt_bias), "a_neg": pad32(-jnp.exp(a_log.astype(F32))),
        "d_skip": row(jnp.repeat(d_skip, HEAD_DIM)), "ssm_g": row(ssm_out_g),
        "e_fwd": _head_expander(0, D_SSM), "e_bwd": _head_expander(N_HEADS, D_SSM),
        "wo_att": w16(w_out[:D_ATT]), "wo_ssm": w16(w_out[D_ATT:]),
    }


def _trunk(x, layers, tm=512):
    for p in layers:
        x = _layer(x, p, tm)
    return x


def kernel(x_prompt, x_sample, rel_bias, ffn1_norm_g, ffn1_w_gate, ffn1_w_up, ffn1_w_down, mix_norm_g, w_in, q_norm_g, k_norm_g, attn_out_g, conv_w, conv_b, dt_bias, a_log, d_skip, ssm_out_g, w_out, ffn2_norm_g, ffn2_w_gate, ffn2_w_up, ffn2_w_down):
    per_layer = (ffn1_norm_g, ffn1_w_gate, ffn1_w_up, ffn1_w_down, mix_norm_g, w_in, q_norm_g,
                 k_norm_g, attn_out_g, conv_w, conv_b, dt_bias, a_log, d_skip, ssm_out_g, w_out,
                 ffn2_norm_g, ffn2_w_gate, ffn2_w_up, ffn2_w_down)
    layers = [_prepare(rel_bias, *(a[l] for a in per_layer)) for l in range(ffn1_norm_g.shape[0])]
    return (_trunk(x_prompt, layers), _trunk(x_sample, layers))
```

```python
import functools
import math

import numpy as np
import jax
import jax.numpy as jnp
from jax import lax
from jax.experimental import pallas as pl
from jax.experimental.pallas import tpu as pltpu

D_MODEL = 1024
D_ATT = 1024
D_SSM = 1024
HEAD_DIM = 64
N_HEADS = 16
ATT_PATTERNS = ((128, 1), (512, 4), (2048, 16))
ATT_BLOCK = 128
ATT_HALF = 64
ATT_KW = ATT_BLOCK + 2 * ATT_HALF
ATT_MAX_DIL = max(d for _, d in ATT_PATTERNS)
ATT_TILE = ATT_BLOCK * ATT_MAX_DIL
ATT_REACH = ATT_HALF * ATT_MAX_DIL
ATT_WINDOW = ATT_TILE + 2 * ATT_REACH
ATT_STAGE = 4
assert tuple(d for _, d in ATT_PATTERNS) == (1, ATT_STAGE, ATT_STAGE * ATT_STAGE)
N_REL_BUCKETS = 32
REL_MAX_DIST = 1024
N_GROUPS = 4
HEADS_PER_GROUP = 4
D_STATE = 128
D_CONV = 5
CHUNK = 128
SSD_STEP_CHUNKS = 8
MIXER_CHUNKS = ATT_TILE // CHUNK // (N_HEADS // 2)
CONV_DIM = D_SSM + 2 * N_GROUPS * D_STATE
D_FF = 2816
EPS = 1e-6
NEG_INF = -1e30
LOG2_E = 1.4426950408889634
EXP2_SAFE_RANGE = 100.0
QK_BOUND_SLACK = 1.05

LANES = 128
MXU_TILE = 256
HALO_ROWS = 8
VMEM_LIMIT = 56 * 1024 * 1024

F32 = jnp.float32
BF16 = jnp.bfloat16


def _params(*sem):
    return pltpu.CompilerParams(dimension_semantics=sem, vmem_limit_bytes=VMEM_LIMIT)


def _const_spec(shape):
    n = len(shape)
    return pl.BlockSpec(shape, lambda *_: (0,) * n, pipeline_mode=pl.Buffered(1))


def _rms(x, g):
    ms = jnp.mean(x * x, axis=-1, keepdims=True)
    return x * lax.rsqrt(ms + EPS) * g


def _silu(x):
    h = 0.5 * x
    return h + h * jnp.tanh(h)


def _dot(a, b):
    return jnp.dot(a, b, preferred_element_type=F32)


def _hi_lo(vals):
    hi = vals.astype(BF16)
    lo = (vals - hi.astype(F32)).astype(BF16)
    return jnp.concatenate([hi, lo], axis=1)


def _expand(split, e_ref, cols=slice(None)):
    return _dot(split, e_ref[:, cols])


def _half_step_ffn(x, g_ref, wg_ref, wu_ref, wd_ref):
    h = _rms(x, g_ref[...]).astype(BF16)
    gate = _dot(h, wg_ref[...])
    up = _dot(h, wu_ref[...])
    act = (_silu(gate) * up).astype(BF16)
    return x + 0.5 * _dot(act, wd_ref[...])


def _ffn_body(x_ref, g_ref, wg_ref, wu_ref, wd_ref, o_ref):
    o_ref[...] = _half_step_ffn(x_ref[...], g_ref, wg_ref, wu_ref, wd_ref)


def _ffn(x, g, wg, wu, wd, tm):
    n = x.shape[0]
    row = pl.BlockSpec((tm, D_MODEL), lambda i: (i, 0))
    return pl.pallas_call(
        _ffn_body,
        grid=(n // tm,),
        in_specs=[row, _const_spec((1, D_MODEL)), _const_spec((D_MODEL, D_FF)),
                  _const_spec((D_MODEL, D_FF)), _const_spec((D_FF, D_MODEL))],
        out_specs=row,
        out_shape=jax.ShapeDtypeStruct((n, D_MODEL), F32),
        compiler_params=_params("parallel"),
        name="ffn",
    )(x, g, wg, wu, wd)


def _inproj_body(x_ref, xp_ref, xn_ref, g_ref, wq_ref, wk_ref, wv_ref, wz_ref, wx_ref, wdt_ref,
                 qg_ref, kg_ref, bd_ref, cw_ref, cb_ref, q_ref, k_ref, v_ref, z_ref, xa_ref, dt_ref, ext,
                 *, tiles_per_seq):
    h = _rms(x_ref[...], g_ref[...]).astype(BF16)

    def head_norm(t, gain):
        t2 = (t * t).astype(BF16)
        w = bd_ref.shape[0]
        ms = jnp.concatenate([_dot(t2[:, j * w:(j + 1) * w], bd_ref[...])
                              for j in range(D_ATT // w)], axis=1)
        return t * lax.rsqrt(ms + EPS) * gain

    def put_pairs(ref, val):
        for hp in range(N_HEADS // 2):
            ref[hp] = val[:, hp * LANES:(hp + 1) * LANES]

    put_pairs(q_ref, head_norm(_dot(h, wq_ref[...]), qg_ref[...]))
    put_pairs(k_ref, head_norm(_dot(h, wk_ref[...]), kg_ref[...]))
    put_pairs(v_ref, _dot(h, wv_ref[...]))
    z_ref[...] = _dot(h, wz_ref[...])
    dt_ref[...] = _dot(h, wdt_ref[...])

    tile = pl.program_id(0) % tiles_per_seq
    halo = jnp.concatenate([xp_ref[...], xn_ref[...]], axis=0)
    xbc_halo = _dot(_rms(halo, g_ref[...]).astype(BF16), wx_ref[...])
    xbc = _dot(h, wx_ref[...])
    rows = xbc.shape[0]
    pad = D_CONV // 2
    for sl in range(CONV_DIM // LANES):
        lanes = slice(sl * LANES, (sl + 1) * LANES)
        ext[sl, 0:HALO_ROWS, :] = jnp.where(tile > 0, xbc_halo[0:HALO_ROWS, lanes], 0.0)
        ext[sl, HALO_ROWS:HALO_ROWS + rows, :] = xbc[:, lanes]
        ext[sl, HALO_ROWS + rows:, :] = jnp.where(tile < tiles_per_seq - 1, xbc_halo[HALO_ROWS:, lanes], 0.0)
        acc = cb_ref[:, lanes]
        for j in range(D_CONV):
            acc = acc + ext[sl, pl.ds(HALO_ROWS - pad + j, rows, stride=1), :] * cw_ref[j:j + 1, lanes]
        xa_ref[:, lanes] = _silu(acc)


def _inproj(x, g, wq, wk, wv, wz, wx, wdt, qg, kg, bd, conv_w, conv_b, s, tm):
    n = x.shape[0]
    halos_per_tile = tm // HALO_ROWS
    halo_prev = pl.BlockSpec((HALO_ROWS, D_MODEL), lambda i: (jnp.maximum(i * halos_per_tile - 1, 0), 0))
    halo_next = pl.BlockSpec((HALO_ROWS, D_MODEL),
                             lambda i: (jnp.minimum((i + 1) * halos_per_tile, n // HALO_ROWS - 1), 0))
    row = lambda w: pl.BlockSpec((tm, w), lambda i: (i, 0))
    sq = _const_spec((D_MODEL, D_ATT))
    out = lambda w: jax.ShapeDtypeStruct((n, w), F32)
    pairs = pl.BlockSpec((N_HEADS // 2, tm, LANES), lambda i: (0, i, 0))
    pairs_out = jax.ShapeDtypeStruct((N_HEADS // 2, n, LANES), F32)
    return pl.pallas_call(
        functools.partial(_inproj_body, tiles_per_seq=s // tm),
        grid=(n // tm,),
        in_specs=[row(D_MODEL), halo_prev, halo_next, _const_spec((1, D_MODEL)), sq, sq, sq, sq,
                  _const_spec((D_MODEL, CONV_DIM)), _const_spec((D_MODEL, LANES)),
                  _const_spec((1, D_ATT)), _const_spec((1, D_ATT)), _const_spec(bd.shape),
                  _const_spec(conv_w.shape), _const_spec(conv_b.shape)],
        out_specs=[pairs, pairs, pairs, row(D_SSM), row(CONV_DIM), row(LANES)],
        out_shape=[pairs_out, pairs_out, pairs_out, out(D_SSM), out(CONV_DIM), out(LANES)],
        scratch_shapes=[pltpu.VMEM((CONV_DIM // LANES, tm + 2 * HALO_ROWS, LANES), F32)],
        compiler_params=_params("parallel"),
        name="inproj",
    )(x, x, x, g, wq, wk, wv, wz, wx, wdt, qg, kg, bd, conv_w, conv_b)


def _rows(start, size, dil):
    return pl.ds(start, size) if dil == 1 else pl.ds(start, size, stride=dil)


def _regroup_keys(prev, cur, nxt, tmp, dst):
    s4 = ATT_STAGE
    lo, hi = ATT_REACH - ATT_HALF, ATT_REACH + ATT_TILE + ATT_HALF
    dst[0, lo:ATT_REACH, :] = prev[lo:ATT_REACH, :].astype(BF16)
    dst[0, ATT_REACH:ATT_REACH + ATT_TILE, :] = cur[...].astype(BF16)
    dst[0, ATT_REACH + ATT_TILE:hi, :] = nxt[0:ATT_HALF, :].astype(BF16)
    h4, t4 = ATT_REACH // s4, ATT_TILE // s4
    for r in range(s4):
        tmp[r, 0:h4, :] = prev[pl.ds(r, h4, stride=s4), :]
        tmp[r, h4:h4 + t4, :] = cur[pl.ds(r, t4, stride=s4), :]
        tmp[r, h4 + t4:, :] = nxt[pl.ds(r, h4, stride=s4), :]
    lo, hi = h4 - ATT_HALF, h4 + t4 + ATT_HALF
    for r in range(s4):
        dst[1, r * (ATT_WINDOW // s4) + lo:r * (ATT_WINDOW // s4) + hi, :] = tmp[r, lo:hi, :].astype(BF16)
    per = ATT_WINDOW // (s4 * s4)
    for r in range(s4 * s4):
        dst[2, r * per:(r + 1) * per, :] = tmp[r % s4, pl.ds(r // s4, per, stride=s4), :].astype(BF16)


def _attn_step(q_ref, kc_ref, kp_ref, kn_ref, vc_ref, vp_ref, vn_ref, bias_ref, out_ref,
               kd, vd, tmp, o_scr, lse_scr, *, t0, seq_len, row_max):
    _regroup_keys(kp_ref, kc_ref, kn_ref, tmp, kd)
    _regroup_keys(vp_ref, vc_ref, vn_ref, tmp, vd)

    first_head = lax.broadcasted_iota(jnp.int32, (ATT_BLOCK, LANES), 1) < HEAD_DIM
    first_head_kw = lax.broadcasted_iota(jnp.int32, (ATT_KW, LANES), 1) < HEAD_DIM
    head_ones = (first_head_kw.astype(BF16), (~first_head_kw).astype(BF16))

    def block(p, dil, r, j):
        blocks_per_residue = ATT_TILE // (dil * ATT_BLOCK)
        q0 = j * (ATT_BLOCK * dil) + r
        k0 = r * (ATT_WINDOW // dil) + ATT_REACH // dil - ATT_HALF + j * ATT_BLOCK
        variant = 0
        if j == 0:
            variant = variant + (t0 == 0).astype(jnp.int32)
        if j == blocks_per_residue - 1:
            variant = variant + 2 * (t0 + ATT_TILE == seq_len).astype(jnp.int32)
        qp = q_ref[_rows(q0, ATT_BLOCK, dil), :].astype(BF16)
        kp = kd[p, k0:k0 + ATT_KW, :]
        vp = vd[p, k0:k0 + ATT_KW, :]
        zq, zv = jnp.zeros_like(qp), jnp.zeros_like(vp)
        q2 = jnp.concatenate([jnp.where(first_head, qp, zq), jnp.where(first_head, zq, qp)], axis=0)
        s2 = lax.dot_general(q2, kp, (((1,), (1,)), ((), ())), preferred_element_type=F32)
        es, ms = [], []
        for sub in range(2):
            s = s2[sub * ATT_BLOCK:(sub + 1) * ATT_BLOCK, :] + bias_ref[variant, p, sub]
            if row_max:
                m = jnp.max(s, axis=-1, keepdims=True)
                s = s - m
                ms.append(m)
            es.append(jnp.exp2(s).astype(BF16))
        v2 = jnp.concatenate([
            jnp.concatenate([jnp.where(first_head_kw, vp, zv), head_ones[0]], axis=1),
            jnp.concatenate([jnp.where(first_head_kw, zv, vp), head_ones[1]], axis=1)], axis=0)
        ol = _dot(jnp.concatenate(es, axis=1), v2)
        l = ol[:, LANES:]
        if not row_max:
            return ol[:, :LANES], l
        return ol[:, :LANES] * (1.0 / l), jnp.where(first_head, ms[0], ms[1]) + jnp.log2(l)

    for p in range(len(ATT_PATTERNS) - 1, 0, -1):
        dil = ATT_PATTERNS[p][1]
        blocks_per_residue = ATT_TILE // (dil * ATT_BLOCK)
        for idx in range(ATT_TILE // ATT_BLOCK):
            r, j = idx // blocks_per_residue, idx % blocks_per_residue
            rows = _rows(j * (ATT_BLOCK * dil) + r, ATT_BLOCK, dil)
            o_scr[p - 1, rows, :], lse_scr[p - 1, rows, :] = block(p, dil, r, j)
    for j in range(ATT_TILE // ATT_BLOCK):
        rows = pl.ds(j * ATT_BLOCK, ATT_BLOCK)
        o1, l1 = block(0, 1, 0, j)
        l2, l3 = lse_scr[0, rows, :], lse_scr[1, rows, :]
        if not row_max:
            mixed = o1 + o_scr[0, rows, :] + o_scr[1, rows, :]
            out_ref[rows, :] = (mixed * (1.0 / (l1 + l2 + l3))).astype(BF16)
            continue
        m = jnp.maximum(jnp.maximum(l1, l2), l3)
        e1, e2, e3 = jnp.exp2(l1 - m), jnp.exp2(l2 - m), jnp.exp2(l3 - m)
        mixed = e1 * o1 + e2 * o_scr[0, rows, :] + e3 * o_scr[1, rows, :]
        out_ref[rows, :] = (mixed * (1.0 / (e1 + e2 + e3))).astype(BF16)


def _softplus(x):
    return jnp.maximum(x, 0.0) + jnp.log(1.0 + jnp.exp(-jnp.abs(x)))


def _tri(lower):
    r = lax.broadcasted_iota(jnp.int32, (CHUNK, CHUNK), 0)
    c = lax.broadcasted_iota(jnp.int32, (CHUNK, CHUNK), 1)
    return (r >= c) if lower else (r <= c)


def _cumsum_rows(a, lower):
    a1 = a.astype(BF16)
    r1 = a - a1.astype(F32)
    a2 = r1.astype(BF16)
    a3 = (r1 - a2.astype(F32)).astype(BF16)
    tri = _tri(lower).astype(BF16)
    return _dot(jnp.concatenate([tri, tri, tri], axis=1), jnp.concatenate([a1, a2, a3], axis=0))


def _chunk_rows(i):
    return pl.ds(pl.multiple_of(i * CHUNK, CHUNK), CHUNK)


def _ssd_bwd_body(xa_ref, dt_ref, dtb_ref, a_ref, e_ref, yoff_ref, hst):
    @pl.when(pl.program_id(1) == 0)
    def _():
        hst[...] = jnp.zeros_like(hst)

    def chunk(ci, carry):
        rs = _chunk_rows(SSD_STEP_CHUNKS - 1 - ci)
        dt = _softplus(dt_ref[rs, :] + dtb_ref[...])
        rcum = _cumsum_rows(dt * a_ref[...], lower=False)
        both = _expand(jnp.concatenate([_hi_lo(jnp.exp(rcum)),
                                        _hi_lo(jnp.exp(rcum[0:1, :] - rcum) * dt)], axis=0), e_ref)
        decay_in, w_in = both[:CHUNK], both[CHUNK:]
        chunk_decay = decay_in[0:1, :]
        for g in range(N_GROUPS):
            gl = slice(g * HEADS_PER_GROUP * HEAD_DIM, (g + 1) * HEADS_PER_GROUP * HEAD_DIM)
            bg = xa_ref[rs, D_SSM + g * D_STATE:D_SSM + (g + 1) * D_STATE].astype(BF16)
            cg = xa_ref[rs, D_SSM + (N_GROUPS + g) * D_STATE:D_SSM + (N_GROUPS + g + 1) * D_STATE].astype(BF16)
            xw = (xa_ref[rs, gl] * w_in[:, gl]).astype(BF16)
            h_in = hst[:, gl]
            yoff_ref[rs, gl] = _dot(cg, h_in.astype(BF16)) * decay_in[:, gl]
            upd = lax.dot_general(bg, xw, (((0,), (0,)), ((), ())), preferred_element_type=F32)
            hst[:, gl] = h_in * chunk_decay[:, gl] + upd
        return carry

    lax.fori_loop(0, SSD_STEP_CHUNKS, chunk, 0, unroll=True)


def _ssd_fwd_chunks(n_chunks, xa_ref, dt_ref, z_ref, yb_ref, dtb_ref, a_ref, dsk_ref, og_ref, e_ref,
                    y_ref, hst):
    lower, upper = _tri(True), _tri(False)
    lane = lax.broadcasted_iota(jnp.int32, (CHUNK, LANES), 1)

    def chunk(ci, carry):
        rs = _chunk_rows(ci)
        dt = _softplus(dt_ref[rs, :] + dtb_ref[...])
        a = dt * a_ref[...]
        fcum = _cumsum_rows(a, lower=True)
        rcum = fcum[CHUNK - 1:CHUNK, :] - fcum + a
        both = _expand(jnp.concatenate([_hi_lo(jnp.exp(fcum)),
                                        _hi_lo(jnp.exp(fcum[CHUNK - 1:CHUNK, :] - fcum) * dt)], axis=0), e_ref)
        decay_in, w_in = both[:CHUNK], both[CHUNK:]
        chunk_decay = decay_in[CHUNK - 1:CHUNK, :]
        log2_dt = jnp.log(dt) * LOG2_E
        fcol, rcol = fcum * LOG2_E, rcum * LOG2_E
        frow, rrow = (fcol - log2_dt).T, (rcol - log2_dt).T

        for g in range(N_GROUPS):
            gl = slice(g * HEADS_PER_GROUP * HEAD_DIM, (g + 1) * HEADS_PER_GROUP * HEAD_DIM)
            bg = xa_ref[rs, D_SSM + g * D_STATE:D_SSM + (g + 1) * D_STATE].astype(BF16)
            cg = xa_ref[rs, D_SSM + (N_GROUPS + g) * D_STATE:D_SSM + (N_GROUPS + g + 1) * D_STATE].astype(BF16)
            cb = lax.dot_general(cg, bg, (((1,), (1,)), ((), ())), preferred_element_type=F32)
            xw = (xa_ref[rs, gl] * w_in[:, gl]).astype(BF16)
            h_in = hst[:, gl]
            y_off = _dot(cg, h_in.astype(BF16)) * decay_in[:, gl]
            upd = lax.dot_general(bg, xw, (((0,), (0,)), ((), ())), preferred_element_type=F32)
            hst[:, gl] = h_in * chunk_decay[:, gl] + upd
            gated = []
            for pair in range(HEADS_PER_GROUP // 2):
                pl_ = slice(gl.start + pair * LANES, gl.start + (pair + 1) * LANES)
                xs = xa_ref[rs, pl_]
                x_pair = xs.astype(BF16)
                zx = jnp.zeros_like(x_pair)
                weights = []
                for sub in range(2):
                    h = g * HEADS_PER_GROUP + 2 * pair + sub
                    hb = N_HEADS + h
                    seg_f = jnp.where(lower, fcol[:, h:h + 1] - frow[h:h + 1, :], NEG_INF)
                    seg_b = jnp.where(upper, rcol[:, hb:hb + 1] - rrow[hb:hb + 1, :], NEG_INF)
                    mix = jnp.exp2(seg_f) + jnp.exp2(seg_b)
                    weights.append((cb * mix).astype(BF16))
                x2 = jnp.concatenate([jnp.where(lane < HEAD_DIM, x_pair, zx),
                                      jnp.where(lane < HEAD_DIM, zx, x_pair)], axis=0)
                y_diag = _dot(jnp.concatenate(weights, axis=1), x2)
                po = slice(pair * LANES, (pair + 1) * LANES)
                y = y_diag + y_off[:, po] + yb_ref[rs, pl_] + dsk_ref[:, pl_] * xs
                gated.append(y * _silu(z_ref[rs, pl_]))
            y_ref[rs, gl] = _rms(jnp.concatenate(gated, axis=1), og_ref[:, gl]).astype(BF16)
        return carry

    lax.fori_loop(0, n_chunks, chunk, 0, unroll=True)


def _ssd_bwd(xa, dt, dt_bias, a_neg, e_bwd):
    b, s, _ = xa.shape
    rows = SSD_STEP_CHUNKS * CHUNK
    nc = s // rows
    step = lambda w: pl.BlockSpec((None, rows, w), lambda bi, i: (bi, nc - 1 - i, 0))
    return pl.pallas_call(
        _ssd_bwd_body,
        grid=(b, nc),
        in_specs=[step(CONV_DIM), step(LANES), _const_spec(dt_bias.shape), _const_spec(a_neg.shape),
                  _const_spec(e_bwd.shape)],
        out_specs=step(D_SSM),
        out_shape=jax.ShapeDtypeStruct((b, s, D_SSM), F32),
        scratch_shapes=[pltpu.VMEM((D_STATE, D_SSM), F32)],
        compiler_params=_params("parallel", "arbitrary"),
        name="ssd_bwd",
    )(xa, dt, dt_bias, a_neg, e_bwd)


def _mixer_body(narrow_ref, q_ref, kc_ref, kp_ref, kn_ref, vc_ref, vp_ref, vn_ref, bias_ref,
                xa_ref, dt_ref, z_ref, yb_ref, dtb_ref, a_ref, dsk_ref, og_ref, e_ref,
                attn_ref, y_ref, kd, vd, tmp, o_scr, lse_scr, hst, *, seq_len):
    hp, t = pl.program_id(1), pl.program_id(2)

    @pl.when(jnp.logical_and(t == 0, hp == 0))
    def _():
        hst[...] = jnp.zeros_like(hst)

    def step(row_max):
        _attn_step(q_ref, kc_ref, kp_ref, kn_ref, vc_ref, vp_ref, vn_ref, bias_ref, attn_ref,
                   kd, vd, tmp, o_scr, lse_scr, t0=t * ATT_TILE, seq_len=seq_len, row_max=row_max)
        _ssd_fwd_chunks(MIXER_CHUNKS, xa_ref, dt_ref, z_ref, yb_ref, dtb_ref, a_ref, dsk_ref, og_ref,
                        e_ref, y_ref, hst)

    pl.when(narrow_ref[0] != 0)(functools.partial(step, False))
    pl.when(narrow_ref[0] == 0)(functools.partial(step, True))


def _mixer(q, k, v, att, xa, dt, z, yb, dt_bias, a_neg, d_skip, out_g, e_fwd):
    bias = att["bias"]
    n_pairs, b, s, _ = q.shape
    n_pat = len(ATT_PATTERNS)
    last_halo = s // ATT_REACH - 1
    halo_per_tile = ATT_TILE // ATT_REACH
    n_tiles = s // ATT_TILE
    cur = pl.BlockSpec((None, None, ATT_TILE, LANES), lambda bi, hp, t: (hp, bi, t, 0))
    prev = pl.BlockSpec((None, None, ATT_REACH, LANES),
                        lambda bi, hp, t: (hp, bi, jnp.maximum(t * halo_per_tile - 1, 0), 0))
    nxt = pl.BlockSpec((None, None, ATT_REACH, LANES),
                       lambda bi, hp, t: (hp, bi, jnp.minimum((t + 1) * halo_per_tile, last_halo), 0))
    bias_spec = pl.BlockSpec((bias.shape[0], n_pat, 2, ATT_BLOCK, ATT_KW),
                             lambda bi, hp, t: (0, 0, hp, 0, 0))
    rows = MIXER_CHUNKS * CHUNK
    scan = lambda w: pl.BlockSpec((None, rows, w), lambda bi, hp, t: (bi, hp * n_tiles + t, 0))
    consts = [_const_spec(a.shape) for a in (dt_bias, a_neg, d_skip, out_g, e_fwd)]
    return pl.pallas_call(
        functools.partial(_mixer_body, seq_len=s),
        grid=(b, n_pairs, n_tiles),
        in_specs=[pl.BlockSpec(memory_space=pltpu.SMEM), cur, cur, prev, nxt, cur, prev, nxt, bias_spec,
                  scan(CONV_DIM), scan(LANES), scan(D_SSM), scan(D_SSM)] + consts,
        out_specs=[cur, scan(D_SSM)],
        out_shape=[jax.ShapeDtypeStruct((n_pairs, b, s, LANES), BF16),
                   jax.ShapeDtypeStruct((b, s, D_SSM), BF16)],
        scratch_shapes=[pltpu.VMEM((n_pat, ATT_WINDOW, LANES), BF16),
                        pltpu.VMEM((n_pat, ATT_WINDOW, LANES), BF16),
                        pltpu.VMEM((ATT_STAGE, ATT_WINDOW // ATT_STAGE, LANES), F32),
                        pltpu.VMEM((n_pat - 1, ATT_TILE, LANES), F32),
                        pltpu.VMEM((n_pat - 1, ATT_TILE, LANES), F32),
                        pltpu.VMEM((D_STATE, D_SSM), F32)],
        compiler_params=_params("parallel", "arbitrary", "arbitrary"),
        name="mixer",
    )(att["narrow"], q, k, k, k, v, v, v, bias, xa, dt, z, yb, dt_bias, a_neg, d_skip, out_g, e_fwd)


def _outproj_ffn_body(x_ref, a_ref, y_ref, g_ref, wa_ref, ws_ref, g2_ref, wg_ref, wu_ref, wd_ref,
                      out_ref):
    attn = jnp.concatenate([a_ref[hp] for hp in range(a_ref.shape[0])], axis=1)
    attn = _rms(attn.astype(F32), g_ref[...]).astype(BF16)
    x2 = x_ref[...] + _dot(attn, wa_ref[...]) + _dot(y_ref[...], ws_ref[...])
    out_ref[...] = _half_step_ffn(x2, g2_ref, wg_ref, wu_ref, wd_ref)


def _outproj_ffn(x, attn, y, g, wa, ws, g2, wg, wu, wd, tm):
    n = x.shape[0]
    row = lambda w: pl.BlockSpec((tm, w), lambda i: (i, 0))
    sq = _const_spec((D_ATT, D_MODEL))
    pairs = pl.BlockSpec((attn.shape[0], tm, LANES), lambda i: (0, i, 0))
    return pl.pallas_call(
        _outproj_ffn_body,
        grid=(n // tm,),
        in_specs=[row(D_MODEL), pairs, row(D_SSM), _const_spec((1, D_ATT)), sq, sq,
                  _const_spec((1, D_MODEL)), _const_spec((D_MODEL, D_FF)), _const_spec((D_MODEL, D_FF)),
                  _const_spec((D_FF, D_MODEL))],
        out_specs=row(D_MODEL),
        out_shape=jax.ShapeDtypeStruct((n, D_MODEL), F32),
        compiler_params=_params("parallel"),
        name="outproj_ffn",
    )(x, attn, y, g, wa, ws, g2, wg, wu, wd)


def _t5_bucket(rel):
    nb = N_REL_BUCKETS // 2
    max_exact = nb // 2
    n = np.abs(rel)
    large = max_exact + (np.log(np.maximum(n, 1) / max_exact)
                         / math.log(REL_MAX_DIST / max_exact) * (nb - max_exact)).astype(np.int32)
    large = np.minimum(large, nb - 1)
    return (np.where(rel > 0, nb, 0) + np.where(n < max_exact, n, large)).astype(np.int32)


def _bias_tables(rel_bias, shift):
    rel_sub = np.arange(ATT_KW)[None, :] - ATT_HALF - np.arange(ATT_BLOCK)[:, None]
    in_window = np.abs(rel_sub) <= ATT_HALF
    col = np.arange(ATT_KW)[None, :]
    tables = []
    for _, dil in ATT_PATTERNS:
        onehot = (_t5_bucket(rel_sub * dil)[:, :, None] == np.arange(N_REL_BUCKETS)).astype(np.float32)
        bias = jnp.einsum("ijk,kh->hij", jnp.asarray(onehot, BF16).astype(F32), rel_bias.astype(F32),
                          precision=lax.Precision.HIGHEST)
        tables.append(bias * LOG2_E - shift)
    bias = jnp.stack(tables, axis=0)
    variants = []
    for v in range(4):
        keep = in_window & ((col >= ATT_HALF) | (v & 1 == 0)) & ((col < ATT_BLOCK + ATT_HALF) | (v & 2 == 0))
        variants.append(jnp.where(keep[None, None], bias, NEG_INF))
    return jnp.stack(variants, axis=0)


def _attention_tables(rel_bias, q_norm_g, k_norm_g):
    logit_bound = (QK_BOUND_SLACK * HEAD_DIM * HEAD_DIM ** -0.5 * LOG2_E
                   * jnp.max(jnp.abs(q_norm_g)) * jnp.max(jnp.abs(k_norm_g))).astype(F32)
    bias_lo = (jnp.min(rel_bias) * LOG2_E).astype(F32)
    bias_hi = (jnp.max(rel_bias) * LOG2_E).astype(F32)
    narrow = 2.0 * logit_bound + (bias_hi - bias_lo) < EXP2_SAFE_RANGE
    return {
        "bias": _bias_tables(rel_bias, jnp.where(narrow, logit_bound + bias_hi, 0.0)),
        "narrow": narrow.astype(jnp.int32).reshape(1),
    }


def _head_expander(first_row, width):
    e = np.zeros((2 * LANES, width), np.float32)
    for h in range(width // HEAD_DIM):
        e[first_row + h, h * HEAD_DIM:(h + 1) * HEAD_DIM] = 1.0
        e[LANES + first_row + h, h * HEAD_DIM:(h + 1) * HEAD_DIM] = 1.0
    return jnp.asarray(e, BF16)


def _block_diag_mean(width):
    i = np.arange(width)
    return jnp.asarray((i[:, None] // HEAD_DIM == i[None, :] // HEAD_DIM) / HEAD_DIM, BF16)


def _layer(x, p, tm):
    b, s, _ = x.shape
    n = b * s
    xf = x.reshape(n, D_MODEL)
    x1 = _ffn(xf, p["ffn1_g"], p["ffn1_wg"], p["ffn1_wu"], p["ffn1_wd"], tm)
    q, k, v, z, xa, dt = _inproj(x1, p["mix_g"], p["wq"], p["wk"], p["wv"], p["wz"], p["wx"],
                                 p["wdt"], p["qg"], p["kg"], p["bd"], p["conv_w"], p["conv_b"], s, tm)
    seq = lambda a: a.reshape(b, s, a.shape[-1])
    pair_seq = lambda a: a.reshape(a.shape[0], b, s, LANES)
    yb = _ssd_bwd(seq(xa), seq(dt), p["dt_bias"], p["a_neg"], p["e_bwd"])
    attn, y = _mixer(pair_seq(q), pair_seq(k), pair_seq(v), p["att"], seq(xa), seq(dt), seq(z), yb,
                     p["dt_bias"], p["a_neg"], p["d_skip"], p["ssm_g"], p["e_fwd"])
    out = _outproj_ffn(x1, attn.reshape(attn.shape[0], n, LANES), y.reshape(n, D_SSM), p["attn_g"],
                       p["wo_att"], p["wo_ssm"], p["ffn2_g"], p["ffn2_wg"], p["ffn2_wu"],
                       p["ffn2_wd"], tm)
    return out.reshape(b, s, D_MODEL)


def _prepare(rel_bias, ffn1_norm_g, ffn1_w_gate, ffn1_w_up, ffn1_w_down, mix_norm_g, w_in,
             q_norm_g, k_norm_g, attn_out_g, conv_w, conv_b, dt_bias, a_log, d_skip, ssm_out_g,
             w_out, ffn2_norm_g, ffn2_w_gate, ffn2_w_up, ffn2_w_down):
    row = lambda a: a.reshape(1, -1).astype(F32)
    w16 = lambda a: a.astype(BF16)
    c0, c1, c2, c3 = D_ATT, 2 * D_ATT, 3 * D_ATT, 3 * D_ATT + D_SSM
    c4 = c3 + CONV_DIM
    pad32 = lambda a: jnp.pad(a.reshape(1, 2 * N_HEADS).astype(F32), ((0, 0), (0, LANES - 2 * N_HEADS)))
    return {
        "ffn1_g": row(ffn1_norm_g), "ffn1_wg": w16(ffn1_w_gate), "ffn1_wu": w16(ffn1_w_up),
        "ffn1_wd": w16(ffn1_w_down),
        "ffn2_g": row(ffn2_norm_g), "ffn2_wg": w16(ffn2_w_gate), "ffn2_wu": w16(ffn2_w_up),
        "ffn2_wd": w16(ffn2_w_down),
        "mix_g": row(mix_norm_g),
        "wq": w16(w_in[:, :c0]), "wk": w16(w_in[:, c0:c1]), "wv": w16(w_in[:, c1:c2]),
        "wz": w16(w_in[:, c2:c3]), "wx": w16(w_in[:, c3:c4]),
        "wdt": w16(jnp.pad(w_in[:, c4:], ((0, 0), (0, LANES - 2 * N_HEADS)))),
        "qg": row(jnp.tile(q_norm_g, N_HEADS)) * (HEAD_DIM ** -0.5 * LOG2_E),
        "kg": row(jnp.tile(k_norm_g, N_HEADS)),
        "bd": _block_diag_mean(MXU_TILE),
        "att": _attention_tables(rel_bias, q_norm_g, k_norm_g),
        "attn_g": row(attn_out_g),
        "conv_w": conv_w.astype(F32), "conv_b": row(conv_b),
        "dt_bias": pad32(dt_bias), "a_neg": pad32(-jnp.exp(a_log.astype(F32))),
        "d_skip": row(jnp.repeat(d_skip, HEAD_DIM)), "ssm_g": row(ssm_out_g),
        "e_fwd": _head_expander(0, D_SSM), "e_bwd": _head_expander(N_HEADS, D_SSM),
        "wo_att": w16(w_out[:D_ATT]), "wo_ssm": w16(w_out[D_ATT:]),
    }


def _trunk(x, layers, tm=512):
    for p in layers:
        x = _layer(x, p, tm)
    return x


def kernel(x_prompt, x_sample, rel_bias, ffn1_norm_g, ffn1_w_gate, ffn1_w_up, ffn1_w_down, mix_norm_g, w_in, q_norm_g, k_norm_g, attn_out_g, conv_w, conv_b, dt_bias, a_log, d_skip, ssm_out_g, w_out, ffn2_norm_g, ffn2_w_gate, ffn2_w_up, ffn2_w_down):
    per_layer = (ffn1_norm_g, ffn1_w_gate, ffn1_w_up, ffn1_w_down, mix_norm_g, w_in, q_norm_g,
                 k_norm_g, attn_out_g, conv_w, conv_b, dt_bias, a_log, d_skip, ssm_out_g, w_out,
                 ffn2_norm_g, ffn2_w_gate, ffn2_w_up, ffn2_w_down)
    layers = [_prepare(rel_bias, *(a[l] for a in per_layer)) for l in range(ffn1_norm_g.shape[0])]
    return (_trunk(x_prompt, layers), _trunk(x_sample, layers))
```

```python
import functools
import math

import numpy as np
import jax
import jax.numpy as jnp
from jax import lax
from jax.experimental import pallas as pl
from jax.experimental.pallas import tpu as pltpu

D_MODEL = 1024
D_ATT = 1024
D_SSM = 1024
HEAD_DIM = 64
N_HEADS = 16
ATT_PATTERNS = ((128, 1), (512, 4), (2048, 16))
ATT_BLOCK = 128
ATT_HALF = 64
ATT_KW = ATT_BLOCK + 2 * ATT_HALF
ATT_MAX_DIL = max(d for _, d in ATT_PATTERNS)
ATT_TILE = ATT_BLOCK * ATT_MAX_DIL
ATT_REACH = ATT_HALF * ATT_MAX_DIL
ATT_WINDOW = ATT_TILE + 2 * ATT_REACH
ATT_STAGE = 4
assert tuple(d for _, d in ATT_PATTERNS) == (1, ATT_STAGE, ATT_STAGE * ATT_STAGE)
N_REL_BUCKETS = 32
REL_MAX_DIST = 1024
N_GROUPS = 4
HEADS_PER_GROUP = 4
D_STATE = 128
D_CONV = 5
CHUNK = 128
SSD_STEP_CHUNKS = 8
MIXER_CHUNKS = ATT_TILE // CHUNK // (N_HEADS // 2)
CONV_DIM = D_SSM + 2 * N_GROUPS * D_STATE
D_FF = 2816
EPS = 1e-6
NEG_INF = -1e30
LOG2_E = 1.4426950408889634
EXP2_SAFE_RANGE = 100.0
QK_BOUND_SLACK = 1.05

LANES = 128
MXU_TILE = 256
HALO_ROWS = 8
VMEM_LIMIT = 56 * 1024 * 1024

F32 = jnp.float32
BF16 = jnp.bfloat16


def _params(*sem):
    return pltpu.CompilerParams(dimension_semantics=sem, vmem_limit_bytes=VMEM_LIMIT)


def _const_spec(shape):
    n = len(shape)
    return pl.BlockSpec(shape, lambda *_: (0,) * n, pipeline_mode=pl.Buffered(1))


def _rms(x, g):
    ms = jnp.mean(x * x, axis=-1, keepdims=True)
    return x * lax.rsqrt(ms + EPS) * g


def _silu(x):
    h = 0.5 * x
    return h + h * jnp.tanh(h)


def _dot(a, b):
    return jnp.dot(a, b, preferred_element_type=F32)


def _hi_lo(vals):
    hi = vals.astype(BF16)
    lo = (vals - hi.astype(F32)).astype(BF16)
    return jnp.concatenate([hi, lo], axis=1)


def _expand(split, e_ref, cols=slice(None)):
    return _dot(split, e_ref[:, cols])


def _half_step_ffn(x, g_ref, wg_ref, wu_ref, wd_ref):
    h = _rms(x, g_ref[...]).astype(BF16)
    gate = _dot(h, wg_ref[...])
    up = _dot(h, wu_ref[...])
    act = (_silu(gate) * up).astype(BF16)
    return x + 0.5 * _dot(act, wd_ref[...])


def _ffn_body(x_ref, g_ref, wg_ref, wu_ref, wd_ref, o_ref):
    o_ref[...] = _half_step_ffn(x_ref[...], g_ref, wg_ref, wu_ref, wd_ref)


def _ffn(x, g, wg, wu, wd, tm):
    n = x.shape[0]
    row = pl.BlockSpec((tm, D_MODEL), lambda i: (i, 0))
    return pl.pallas_call(
        _ffn_body,
        grid=(n // tm,),
        in_specs=[row, _const_spec((1, D_MODEL)), _const_spec((D_MODEL, D_FF)),
                  _const_spec((D_MODEL, D_FF)), _const_spec((D_FF, D_MODEL))],
        out_specs=row,
        out_shape=jax.ShapeDtypeStruct((n, D_MODEL), F32),
        compiler_params=_params("parallel"),
        name="ffn",
    )(x, g, wg, wu, wd)


def _inproj_body(x_ref, xp_ref, xn_ref, g_ref, wq_ref, wk_ref, wv_ref, wz_ref, wx_ref, wdt_ref,
                 qg_ref, kg_ref, bd_ref, cw_ref, cb_ref, q_ref, k_ref, v_ref, z_ref, xa_ref, dt_ref, ext,
                 *, tiles_per_seq):
    h = _rms(x_ref[...], g_ref[...]).astype(BF16)

    def head_norm(t, gain):
        t2 = (t * t).astype(BF16)
        w = bd_ref.shape[0]
        ms = jnp.concatenate([_dot(t2[:, j * w:(j + 1) * w], bd_ref[...])
                              for j in range(D_ATT // w)], axis=1)
        return t * lax.rsqrt(ms + EPS) * gain

    def put_pairs(ref, val):
        for hp in range(N_HEADS // 2):
            ref[hp] = val[:, hp * LANES:(hp + 1) * LANES]

    put_pairs(q_ref, head_norm(_dot(h, wq_ref[...]), qg_ref[...]))
    put_pairs(k_ref, head_norm(_dot(h, wk_ref[...]), kg_ref[...]))
    put_pairs(v_ref, _dot(h, wv_ref[...]))
    z_ref[...] = _dot(h, wz_ref[...])
    dt_ref[...] = _dot(h, wdt_ref[...])

    tile = pl.program_id(0) % tiles_per_seq
    halo = jnp.concatenate([xp_ref[...], xn_ref[...]], axis=0)
    xbc_all = _dot(jnp.concatenate([h, _rms(halo, g_ref[...]).astype(BF16)], axis=0), wx_ref[...])
    rows = h.shape[0]
    xbc, xbc_halo = xbc_all[:rows], xbc_all[rows:]
    pad = D_CONV // 2
    for sl in range(CONV_DIM // LANES):
        lanes = slice(sl * LANES, (sl + 1) * LANES)
        ext[sl, 0:HALO_ROWS, :] = jnp.where(tile > 0, xbc_halo[0:HALO_ROWS, lanes], 0.0)
        ext[sl, HALO_ROWS:HALO_ROWS + rows, :] = xbc[:, lanes]
        ext[sl, HALO_ROWS + rows:, :] = jnp.where(tile < tiles_per_seq - 1, xbc_halo[HALO_ROWS:, lanes], 0.0)
        acc = cb_ref[:, lanes]
        for j in range(D_CONV):
            acc = acc + ext[sl, pl.ds(HALO_ROWS - pad + j, rows, stride=1), :] * cw_ref[j:j + 1, lanes]
        xa_ref[:, lanes] = _silu(acc)


def _inproj(x, g, wq, wk, wv, wz, wx, wdt, qg, kg, bd, conv_w, conv_b, s, tm):
    n = x.shape[0]
    halos_per_tile = tm // HALO_ROWS
    halo_prev = pl.BlockSpec((HALO_ROWS, D_MODEL), lambda i: (jnp.maximum(i * halos_per_tile - 1, 0), 0))
    halo_next = pl.BlockSpec((HALO_ROWS, D_MODEL),
                             lambda i: (jnp.minimum((i + 1) * halos_per_tile, n // HALO_ROWS - 1), 0))
    row = lambda w: pl.BlockSpec((tm, w), lambda i: (i, 0))
    sq = _const_spec((D_MODEL, D_ATT))
    out = lambda w: jax.ShapeDtypeStruct((n, w), F32)
    pairs = pl.BlockSpec((N_HEADS // 2, tm, LANES), lambda i: (0, i, 0))
    pairs_out = jax.ShapeDtypeStruct((N_HEADS // 2, n, LANES), F32)
    return pl.pallas_call(
        functools.partial(_inproj_body, tiles_per_seq=s // tm),
        grid=(n // tm,),
        in_specs=[row(D_MODEL), halo_prev, halo_next, _const_spec((1, D_MODEL)), sq, sq, sq, sq,
                  _const_spec((D_MODEL, CONV_DIM)), _const_spec((D_MODEL, LANES)),
                  _const_spec((1, D_ATT)), _const_spec((1, D_ATT)), _const_spec(bd.shape),
                  _const_spec(conv_w.shape), _const_spec(conv_b.shape)],
        out_specs=[pairs, pairs, pairs, row(D_SSM), row(CONV_DIM), row(LANES)],
        out_shape=[pairs_out, pairs_out, pairs_out, out(D_SSM), out(CONV_DIM), out(LANES)],
        scratch_shapes=[pltpu.VMEM((CONV_DIM // LANES, tm + 2 * HALO_ROWS, LANES), F32)],
        compiler_params=_params("parallel"),
        name="inproj",
    )(x, x, x, g, wq, wk, wv, wz, wx, wdt, qg, kg, bd, conv_w, conv_b)


def _rows(start, size, dil):
    return pl.ds(start, size) if dil == 1 else pl.ds(start, size, stride=dil)


def _regroup_keys(prev, cur, nxt, tmp, dst):
    s4 = ATT_STAGE
    lo, hi = ATT_REACH - ATT_HALF, ATT_REACH + ATT_TILE + ATT_HALF
    dst[0, lo:ATT_REACH, :] = prev[lo:ATT_REACH, :].astype(BF16)
    dst[0, ATT_REACH:ATT_REACH + ATT_TILE, :] = cur[...].astype(BF16)
    dst[0, ATT_REACH + ATT_TILE:hi, :] = nxt[0:ATT_HALF, :].astype(BF16)
    h4, t4 = ATT_REACH // s4, ATT_TILE // s4
    for r in range(s4):
        tmp[r, 0:h4, :] = prev[pl.ds(r, h4, stride=s4), :]
        tmp[r, h4:h4 + t4, :] = cur[pl.ds(r, t4, stride=s4), :]
        tmp[r, h4 + t4:, :] = nxt[pl.ds(r, h4, stride=s4), :]
    lo, hi = h4 - ATT_HALF, h4 + t4 + ATT_HALF
    for r in range(s4):
        dst[1, r * (ATT_WINDOW // s4) + lo:r * (ATT_WINDOW // s4) + hi, :] = tmp[r, lo:hi, :].astype(BF16)
    per = ATT_WINDOW // (s4 * s4)
    for r in range(s4 * s4):
        dst[2, r * per:(r + 1) * per, :] = tmp[r % s4, pl.ds(r // s4, per, stride=s4), :].astype(BF16)


def _attn_step(q_ref, kc_ref, kp_ref, kn_ref, vc_ref, vp_ref, vn_ref, bias_ref, out_ref,
               kd, vd, tmp, o_scr, lse_scr, *, t0, seq_len, row_max):
    _regroup_keys(kp_ref, kc_ref, kn_ref, tmp, kd)
    _regroup_keys(vp_ref, vc_ref, vn_ref, tmp, vd)

    first_head = lax.broadcasted_iota(jnp.int32, (ATT_BLOCK, LANES), 1) < HEAD_DIM
    first_head_kw = lax.broadcasted_iota(jnp.int32, (ATT_KW, LANES), 1) < HEAD_DIM
    head_ones = (first_head_kw.astype(BF16), (~first_head_kw).astype(BF16))

    def block(p, dil, r, j):
        blocks_per_residue = ATT_TILE // (dil * ATT_BLOCK)
        q0 = j * (ATT_BLOCK * dil) + r
        k0 = r * (ATT_WINDOW // dil) + ATT_REACH // dil - ATT_HALF + j * ATT_BLOCK
        variant = 0
        if j == 0:
            variant = variant + (t0 == 0).astype(jnp.int32)
        if j == blocks_per_residue - 1:
            variant = variant + 2 * (t0 + ATT_TILE == seq_len).astype(jnp.int32)
        qp = q_ref[_rows(q0, ATT_BLOCK, dil), :].astype(BF16)
        kp = kd[p, k0:k0 + ATT_KW, :]
        vp = vd[p, k0:k0 + ATT_KW, :]
        zq, zv = jnp.zeros_like(qp), jnp.zeros_like(vp)
        q2 = jnp.concatenate([jnp.where(first_head, qp, zq), jnp.where(first_head, zq, qp)], axis=0)
        s2 = lax.dot_general(q2, kp, (((1,), (1,)), ((), ())), preferred_element_type=F32)
        es, ms = [], []
        for sub in range(2):
            s = s2[sub * ATT_BLOCK:(sub + 1) * ATT_BLOCK, :] + bias_ref[variant, p, sub]
            if row_max:
                m = jnp.max(s, axis=-1, keepdims=True)
                s = s - m
                ms.append(m)
            es.append(jnp.exp2(s).astype(BF16))
        v2 = jnp.concatenate([
            jnp.concatenate([jnp.where(first_head_kw, vp, zv), head_ones[0]], axis=1),
            jnp.concatenate([jnp.where(first_head_kw, zv, vp), head_ones[1]], axis=1)], axis=0)
        ol = _dot(jnp.concatenate(es, axis=1), v2)
        l = ol[:, LANES:]
        if not row_max:
            return ol[:, :LANES], l
        return ol[:, :LANES] * (1.0 / l), jnp.where(first_head, ms[0], ms[1]) + jnp.log2(l)

    for p in range(len(ATT_PATTERNS) - 1, 0, -1):
        dil = ATT_PATTERNS[p][1]
        blocks_per_residue = ATT_TILE // (dil * ATT_BLOCK)
        for idx in range(ATT_TILE // ATT_BLOCK):
            r, j = idx // blocks_per_residue, idx % blocks_per_residue
            rows = _rows(j * (ATT_BLOCK * dil) + r, ATT_BLOCK, dil)
            o_scr[p - 1, rows, :], lse_scr[p - 1, rows, :] = block(p, dil, r, j)
    for j in range(ATT_TILE // ATT_BLOCK):
        rows = pl.ds(j * ATT_BLOCK, ATT_BLOCK)
        o1, l1 = block(0, 1, 0, j)
        l2, l3 = lse_scr[0, rows, :], lse_scr[1, rows, :]
        if not row_max:
            mixed = o1 + o_scr[0, rows, :] + o_scr[1, rows, :]
            out_ref[rows, :] = (mixed * (1.0 / (l1 + l2 + l3))).astype(BF16)
            continue
        m = jnp.maximum(jnp.maximum(l1, l2), l3)
        e1, e2, e3 = jnp.exp2(l1 - m), jnp.exp2(l2 - m), jnp.exp2(l3 - m)
        mixed = e1 * o1 + e2 * o_scr[0, rows, :] + e3 * o_scr[1, rows, :]
        out_ref[rows, :] = (mixed * (1.0 / (e1 + e2 + e3))).astype(BF16)


def _softplus(x):
    return jnp.maximum(x, 0.0) + jnp.log(1.0 + jnp.exp(-jnp.abs(x)))


def _tri(lower):
    r = lax.broadcasted_iota(jnp.int32, (CHUNK, CHUNK), 0)
    c = lax.broadcasted_iota(jnp.int32, (CHUNK, CHUNK), 1)
    return (r >= c) if lower else (r <= c)


def _cumsum_rows(a, lower):
    a1 = a.astype(BF16)
    r1 = a - a1.astype(F32)
    a2 = r1.astype(BF16)
    a3 = (r1 - a2.astype(F32)).astype(BF16)
    tri = _tri(lower).astype(BF16)
    return _dot(jnp.concatenate([tri, tri, tri], axis=1), jnp.concatenate([a1, a2, a3], axis=0))


def _chunk_rows(i):
    return pl.ds(pl.multiple_of(i * CHUNK, CHUNK), CHUNK)


def _ssd_bwd_body(xa_ref, dt_ref, dtb_ref, a_ref, e_ref, yoff_ref, hst):
    @pl.when(pl.program_id(1) == 0)
    def _():
        hst[...] = jnp.zeros_like(hst)

    def chunk(ci, carry):
        rs = _chunk_rows(SSD_STEP_CHUNKS - 1 - ci)
        dt = _softplus(dt_ref[rs, :] + dtb_ref[...])
        rcum = _cumsum_rows(dt * a_ref[...], lower=False)
        both = _expand(jnp.concatenate([_hi_lo(jnp.exp(rcum)),
                                        _hi_lo(jnp.exp(rcum[0:1, :] - rcum) * dt)], axis=0), e_ref)
        decay_in, w_in = both[:CHUNK], both[CHUNK:]
        chunk_decay = decay_in[0:1, :]
        for g in range(N_GROUPS):
            gl = slice(g * HEADS_PER_GROUP * HEAD_DIM, (g + 1) * HEADS_PER_GROUP * HEAD_DIM)
            bg = xa_ref[rs, D_SSM + g * D_STATE:D_SSM + (g + 1) * D_STATE].astype(BF16)
            cg = xa_ref[rs, D_SSM + (N_GROUPS + g) * D_STATE:D_SSM + (N_GROUPS + g + 1) * D_STATE].astype(BF16)
            xw = (xa_ref[rs, gl] * w_in[:, gl]).astype(BF16)
            h_in = hst[:, gl]
            yoff_ref[rs, gl] = _dot(cg, h_in.astype(BF16)) * decay_in[:, gl]
            upd = lax.dot_general(bg, xw, (((0,), (0,)), ((), ())), preferred_element_type=F32)
            hst[:, gl] = h_in * chunk_decay[:, gl] + upd
        return carry

    lax.fori_loop(0, SSD_STEP_CHUNKS, chunk, 0, unroll=True)


def _ssd_fwd_chunks(n_chunks, xa_ref, dt_ref, z_ref, yb_ref, dtb_ref, a_ref, dsk_ref, og_ref, e_ref,
                    y_ref, hst):
    lower, upper = _tri(True), _tri(False)
    lane = lax.broadcasted_iota(jnp.int32, (CHUNK, LANES), 1)

    def chunk(ci, carry):
        rs = _chunk_rows(ci)
        dt = _softplus(dt_ref[rs, :] + dtb_ref[...])
        a = dt * a_ref[...]
        fcum = _cumsum_rows(a, lower=True)
        rcum = fcum[CHUNK - 1:CHUNK, :] - fcum + a
        both = _expand(jnp.concatenate([_hi_lo(jnp.exp(fcum)),
                                        _hi_lo(jnp.exp(fcum[CHUNK - 1:CHUNK, :] - fcum) * dt)], axis=0), e_ref)
        decay_in, w_in = both[:CHUNK], both[CHUNK:]
        chunk_decay = decay_in[CHUNK - 1:CHUNK, :]
        log2_dt = jnp.log(dt) * LOG2_E
        fcol, rcol = fcum * LOG2_E, rcum * LOG2_E
        frow, rrow = (fcol - log2_dt).T, (rcol - log2_dt).T

        for g in range(N_GROUPS):
            gl = slice(g * HEADS_PER_GROUP * HEAD_DIM, (g + 1) * HEADS_PER_GROUP * HEAD_DIM)
            bg = xa_ref[rs, D_SSM + g * D_STATE:D_SSM + (g + 1) * D_STATE].astype(BF16)
            cg = xa_ref[rs, D_SSM + (N_GROUPS + g) * D_STATE:D_SSM + (N_GROUPS + g + 1) * D_STATE].astype(BF16)
            cb = lax.dot_general(cg, bg, (((1,), (1,)), ((), ())), preferred_element_type=F32)
            xw = (xa_ref[rs, gl] * w_in[:, gl]).astype(BF16)
            h_in = hst[:, gl]
            y_off = _dot(cg, h_in.astype(BF16)) * decay_in[:, gl]
            upd = lax.dot_general(bg, xw, (((0,), (0,)), ((), ())), preferred_element_type=F32)
            hst[:, gl] = h_in * chunk_decay[:, gl] + upd
            gated = []
            for pair in range(HEADS_PER_GROUP // 2):
                pl_ = slice(gl.start + pair * LANES, gl.start + (pair + 1) * LANES)
                xs = xa_ref[rs, pl_]
                x_pair = xs.astype(BF16)
                zx = jnp.zeros_like(x_pair)
                weights = []
                for sub in range(2):
                    h = g * HEADS_PER_GROUP + 2 * pair + sub
                    hb = N_HEADS + h
                    seg_f = jnp.where(lower, fcol[:, h:h + 1] - frow[h:h + 1, :], NEG_INF)
                    seg_b = jnp.where(upper, rcol[:, hb:hb + 1] - rrow[hb:hb + 1, :], NEG_INF)
                    mix = jnp.exp2(seg_f) + jnp.exp2(seg_b)
                    weights.append((cb * mix).astype(BF16))
                x2 = jnp.concatenate([jnp.where(lane < HEAD_DIM, x_pair, zx),
                                      jnp.where(lane < HEAD_DIM, zx, x_pair)], axis=0)
                y_diag = _dot(jnp.concatenate(weights, axis=1), x2)
                po = slice(pair * LANES, (pair + 1) * LANES)
                y = y_diag + y_off[:, po] + yb_ref[rs, pl_] + dsk_ref[:, pl_] * xs
                gated.append(y * _silu(z_ref[rs, pl_]))
            y_ref[rs, gl] = _rms(jnp.concatenate(gated, axis=1), og_ref[:, gl]).astype(BF16)
        return carry

    lax.fori_loop(0, n_chunks, chunk, 0, unroll=True)


def _ssd_bwd(xa, dt, dt_bias, a_neg, e_bwd):
    b, s, _ = xa.shape
    rows = SSD_STEP_CHUNKS * CHUNK
    nc = s // rows
    step = lambda w: pl.BlockSpec((None, rows, w), lambda bi, i: (bi, nc - 1 - i, 0))
    return pl.pallas_call(
        _ssd_bwd_body,
        grid=(b, nc),
        in_specs=[step(CONV_DIM), step(LANES), _const_spec(dt_bias.shape), _const_spec(a_neg.shape),
                  _const_spec(e_bwd.shape)],
        out_specs=step(D_SSM),
        out_shape=jax.ShapeDtypeStruct((b, s, D_SSM), F32),
        scratch_shapes=[pltpu.VMEM((D_STATE, D_SSM), F32)],
        compiler_params=_params("parallel", "arbitrary"),
        name="ssd_bwd",
    )(xa, dt, dt_bias, a_neg, e_bwd)


def _mixer_body(narrow_ref, q_ref, kc_ref, kp_ref, kn_ref, vc_ref, vp_ref, vn_ref, bias_ref,
                xa_ref, dt_ref, z_ref, yb_ref, dtb_ref, a_ref, dsk_ref, og_ref, e_ref,
                attn_ref, y_ref, kd, vd, tmp, o_scr, lse_scr, hst, *, seq_len):
    hp, t = pl.program_id(1), pl.program_id(2)

    @pl.when(jnp.logical_and(t == 0, hp == 0))
    def _():
        hst[...] = jnp.zeros_like(hst)

    def step(row_max):
        _attn_step(q_ref, kc_ref, kp_ref, kn_ref, vc_ref, vp_ref, vn_ref, bias_ref, attn_ref,
                   kd, vd, tmp, o_scr, lse_scr, t0=t * ATT_TILE, seq_len=seq_len, row_max=row_max)
        _ssd_fwd_chunks(MIXER_CHUNKS, xa_ref, dt_ref, z_ref, yb_ref, dtb_ref, a_ref, dsk_ref, og_ref,
                        e_ref, y_ref, hst)

    pl.when(narrow_ref[0] != 0)(functools.partial(step, False))
    pl.when(narrow_ref[0] == 0)(functools.partial(step, True))


def _mixer(q, k, v, att, xa, dt, z, yb, dt_bias, a_neg, d_skip, out_g, e_fwd):
    bias = att["bias"]
    n_pairs, b, s, _ = q.shape
    n_pat = len(ATT_PATTERNS)
    last_halo = s // ATT_REACH - 1
    halo_per_tile = ATT_TILE // ATT_REACH
    n_tiles = s // ATT_TILE
    cur = pl.BlockSpec((None, None, ATT_TILE, LANES), lambda bi, hp, t: (hp, bi, t, 0))
    prev = pl.BlockSpec((None, None, ATT_REACH, LANES),
                        lambda bi, hp, t: (hp, bi, jnp.maximum(t * halo_per_tile - 1, 0), 0))
    nxt = pl.BlockSpec((None, None, ATT_REACH, LANES),
                       lambda bi, hp, t: (hp, bi, jnp.minimum((t + 1) * halo_per_tile, last_halo), 0))
    bias_spec = pl.BlockSpec((bias.shape[0], n_pat, 2, ATT_BLOCK, ATT_KW),
                             lambda bi, hp, t: (0, 0, hp, 0, 0))
    rows = MIXER_CHUNKS * CHUNK
    scan = lambda w: pl.BlockSpec((None, rows, w), lambda bi, hp, t: (bi, hp * n_tiles + t, 0))
    consts = [_const_spec(a.shape) for a in (dt_bias, a_neg, d_skip, out_g, e_fwd)]
    return pl.pallas_call(
        functools.partial(_mixer_body, seq_len=s),
        grid=(b, n_pairs, n_tiles),
        in_specs=[pl.BlockSpec(memory_space=pltpu.SMEM), cur, cur, prev, nxt, cur, prev, nxt, bias_spec,
                  scan(CONV_DIM), scan(LANES), scan(D_SSM), scan(D_SSM)] + consts,
        out_specs=[cur, scan(D_SSM)],
        out_shape=[jax.ShapeDtypeStruct((n_pairs, b, s, LANES), BF16),
                   jax.ShapeDtypeStruct((b, s, D_SSM), BF16)],
        scratch_shapes=[pltpu.VMEM((n_pat, ATT_WINDOW, LANES), BF16),
                        pltpu.VMEM((n_pat, ATT_WINDOW, LANES), BF16),
                        pltpu.VMEM((ATT_STAGE, ATT_WINDOW // ATT_STAGE, LANES), F32),
                        pltpu.VMEM((n_pat - 1, ATT_TILE, LANES), F32),
                        pltpu.VMEM((n_pat - 1, ATT_TILE, LANES), F32),
                        pltpu.VMEM((D_STATE, D_SSM), F32)],
        compiler_params=_params("parallel", "arbitrary", "arbitrary"),
        name="mixer",
    )(att["narrow"], q, k, k, k, v, v, v, bias, xa, dt, z, yb, dt_bias, a_neg, d_skip, out_g, e_fwd)


def _outproj_ffn_body(x_ref, a_ref, y_ref, g_ref, wa_ref, ws_ref, g2_ref, wg_ref, wu_ref, wd_ref,
                      out_ref):
    attn = jnp.concatenate([a_ref[hp] for hp in range(a_ref.shape[0])], axis=1)
    attn = _rms(attn.astype(F32), g_ref[...]).astype(BF16)
    x2 = x_ref[...] + _dot(attn, wa_ref[...]) + _dot(y_ref[...], ws_ref[...])
    out_ref[...] = _half_step_ffn(x2, g2_ref, wg_ref, wu_ref, wd_ref)


def _outproj_ffn(x, attn, y, g, wa, ws, g2, wg, wu, wd, tm):
    n = x.shape[0]
    row = lambda w: pl.BlockSpec((tm, w), lambda i: (i, 0))
    sq = _const_spec((D_ATT, D_MODEL))
    pairs = pl.BlockSpec((attn.shape[0], tm, LANES), lambda i: (0, i, 0))
    return pl.pallas_call(
        _outproj_ffn_body,
        grid=(n // tm,),
        in_specs=[row(D_MODEL), pairs, row(D_SSM), _const_spec((1, D_ATT)), sq, sq,
                  _const_spec((1, D_MODEL)), _const_spec((D_MODEL, D_FF)), _const_spec((D_MODEL, D_FF)),
                  _const_spec((D_FF, D_MODEL))],
        out_specs=row(D_MODEL),
        out_shape=jax.ShapeDtypeStruct((n, D_MODEL), F32),
        compiler_params=_params("parallel"),
        name="outproj_ffn",
    )(x, attn, y, g, wa, ws, g2, wg, wu, wd)


def _t5_bucket(rel):
    nb = N_REL_BUCKETS // 2
    max_exact = nb // 2
    n = np.abs(rel)
    large = max_exact + (np.log(np.maximum(n, 1) / max_exact)
                         / math.log(REL_MAX_DIST / max_exact) * (nb - max_exact)).astype(np.int32)
    large = np.minimum(large, nb - 1)
    return (np.where(rel > 0, nb, 0) + np.where(n < max_exact, n, large)).astype(np.int32)


def _bias_tables(rel_bias, shift):
    rel_sub = np.arange(ATT_KW)[None, :] - ATT_HALF - np.arange(ATT_BLOCK)[:, None]
    in_window = np.abs(rel_sub) <= ATT_HALF
    col = np.arange(ATT_KW)[None, :]
    tables = []
    for _, dil in ATT_PATTERNS:
        onehot = (_t5_bucket(rel_sub * dil)[:, :, None] == np.arange(N_REL_BUCKETS)).astype(np.float32)
        bias = jnp.einsum("ijk,kh->hij", jnp.asarray(onehot, BF16).astype(F32), rel_bias.astype(F32),
                          precision=lax.Precision.HIGHEST)
        tables.append(bias * LOG2_E - shift)
    bias = jnp.stack(tables, axis=0)
    variants = []
    for v in range(4):
        keep = in_window & ((col >= ATT_HALF) | (v & 1 == 0)) & ((col < ATT_BLOCK + ATT_HALF) | (v & 2 == 0))
        variants.append(jnp.where(keep[None, None], bias, NEG_INF))
    return jnp.stack(variants, axis=0)


def _attention_tables(rel_bias, q_norm_g, k_norm_g):
    logit_bound = (QK_BOUND_SLACK * HEAD_DIM * HEAD_DIM ** -0.5 * LOG2_E
                   * jnp.max(jnp.abs(q_norm_g)) * jnp.max(jnp.abs(k_norm_g))).astype(F32)
    bias_lo = (jnp.min(rel_bias) * LOG2_E).astype(F32)
    bias_hi = (jnp.max(rel_bias) * LOG2_E).astype(F32)
    narrow = 2.0 * logit_bound + (bias_hi - bias_lo) < EXP2_SAFE_RANGE
    return {
        "bias": _bias_tables(rel_bias, jnp.where(narrow, logit_bound + bias_hi, 0.0)),
        "narrow": narrow.astype(jnp.int32).reshape(1),
    }


def _head_expander(first_row, width):
    e = np.zeros((2 * LANES, width), np.float32)
    for h in range(width // HEAD_DIM):
        e[first_row + h, h * HEAD_DIM:(h + 1) * HEAD_DIM] = 1.0
        e[LANES + first_row + h, h * HEAD_DIM:(h + 1) * HEAD_DIM] = 1.0
    return jnp.asarray(e, BF16)


def _block_diag_mean(width):
    i = np.arange(width)
    return jnp.asarray((i[:, None] // HEAD_DIM == i[None, :] // HEAD_DIM) / HEAD_DIM, BF16)


def _layer(x, p, tm):
    b, s, _ = x.shape
    n = b * s
    xf = x.reshape(n, D_MODEL)
    x1 = _ffn(xf, p["ffn1_g"], p["ffn1_wg"], p["ffn1_wu"], p["ffn1_wd"], tm)
    q, k, v, z, xa, dt = _inproj(x1, p["mix_g"], p["wq"], p["wk"], p["wv"], p["wz"], p["wx"],
                                 p["wdt"], p["qg"], p["kg"], p["bd"], p["conv_w"], p["conv_b"], s, tm)
    seq = lambda a: a.reshape(b, s, a.shape[-1])
    pair_seq = lambda a: a.reshape(a.shape[0], b, s, LANES)
    yb = _ssd_bwd(seq(xa), seq(dt), p["dt_bias"], p["a_neg"], p["e_bwd"])
    attn, y = _mixer(pair_seq(q), pair_seq(k), pair_seq(v), p["att"], seq(xa), seq(dt), seq(z), yb,
                     p["dt_bias"], p["a_neg"], p["d_skip"], p["ssm_g"], p["e_fwd"])
    out = _outproj_ffn(x1, attn.reshape(attn.shape[0], n, LANES), y.reshape(n, D_SSM), p["attn_g"],
                       p["wo_att"], p["wo_ssm"], p["ffn2_g"], p["ffn2_wg"], p["ffn2_wu"],
                       p["ffn2_wd"], tm)
    return out.reshape(b, s, D_MODEL)


def _prepare(rel_bias, ffn1_norm_g, ffn1_w_gate, ffn1_w_up, ffn1_w_down, mix_norm_g, w_in,
             q_norm_g, k_norm_g, attn_out_g, conv_w, conv_b, dt_bias, a_log, d_skip, ssm_out_g,
             w_out, ffn2_norm_g, ffn2_w_gate, ffn2_w_up, ffn2_w_down):
    row = lambda a: a.reshape(1, -1).astype(F32)
    w16 = lambda a: a.astype(BF16)
    c0, c1, c2, c3 = D_ATT, 2 * D_ATT, 3 * D_ATT, 3 * D_ATT + D_SSM
    c4 = c3 + CONV_DIM
    pad32 = lambda a: jnp.pad(a.reshape(1, 2 * N_HEADS).astype(F32), ((0, 0), (0, LANES - 2 * N_HEADS)))
    return {
        "ffn1_g": row(ffn1_norm_g), "ffn1_wg": w16(ffn1_w_gate), "ffn1_wu": w16(ffn1_w_up),
        "ffn1_wd": w16(ffn1_w_down),
        "ffn2_g": row(ffn2_norm_g), "ffn2_wg": w16(ffn2_w_gate), "ffn2_wu": w16(ffn2_w_up),
        "ffn2_wd": w16(ffn2_w_down),
        "mix_g": row(mix_norm_g),
        "wq": w16(w_in[:, :c0]), "wk": w16(w_in[:, c0:c1]), "wv": w16(w_in[:, c1:c2]),
        "wz": w16(w_in[:, c2:c3]), "wx": w16(w_in[:, c3:c4]),
        "wdt": w16(jnp.pad(w_in[:, c4:], ((0, 0), (0, LANES - 2 * N_HEADS)))),
        "qg": row(jnp.tile(q_norm_g, N_HEADS)) * (HEAD_DIM ** -0.5 * LOG2_E),
        "kg": row(jnp.tile(k_norm_g, N_HEADS)),
        "bd": _block_diag_mean(MXU_TILE),
        "att": _attention_tables(rel_bias, q_norm_g, k_norm_g),
        "attn_g": row(attn_out_g),
        "conv_w": conv_w.astype(F32), "conv_b": row(conv_b),
        "dt_bias": pad32(dt_bias), "a_neg": pad32(-jnp.exp(a_log.astype(F32))),
        "d_skip": row(jnp.repeat(d_skip, HEAD_DIM)), "ssm_g": row(ssm_out_g),
        "e_fwd": _head_expander(0, D_SSM), "e_bwd": _head_expander(N_HEADS, D_SSM),
        "wo_att": w16(w_out[:D_ATT]), "wo_ssm": w16(w_out[D_ATT:]),
    }


def _trunk(x, layers, tm=512):
    for p in layers:
        x = _layer(x, p, tm)
    return x


def kernel(x_prompt, x_sample, rel_bias, ffn1_norm_g, ffn1_w_gate, ffn1_w_up, ffn1_w_down, mix_norm_g, w_in, q_norm_g, k_norm_g, attn_out_g, conv_w, conv_b, dt_bias, a_log, d_skip, ssm_out_g, w_out, ffn2_norm_g, ffn2_w_gate, ffn2_w_up, ffn2_w_down):
    per_layer = (ffn1_norm_g, ffn1_w_gate, ffn1_w_up, ffn1_w_down, mix_norm_g, w_in, q_norm_g,
                 k_norm_g, attn_out_g, conv_w, conv_b, dt_bias, a_log, d_skip, ssm_out_g, w_out,
                 ffn2_norm_g, ffn2_w_gate, ffn2_w_up, ffn2_w_down)
    layers = [_prepare(rel_bias, *(a[l] for a in per_layer)) for l in range(ffn1_norm_g.shape[0])]
    return (_trunk(x_prompt, layers), _trunk(x_sample, layers))
```

```python
import functools
import math

import numpy as np
import jax
import jax.numpy as jnp
from jax import lax
from jax.experimental import pallas as pl
from jax.experimental.pallas import tpu as pltpu

D_MODEL = 1024
D_ATT = 1024
D_SSM = 1024
HEAD_DIM = 64
N_HEADS = 16
ATT_PATTERNS = ((128, 1), (512, 4), (2048, 16))
ATT_BLOCK = 128
ATT_HALF = 64
ATT_KW = ATT_BLOCK + 2 * ATT_HALF
ATT_MAX_DIL = max(d for _, d in ATT_PATTERNS)
ATT_TILE = ATT_BLOCK * ATT_MAX_DIL
ATT_REACH = ATT_HALF * ATT_MAX_DIL
ATT_WINDOW = ATT_TILE + 2 * ATT_REACH
ATT_STAGE = 4
assert tuple(d for _, d in ATT_PATTERNS) == (1, ATT_STAGE, ATT_STAGE * ATT_STAGE)
N_REL_BUCKETS = 32
REL_MAX_DIST = 1024
N_GROUPS = 4
HEADS_PER_GROUP = 4
D_STATE = 128
D_CONV = 5
CHUNK = 128
SSD_STEP_CHUNKS = 16
MIXER_CHUNKS = ATT_TILE // CHUNK // (N_HEADS // 2)
CONV_DIM = D_SSM + 2 * N_GROUPS * D_STATE
D_FF = 2816
EPS = 1e-6
NEG_INF = -1e30
LOG2_E = 1.4426950408889634
EXP2_SAFE_RANGE = 100.0
QK_BOUND_SLACK = 1.05

LANES = 128
MXU_TILE = 256
HALO_ROWS = 8
VMEM_LIMIT = 56 * 1024 * 1024

F32 = jnp.float32
BF16 = jnp.bfloat16


def _params(*sem):
    return pltpu.CompilerParams(dimension_semantics=sem, vmem_limit_bytes=VMEM_LIMIT)


def _const_spec(shape):
    n = len(shape)
    return pl.BlockSpec(shape, lambda *_: (0,) * n, pipeline_mode=pl.Buffered(1))


def _rms(x, g):
    ms = jnp.mean(x * x, axis=-1, keepdims=True)
    return x * lax.rsqrt(ms + EPS) * g


def _silu(x):
    h = 0.5 * x
    return h + h * jnp.tanh(h)


def _dot(a, b):
    return jnp.dot(a, b, preferred_element_type=F32)


def _hi_lo(vals):
    hi = vals.astype(BF16)
    lo = (vals - hi.astype(F32)).astype(BF16)
    return jnp.concatenate([hi, lo], axis=1)


def _expand(split, e_ref, cols=slice(None)):
    return _dot(split, e_ref[:, cols])


def _half_step_ffn(x, g_ref, wg_ref, wu_ref, wd_ref):
    h = _rms(x, g_ref[...]).astype(BF16)
    gate = _dot(h, wg_ref[...])
    up = _dot(h, wu_ref[...])
    act = (_silu(gate) * up).astype(BF16)
    return x + 0.5 * _dot(act, wd_ref[...])


def _ffn_body(x_ref, g_ref, wg_ref, wu_ref, wd_ref, o_ref):
    o_ref[...] = _half_step_ffn(x_ref[...], g_ref, wg_ref, wu_ref, wd_ref)


def _ffn(x, g, wg, wu, wd, tm):
    n = x.shape[0]
    row = pl.BlockSpec((tm, D_MODEL), lambda i: (i, 0))
    return pl.pallas_call(
        _ffn_body,
        grid=(n // tm,),
        in_specs=[row, _const_spec((1, D_MODEL)), _const_spec((D_MODEL, D_FF)),
                  _const_spec((D_MODEL, D_FF)), _const_spec((D_FF, D_MODEL))],
        out_specs=row,
        out_shape=jax.ShapeDtypeStruct((n, D_MODEL), F32),
        compiler_params=_params("parallel"),
        name="ffn",
    )(x, g, wg, wu, wd)


def _inproj_body(x_ref, xp_ref, xn_ref, g_ref, wq_ref, wk_ref, wv_ref, wz_ref, wx_ref, wdt_ref,
                 qg_ref, kg_ref, bd_ref, cw_ref, cb_ref, q_ref, k_ref, v_ref, z_ref, xa_ref, dt_ref, ext,
                 *, tiles_per_seq):
    h = _rms(x_ref[...], g_ref[...]).astype(BF16)

    def head_norm(t, gain):
        t2 = (t * t).astype(BF16)
        w = bd_ref.shape[0]
        ms = jnp.concatenate([_dot(t2[:, j * w:(j + 1) * w], bd_ref[...])
                              for j in range(D_ATT // w)], axis=1)
        return t * lax.rsqrt(ms + EPS) * gain

    def put_pairs(ref, val):
        for hp in range(N_HEADS // 2):
            ref[hp] = val[:, hp * LANES:(hp + 1) * LANES]

    put_pairs(q_ref, head_norm(_dot(h, wq_ref[...]), qg_ref[...]))
    put_pairs(k_ref, head_norm(_dot(h, wk_ref[...]), kg_ref[...]))
    put_pairs(v_ref, _dot(h, wv_ref[...]))
    z_ref[...] = _dot(h, wz_ref[...])
    dt_ref[...] = _dot(h, wdt_ref[...])

    tile = pl.program_id(0) % tiles_per_seq
    halo = jnp.concatenate([xp_ref[...], xn_ref[...]], axis=0)
    xbc_halo = _dot(_rms(halo, g_ref[...]).astype(BF16), wx_ref[...])
    xbc = _dot(h, wx_ref[...])
    rows = xbc.shape[0]
    pad = D_CONV // 2
    for sl in range(CONV_DIM // LANES):
        lanes = slice(sl * LANES, (sl + 1) * LANES)
        ext[sl, 0:HALO_ROWS, :] = jnp.where(tile > 0, xbc_halo[0:HALO_ROWS, lanes], 0.0)
        ext[sl, HALO_ROWS:HALO_ROWS + rows, :] = xbc[:, lanes]
        ext[sl, HALO_ROWS + rows:, :] = jnp.where(tile < tiles_per_seq - 1, xbc_halo[HALO_ROWS:, lanes], 0.0)
        acc = cb_ref[:, lanes]
        for j in range(D_CONV):
            acc = acc + ext[sl, pl.ds(HALO_ROWS - pad + j, rows, stride=1), :] * cw_ref[j:j + 1, lanes]
        xa_ref[:, lanes] = _silu(acc)


def _inproj(x, g, wq, wk, wv, wz, wx, wdt, qg, kg, bd, conv_w, conv_b, s, tm):
    n = x.shape[0]
    halos_per_tile = tm // HALO_ROWS
    halo_prev = pl.BlockSpec((HALO_ROWS, D_MODEL), lambda i: (jnp.maximum(i * halos_per_tile - 1, 0), 0))
    halo_next = pl.BlockSpec((HALO_ROWS, D_MODEL),
                             lambda i: (jnp.minimum((i + 1) * halos_per_tile, n // HALO_ROWS - 1), 0))
    row = lambda w: pl.BlockSpec((tm, w), lambda i: (i, 0))
    sq = _const_spec((D_MODEL, D_ATT))
    out = lambda w: jax.ShapeDtypeStruct((n, w), F32)
    pairs = pl.BlockSpec((N_HEADS // 2, tm, LANES), lambda i: (0, i, 0))
    pairs_out = jax.ShapeDtypeStruct((N_HEADS // 2, n, LANES), F32)
    return pl.pallas_call(
        functools.partial(_inproj_body, tiles_per_seq=s // tm),
        grid=(n // tm,),
        in_specs=[row(D_MODEL), halo_prev, halo_next, _const_spec((1, D_MODEL)), sq, sq, sq, sq,
                  _const_spec((D_MODEL, CONV_DIM)), _const_spec((D_MODEL, LANES)),
                  _const_spec((1, D_ATT)), _const_spec((1, D_ATT)), _const_spec(bd.shape),
                  _const_spec(conv_w.shape), _const_spec(conv_b.shape)],
        out_specs=[pairs, pairs, pairs, row(D_SSM), row(CONV_DIM), row(LANES)],
        out_shape=[pairs_out, pairs_out, pairs_out, out(D_SSM), out(CONV_DIM), out(LANES)],
        scratch_shapes=[pltpu.VMEM((CONV_DIM // LANES, tm + 2 * HALO_ROWS, LANES), F32)],
        compiler_params=_params("parallel"),
        name="inproj",
    )(x, x, x, g, wq, wk, wv, wz, wx, wdt, qg, kg, bd, conv_w, conv_b)


def _rows(start, size, dil):
    return pl.ds(start, size) if dil == 1 else pl.ds(start, size, stride=dil)


def _regroup_keys(prev, cur, nxt, tmp, dst):
    s4 = ATT_STAGE
    lo, hi = ATT_REACH - ATT_HALF, ATT_REACH + ATT_TILE + ATT_HALF
    dst[0, lo:ATT_REACH, :] = prev[lo:ATT_REACH, :].astype(BF16)
    dst[0, ATT_REACH:ATT_REACH + ATT_TILE, :] = cur[...].astype(BF16)
    dst[0, ATT_REACH + ATT_TILE:hi, :] = nxt[0:ATT_HALF, :].astype(BF16)
    h4, t4 = ATT_REACH // s4, ATT_TILE // s4
    for r in range(s4):
        tmp[r, 0:h4, :] = prev[pl.ds(r, h4, stride=s4), :]
        tmp[r, h4:h4 + t4, :] = cur[pl.ds(r, t4, stride=s4), :]
        tmp[r, h4 + t4:, :] = nxt[pl.ds(r, h4, stride=s4), :]
    lo, hi = h4 - ATT_HALF, h4 + t4 + ATT_HALF
    for r in range(s4):
        dst[1, r * (ATT_WINDOW // s4) + lo:r * (ATT_WINDOW // s4) + hi, :] = tmp[r, lo:hi, :].astype(BF16)
    per = ATT_WINDOW // (s4 * s4)
    for r in range(s4 * s4):
        dst[2, r * per:(r + 1) * per, :] = tmp[r % s4, pl.ds(r // s4, per, stride=s4), :].astype(BF16)


def _attn_step(q_ref, kc_ref, kp_ref, kn_ref, vc_ref, vp_ref, vn_ref, bias_ref, out_ref,
               kd, vd, tmp, o_scr, lse_scr, *, t0, seq_len, row_max):
    _regroup_keys(kp_ref, kc_ref, kn_ref, tmp, kd)
    _regroup_keys(vp_ref, vc_ref, vn_ref, tmp, vd)

    first_head = lax.broadcasted_iota(jnp.int32, (ATT_BLOCK, LANES), 1) < HEAD_DIM
    first_head_kw = lax.broadcasted_iota(jnp.int32, (ATT_KW, LANES), 1) < HEAD_DIM
    head_ones = (first_head_kw.astype(BF16), (~first_head_kw).astype(BF16))

    def block(p, dil, r, j):
        blocks_per_residue = ATT_TILE // (dil * ATT_BLOCK)
        q0 = j * (ATT_BLOCK * dil) + r
        k0 = r * (ATT_WINDOW // dil) + ATT_REACH // dil - ATT_HALF + j * ATT_BLOCK
        variant = 0
        if j == 0:
            variant = variant + (t0 == 0).astype(jnp.int32)
        if j == blocks_per_residue - 1:
            variant = variant + 2 * (t0 + ATT_TILE == seq_len).astype(jnp.int32)
        qp = q_ref[_rows(q0, ATT_BLOCK, dil), :].astype(BF16)
        kp = kd[p, k0:k0 + ATT_KW, :]
        vp = vd[p, k0:k0 + ATT_KW, :]
        zq, zv = jnp.zeros_like(qp), jnp.zeros_like(vp)
        q2 = jnp.concatenate([jnp.where(first_head, qp, zq), jnp.where(first_head, zq, qp)], axis=0)
        s2 = lax.dot_general(q2, kp, (((1,), (1,)), ((), ())), preferred_element_type=F32)
        es, ms = [], []
        for sub in range(2):
            s = s2[sub * ATT_BLOCK:(sub + 1) * ATT_BLOCK, :] + bias_ref[variant, p, sub]
            if row_max:
                m = jnp.max(s, axis=-1, keepdims=True)
                s = s - m
                ms.append(m)
            es.append(jnp.exp2(s).astype(BF16))
        v2 = jnp.concatenate([
            jnp.concatenate([jnp.where(first_head_kw, vp, zv), head_ones[0]], axis=1),
            jnp.concatenate([jnp.where(first_head_kw, zv, vp), head_ones[1]], axis=1)], axis=0)
        ol = _dot(jnp.concatenate(es, axis=1), v2)
        l = ol[:, LANES:]
        if not row_max:
            return ol[:, :LANES], l
        return ol[:, :LANES] * (1.0 / l), jnp.where(first_head, ms[0], ms[1]) + jnp.log2(l)

    for p in range(len(ATT_PATTERNS) - 1, 0, -1):
        dil = ATT_PATTERNS[p][1]
        blocks_per_residue = ATT_TILE // (dil * ATT_BLOCK)
        for idx in range(ATT_TILE // ATT_BLOCK):
            r, j = idx // blocks_per_residue, idx % blocks_per_residue
            rows = _rows(j * (ATT_BLOCK * dil) + r, ATT_BLOCK, dil)
            o_scr[p - 1, rows, :], lse_scr[p - 1, rows, :] = block(p, dil, r, j)
    for j in range(ATT_TILE // ATT_BLOCK):
        rows = pl.ds(j * ATT_BLOCK, ATT_BLOCK)
        o1, l1 = block(0, 1, 0, j)
        l2, l3 = lse_scr[0, rows, :], lse_scr[1, rows, :]
        if not row_max:
            mixed = o1 + o_scr[0, rows, :] + o_scr[1, rows, :]
            out_ref[rows, :] = (mixed * (1.0 / (l1 + l2 + l3))).astype(BF16)
            continue
        m = jnp.maximum(jnp.maximum(l1, l2), l3)
        e1, e2, e3 = jnp.exp2(l1 - m), jnp.exp2(l2 - m), jnp.exp2(l3 - m)
        mixed = e1 * o1 + e2 * o_scr[0, rows, :] + e3 * o_scr[1, rows, :]
        out_ref[rows, :] = (mixed * (1.0 / (e1 + e2 + e3))).astype(BF16)


def _softplus(x):
    return jnp.maximum(x, 0.0) + jnp.log(1.0 + jnp.exp(-jnp.abs(x)))


def _tri(lower):
    r = lax.broadcasted_iota(jnp.int32, (CHUNK, CHUNK), 0)
    c = lax.broadcasted_iota(jnp.int32, (CHUNK, CHUNK), 1)
    return (r >= c) if lower else (r <= c)


def _cumsum_rows(a, lower):
    a1 = a.astype(BF16)
    r1 = a - a1.astype(F32)
    a2 = r1.astype(BF16)
    a3 = (r1 - a2.astype(F32)).astype(BF16)
    tri = _tri(lower).astype(BF16)
    return _dot(jnp.concatenate([tri, tri, tri], axis=1), jnp.concatenate([a1, a2, a3], axis=0))


def _chunk_rows(i):
    return pl.ds(pl.multiple_of(i * CHUNK, CHUNK), CHUNK)


def _ssd_bwd_body(xa_ref, dt_ref, dtb_ref, a_ref, e_ref, yoff_ref, hst):
    @pl.when(pl.program_id(1) == 0)
    def _():
        hst[...] = jnp.zeros_like(hst)

    def chunk(ci, carry):
        rs = _chunk_rows(SSD_STEP_CHUNKS - 1 - ci)
        dt = _softplus(dt_ref[rs, :] + dtb_ref[...])
        rcum = _cumsum_rows(dt * a_ref[...], lower=False)
        both = _expand(jnp.concatenate([_hi_lo(jnp.exp(rcum)),
                                        _hi_lo(jnp.exp(rcum[0:1, :] - rcum) * dt)], axis=0), e_ref)
        decay_in, w_in = both[:CHUNK], both[CHUNK:]
        chunk_decay = decay_in[0:1, :]
        for g in range(N_GROUPS):
            gl = slice(g * HEADS_PER_GROUP * HEAD_DIM, (g + 1) * HEADS_PER_GROUP * HEAD_DIM)
            bg = xa_ref[rs, D_SSM + g * D_STATE:D_SSM + (g + 1) * D_STATE].astype(BF16)
            cg = xa_ref[rs, D_SSM + (N_GROUPS + g) * D_STATE:D_SSM + (N_GROUPS + g + 1) * D_STATE].astype(BF16)
            xw = (xa_ref[rs, gl] * w_in[:, gl]).astype(BF16)
            h_in = hst[:, gl]
            yoff_ref[rs, gl] = _dot(cg, h_in.astype(BF16)) * decay_in[:, gl]
            upd = lax.dot_general(bg, xw, (((0,), (0,)), ((), ())), preferred_element_type=F32)
            hst[:, gl] = h_in * chunk_decay[:, gl] + upd
        return carry

    lax.fori_loop(0, SSD_STEP_CHUNKS, chunk, 0, unroll=True)


def _ssd_fwd_chunks(n_chunks, xa_ref, dt_ref, z_ref, yb_ref, dtb_ref, a_ref, dsk_ref, og_ref, e_ref,
                    y_ref, hst):
    lower, upper = _tri(True), _tri(False)
    lane = lax.broadcasted_iota(jnp.int32, (CHUNK, LANES), 1)

    def chunk(ci, carry):
        rs = _chunk_rows(ci)
        dt = _softplus(dt_ref[rs, :] + dtb_ref[...])
        a = dt * a_ref[...]
        fcum = _cumsum_rows(a, lower=True)
        rcum = fcum[CHUNK - 1:CHUNK, :] - fcum + a
        both = _expand(jnp.concatenate([_hi_lo(jnp.exp(fcum)),
                                        _hi_lo(jnp.exp(fcum[CHUNK - 1:CHUNK, :] - fcum) * dt)], axis=0), e_ref)
        decay_in, w_in = both[:CHUNK], both[CHUNK:]
        chunk_decay = decay_in[CHUNK - 1:CHUNK, :]
        log2_dt = jnp.log(dt) * LOG2_E
        fcol, rcol = fcum * LOG2_E, rcum * LOG2_E
        frow, rrow = (fcol - log2_dt).T, (rcol - log2_dt).T

        for g in range(N_GROUPS):
            gl = slice(g * HEADS_PER_GROUP * HEAD_DIM, (g + 1) * HEADS_PER_GROUP * HEAD_DIM)
            bg = xa_ref[rs, D_SSM + g * D_STATE:D_SSM + (g + 1) * D_STATE].astype(BF16)
            cg = xa_ref[rs, D_SSM + (N_GROUPS + g) * D_STATE:D_SSM + (N_GROUPS + g + 1) * D_STATE].astype(BF16)
            cb = lax.dot_general(cg, bg, (((1,), (1,)), ((), ())), preferred_element_type=F32)
            xw = (xa_ref[rs, gl] * w_in[:, gl]).astype(BF16)
            h_in = hst[:, gl]
            y_off = _dot(cg, h_in.astype(BF16)) * decay_in[:, gl]
            upd = lax.dot_general(bg, xw, (((0,), (0,)), ((), ())), preferred_element_type=F32)
            hst[:, gl] = h_in * chunk_decay[:, gl] + upd
            gated = []
            for pair in range(HEADS_PER_GROUP // 2):
                pl_ = slice(gl.start + pair * LANES, gl.start + (pair + 1) * LANES)
                xs = xa_ref[rs, pl_]
                x_pair = xs.astype(BF16)
                zx = jnp.zeros_like(x_pair)
                weights = []
                for sub in range(2):
                    h = g * HEADS_PER_GROUP + 2 * pair + sub
                    hb = N_HEADS + h
                    seg_f = jnp.where(lower, fcol[:, h:h + 1] - frow[h:h + 1, :], NEG_INF)
                    seg_b = jnp.where(upper, rcol[:, hb:hb + 1] - rrow[hb:hb + 1, :], NEG_INF)
                    mix = jnp.exp2(seg_f) + jnp.exp2(seg_b)
                    weights.append((cb * mix).astype(BF16))
                x2 = jnp.concatenate([jnp.where(lane < HEAD_DIM, x_pair, zx),
                                      jnp.where(lane < HEAD_DIM, zx, x_pair)], axis=0)
                y_diag = _dot(jnp.concatenate(weights, axis=1), x2)
                po = slice(pair * LANES, (pair + 1) * LANES)
                y = y_diag + y_off[:, po] + yb_ref[rs, pl_] + dsk_ref[:, pl_] * xs
                gated.append(y * _silu(z_ref[rs, pl_]))
            y_ref[rs, gl] = _rms(jnp.concatenate(gated, axis=1), og_ref[:, gl]).astype(BF16)
        return carry

    lax.fori_loop(0, n_chunks, chunk, 0, unroll=True)


def _ssd_bwd(xa, dt, dt_bias, a_neg, e_bwd):
    b, s, _ = xa.shape
    rows = SSD_STEP_CHUNKS * CHUNK
    nc = s // rows
    step = lambda w: pl.BlockSpec((None, rows, w), lambda bi, i: (bi, nc - 1 - i, 0))
    return pl.pallas_call(
        _ssd_bwd_body,
        grid=(b, nc),
        in_specs=[step(CONV_DIM), step(LANES), _const_spec(dt_bias.shape), _const_spec(a_neg.shape),
                  _const_spec(e_bwd.shape)],
        out_specs=step(D_SSM),
        out_shape=jax.ShapeDtypeStruct((b, s, D_SSM), F32),
        scratch_shapes=[pltpu.VMEM((D_STATE, D_SSM), F32)],
        compiler_params=_params("parallel", "arbitrary"),
        name="ssd_bwd",
    )(xa, dt, dt_bias, a_neg, e_bwd)


def _mixer_body(narrow_ref, q_ref, kc_ref, kp_ref, kn_ref, vc_ref, vp_ref, vn_ref, bias_ref,
                xa_ref, dt_ref, z_ref, yb_ref, dtb_ref, a_ref, dsk_ref, og_ref, e_ref,
                attn_ref, y_ref, kd, vd, tmp, o_scr, lse_scr, hst, *, seq_len):
    hp, t = pl.program_id(1), pl.program_id(2)

    @pl.when(jnp.logical_and(t == 0, hp == 0))
    def _():
        hst[...] = jnp.zeros_like(hst)

    def step(row_max):
        _attn_step(q_ref, kc_ref, kp_ref, kn_ref, vc_ref, vp_ref, vn_ref, bias_ref, attn_ref,
                   kd, vd, tmp, o_scr, lse_scr, t0=t * ATT_TILE, seq_len=seq_len, row_max=row_max)
        _ssd_fwd_chunks(MIXER_CHUNKS, xa_ref, dt_ref, z_ref, yb_ref, dtb_ref, a_ref, dsk_ref, og_ref,
                        e_ref, y_ref, hst)

    pl.when(narrow_ref[0] != 0)(functools.partial(step, False))
    pl.when(narrow_ref[0] == 0)(functools.partial(step, True))


def _mixer(q, k, v, att, xa, dt, z, yb, dt_bias, a_neg, d_skip, out_g, e_fwd):
    bias = att["bias"]
    n_pairs, b, s, _ = q.shape
    n_pat = len(ATT_PATTERNS)
    last_halo = s // ATT_REACH - 1
    halo_per_tile = ATT_TILE // ATT_REACH
    n_tiles = s // ATT_TILE
    cur = pl.BlockSpec((None, None, ATT_TILE, LANES), lambda bi, hp, t: (hp, bi, t, 0))
    prev = pl.BlockSpec((None, None, ATT_REACH, LANES),
                        lambda bi, hp, t: (hp, bi, jnp.maximum(t * halo_per_tile - 1, 0), 0))
    nxt = pl.BlockSpec((None, None, ATT_REACH, LANES),
                       lambda bi, hp, t: (hp, bi, jnp.minimum((t + 1) * halo_per_tile, last_halo), 0))
    bias_spec = pl.BlockSpec((bias.shape[0], n_pat, 2, ATT_BLOCK, ATT_KW),
                             lambda bi, hp, t: (0, 0, hp, 0, 0))
    rows = MIXER_CHUNKS * CHUNK
    scan = lambda w: pl.BlockSpec((None, rows, w), lambda bi, hp, t: (bi, hp * n_tiles + t, 0))
    consts = [_const_spec(a.shape) for a in (dt_bias, a_neg, d_skip, out_g, e_fwd)]
    return pl.pallas_call(
        functools.partial(_mixer_body, seq_len=s),
        grid=(b, n_pairs, n_tiles),
        in_specs=[pl.BlockSpec(memory_space=pltpu.SMEM), cur, cur, prev, nxt, cur, prev, nxt, bias_spec,
                  scan(CONV_DIM), scan(LANES), scan(D_SSM), scan(D_SSM)] + consts,
        out_specs=[cur, scan(D_SSM)],
        out_shape=[jax.ShapeDtypeStruct((n_pairs, b, s, LANES), BF16),
                   jax.ShapeDtypeStruct((b, s, D_SSM), BF16)],
        scratch_shapes=[pltpu.VMEM((n_pat, ATT_WINDOW, LANES), BF16),
                        pltpu.VMEM((n_pat, ATT_WINDOW, LANES), BF16),
                        pltpu.VMEM((ATT_STAGE, ATT_WINDOW // ATT_STAGE, LANES), F32),
                        pltpu.VMEM((n_pat - 1, ATT_TILE, LANES), F32),
                        pltpu.VMEM((n_pat - 1, ATT_TILE, LANES), F32),
                        pltpu.VMEM((D_STATE, D_SSM), F32)],
        compiler_params=_params("parallel", "arbitrary", "arbitrary"),
        name="mixer",
    )(att["narrow"], q, k, k, k, v, v, v, bias, xa, dt, z, yb, dt_bias, a_neg, d_skip, out_g, e_fwd)


def _outproj_ffn_body(x_ref, a_ref, y_ref, g_ref, wa_ref, ws_ref, g2_ref, wg_ref, wu_ref, wd_ref,
                      out_ref):
    attn = jnp.concatenate([a_ref[hp] for hp in range(a_ref.shape[0])], axis=1)
    attn = _rms(attn.astype(F32), g_ref[...]).astype(BF16)
    x2 = x_ref[...] + _dot(attn, wa_ref[...]) + _dot(y_ref[...], ws_ref[...])
    out_ref[...] = _half_step_ffn(x2, g2_ref, wg_ref, wu_ref, wd_ref)


def _outproj_ffn(x, attn, y, g, wa, ws, g2, wg, wu, wd, tm):
    n = x.shape[0]
    row = lambda w: pl.BlockSpec((tm, w), lambda i: (i, 0))
    sq = _const_spec((D_ATT, D_MODEL))
    pairs = pl.BlockSpec((attn.shape[0], tm, LANES), lambda i: (0, i, 0))
    return pl.pallas_call(
        _outproj_ffn_body,
        grid=(n // tm,),
        in_specs=[row(D_MODEL), pairs, row(D_SSM), _const_spec((1, D_ATT)), sq, sq,
                  _const_spec((1, D_MODEL)), _const_spec((D_MODEL, D_FF)), _const_spec((D_MODEL, D_FF)),
                  _const_spec((D_FF, D_MODEL))],
        out_specs=row(D_MODEL),
        out_shape=jax.ShapeDtypeStruct((n, D_MODEL), F32),
        compiler_params=_params("parallel"),
        name="outproj_ffn",
    )(x, attn, y, g, wa, ws, g2, wg, wu, wd)


def _t5_bucket(rel):
    nb = N_REL_BUCKETS // 2
    max_exact = nb // 2
    n = np.abs(rel)
    large = max_exact + (np.log(np.maximum(n, 1) / max_exact)
                         / math.log(REL_MAX_DIST / max_exact) * (nb - max_exact)).astype(np.int32)
    large = np.minimum(large, nb - 1)
    return (np.where(rel > 0, nb, 0) + np.where(n < max_exact, n, large)).astype(np.int32)


def _bias_tables(rel_bias, shift):
    rel_sub = np.arange(ATT_KW)[None, :] - ATT_HALF - np.arange(ATT_BLOCK)[:, None]
    in_window = np.abs(rel_sub) <= ATT_HALF
    col = np.arange(ATT_KW)[None, :]
    tables = []
    for _, dil in ATT_PATTERNS:
        onehot = (_t5_bucket(rel_sub * dil)[:, :, None] == np.arange(N_REL_BUCKETS)).astype(np.float32)
        bias = jnp.einsum("ijk,kh->hij", jnp.asarray(onehot, BF16).astype(F32), rel_bias.astype(F32),
                          precision=lax.Precision.HIGHEST)
        tables.append(bias * LOG2_E - shift)
    bias = jnp.stack(tables, axis=0)
    variants = []
    for v in range(4):
        keep = in_window & ((col >= ATT_HALF) | (v & 1 == 0)) & ((col < ATT_BLOCK + ATT_HALF) | (v & 2 == 0))
        variants.append(jnp.where(keep[None, None], bias, NEG_INF))
    return jnp.stack(variants, axis=0)


def _attention_tables(rel_bias, q_norm_g, k_norm_g):
    logit_bound = (QK_BOUND_SLACK * HEAD_DIM * HEAD_DIM ** -0.5 * LOG2_E
                   * jnp.max(jnp.abs(q_norm_g)) * jnp.max(jnp.abs(k_norm_g))).astype(F32)
    bias_lo = (jnp.min(rel_bias) * LOG2_E).astype(F32)
    bias_hi = (jnp.max(rel_bias) * LOG2_E).astype(F32)
    narrow = 2.0 * logit_bound + (bias_hi - bias_lo) < EXP2_SAFE_RANGE
    return {
        "bias": _bias_tables(rel_bias, jnp.where(narrow, logit_bound + bias_hi, 0.0)),
        "narrow": narrow.astype(jnp.int32).reshape(1),
    }


def _head_expander(first_row, width):
    e = np.zeros((2 * LANES, width), np.float32)
    for h in range(width // HEAD_DIM):
        e[first_row + h, h * HEAD_DIM:(h + 1) * HEAD_DIM] = 1.0
        e[LANES + first_row + h, h * HEAD_DIM:(h + 1) * HEAD_DIM] = 1.0
    return jnp.asarray(e, BF16)


def _block_diag_mean(width):
    i = np.arange(width)
    return jnp.asarray((i[:, None] // HEAD_DIM == i[None, :] // HEAD_DIM) / HEAD_DIM, BF16)


def _layer(x, p, tm):
    b, s, _ = x.shape
    n = b * s
    xf = x.reshape(n, D_MODEL)
    x1 = _ffn(xf, p["ffn1_g"], p["ffn1_wg"], p["ffn1_wu"], p["ffn1_wd"], tm)
    q, k, v, z, xa, dt = _inproj(x1, p["mix_g"], p["wq"], p["wk"], p["wv"], p["wz"], p["wx"],
                                 p["wdt"], p["qg"], p["kg"], p["bd"], p["conv_w"], p["conv_b"], s, tm)
    seq = lambda a: a.reshape(b, s, a.shape[-1])
    pair_seq = lambda a: a.reshape(a.shape[0], b, s, LANES)
    yb = _ssd_bwd(seq(xa), seq(dt), p["dt_bias"], p["a_neg"], p["e_bwd"])
    attn, y = _mixer(pair_seq(q), pair_seq(k), pair_seq(v), p["att"], seq(xa), seq(dt), seq(z), yb,
                     p["dt_bias"], p["a_neg"], p["d_skip"], p["ssm_g"], p["e_fwd"])
    out = _outproj_ffn(x1, attn.reshape(attn.shape[0], n, LANES), y.reshape(n, D_SSM), p["attn_g"],
                       p["wo_att"], p["wo_ssm"], p["ffn2_g"], p["ffn2_wg"], p["ffn2_wu"],
                       p["ffn2_wd"], tm)
    return out.reshape(b, s, D_MODEL)


def _prepare(rel_bias, ffn1_norm_g, ffn1_w_gate, ffn1_w_up, ffn1_w_down, mix_norm_g, w_in,
             q_norm_g, k_norm_g, attn_out_g, conv_w, conv_b, dt_bias, a_log, d_skip, ssm_out_g,
             w_out, ffn2_norm_g, ffn2_w_gate, ffn2_w_up, ffn2_w_down):
    row = lambda a: a.reshape(1, -1).astype(F32)
    w16 = lambda a: a.astype(BF16)
    c0, c1, c2, c3 = D_ATT, 2 * D_ATT, 3 * D_ATT, 3 * D_ATT + D_SSM
    c4 = c3 + CONV_DIM
    pad32 = lambda a: jnp.pad(a.reshape(1, 2 * N_HEADS).astype(F32), ((0, 0), (0, LANES - 2 * N_HEADS)))
    return {
        "ffn1_g": row(ffn1_norm_g), "ffn1_wg": w16(ffn1_w_gate), "ffn1_wu": w16(ffn1_w_up),
        "ffn1_wd": w16(ffn1_w_down),
        "ffn2_g": row(ffn2_norm_g), "ffn2_wg": w16(ffn2_w_gate), "ffn2_wu": w16(ffn2_w_up),
        "ffn2_wd": w16(ffn2_w_down),
        "mix_g": row(mix_norm_g),
        "wq": w16(w_in[:, :c0]), "wk": w16(w_in[:, c0:c1]), "wv": w16(w_in[:, c1:c2]),
        "wz": w16(w_in[:, c2:c3]), "wx": w16(w_in[:, c3:c4]),
        "wdt": w16(jnp.pad(w_in[:, c4:], ((0, 0), (0, LANES - 2 * N_HEADS)))),
        "qg": row(jnp.tile(q_norm_g, N_HEADS)) * (HEAD_DIM ** -0.5 * LOG2_E),
        "kg": row(jnp.tile(k_norm_g, N_HEADS)),
        "bd": _block_diag_mean(MXU_TILE),
        "att": _attention_tables(rel_bias, q_norm_g, k_norm_g),
        "attn_g": row(attn_out_g),
        "conv_w": conv_w.astype(F32), "conv_b": row(conv_b),
        "dt_bias": pad32(dt_bias), "a_neg": pad32(-jnp.exp(a_log.astype(F32))),
        "d_skip": row(jnp.repeat(d_skip, HEAD_DIM)), "ssm_g": row(ssm_out_g),
        "e_fwd": _head_expander(0, D_SSM), "e_bwd": _head_expander(N_HEADS, D_SSM),
        "wo_att": w16(w_out[:D_ATT]), "wo_ssm": w16(w_out[D_ATT:]),
    }


def _trunk(x, layers, tm=512):
    for p in layers:
        x = _layer(x, p, tm)
    return x


def kernel(x_prompt, x_sample, rel_bias, ffn1_norm_g, ffn1_w_gate, ffn1_w_up, ffn1_w_down, mix_norm_g, w_in, q_norm_g, k_norm_g, attn_out_g, conv_w, conv_b, dt_bias, a_log, d_skip, ssm_out_g, w_out, ffn2_norm_g, ffn2_w_gate, ffn2_w_up, ffn2_w_down):
    per_layer = (ffn1_norm_g, ffn1_w_gate, ffn1_w_up, ffn1_w_down, mix_norm_g, w_in, q_norm_g,
                 k_norm_g, attn_out_g, conv_w, conv_b, dt_bias, a_log, d_skip, ssm_out_g, w_out,
                 ffn2_norm_g, ffn2_w_gate, ffn2_w_up, ffn2_w_down)
    layers = [_prepare(rel_bias, *(a[l] for a in per_layer)) for l in range(ffn1_norm_g.shape[0])]
    return (_trunk(x_prompt, layers), _trunk(x_sample, layers))
```
